```python
import math
import jax, jax.numpy as jnp
from jax import lax

D_MODEL = 1024
BATCH = 8
SEQ = 4096
DEPTH = 2
DEC_BATCH = 128
DEC_SEQ = 1
PAST_LEN = 16384
PAGE_SIZE = 128

MIX_DIM = D_MODEL
DN_DK = 128
DN_DV = 128
DN_HEADS = (MIX_DIM // 2) // DN_DV
SWA_DH = 64
SWA_HEADS = (MIX_DIM - DN_HEADS * DN_DV) // SWA_DH
SWA_KV_HEADS = SWA_HEADS // 4
SWA_GROUP = SWA_HEADS // SWA_KV_HEADS
WINDOW = 128
DN_CONV = 4
DN_CHUNK = 64
ROPE_THETA = 10000.0
D_FF = ((8 * D_MODEL // 3 + 127) // 128) * 128
N_MOD = 9
EPS = 1e-6
DN_QK = DN_HEADS * DN_DK
DN_V = DN_HEADS * DN_DV
DN_CONV_CH = 2 * DN_QK + DN_V
SWA_Q = SWA_HEADS * SWA_DH
SWA_KV = SWA_KV_HEADS * SWA_DH
SPLIT_POINTS = (DN_CONV_CH, DN_CONV_CH + DN_V, DN_CONV_CH + DN_V + DN_HEADS,
                DN_CONV_CH + DN_V + 2 * DN_HEADS, DN_CONV_CH + DN_V + 2 * DN_HEADS + SWA_Q,
                DN_CONV_CH + DN_V + 2 * DN_HEADS + SWA_Q + SWA_KV)
IN_DIM = SPLIT_POINTS[-1] + SWA_KV

kernel_name = 'hymba_gdn_swa_macaron_adaln_step'


def _rms_norm(x, w):
    xf = x.astype(jnp.float32)
    y = xf * lax.rsqrt(jnp.mean(xf * xf, axis=-1, keepdims=True) + EPS)
    return (y * w.astype(jnp.float32)).astype(x.dtype)


def _l2norm(x):
    return x * lax.rsqrt(jnp.sum(x * x, axis=-1, keepdims=True) + EPS)


def _adaln(c, w_ada, b_ada):
    m = jax.nn.silu(c) @ w_ada + b_ada
    return m.reshape(c.shape[0], N_MOD, 1, D_MODEL)


def _modulate(x, g, shift, scale):
    return _rms_norm(x, g) * (1.0 + scale) + shift


def _swiglu(h, w_gu, w_down):
    a, b = jnp.split(h @ w_gu, 2, axis=-1)
    return (jax.nn.silu(a) * b) @ w_down


def _rope(x, pos):
    d = x.shape[-1]
    half = d // 2
    inv = jnp.power(ROPE_THETA, -jnp.arange(half, dtype=jnp.float32) * 2.0 / d)
    ang = pos.astype(jnp.float32)[:, None] * inv[None, :]
    cos = jnp.cos(ang)[:, None, :]
    sin = jnp.sin(ang)[:, None, :]
    xf = x.astype(jnp.float32)
    x1, x2 = xf[..., :half], xf[..., half:]
    return jnp.concatenate([x1 * cos - x2 * sin, x2 * cos + x1 * sin], axis=-1).astype(x.dtype)


def _causal_conv(x, buf, w):
    T = x.shape[1]
    xp = jnp.concatenate([buf.astype(x.dtype), x], axis=1)
    y = xp[:, 0:T] * w[0]
    for j in range(1, DN_CONV):
        y = y + xp[:, j:j + T] * w[j]
    return y, xp[:, T:]


def _gated_delta_chunked(q, k, v, g, beta, S0):
    B, T, H, dk = q.shape
    dv = v.shape[-1]
    C = min(DN_CHUNK, T)
    pad = (-T) % C
    Tp = T + pad
    N = Tp // C

    def chunks(a):
        a = jnp.pad(a, [(0, 0), (0, pad)] + [(0, 0)] * (a.ndim - 2))
        a = a.reshape((B, N, C) + a.shape[2:])
        return jnp.moveaxis(a, 2, 3)

    q, k, v, g, beta = chunks(q), chunks(k), chunks(v), chunks(g), chunks(beta)
    gc = jnp.cumsum(g, axis=-1)
    idx = jnp.arange(C)
    incl = idx[:, None] >= idx[None, :]
    strict = idx[:, None] > idx[None, :]
    decay = jnp.exp(jnp.where(incl, gc[..., :, None] - gc[..., None, :], -jnp.inf))
    kb = k * beta[..., None]
    A = jnp.where(strict, jnp.einsum('bnhid,bnhjd->bnhij', kb, k) * decay, 0.0)
    eye = jnp.eye(C, dtype=jnp.float32)
    Tinv = lax.linalg.triangular_solve(eye + A, jnp.broadcast_to(eye, A.shape),
                                       left_side=True, lower=True, unit_diagonal=True)
    u = jnp.einsum('bnhij,bnhjd->bnhid', Tinv, v * beta[..., None])
    w = jnp.einsum('bnhij,bnhjd->bnhid', Tinv, kb * jnp.exp(gc)[..., None])
    qk = jnp.where(incl, jnp.einsum('bnhid,bnhjd->bnhij', q, k) * decay, 0.0)
    qd = q * jnp.exp(gc)[..., None]
    g_last = gc[..., -1]
    kd = k * jnp.exp(g_last[..., None] - gc)[..., None]

    def step(S, xs):
        u_n, w_n, qk_n, qd_n, kd_n, gl_n = xs
        v_new = u_n - jnp.einsum('bhcd,bhde->bhce', w_n, S)
        o_n = jnp.einsum('bhcd,bhde->bhce', qd_n, S) + jnp.einsum('bhij,bhje->bhie', qk_n, v_new)
        S = S * jnp.exp(gl_n)[..., None, None] + jnp.einsum('bhcd,bhce->bhde', kd_n, v_new)
        return S, o_n

    xs = tuple(jnp.moveaxis(a, 1, 0) for a in (u, w, qk, qd, kd, g_last))
    S, o = lax.scan(step, S0, xs)
    o = jnp.transpose(o, (1, 0, 3, 2, 4)).reshape(B, Tp, H, dv)[:, :T]
    return o, S


def _deltanet(p_conv, p_z, p_b, p_a, conv_buf, S0, conv_w, A_log, dt_bias, norm_w):
    B, T, _ = p_conv.shape
    qkv, conv_new = _causal_conv(p_conv, conv_buf, conv_w)
    qkv = jax.nn.silu(qkv.astype(jnp.float32))
    q, k, v = jnp.split(qkv, [DN_QK, 2 * DN_QK], axis=-1)
    q = _l2norm(q.reshape(B, T, DN_HEADS, DN_DK)) * (DN_DK ** -0.5)
    k = _l2norm(k.reshape(B, T, DN_HEADS, DN_DK))
    v = v.reshape(B, T, DN_HEADS, DN_DV)
    beta = jax.nn.sigmoid(p_b.astype(jnp.float32))
    g = -jnp.exp(A_log.astype(jnp.float32)) * jax.nn.softplus(
        p_a.astype(jnp.float32) + dt_bias.astype(jnp.float32))
    o, S = _gated_delta_chunked(q, k, v, g, beta, S0.astype(jnp.float32))
    z = p_z.reshape(B, T, DN_HEADS, DN_DV).astype(jnp.float32)
    o = _rms_norm(o, norm_w) * jax.nn.silu(z)
    return o.reshape(B, T, DN_V).astype(p_conv.dtype), S, conv_new


def _sink_softmax(s, mask, sink):
    s = jnp.where(mask, s, -jnp.inf)
    m = jnp.maximum(jnp.max(s, axis=-1, keepdims=True), sink)
    p = jnp.exp(s - m)
    return p / (jnp.sum(p, axis=-1, keepdims=True) + jnp.exp(sink - m))


def _swa_banded(q, k, v, sinks):
    B, T = q.shape[0], q.shape[1]
    nb = T // WINDOW
    qb = q.reshape(B, nb, WINDOW, SWA_KV_HEADS, SWA_GROUP, SWA_DH).astype(jnp.float32)
    kb = k.reshape(B, nb, WINDOW, SWA_KV_HEADS, SWA_DH).astype(jnp.float32)
    vb = v.reshape(B, nb, WINDOW, SWA_KV_HEADS, SWA_DH).astype(jnp.float32)
    shift = ((0, 0), (1, 0), (0, 0), (0, 0), (0, 0))
    kk = jnp.concatenate([jnp.pad(kb, shift)[:, :-1], kb], axis=2)
    vv = jnp.concatenate([jnp.pad(vb, shift)[:, :-1], vb], axis=2)
    s = jnp.einsum('bnqhgd,bnkhd->bnhgqk', qb, kk) * (SWA_DH ** -0.5)
    qi = jnp.arange(WINDOW)[:, None] + WINDOW
    kj = jnp.arange(2 * WINDOW)[None, :]
    dist = qi - kj
    kglob = jnp.arange(nb)[:, None, None] * WINDOW - WINDOW + kj[None]
    mask = (dist >= 0) & (dist <= WINDOW) & (kglob >= 0)
    sink = sinks.astype(jnp.float32).reshape(1, 1, SWA_KV_HEADS, SWA_GROUP, 1, 1)
    p = _sink_softmax(s, mask[None, :, None, None], sink)
    o = jnp.einsum('bnhgqk,bnkhd->bnqhgd', p, vv)
    return o.reshape(B, T, SWA_Q).astype(q.dtype)


def _swa_step(q, k, v, k_buf, v_buf, sinks):
    B, T = q.shape[0], q.shape[1]
    W0 = k_buf.shape[1]
    kk = jnp.concatenate([k_buf.astype(k.dtype), k], axis=1)
    vv = jnp.concatenate([v_buf.astype(v.dtype), v], axis=1)
    qpos = PAST_LEN + jnp.arange(T)
    kpos = PAST_LEN - W0 + jnp.arange(W0 + T)
    dist = qpos[:, None] - kpos[None, :]
    mask = (dist >= 0) & (dist <= WINDOW)
    qg = q.reshape(B, T, SWA_KV_HEADS, SWA_GROUP, SWA_DH).astype(jnp.float32)
    s = jnp.einsum('bqhgd,bkhd->bhgqk', qg, kk.astype(jnp.float32)) * (SWA_DH ** -0.5)
    sink = sinks.astype(jnp.float32).reshape(1, SWA_KV_HEADS, SWA_GROUP, 1, 1)
    p = _sink_softmax(s, mask, sink)
    o = jnp.einsum('bhgqk,bkhd->bqhgd', p, vv.astype(jnp.float32))
    return o.reshape(B, T, SWA_Q).astype(q.dtype), kk[:, T:], vv[:, T:]


def _layer(x, c, pos, conv_buf, S0, k_buf, v_buf, prm):
    (w_ada, b_ada, g_ffn1, w_ffn1_gu, w_ffn1_down, g_mix, w_in, dn_conv_w, dn_A_log, dn_dt_bias,
     dn_norm_w, swa_q_norm, swa_k_norm, swa_sinks, w_out, g_ffn2, w_ffn2_gu, w_ffn2_down) = prm
    B, T, _ = x.shape
    m = _adaln(c, w_ada, b_ada)
    h = _modulate(x, g_ffn1, m[:, 0], m[:, 1])
    x = x + 0.5 * m[:, 2] * _swiglu(h, w_ffn1_gu, w_ffn1_down)
    h = _modulate(x, g_mix, m[:, 3], m[:, 4])
    p_conv, p_z, p_b, p_a, p_q, p_k, p_v = jnp.split(h @ w_in, SPLIT_POINTS, axis=-1)
    dn_out, S_new, conv_new = _deltanet(p_conv, p_z, p_b, p_a, conv_buf, S0, dn_conv_w,
                                        dn_A_log, dn_dt_bias, dn_norm_w)
    q = _rope(_rms_norm(p_q.reshape(B, T, SWA_HEADS, SWA_DH), swa_q_norm), pos)
    k = _rope(_rms_norm(p_k.reshape(B, T, SWA_KV_HEADS, SWA_DH), swa_k_norm), pos)
    v = p_v.reshape(B, T, SWA_KV_HEADS, SWA_DH)
    if k_buf is None:
        swa_out = _swa_banded(q, k, v, swa_sinks)
        nkeep = min(WINDOW, T)
        k_new, v_new = k[:, T - nkeep:], v[:, T - nkeep:]
    else:
        swa_out, k_new, v_new = _swa_step(q, k, v, k_buf, v_buf, swa_sinks)
    mix = jnp.concatenate([dn_out, swa_out], axis=-1) @ w_out
    x = x + m[:, 5] * mix
    h = _modulate(x, g_ffn2, m[:, 6], m[:, 7])
    x = x + 0.5 * m[:, 8] * _swiglu(h, w_ffn2_gu, w_ffn2_down)
    return x, conv_new, S_new, k_new, v_new


def setup_inputs(seed: int = 0) -> dict:
    key = jax.random.key(seed)
    ks = jax.random.split(key, 32)
    f32 = jnp.float32
    L, D = DEPTH, D_MODEL
    swa_buf = min(WINDOW, PAST_LEN)

    def nrm(k, shape, s):
        return jax.random.normal(k, shape, f32) * s

    def gain(k, shape):
        return 1.0 + 0.02 * jax.random.normal(k, shape, f32)

    A = jax.random.uniform(ks[20], (L, DN_HEADS), f32, 1.0, 16.0)
    dt = jnp.exp(jax.random.uniform(ks[21], (L, DN_HEADS), f32, math.log(1e-3), math.log(1e-1)))
    return {
        'x_prompt': nrm(ks[0], (BATCH, SEQ, D), 1.0),
        'x_sample': nrm(ks[1], (DEC_BATCH, DEC_SEQ, D), 1.0),
        'c_prompt': nrm(ks[2], (BATCH, D), 1.0),
        'c_sample': nrm(ks[3], (DEC_BATCH, D), 1.0),
        'state_dn_conv': nrm(ks[4], (L, DEC_BATCH, DN_CONV - 1, DN_CONV_CH), 1.0),
        'state_dn_S': nrm(ks[5], (L, DEC_BATCH, DN_HEADS, DN_DK, DN_DV), DN_DK ** -0.5),
        'cache_swa_k': nrm(ks[6], (L, DEC_BATCH, swa_buf, SWA_KV_HEADS, SWA_DH), 1.0),
        'cache_swa_v': nrm(ks[7], (L, DEC_BATCH, swa_buf, SWA_KV_HEADS, SWA_DH), 1.0),
        'w_ada': nrm(ks[8], (L, D, N_MOD * D), 0.5 * D ** -0.5),
        'b_ada': nrm(ks[9], (L, N_MOD * D), 0.01),
        'g_ffn1': gain(ks[10], (L, D)),
        'w_ffn1_gu': nrm(ks[11], (L, D, 2 * D_FF), D ** -0.5),
        'w_ffn1_down': nrm(ks[12], (L, D_FF, D), D_FF ** -0.5),
        'g_mix': gain(ks[13], (L, D)),
        'w_in': nrm(ks[14], (L, D, IN_DIM), D ** -0.5),
        'dn_conv_w': nrm(ks[15], (L, DN_CONV, DN_CONV_CH), DN_CONV ** -0.5),
        'dn_A_log': jnp.log(A),
        'dn_dt_bias': dt + jnp.log(-jnp.expm1(-dt)),
        'dn_norm_w': gain(ks[16], (L, DN_DV)),
        'swa_q_norm': gain(ks[17], (L, SWA_DH)),
        'swa_k_norm': gain(ks[18], (L, SWA_DH)),
        'swa_sinks': nrm(ks[19], (L, SWA_HEADS), 0.5),
        'w_out': nrm(ks[22], (L, MIX_DIM, D), MIX_DIM ** -0.5),
        'g_ffn2': gain(ks[23], (L, D)),
        'w_ffn2_gu': nrm(ks[24], (L, D, 2 * D_FF), D ** -0.5),
        'w_ffn2_down': nrm(ks[25], (L, D_FF, D), D_FF ** -0.5),
    }


def reference(x_prompt, x_sample, c_prompt, c_sample, state_dn_conv, state_dn_S, cache_swa_k,
              cache_swa_v, w_ada, b_ada, g_ffn1, w_ffn1_gu, w_ffn1_down, g_mix, w_in, dn_conv_w,
              dn_A_log, dn_dt_bias, dn_norm_w, swa_q_norm, swa_k_norm, swa_sinks, w_out, g_ffn2,
              w_ffn2_gu, w_ffn2_down):
    Bp, Tp = x_prompt.shape[0], x_prompt.shape[1]
    pos_p = jnp.arange(Tp)
    pos_s = PAST_LEN + jnp.arange(x_sample.shape[1])
    yp, ys = x_prompt, x_sample
    p_conv, p_S, p_k, p_v = [], [], [], []
    s_conv, s_S, s_k, s_v = [], [], [], []
    for l in range(DEPTH):
        prm = (w_ada[l], b_ada[l], g_ffn1[l], w_ffn1_gu[l], w_ffn1_down[l], g_mix[l], w_in[l],
               dn_conv_w[l], dn_A_log[l], dn_dt_bias[l], dn_norm_w[l], swa_q_norm[l],
               swa_k_norm[l], swa_sinks[l], w_out[l], g_ffn2[l], w_ffn2_gu[l], w_ffn2_down[l])
        conv0 = jnp.zeros((Bp, DN_CONV - 1, DN_CONV_CH), x_prompt.dtype)
        S0 = jnp.zeros((Bp, DN_HEADS, DN_DK, DN_DV), jnp.float32)
        yp, cv, S, kn, vn = _layer(yp, c_prompt, pos_p, conv0, S0, None, None, prm)
        p_conv.append(cv); p_S.append(S); p_k.append(kn); p_v.append(vn)
        ys, cv, S, kn, vn = _layer(ys, c_sample, pos_s, state_dn_conv[l], state_dn_S[l],
                                   cache_swa_k[l], cache_swa_v[l], prm)
        s_conv.append(cv); s_S.append(S); s_k.append(kn); s_v.append(vn)
    return (yp, ys, jnp.stack(p_conv), jnp.stack(p_S), jnp.stack(p_k), jnp.stack(p_v),
            jnp.stack(s_conv), jnp.stack(s_S), jnp.stack(s_k), jnp.stack(s_v))
```

```python
import functools
import math

import jax
import jax.numpy as jnp
from jax import lax
from jax.experimental import pallas as pl
from jax.experimental.pallas import tpu as pltpu

F32 = jnp.float32
BF16 = jnp.bfloat16

D_MODEL = 1024
D_FF = 2816
N_MOD = 9
EPS = 1e-6
DN_HEADS = 4
DN_DK = 128
DN_DV = 128
DN_QK = DN_HEADS * DN_DK
DN_V = DN_HEADS * DN_DV
DN_CONV = 4
DN_CONV_CH = 2 * DN_QK + DN_V
DN_CHUNK = 64
INV_BLOCK = 16
SWA_DH = 64
SWA_HEADS = 8
SWA_KV_HEADS = 2
SWA_GROUP = SWA_HEADS // SWA_KV_HEADS
SWA_Q = SWA_HEADS * SWA_DH
SWA_KV = SWA_KV_HEADS * SWA_DH
WINDOW = 128
ROPE_THETA = 10000.0
PAST_LEN = 16384

LANES = 128
SUBLANES = 8
HALO = SUBLANES

C_CONV = 0
C_Z = C_CONV + DN_CONV_CH
C_SQ = C_Z + DN_V
C_SK = C_SQ + SWA_Q
C_SV = C_SK + SWA_KV
C_BG = C_SV + SWA_KV
IN_PACKED = C_BG + LANES
SWA_OUT = SWA_Q + 2 * SWA_KV

VMEM_LIMIT = 56 * 1024 * 1024


def _cparams(sem):
    return pltpu.CompilerParams(dimension_semantics=sem, vmem_limit_bytes=VMEM_LIMIT)


def _const_spec(shape):
    nd = len(shape)
    return pl.BlockSpec(shape, lambda *_: (0,) * nd, pipeline_mode=pl.Buffered(1))


def _sigmoid(x):
    return 1.0 / (1.0 + jnp.exp(-x))


def _silu(x):
    return x * _sigmoid(x)


def _adaln_kernel(cp_ref, cs_ref, w_ref, b_ref, mp_ref, ms_ref):
    w = w_ref[...].astype(BF16)
    b = b_ref[...]
    for c_ref, m_ref in ((cp_ref, mp_ref), (cs_ref, ms_ref)):
        c = c_ref[...]
        m_ref[...] = jnp.dot(_silu(c).astype(BF16), w, preferred_element_type=F32) + b


def _adaln(c_p, c_s, w_ada, b_ada):
    L = w_ada.shape[0]
    bp, bs = c_p.shape[0], c_s.shape[0]
    b4 = b_ada.reshape(L, N_MOD, 1, D_MODEL)
    return pl.pallas_call(
        _adaln_kernel,
        grid=(L, N_MOD),
        in_specs=[
            pl.BlockSpec((bp, D_MODEL), lambda l, j: (0, 0)),
            pl.BlockSpec((bs, D_MODEL), lambda l, j: (0, 0)),
            pl.BlockSpec((None, D_MODEL, D_MODEL), lambda l, j: (l, 0, j)),
            pl.BlockSpec((None, None, 1, D_MODEL), lambda l, j: (l, j, 0, 0)),
        ],
        out_specs=[
            pl.BlockSpec((None, None, bp, D_MODEL), lambda l, j: (l, j, 0, 0)),
            pl.BlockSpec((None, None, bs, D_MODEL), lambda l, j: (l, j, 0, 0)),
        ],
        out_shape=[
            jax.ShapeDtypeStruct((L, N_MOD, bp, D_MODEL), F32),
            jax.ShapeDtypeStruct((L, N_MOD, bs, D_MODEL), F32),
        ],
        compiler_params=_cparams(("parallel", "parallel")),
        name="adaln",
    )(c_p, c_s, w_ada, b4)


def _mod_norm(x, g, shift, scale):
    ms = jnp.mean(x * x, axis=-1, keepdims=True)
    y = x * lax.rsqrt(ms + EPS) * g
    return y * (1.0 + scale) + shift


def _ffn_body(x, shift, scale, gate, g, wgu_ref, wd_ref):
    h = _mod_norm(x, g, shift, scale).astype(BF16)
    gu = jnp.dot(h, wgu_ref[...], preferred_element_type=F32)
    a = gu[:, :D_FF]
    b = gu[:, D_FF:]
    act = (_silu(a) * b).astype(BF16)
    y = jnp.dot(act, wd_ref[...], preferred_element_type=F32)
    return x + (0.5 * gate) * y


def _mod_specs(m, sub, tm):
    rm = m.shape[2]
    blk = 1 if rm == 1 else tm

    def spec(i):
        if rm == 1:
            return pl.BlockSpec((None, None, 1, D_MODEL), lambda g, r: (i, g, 0, 0))
        return pl.BlockSpec((None, None, blk, D_MODEL), lambda g, r: (i, g, r, 0))

    return [spec(3 * sub), spec(3 * sub + 1), spec(3 * sub + 2)]


def _row_spec(tm, width):
    return pl.BlockSpec((None, tm, width), lambda g, r: (g, r, 0))


def _ffn_kernel(x_ref, sh_ref, sc_ref, ga_ref, g_ref, wgu_ref, wd_ref, o_ref):
    o_ref[...] = _ffn_body(x_ref[...], sh_ref[...], sc_ref[...], ga_ref[...], g_ref[...],
                           wgu_ref, wd_ref)


def _ffn(x, m, sub, g, wgu, wd, tm):
    G, R, _ = x.shape
    return pl.pallas_call(
        _ffn_kernel,
        grid=(G, R // tm),
        in_specs=[_row_spec(tm, D_MODEL)] + _mod_specs(m, sub, tm) + [
            _const_spec((1, D_MODEL)),
            _const_spec((D_MODEL, 2 * D_FF)),
            _const_spec((D_FF, D_MODEL)),
        ],
        out_specs=_row_spec(tm, D_MODEL),
        out_shape=jax.ShapeDtypeStruct(x.shape, F32),
        compiler_params=_cparams(("parallel", "parallel")),
        name="ffn",
    )(x, m, m, m, g, wgu, wd)


def _group_mean_sq(x, gmat_ref):
    sq = x * x
    hi = sq.astype(BF16)
    lo = (sq - hi.astype(F32)).astype(BF16)
    gm = gmat_ref[...]
    return (jnp.dot(hi, gm, preferred_element_type=F32)
            + jnp.dot(lo, gm, preferred_element_type=F32))


def _rope(x, cos, sin_signed):
    width = x.shape[-1]
    reps = width // LANES
    lane = lax.broadcasted_iota(jnp.int32, x.shape, 1)
    first_half = (lane & (SWA_DH - 1)) < (SWA_DH // 2)
    partner = jnp.where(first_half,
                        pltpu.roll(x, width - SWA_DH // 2, 1),
                        pltpu.roll(x, SWA_DH // 2, 1))
    if reps > 1:
        cos = jnp.concatenate([cos] * reps, axis=1)
        sin_signed = jnp.concatenate([sin_signed] * reps, axis=1)
    return x * cos + partner * sin_signed


def _mix_post(p, qkv_conv, prm_ref, cos, sin, gq_ref, gk_ref, qkv_ref, z_ref, bg_ref, swa_ref):
    qkv = _silu(qkv_conv)
    for hh in range(2 * DN_HEADS):
        xh = qkv[:, hh * DN_DK:(hh + 1) * DN_DK]
        s = lax.rsqrt(jnp.sum(xh * xh, axis=-1, keepdims=True) + EPS)
        if hh < DN_HEADS:
            s = s * (DN_DK ** -0.5)
        qkv_ref[:, hh * DN_DK:(hh + 1) * DN_DK] = xh * s
    qkv_ref[:, 2 * DN_QK:] = qkv[:, 2 * DN_QK:]
    z_ref[...] = p[:, C_Z:C_Z + DN_V]

    pb = p[:, C_BG:C_BG + LANES]
    neg_a = prm_ref[0:1, :]
    dt_b = prm_ref[1:2, :]
    xa = pb + dt_b
    softplus = jnp.maximum(xa, 0.0) + jnp.log(1.0 + jnp.exp(-jnp.abs(xa)))
    lane = lax.broadcasted_iota(jnp.int32, pb.shape, 1)
    bg_ref[...] = jnp.where(lane < DN_HEADS, _sigmoid(pb), neg_a * softplus)

    pq = p[:, C_SQ:C_SQ + SWA_Q]
    pk = p[:, C_SK:C_SK + SWA_KV]
    wq = jnp.concatenate([prm_ref[2:3, :]] * (SWA_Q // LANES), axis=1)
    wk = prm_ref[3:4, :]
    qn = pq * lax.rsqrt(_group_mean_sq(pq, gq_ref) + EPS) * wq
    kn = pk * lax.rsqrt(_group_mean_sq(pk, gk_ref) + EPS) * wk
    swa_ref[:, :SWA_Q] = _rope(qn, cos, sin)
    swa_ref[:, SWA_Q:SWA_Q + SWA_KV] = _rope(kn, cos, sin)
    swa_ref[:, SWA_Q + SWA_KV:] = p[:, C_SV:C_SV + SWA_KV]


def _inproj_prompt_kernel(x_ref, sh_ref, sc_ref, g_ref, win_ref, cw_ref, prm_ref, cos_ref, sin_ref,
                          gq_ref, gk_ref, qkv_ref, z_ref, bg_ref, swa_ref, tail_ref, ext_ref):
    tm = x_ref.shape[0]
    h = _mod_norm(x_ref[...], g_ref[...], sh_ref[...], sc_ref[...]).astype(BF16)
    p = jnp.dot(h, win_ref[...], preferred_element_type=F32)
    pc = p[:, :DN_CONV_CH]

    @pl.when(pl.program_id(1) == 0)
    def _():
        ext_ref[0:HALO, :] = jnp.zeros((HALO, DN_CONV_CH), F32)

    ext_ref[HALO:, :] = pc
    y = pc * cw_ref[DN_CONV - 1:DN_CONV, :]
    for j in range(DN_CONV - 1):
        off = HALO - (DN_CONV - 1) + j
        y = y + ext_ref[off:off + tm, :] * cw_ref[j:j + 1, :]
    ext_ref[0:HALO, :] = pc[tm - HALO:, :]
    tail_ref[...] = pc[tm - HALO:, :]
    _mix_post(p, y, prm_ref, cos_ref[...], sin_ref[...], gq_ref, gk_ref,
              qkv_ref, z_ref, bg_ref, swa_ref)


def _inproj_sample_kernel(x_ref, sh_ref, sc_ref, g_ref, win_ref, cw_ref, prm_ref, cos_ref, sin_ref,
                          gq_ref, gk_ref, buf_ref, qkv_ref, z_ref, bg_ref, swa_ref, cnew_ref):
    h = _mod_norm(x_ref[...], g_ref[...], sh_ref[...], sc_ref[...]).astype(BF16)
    p = jnp.dot(h, win_ref[...], preferred_element_type=F32)
    pc = p[:, :DN_CONV_CH]
    y = pc * cw_ref[DN_CONV - 1:DN_CONV, :]
    for j in range(DN_CONV - 1):
        y = y + buf_ref[j] * cw_ref[j:j + 1, :]
    for j in range(DN_CONV - 2):
        cnew_ref[j] = buf_ref[j + 1]
    cnew_ref[DN_CONV - 2] = pc
    _mix_post(p, y, prm_ref, cos_ref[...], sin_ref[...], gq_ref, gk_ref,
              qkv_ref, z_ref, bg_ref, swa_ref)


def _inproj_common_specs(m, tm):
    sh, sc, _ = _mod_specs(m, 1, tm)
    return [_row_spec(tm, D_MODEL), sh, sc,
            _const_spec((1, D_MODEL)),
            _const_spec((D_MODEL, IN_PACKED)),
            _const_spec((DN_CONV, DN_CONV_CH)),
            _const_spec((SUBLANES, LANES))]


def _inproj_prompt(x, m, g, win, cw, prm, cos, sin, gq, gk, tm):
    B, T, _ = x.shape
    outs = pl.pallas_call(
        _inproj_prompt_kernel,
        grid=(B, T // tm),
        in_specs=_inproj_common_specs(m, tm) + [
            pl.BlockSpec((tm, LANES), lambda b, r: (r, 0)),
            pl.BlockSpec((tm, LANES), lambda b, r: (r, 0)),
            _const_spec((SWA_Q, SWA_Q)),
            _const_spec((SWA_KV, SWA_KV)),
        ],
        out_specs=[
            _row_spec(tm, DN_CONV_CH), _row_spec(tm, DN_V), _row_spec(tm, LANES),
            _row_spec(tm, SWA_OUT),
            pl.BlockSpec((None, HALO, DN_CONV_CH), lambda b, r: (b, 0, 0)),
        ],
        out_shape=[
            jax.ShapeDtypeStruct((B, T, DN_CONV_CH), F32),
            jax.ShapeDtypeStruct((B, T, DN_V), F32),
            jax.ShapeDtypeStruct((B, T, LANES), F32),
            jax.ShapeDtypeStruct((B, T, SWA_OUT), F32),
            jax.ShapeDtypeStruct((B, HALO, DN_CONV_CH), F32),
        ],
        scratch_shapes=[pltpu.VMEM((tm + HALO, DN_CONV_CH), F32)],
        compiler_params=_cparams(("parallel", "arbitrary")),
        name="inproj_prompt",
    )(x, m, m, g, win, cw, prm, cos, sin, gq, gk)
    return outs


def _inproj_sample(x, m, g, win, cw, prm, cos, sin, gq, gk, buf):
    _, R, _ = x.shape
    nbuf = DN_CONV - 1
    return pl.pallas_call(
        _inproj_sample_kernel,
        grid=(1, 1),
        in_specs=_inproj_common_specs(m, R) + [
            _const_spec((1, LANES)), _const_spec((1, LANES)),
            _const_spec((SWA_Q, SWA_Q)), _const_spec((SWA_KV, SWA_KV)),
            _const_spec((nbuf, R, DN_CONV_CH)),
        ],
        out_specs=[
            _row_spec(R, DN_CONV_CH), _row_spec(R, DN_V), _row_spec(R, LANES), _row_spec(R, SWA_OUT),
            pl.BlockSpec((nbuf, R, DN_CONV_CH), lambda g_, r: (0, 0, 0)),
        ],
        out_shape=[
            jax.ShapeDtypeStruct((1, R, DN_CONV_CH), F32),
            jax.ShapeDtypeStruct((1, R, DN_V), F32),
            jax.ShapeDtypeStruct((1, R, LANES), F32),
            jax.ShapeDtypeStruct((1, R, SWA_OUT), F32),
            jax.ShapeDtypeStruct((nbuf, R, DN_CONV_CH), F32),
        ],
        compiler_params=_cparams(("arbitrary", "arbitrary")),
        name="inproj_sample",
    )(x, m, m, g, win, cw, prm, cos, sin, gq, gk, buf)


def _gated_norm(o, z, nw):
    on = o * lax.rsqrt(jnp.mean(o * o, axis=-1, keepdims=True) + EPS) * nw
    return on * _silu(z)


def _unit_lower_inverse(a):
    c = a.shape[0]
    row = lax.broadcasted_iota(jnp.int32, (c, c), 0)
    col = lax.broadcasted_iota(jnp.int32, (c, c), 1)
    s = INV_BLOCK
    a0 = jnp.where((row // s) == (col // s), a, 0.0)
    x = jnp.where(row == col, 1.0, 0.0) - a0
    pw = a0
    k = 2
    while k < s:
        pw_b = pw.astype(BF16)
        pw = jnp.dot(pw_b, pw_b, preferred_element_type=F32)
        x = x + jnp.dot(x.astype(BF16), pw.astype(BF16), preferred_element_type=F32)
        k *= 2
    while s < c:
        pair = ((row // (2 * s)) == (col // (2 * s))) & ((row // s) != (col // s))
        off = jnp.where(pair, a, 0.0)
        t = jnp.dot(x.astype(BF16), off.astype(BF16), preferred_element_type=F32)
        x = x - jnp.dot(t.astype(BF16), x.astype(BF16), preferred_element_type=F32)
        s *= 2
    return x


def _cumsum_rows(x):
    n = x.shape[0]
    row = lax.broadcasted_iota(jnp.int32, x.shape, 0)
    s = 1
    while s < n:
        x = x + jnp.where(row >= s, pltpu.roll(x, s, 0), 0.0)
        s *= 2
    return x


def _delta_prompt_kernel(q_ref, k_ref, v_ref, bg_ref, z_ref, nw_ref, o_ref, sout_ref, s_ref):
    n = pl.program_id(1)
    c = q_ref.shape[0]

    @pl.when(n == 0)
    def _():
        s_ref[...] = jnp.zeros_like(s_ref)

    bg = bg_ref[...]
    gc = _cumsum_rows(bg)
    gct = gc.T
    row = lax.broadcasted_iota(jnp.int32, (c, c), 0)
    col = lax.broadcasted_iota(jnp.int32, (c, c), 1)
    incl = row >= col
    strict = row > col
    nw = nw_ref[...]
    for h in range(DN_HEADS):
        sl = slice(h * DN_DK, (h + 1) * DN_DK)
        q = q_ref[:, sl]
        k = k_ref[:, sl]
        v = v_ref[:, sl]
        beta = bg[:, h:h + 1]
        gcol = gc[:, DN_HEADS + h:DN_HEADS + h + 1]
        grow = gct[DN_HEADS + h:DN_HEADS + h + 1, :]
        glast = gc[c - 1:c, DN_HEADS + h:DN_HEADS + h + 1]
        decay = jnp.exp(jnp.where(incl, gcol - grow, -jnp.inf))
        egc = jnp.exp(gcol)
        kb = k * beta
        k_b = k.astype(BF16)
        kq = lax.dot_general(jnp.concatenate([kb, q], axis=0).astype(BF16), k_b,
                             (((1,), (1,)), ((), ())), preferred_element_type=F32)
        a = jnp.where(strict, kq[:c] * decay, 0.0)
        qk = jnp.where(incl, kq[c:] * decay, 0.0)
        tinv = _unit_lower_inverse(a)
        rhs = jnp.concatenate([v * beta, kb * egc], axis=1).astype(BF16)
        uw = jnp.dot(tinv.astype(BF16), rhs, preferred_element_type=F32)
        u = uw[:, :DN_DV]
        w = uw[:, DN_DV:]
        s_old = s_ref[h]
        ws = jnp.dot(jnp.concatenate([w, q * egc], axis=0).astype(BF16), s_old.astype(BF16),
                     preferred_element_type=F32)
        v_new = u - ws[:c]
        v_new_b = v_new.astype(BF16)
        o = ws[c:] + jnp.dot(qk.astype(BF16), v_new_b, preferred_element_type=F32)
        kd = k * jnp.exp(glast - gcol)
        s_new = s_old * jnp.exp(glast) + jnp.dot(kd.T.astype(BF16), v_new_b,
                                                 preferred_element_type=F32)
        s_ref[h] = s_new
        o_ref[:, sl] = _gated_norm(o, z_ref[:, sl], nw)

    @pl.when(n == pl.num_programs(1) - 1)
    def _():
        sout_ref[...] = s_ref[...]


def _delta_prompt(qkv, bg, z, nw):
    B, T, _ = qkv.shape
    c = DN_CHUNK
    qkv_spec = lambda j: pl.BlockSpec((None, c, DN_QK), lambda b, n: (b, n, j))
    return pl.pallas_call(
        _delta_prompt_kernel,
        grid=(B, T // c),
        in_specs=[qkv_spec(0), qkv_spec(1), qkv_spec(2),
                  pl.BlockSpec((None, c, LANES), lambda b, n: (b, n, 0)),
                  pl.BlockSpec((None, c, DN_V), lambda b, n: (b, n, 0)),
                  _const_spec((1, DN_DV))],
        out_specs=[pl.BlockSpec((None, c, DN_V), lambda b, n: (b, n, 0)),
                   pl.BlockSpec((None, DN_HEADS, DN_DK, DN_DV), lambda b, n: (b, 0, 0, 0))],
        out_shape=[jax.ShapeDtypeStruct((B, T, DN_V), F32),
                   jax.ShapeDtypeStruct((B, DN_HEADS, DN_DK, DN_DV), F32)],
        scratch_shapes=[pltpu.VMEM((DN_HEADS, DN_DK, DN_DV), F32)],
        compiler_params=_cparams(("parallel", "arbitrary")),
        name="delta_prompt",
    )(qkv, qkv, qkv, bg, z, nw)


def _delta_sample_kernel(qkv_ref, bg_ref, z_ref, nw_ref, s0_ref, o_ref, s_ref):
    bb = qkv_ref.shape[0]
    nw = nw_ref[...]
    zeros = jnp.zeros((SUBLANES - 2, DN_DK), F32)
    for i in range(bb):
        for h in range(DN_HEADS):
            sl = slice(h * DN_DK, (h + 1) * DN_DK)
            q = qkv_ref[i:i + 1, sl]
            k = qkv_ref[i:i + 1, DN_QK + h * DN_DK:DN_QK + (h + 1) * DN_DK]
            v = qkv_ref[i:i + 1, 2 * DN_QK + h * DN_DV:2 * DN_QK + (h + 1) * DN_DV]
            beta = bg_ref[i:i + 1, h:h + 1]
            eg = jnp.exp(bg_ref[i:i + 1, DN_HEADS + h:DN_HEADS + h + 1])
            s_old = s0_ref[i, h]
            kb = k * beta
            lhs = jnp.concatenate([kb * eg, q * eg, zeros], axis=0)
            ws = jnp.dot(lhs, s_old, preferred_element_type=F32)
            v_new = v * beta - ws[0:1]
            qk = jnp.sum(q * k, axis=-1, keepdims=True)
            o = ws[1:2] + qk * v_new
            kcol = jnp.concatenate([k, zeros, jnp.zeros((1, DN_DK), F32)], axis=0).T
            vrow = jnp.concatenate([v_new, zeros, jnp.zeros((1, DN_DV), F32)], axis=0)
            s_ref[i, h] = s_old * eg + jnp.dot(kcol, vrow, preferred_element_type=F32)
            o_ref[i:i + 1, sl] = _gated_norm(o, z_ref[i:i + 1, sl], nw)


def _delta_sample(qkv, bg, z, nw, s0, bb):
    R = qkv.shape[0]
    return pl.pallas_call(
        _delta_sample_kernel,
        grid=(R // bb,),
        in_specs=[pl.BlockSpec((bb, DN_CONV_CH), lambda i: (i, 0)),
                  pl.BlockSpec((bb, LANES), lambda i: (i, 0)),
                  pl.BlockSpec((bb, DN_V), lambda i: (i, 0)),
                  _const_spec((1, DN_DV)),
                  pl.BlockSpec((bb, DN_HEADS, DN_DK, DN_DV), lambda i: (i, 0, 0, 0))],
        out_specs=[pl.BlockSpec((bb, DN_V), lambda i: (i, 0)),
                   pl.BlockSpec((bb, DN_HEADS, DN_DK, DN_DV), lambda i: (i, 0, 0, 0))],
        out_shape=[jax.ShapeDtypeStruct((R, DN_V), F32),
                   jax.ShapeDtypeStruct(s0.shape, F32)],
        compiler_params=_cparams(("parallel",)),
        name="delta_sample",
    )(qkv, bg, z, nw, s0)


def _swa_prompt_kernel(q_ref, kc_ref, vc_ref, kp_ref, vp_ref, sink_ref, o_ref):
    n = pl.program_id(1)
    w = q_ref.shape[0]
    row = lax.broadcasted_iota(jnp.int32, (w, 2 * w), 0)
    col = lax.broadcasted_iota(jnp.int32, (w, 2 * w), 1)
    lo = jnp.where(n == 0, w, 0)
    mask = (col >= row) & (col <= row + w) & (col >= lo)
    scale = SWA_DH ** -0.5
    for j in range(SWA_KV_HEADS):
        ks = slice(j * SWA_DH, (j + 1) * SWA_DH)
        kk = jnp.concatenate([kp_ref[:, ks], kc_ref[:, ks]], axis=0).astype(BF16)
        vv = jnp.concatenate([vp_ref[:, ks], vc_ref[:, ks]], axis=0).astype(BF16)
        for g in range(SWA_GROUP):
            hq = j * SWA_GROUP + g
            qs = slice(hq * SWA_DH, (hq + 1) * SWA_DH)
            q = q_ref[:, qs].astype(BF16)
            s = lax.dot_general(q, kk, (((1,), (1,)), ((), ())),
                                preferred_element_type=F32) * scale
            s = jnp.where(mask, s, -jnp.inf)
            sink = sink_ref[0:1, hq:hq + 1]
            mx = jnp.maximum(jnp.max(s, axis=-1, keepdims=True), sink)
            p = jnp.exp(s - mx)
            den = jnp.sum(p, axis=-1, keepdims=True) + jnp.exp(sink - mx)
            o = jnp.dot(p.astype(BF16), vv, preferred_element_type=F32)
            o_ref[:, qs] = o / den


def _swa_prompt(swa, sinks):
    B, T, _ = swa.shape
    w = WINDOW
    kcol = SWA_Q // SWA_KV
    prev = lambda n: jnp.maximum(n - 1, 0)
    return pl.pallas_call(
        _swa_prompt_kernel,
        grid=(B, T // w),
        in_specs=[pl.BlockSpec((None, w, SWA_Q), lambda b, n: (b, n, 0)),
                  pl.BlockSpec((None, w, SWA_KV), lambda b, n: (b, n, kcol)),
                  pl.BlockSpec((None, w, SWA_KV), lambda b, n: (b, n, kcol + 1)),
                  pl.BlockSpec((None, w, SWA_KV), lambda b, n: (b, prev(n), kcol)),
                  pl.BlockSpec((None, w, SWA_KV), lambda b, n: (b, prev(n), kcol + 1)),
                  _const_spec((1, LANES))],
        out_specs=pl.BlockSpec((None, w, SWA_Q), lambda b, n: (b, n, 0)),
        out_shape=jax.ShapeDtypeStruct((B, T, SWA_Q), F32),
        compiler_params=_cparams(("parallel", "parallel")),
        name="swa_prompt",
    )(swa, swa, swa, swa, swa, sinks)


def _swa_sample_kernel(swa_ref, kc_ref, vc_ref, sink_ref, o_ref, kn_ref, vn_ref):
    bb = swa_ref.shape[0]
    w = kc_ref.shape[1]
    scale = SWA_DH ** -0.5
    for i in range(bb):
        knew = swa_ref[i:i + 1, SWA_Q:SWA_Q + SWA_KV]
        vnew = swa_ref[i:i + 1, SWA_Q + SWA_KV:]
        kc = kc_ref[i]
        vc = vc_ref[i]
        kn_ref[i, 0:w - 1, :] = kc_ref[i, 1:w, :]
        kn_ref[i, w - 1:w, :] = knew
        vn_ref[i, 0:w - 1, :] = vc_ref[i, 1:w, :]
        vn_ref[i, w - 1:w, :] = vnew
        for j in range(SWA_KV_HEADS):
            ks = slice(j * SWA_DH, (j + 1) * SWA_DH)
            q4 = jnp.concatenate(
                [swa_ref[i:i + 1, (j * SWA_GROUP + g) * SWA_DH:(j * SWA_GROUP + g + 1) * SWA_DH]
                 for g in range(SWA_GROUP)] + [jnp.zeros((SUBLANES - SWA_GROUP, SWA_DH), F32)],
                axis=0)
            sink = jnp.concatenate(
                [sink_ref[0:1, j * SWA_GROUP + g:j * SWA_GROUP + g + 1] for g in range(SWA_GROUP)]
                + [jnp.zeros((SUBLANES - SWA_GROUP, 1), F32)], axis=0)
            sc = lax.dot_general(q4, kc[:, ks], (((1,), (1,)), ((), ())),
                                 preferred_element_type=F32) * scale
            sn = jnp.sum(q4 * knew[:, ks], axis=-1, keepdims=True) * scale
            mx = jnp.maximum(jnp.maximum(jnp.max(sc, axis=-1, keepdims=True), sn), sink)
            pc = jnp.exp(sc - mx)
            pn = jnp.exp(sn - mx)
            den = jnp.sum(pc, axis=-1, keepdims=True) + pn + jnp.exp(sink - mx)
            o = (jnp.dot(pc, vc[:, ks], preferred_element_type=F32) + pn * vnew[:, ks]) / den
            for g in range(SWA_GROUP):
                hq = j * SWA_GROUP + g
                o_ref[i:i + 1, hq * SWA_DH:(hq + 1) * SWA_DH] = o[g:g + 1, :]


def _swa_sample(swa, kc, vc, sinks, bb):
    R = swa.shape[0]
    w = kc.shape[1]
    cache_spec = pl.BlockSpec((bb, w, SWA_KV), lambda i: (i, 0, 0))
    return pl.pallas_call(
        _swa_sample_kernel,
        grid=(R // bb,),
        in_specs=[pl.BlockSpec((bb, SWA_OUT), lambda i: (i, 0)), cache_spec, cache_spec,
                  _const_spec((1, LANES))],
        out_specs=[pl.BlockSpec((bb, SWA_Q), lambda i: (i, 0)), cache_spec, cache_spec],
        out_shape=[jax.ShapeDtypeStruct((R, SWA_Q), F32),
                   jax.ShapeDtypeStruct(kc.shape, F32),
                   jax.ShapeDtypeStruct(vc.shape, F32)],
        compiler_params=_cparams(("parallel",)),
        name="swa_sample",
    )(swa, kc, vc, sinks)


def _outproj_ffn_kernel(x_ref, dn_ref, sw_ref, g5_ref, sh_ref, sc_ref, ga_ref, g_ref,
                        woa_ref, wob_ref, wgu_ref, wd_ref, o_ref):
    mix = (jnp.dot(dn_ref[...].astype(BF16), woa_ref[...], preferred_element_type=F32)
           + jnp.dot(sw_ref[...].astype(BF16), wob_ref[...], preferred_element_type=F32))
    x = x_ref[...] + g5_ref[...] * mix
    o_ref[...] = _ffn_body(x, sh_ref[...], sc_ref[...], ga_ref[...], g_ref[...], wgu_ref, wd_ref)


def _outproj_ffn(x, dn, sw, m, g, woa, wob, wgu, wd, tm):
    G, R, _ = x.shape
    g5 = _mod_specs(m, 1, tm)[2]
    return pl.pallas_call(
        _outproj_ffn_kernel,
        grid=(G, R // tm),
        in_specs=[_row_spec(tm, D_MODEL), _row_spec(tm, DN_V), _row_spec(tm, SWA_Q), g5]
        + _mod_specs(m, 2, tm) + [
            _const_spec((1, D_MODEL)),
            _const_spec((DN_V, D_MODEL)),
            _const_spec((SWA_Q, D_MODEL)),
            _const_spec((D_MODEL, 2 * D_FF)),
            _const_spec((D_FF, D_MODEL)),
        ],
        out_specs=_row_spec(tm, D_MODEL),
        out_shape=jax.ShapeDtypeStruct(x.shape, F32),
        compiler_params=_cparams(("parallel", "parallel")),
        name="outproj_ffn",
    )(x, dn, sw, m, m, m, m, g, woa, wob, wgu, wd)


def _rope_tables(pos):
    half = SWA_DH // 2
    inv = jnp.power(ROPE_THETA, -jnp.arange(half, dtype=F32) * 2.0 / SWA_DH)
    ang = pos.astype(F32)[:, None] * inv[None, :]
    cos, sin = jnp.cos(ang), jnp.sin(ang)
    reps = LANES // SWA_DH
    return (jnp.concatenate([cos, cos] * reps, axis=1),
            jnp.concatenate([-sin, sin] * reps, axis=1))


def _group_matrix(width):
    i = jnp.arange(width) // SWA_DH
    return jnp.where(i[:, None] == i[None, :], 1.0 / SWA_DH, 0.0).astype(BF16)


def _pack_w_in(w_in):
    conv = w_in[:, :DN_CONV_CH]
    o = DN_CONV_CH
    z = w_in[:, o:o + DN_V]
    o += DN_V
    ba = w_in[:, o:o + 2 * DN_HEADS]
    o += 2 * DN_HEADS
    sq = w_in[:, o:o + SWA_Q]
    o += SWA_Q
    sk = w_in[:, o:o + SWA_KV]
    o += SWA_KV
    sv = w_in[:, o:o + SWA_KV]
    pad = jnp.zeros((w_in.shape[0], LANES - 2 * DN_HEADS), w_in.dtype)
    return jnp.concatenate([conv, z, sq, sk, sv, ba, pad], axis=1).astype(BF16)


def _lane_row(vec, offset=0):
    return jnp.zeros((LANES,), F32).at[offset:offset + vec.shape[0]].set(vec.astype(F32))


def kernel(x_prompt, x_sample, c_prompt, c_sample, state_dn_conv, state_dn_S, cache_swa_k, cache_swa_v, w_ada, b_ada, g_ffn1, w_ffn1_gu, w_ffn1_down, g_mix, w_in, dn_conv_w, dn_A_log, dn_dt_bias, dn_norm_w, swa_q_norm, swa_k_norm, swa_sinks, w_out, g_ffn2, w_ffn2_gu, w_ffn2_down):
    L = w_ada.shape[0]
    B, T, _ = x_prompt.shape
    Bs = x_sample.shape[0]
    tm = min(512, T)
    bb = min(8, Bs)

    m_p, m_s = _adaln(c_prompt, c_sample, w_ada, b_ada)
    m_p = m_p.reshape(L, N_MOD, B, 1, D_MODEL)
    m_s = m_s.reshape(L, N_MOD, 1, Bs, D_MODEL)

    cos_p, sin_p = _rope_tables(jnp.arange(T))
    cos_s, sin_s = _rope_tables(PAST_LEN + jnp.arange(1))
    gq, gk = _group_matrix(SWA_Q), _group_matrix(SWA_KV)

    yp = x_prompt
    ys = x_sample.reshape(1, Bs, D_MODEL)
    outs = [[] for _ in range(8)]
    for l in range(L):
        wgu1, wd1 = w_ffn1_gu[l].astype(BF16), w_ffn1_down[l].astype(BF16)
        wgu2, wd2 = w_ffn2_gu[l].astype(BF16), w_ffn2_down[l].astype(BF16)
        win = _pack_w_in(w_in[l])
        woa, wob = w_out[l, :DN_V].astype(BF16), w_out[l, DN_V:].astype(BF16)
        g1, gm, g2 = (g_ffn1[l].reshape(1, D_MODEL), g_mix[l].reshape(1, D_MODEL),
                      g_ffn2[l].reshape(1, D_MODEL))
        cw = dn_conv_w[l]
        nw = dn_norm_w[l].reshape(1, DN_DV)
        prm = jnp.stack([
            _lane_row(-jnp.exp(dn_A_log[l].astype(F32)), DN_HEADS),
            _lane_row(dn_dt_bias[l], DN_HEADS),
            jnp.concatenate([swa_q_norm[l]] * (LANES // SWA_DH)).astype(F32),
            jnp.concatenate([swa_k_norm[l]] * (LANES // SWA_DH)).astype(F32),
        ] + [jnp.zeros((LANES,), F32)] * (SUBLANES - 4))
        sinks = _lane_row(swa_sinks[l]).reshape(1, LANES)

        yp = _ffn(yp, m_p[l], 0, g1, wgu1, wd1, tm)
        qkv, z, bg, swa, tail = _inproj_prompt(yp, m_p[l], gm, win, cw, prm, cos_p, sin_p, gq, gk, tm)
        dn_o, s_new = _delta_prompt(qkv, bg, z, nw)
        sw_o = _swa_prompt(swa, sinks)
        yp = _outproj_ffn(yp, dn_o, sw_o, m_p[l], g2, woa, wob, wgu2, wd2, tm)
        nkeep = min(WINDOW, T)
        outs[0].append(tail[:, HALO - (DN_CONV - 1):])
        outs[1].append(s_new)
        outs[2].append(swa[:, T - nkeep:, SWA_Q:SWA_Q + SWA_KV].reshape(B, nkeep, SWA_KV_HEADS, SWA_DH))
        outs[3].append(swa[:, T - nkeep:, SWA_Q + SWA_KV:].reshape(B, nkeep, SWA_KV_HEADS, SWA_DH))

        ys = _ffn(ys, m_s[l], 0, g1, wgu1, wd1, Bs)
        buf = jnp.transpose(state_dn_conv[l], (1, 0, 2))
        qkv, z, bg, swa, cnew = _inproj_sample(ys, m_s[l], gm, win, cw, prm, cos_s, sin_s, gq, gk, buf)
        dn_o, s_new = _delta_sample(qkv[0], bg[0], z[0], nw, state_dn_S[l], bb)
        w0 = cache_swa_k.shape[2]
        sw_o, k_new, v_new = _swa_sample(swa[0], cache_swa_k[l].reshape(Bs, w0, SWA_KV),
                                         cache_swa_v[l].reshape(Bs, w0, SWA_KV), sinks, bb)
        ys = _outproj_ffn(ys, dn_o[None], sw_o[None], m_s[l], g2, woa, wob, wgu2, wd2, Bs)
        outs[4].append(jnp.transpose(cnew, (1, 0, 2)))
        outs[5].append(s_new)
        outs[6].append(k_new.reshape(Bs, w0, SWA_KV_HEADS, SWA_DH))
        outs[7].append(v_new.reshape(Bs, w0, SWA_KV_HEADS, SWA_DH))

    return (yp, ys.reshape(Bs, 1, D_MODEL)) + tuple(jnp.stack(o) for o in outs)
```

```python
import functools
import math

import jax
import jax.numpy as jnp
from jax import lax
from jax.experimental import pallas as pl
from jax.experimental.pallas import tpu as pltpu

F32 = jnp.float32
BF16 = jnp.bfloat16

D_MODEL = 1024
D_FF = 2816
N_MOD = 9
EPS = 1e-6
DN_HEADS = 4
DN_DK = 128
DN_DV = 128
DN_QK = DN_HEADS * DN_DK
DN_V = DN_HEADS * DN_DV
DN_CONV = 4
DN_CONV_CH = 2 * DN_QK + DN_V
DN_CHUNK = 64
INV_BLOCK = 16
SWA_DH = 64
SWA_HEADS = 8
SWA_KV_HEADS = 2
SWA_GROUP = SWA_HEADS // SWA_KV_HEADS
SWA_Q = SWA_HEADS * SWA_DH
SWA_KV = SWA_KV_HEADS * SWA_DH
WINDOW = 128
ROPE_THETA = 10000.0
PAST_LEN = 16384

LANES = 128
SUBLANES = 8
HALO = SUBLANES

C_CONV = 0
C_Z = C_CONV + DN_CONV_CH
C_SQ = C_Z + DN_V
C_SK = C_SQ + SWA_Q
C_SV = C_SK + SWA_KV
C_BG = C_SV + SWA_KV
IN_PACKED = C_BG + LANES
SWA_OUT = SWA_Q + 2 * SWA_KV

VMEM_LIMIT = 56 * 1024 * 1024


def _cparams(sem):
    return pltpu.CompilerParams(dimension_semantics=sem, vmem_limit_bytes=VMEM_LIMIT)


def _const_spec(shape):
    nd = len(shape)
    return pl.BlockSpec(shape, lambda *_: (0,) * nd, pipeline_mode=pl.Buffered(1))


def _sigmoid(x):
    return 1.0 / (1.0 + jnp.exp(-x))


def _silu(x):
    return x * _sigmoid(x)


def _adaln_kernel(cp_ref, cs_ref, w_ref, b_ref, mp_ref, ms_ref):
    w = w_ref[...].astype(BF16)
    b = b_ref[...]
    for c_ref, m_ref in ((cp_ref, mp_ref), (cs_ref, ms_ref)):
        c = c_ref[...]
        m_ref[...] = jnp.dot(_silu(c).astype(BF16), w, preferred_element_type=F32) + b


def _adaln(c_p, c_s, w_ada, b_ada):
    L = w_ada.shape[0]
    bp, bs = c_p.shape[0], c_s.shape[0]
    b4 = b_ada.reshape(L, N_MOD, 1, D_MODEL)
    return pl.pallas_call(
        _adaln_kernel,
        grid=(L, N_MOD),
        in_specs=[
            pl.BlockSpec((bp, D_MODEL), lambda l, j: (0, 0)),
            pl.BlockSpec((bs, D_MODEL), lambda l, j: (0, 0)),
            pl.BlockSpec((None, D_MODEL, D_MODEL), lambda l, j: (l, 0, j)),
            pl.BlockSpec((None, None, 1, D_MODEL), lambda l, j: (l, j, 0, 0)),
        ],
        out_specs=[
            pl.BlockSpec((None, None, bp, D_MODEL), lambda l, j: (l, j, 0, 0)),
            pl.BlockSpec((None, None, bs, D_MODEL), lambda l, j: (l, j, 0, 0)),
        ],
        out_shape=[
            jax.ShapeDtypeStruct((L, N_MOD, bp, D_MODEL), F32),
            jax.ShapeDtypeStruct((L, N_MOD, bs, D_MODEL), F32),
        ],
        compiler_params=_cparams(("parallel", "parallel")),
        name="adaln",
    )(c_p, c_s, w_ada, b4)


def _mod_norm(x, g, shift, scale):
    ms = jnp.mean(x * x, axis=-1, keepdims=True)
    y = x * lax.rsqrt(ms + EPS) * g
    return y * (1.0 + scale) + shift


def _ffn_body(x, shift, scale, gate, g, wgu_ref, wd_ref):
    h = _mod_norm(x, g, shift, scale).astype(BF16)
    gu = jnp.dot(h, wgu_ref[...], preferred_element_type=F32)
    a = gu[:, :D_FF]
    b = gu[:, D_FF:]
    act = (_silu(a) * b).astype(BF16)
    y = jnp.dot(act, wd_ref[...], preferred_element_type=F32)
    return x + (0.5 * gate) * y


def _mod_specs(m, sub, tm):
    rm = m.shape[2]
    blk = 1 if rm == 1 else tm

    def spec(i):
        if rm == 1:
            return pl.BlockSpec((None, None, 1, D_MODEL), lambda g, r: (i, g, 0, 0))
        return pl.BlockSpec((None, None, blk, D_MODEL), lambda g, r: (i, g, r, 0))

    return [spec(3 * sub), spec(3 * sub + 1), spec(3 * sub + 2)]


def _row_spec(tm, width):
    return pl.BlockSpec((None, tm, width), lambda g, r: (g, r, 0))


def _ffn_kernel(x_ref, sh_ref, sc_ref, ga_ref, g_ref, wgu_ref, wd_ref, o_ref):
    o_ref[...] = _ffn_body(x_ref[...], sh_ref[...], sc_ref[...], ga_ref[...], g_ref[...],
                           wgu_ref, wd_ref)


def _ffn(x, m, sub, g, wgu, wd, tm):
    G, R, _ = x.shape
    return pl.pallas_call(
        _ffn_kernel,
        grid=(G, R // tm),
        in_specs=[_row_spec(tm, D_MODEL)] + _mod_specs(m, sub, tm) + [
            _const_spec((1, D_MODEL)),
            _const_spec((D_MODEL, 2 * D_FF)),
            _const_spec((D_FF, D_MODEL)),
        ],
        out_specs=_row_spec(tm, D_MODEL),
        out_shape=jax.ShapeDtypeStruct(x.shape, F32),
        compiler_params=_cparams(("parallel", "parallel")),
        name="ffn",
    )(x, m, m, m, g, wgu, wd)


def _group_mean_sq(x, gmat_ref):
    sq = x * x
    hi = sq.astype(BF16)
    lo = (sq - hi.astype(F32)).astype(BF16)
    gm = gmat_ref[...]
    return (jnp.dot(hi, gm, preferred_element_type=F32)
            + jnp.dot(lo, gm, preferred_element_type=F32))


def _rope(x, cos, sin_signed):
    width = x.shape[-1]
    reps = width // LANES
    lane = lax.broadcasted_iota(jnp.int32, x.shape, 1)
    first_half = (lane & (SWA_DH - 1)) < (SWA_DH // 2)
    partner = jnp.where(first_half,
                        pltpu.roll(x, width - SWA_DH // 2, 1),
                        pltpu.roll(x, SWA_DH // 2, 1))
    if reps > 1:
        cos = jnp.concatenate([cos] * reps, axis=1)
        sin_signed = jnp.concatenate([sin_signed] * reps, axis=1)
    return x * cos + partner * sin_signed


def _mix_post(p, qkv_conv, prm_ref, cos, sin, gq_ref, gk_ref, qkv_ref, z_ref, bg_ref, swa_ref):
    qkv = _silu(qkv_conv)
    for hh in range(2 * DN_HEADS):
        xh = qkv[:, hh * DN_DK:(hh + 1) * DN_DK]
        s = lax.rsqrt(jnp.sum(xh * xh, axis=-1, keepdims=True) + EPS)
        if hh < DN_HEADS:
            s = s * (DN_DK ** -0.5)
        qkv_ref[:, hh * DN_DK:(hh + 1) * DN_DK] = xh * s
    qkv_ref[:, 2 * DN_QK:] = qkv[:, 2 * DN_QK:]
    z_ref[...] = p[:, C_Z:C_Z + DN_V]

    pb = p[:, C_BG:C_BG + LANES]
    neg_a = prm_ref[0:1, :]
    dt_b = prm_ref[1:2, :]
    xa = pb + dt_b
    softplus = jnp.maximum(xa, 0.0) + jnp.log(1.0 + jnp.exp(-jnp.abs(xa)))
    lane = lax.broadcasted_iota(jnp.int32, pb.shape, 1)
    bg_ref[...] = jnp.where(lane < DN_HEADS, _sigmoid(pb), neg_a * softplus)

    pq = p[:, C_SQ:C_SQ + SWA_Q]
    pk = p[:, C_SK:C_SK + SWA_KV]
    wq = jnp.concatenate([prm_ref[2:3, :]] * (SWA_Q // LANES), axis=1)
    wk = prm_ref[3:4, :]
    qn = pq * lax.rsqrt(_group_mean_sq(pq, gq_ref) + EPS) * wq
    kn = pk * lax.rsqrt(_group_mean_sq(pk, gk_ref) + EPS) * wk
    swa_ref[:, :SWA_Q] = _rope(qn, cos, sin)
    swa_ref[:, SWA_Q:SWA_Q + SWA_KV] = _rope(kn, cos, sin)
    swa_ref[:, SWA_Q + SWA_KV:] = p[:, C_SV:C_SV + SWA_KV]


def _inproj_prompt_kernel(x_ref, sh_ref, sc_ref, g_ref, win_ref, cw_ref, prm_ref, cos_ref, sin_ref,
                          gq_ref, gk_ref, qkv_ref, z_ref, bg_ref, swa_ref, tail_ref, ext_ref):
    tm = x_ref.shape[0]
    h = _mod_norm(x_ref[...], g_ref[...], sh_ref[...], sc_ref[...]).astype(BF16)
    p = jnp.dot(h, win_ref[...], preferred_element_type=F32)
    pc = p[:, :DN_CONV_CH]

    @pl.when(pl.program_id(1) == 0)
    def _():
        ext_ref[0:HALO, :] = jnp.zeros((HALO, DN_CONV_CH), F32)

    ext_ref[HALO:, :] = pc
    y = pc * cw_ref[DN_CONV - 1:DN_CONV, :]
    for j in range(DN_CONV - 1):
        off = HALO - (DN_CONV - 1) + j
        y = y + ext_ref[off:off + tm, :] * cw_ref[j:j + 1, :]
    ext_ref[0:HALO, :] = pc[tm - HALO:, :]
    tail_ref[...] = pc[tm - HALO:, :]
    _mix_post(p, y, prm_ref, cos_ref[...], sin_ref[...], gq_ref, gk_ref,
              qkv_ref, z_ref, bg_ref, swa_ref)


def _inproj_sample_kernel(x_ref, sh_ref, sc_ref, g_ref, win_ref, cw_ref, prm_ref, cos_ref, sin_ref,
                          gq_ref, gk_ref, buf_ref, qkv_ref, z_ref, bg_ref, swa_ref, cnew_ref):
    h = _mod_norm(x_ref[...], g_ref[...], sh_ref[...], sc_ref[...]).astype(BF16)
    p = jnp.dot(h, win_ref[...], preferred_element_type=F32)
    pc = p[:, :DN_CONV_CH]
    y = pc * cw_ref[DN_CONV - 1:DN_CONV, :]
    for j in range(DN_CONV - 1):
        y = y + buf_ref[j] * cw_ref[j:j + 1, :]
    for j in range(DN_CONV - 2):
        cnew_ref[j] = buf_ref[j + 1]
    cnew_ref[DN_CONV - 2] = pc
    _mix_post(p, y, prm_ref, cos_ref[...], sin_ref[...], gq_ref, gk_ref,
              qkv_ref, z_ref, bg_ref, swa_ref)


def _inproj_common_specs(m, tm):
    sh, sc, _ = _mod_specs(m, 1, tm)
    return [_row_spec(tm, D_MODEL), sh, sc,
            _const_spec((1, D_MODEL)),
            _const_spec((D_MODEL, IN_PACKED)),
            _const_spec((DN_CONV, DN_CONV_CH)),
            _const_spec((SUBLANES, LANES))]


def _inproj_prompt(x, m, g, win, cw, prm, cos, sin, gq, gk, tm):
    B, T, _ = x.shape
    outs = pl.pallas_call(
        _inproj_prompt_kernel,
        grid=(B, T // tm),
        in_specs=_inproj_common_specs(m, tm) + [
            pl.BlockSpec((tm, LANES), lambda b, r: (r, 0)),
            pl.BlockSpec((tm, LANES), lambda b, r: (r, 0)),
            _const_spec((SWA_Q, SWA_Q)),
            _const_spec((SWA_KV, SWA_KV)),
        ],
        out_specs=[
            _row_spec(tm, DN_CONV_CH), _row_spec(tm, DN_V), _row_spec(tm, LANES),
            _row_spec(tm, SWA_OUT),
            pl.BlockSpec((None, HALO, DN_CONV_CH), lambda b, r: (b, 0, 0)),
        ],
        out_shape=[
            jax.ShapeDtypeStruct((B, T, DN_CONV_CH), F32),
            jax.ShapeDtypeStruct((B, T, DN_V), F32),
            jax.ShapeDtypeStruct((B, T, LANES), F32),
            jax.ShapeDtypeStruct((B, T, SWA_OUT), F32),
            jax.ShapeDtypeStruct((B, HALO, DN_CONV_CH), F32),
        ],
        scratch_shapes=[pltpu.VMEM((tm + HALO, DN_CONV_CH), F32)],
        compiler_params=_cparams(("parallel", "arbitrary")),
        name="inproj_prompt",
    )(x, m, m, g, win, cw, prm, cos, sin, gq, gk)
    return outs


def _inproj_sample(x, m, g, win, cw, prm, cos, sin, gq, gk, buf):
    _, R, _ = x.shape
    nbuf = DN_CONV - 1
    return pl.pallas_call(
        _inproj_sample_kernel,
        grid=(1, 1),
        in_specs=_inproj_common_specs(m, R) + [
            _const_spec((1, LANES)), _const_spec((1, LANES)),
            _const_spec((SWA_Q, SWA_Q)), _const_spec((SWA_KV, SWA_KV)),
            _const_spec((nbuf, R, DN_CONV_CH)),
        ],
        out_specs=[
            _row_spec(R, DN_CONV_CH), _row_spec(R, DN_V), _row_spec(R, LANES), _row_spec(R, SWA_OUT),
            pl.BlockSpec((nbuf, R, DN_CONV_CH), lambda g_, r: (0, 0, 0)),
        ],
        out_shape=[
            jax.ShapeDtypeStruct((1, R, DN_CONV_CH), F32),
            jax.ShapeDtypeStruct((1, R, DN_V), F32),
            jax.ShapeDtypeStruct((1, R, LANES), F32),
            jax.ShapeDtypeStruct((1, R, SWA_OUT), F32),
            jax.ShapeDtypeStruct((nbuf, R, DN_CONV_CH), F32),
        ],
        compiler_params=_cparams(("arbitrary", "arbitrary")),
        name="inproj_sample",
    )(x, m, m, g, win, cw, prm, cos, sin, gq, gk, buf)


def _gated_norm(o, z, nw):
    on = o * lax.rsqrt(jnp.mean(o * o, axis=-1, keepdims=True) + EPS) * nw
    return on * _silu(z)


def _unit_lower_inverse(a, c):
    n = a.shape[0]
    row = lax.broadcasted_iota(jnp.int32, (n, n), 0)
    col = lax.broadcasted_iota(jnp.int32, (n, n), 1)
    s = INV_BLOCK
    a0 = jnp.where((row // s) == (col // s), a, 0.0)
    x = jnp.where(row == col, 1.0, 0.0) - a0
    pw = a0
    k = 2
    while k < s:
        pw_b = pw.astype(BF16)
        pw = jnp.dot(pw_b, pw_b, preferred_element_type=F32)
        x = x + jnp.dot(x.astype(BF16), pw.astype(BF16), preferred_element_type=F32)
        k *= 2
    while s < c:
        pair = ((row // (2 * s)) == (col // (2 * s))) & ((row // s) != (col // s))
        off = jnp.where(pair, a, 0.0)
        t = jnp.dot(x.astype(BF16), off.astype(BF16), preferred_element_type=F32)
        x = x - jnp.dot(t.astype(BF16), x.astype(BF16), preferred_element_type=F32)
        s *= 2
    return x


def _cumsum_rows(x):
    n = x.shape[0]
    row = lax.broadcasted_iota(jnp.int32, x.shape, 0)
    s = 1
    while s < n:
        x = x + jnp.where(row >= s, pltpu.roll(x, s, 0), 0.0)
        s *= 2
    return x


WQ_ROWS = 2 * DN_CHUNK
QKD_ROWS = DN_CHUNK + DN_DK


def _delta_intra_kernel(q_ref, k_ref, v_ref, bg_ref, u_ref, wq_ref, qkd_ref, eg_ref):
    c = DN_CHUNK
    r = q_ref.shape[0]
    nc = r // c
    bg = bg_ref[...]
    gc = jnp.concatenate([_cumsum_rows(bg[ci * c:(ci + 1) * c, :]) for ci in range(nc)], axis=0)
    g_end = [gc[(ci + 1) * c - 1:(ci + 1) * c, :] for ci in range(nc)]
    glast = jnp.concatenate([jnp.broadcast_to(g, (c, LANES)) for g in g_end], axis=0)
    gct = gc.T
    egc = jnp.exp(gc)
    ekd = jnp.exp(glast - gc)
    row = lax.broadcasted_iota(jnp.int32, (r, r), 0)
    col = lax.broadcasted_iota(jnp.int32, (r, r), 1)
    same = (row // c) == (col // c)
    incl = same & (row >= col)
    strict = same & (row > col)
    for ci in range(nc):
        rows = [jnp.broadcast_to(jnp.exp(g_end[ci][:, DN_HEADS + h:DN_HEADS + h + 1]), (1, LANES))
                for h in range(DN_HEADS)]
        eg_ref[ci] = jnp.concatenate(rows + [jnp.zeros((SUBLANES - DN_HEADS, LANES), F32)], axis=0)
    for h in range(DN_HEADS):
        sl = slice(h * DN_DK, (h + 1) * DN_DK)
        gl = slice(DN_HEADS + h, DN_HEADS + h + 1)
        q = q_ref[:, sl]
        k = k_ref[:, sl]
        v = v_ref[:, sl]
        beta = bg[:, h:h + 1]
        decay = jnp.exp(jnp.where(incl, gc[:, gl] - gct[gl, :], -jnp.inf))
        kb = k * beta
        kq = lax.dot_general(jnp.concatenate([kb, q], axis=0).astype(BF16), k.astype(BF16),
                             (((1,), (1,)), ((), ())), preferred_element_type=F32)
        a = jnp.where(strict, kq[:r] * decay, 0.0)
        qk = kq[r:] * decay
        tinv = _unit_lower_inverse(a, c)
        rhs = jnp.concatenate([v * beta, kb * egc[:, gl]], axis=1).astype(BF16)
        uw = jnp.dot(tinv.astype(BF16), rhs, preferred_element_type=F32)
        u_ref[:, sl] = uw[:, :DN_DV]
        w_b = uw[:, DN_DV:].astype(BF16)
        qd_b = (q * egc[:, gl]).astype(BF16)
        for ci in range(nc):
            wq_ref[ci, h * WQ_ROWS:h * WQ_ROWS + c, :] = w_b[ci * c:(ci + 1) * c, :]
            wq_ref[ci, h * WQ_ROWS + c:(h + 1) * WQ_ROWS, :] = qd_b[ci * c:(ci + 1) * c, :]
        kdt = (k * ekd[:, gl]).T
        for pj in range(nc // 2):
            ls = slice(pj * LANES, (pj + 1) * LANES)
            qk2 = qk[2 * pj * c:(2 * pj + 1) * c, ls] + qk[(2 * pj + 1) * c:(2 * pj + 2) * c, ls]
            qkd_ref[pj, h * QKD_ROWS:h * QKD_ROWS + c, :] = qk2.astype(BF16)
            qkd_ref[pj, h * QKD_ROWS + c:(h + 1) * QKD_ROWS, :] = kdt[:, ls].astype(BF16)


def _delta_scan_kernel(u_ref, wq_ref, qkd_ref, eg_ref, z_ref, nw_ref, o_ref, sout_ref, s_ref):
    n = pl.program_id(0)
    c = DN_CHUNK

    @pl.when(n == 0)
    def _():
        s_ref[...] = jnp.zeros_like(s_ref)

    nw = nw_ref[...]
    zeros = jnp.zeros((c, DN_DV), BF16)
    for b in range(u_ref.shape[0]):
        for h in range(DN_HEADS):
            sl = slice(h * DN_DV, (h + 1) * DN_DV)
            s = s_ref[b, h]
            for j in range(2):
                rs = slice(j * c, (j + 1) * c)
                ws = jnp.dot(wq_ref[b, j, h * WQ_ROWS:(h + 1) * WQ_ROWS, :], s.astype(BF16),
                             preferred_element_type=F32)
                v_new = (u_ref[b, rs, sl] - ws[:c]).astype(BF16)
                rhs = jnp.concatenate([v_new, zeros] if j == 0 else [zeros, v_new], axis=0)
                rr = jnp.dot(qkd_ref[b, h * QKD_ROWS:(h + 1) * QKD_ROWS, :], rhs,
                             preferred_element_type=F32)
                o = ws[c:] + rr[:c]
                s = s * eg_ref[b, j, h:h + 1, :] + rr[c:]
                o_ref[b, rs, sl] = _gated_norm(o, z_ref[b, rs, sl], nw)
            s_ref[b, h] = s

    @pl.when(n == pl.num_programs(0) - 1)
    def _():
        sout_ref[...] = s_ref[...]


def _delta_prompt(qkv, bg, z, nw, ra):
    B, T, _ = qkv.shape
    c = DN_CHUNK
    n = T // c
    cpb = ra // c
    qkv_spec = lambda j: pl.BlockSpec((None, ra, DN_QK), lambda b, r: (b, r, j))
    u, wq, qkd, eg = pl.pallas_call(
        _delta_intra_kernel,
        grid=(B, T // ra),
        in_specs=[qkv_spec(0), qkv_spec(1), qkv_spec(2),
                  pl.BlockSpec((None, ra, LANES), lambda b, r: (b, r, 0))],
        out_specs=[pl.BlockSpec((None, ra, DN_V), lambda b, r: (b, r, 0)),
                   pl.BlockSpec((None, cpb, DN_HEADS * WQ_ROWS, DN_DK), lambda b, r: (b, r, 0, 0)),
                   pl.BlockSpec((None, cpb // 2, DN_HEADS * QKD_ROWS, LANES),
                                lambda b, r: (b, r, 0, 0)),
                   pl.BlockSpec((None, cpb, SUBLANES, LANES), lambda b, r: (b, r, 0, 0))],
        out_shape=[jax.ShapeDtypeStruct((B, T, DN_V), F32),
                   jax.ShapeDtypeStruct((B, n, DN_HEADS * WQ_ROWS, DN_DK), BF16),
                   jax.ShapeDtypeStruct((B, n // 2, DN_HEADS * QKD_ROWS, LANES), BF16),
                   jax.ShapeDtypeStruct((B, n, SUBLANES, LANES), F32)],
        compiler_params=_cparams(("parallel", "parallel")),
        name="delta_intra",
    )(qkv, qkv, qkv, bg)
    return pl.pallas_call(
        _delta_scan_kernel,
        grid=(n // 2,),
        in_specs=[pl.BlockSpec((B, 2 * c, DN_V), lambda i: (0, i, 0)),
                  pl.BlockSpec((B, 2, DN_HEADS * WQ_ROWS, DN_DK), lambda i: (0, i, 0, 0)),
                  pl.BlockSpec((B, None, DN_HEADS * QKD_ROWS, LANES), lambda i: (0, i, 0, 0)),
                  pl.BlockSpec((B, 2, SUBLANES, LANES), lambda i: (0, i, 0, 0)),
                  pl.BlockSpec((B, 2 * c, DN_V), lambda i: (0, i, 0)),
                  _const_spec((1, DN_DV))],
        out_specs=[pl.BlockSpec((B, 2 * c, DN_V), lambda i: (0, i, 0)),
                   pl.BlockSpec((B, DN_HEADS, DN_DK, DN_DV), lambda i: (0, 0, 0, 0))],
        out_shape=[jax.ShapeDtypeStruct((B, T, DN_V), F32),
                   jax.ShapeDtypeStruct((B, DN_HEADS, DN_DK, DN_DV), F32)],
        scratch_shapes=[pltpu.VMEM((B, DN_HEADS, DN_DK, DN_DV), F32)],
        compiler_params=_cparams(("arbitrary",)),
        name="delta_scan",
    )(u, wq, qkd, eg, z, nw)


def _delta_sample_kernel(qkv_ref, bg_ref, z_ref, nw_ref, s0_ref, o_ref, s_ref):
    bb = qkv_ref.shape[0]
    nw = nw_ref[...]
    zeros = jnp.zeros((SUBLANES - 2, DN_DK), F32)
    for i in range(bb):
        for h in range(DN_HEADS):
            sl = slice(h * DN_DK, (h + 1) * DN_DK)
            q = qkv_ref[i:i + 1, sl]
            k = qkv_ref[i:i + 1, DN_QK + h * DN_DK:DN_QK + (h + 1) * DN_DK]
            v = qkv_ref[i:i + 1, 2 * DN_QK + h * DN_DV:2 * DN_QK + (h + 1) * DN_DV]
            beta = bg_ref[i:i + 1, h:h + 1]
            eg = jnp.exp(bg_ref[i:i + 1, DN_HEADS + h:DN_HEADS + h + 1])
            s_old = s0_ref[i, h]
            kb = k * beta
            lhs = jnp.concatenate([kb * eg, q * eg, zeros], axis=0)
            ws = jnp.dot(lhs, s_old, preferred_element_type=F32)
            v_new = v * beta - ws[0:1]
            qk = jnp.sum(q * k, axis=-1, keepdims=True)
            o = ws[1:2] + qk * v_new
            kcol = jnp.concatenate([k, zeros, jnp.zeros((1, DN_DK), F32)], axis=0).T
            vrow = jnp.concatenate([v_new, zeros, jnp.zeros((1, DN_DV), F32)], axis=0)
            s_ref[i, h] = s_old * eg + jnp.dot(kcol, vrow, preferred_element_type=F32)
            o_ref[i:i + 1, sl] = _gated_norm(o, z_ref[i:i + 1, sl], nw)


def _delta_sample(qkv, bg, z, nw, s0, bb):
    R = qkv.shape[0]
    return pl.pallas_call(
        _delta_sample_kernel,
        grid=(R // bb,),
        in_specs=[pl.BlockSpec((bb, DN_CONV_CH), lambda i: (i, 0)),
                  pl.BlockSpec((bb, LANES), lambda i: (i, 0)),
                  pl.BlockSpec((bb, DN_V), lambda i: (i, 0)),
                  _const_spec((1, DN_DV)),
                  pl.BlockSpec((bb, DN_HEADS, DN_DK, DN_DV), lambda i: (i, 0, 0, 0))],
        out_specs=[pl.BlockSpec((bb, DN_V), lambda i: (i, 0)),
                   pl.BlockSpec((bb, DN_HEADS, DN_DK, DN_DV), lambda i: (i, 0, 0, 0))],
        out_shape=[jax.ShapeDtypeStruct((R, DN_V), F32),
                   jax.ShapeDtypeStruct(s0.shape, F32)],
        compiler_params=_cparams(("parallel",)),
        name="delta_sample",
    )(qkv, bg, z, nw, s0)


def _swa_prompt_kernel(q_ref, kc_ref, vc_ref, kp_ref, vp_ref, sink_ref, o_ref):
    n = pl.program_id(1)
    w = q_ref.shape[0]
    row = lax.broadcasted_iota(jnp.int32, (w, 2 * w), 0)
    col = lax.broadcasted_iota(jnp.int32, (w, 2 * w), 1)
    lo = jnp.where(n == 0, w, 0)
    mask = (col >= row) & (col <= row + w) & (col >= lo)
    scale = SWA_DH ** -0.5
    for j in range(SWA_KV_HEADS):
        ks = slice(j * SWA_DH, (j + 1) * SWA_DH)
        kk = jnp.concatenate([kp_ref[:, ks], kc_ref[:, ks]], axis=0).astype(BF16)
        vv = jnp.concatenate([vp_ref[:, ks], vc_ref[:, ks]], axis=0).astype(BF16)
        for g in range(SWA_GROUP):
            hq = j * SWA_GROUP + g
            qs = slice(hq * SWA_DH, (hq + 1) * SWA_DH)
            q = q_ref[:, qs].astype(BF16)
            s = lax.dot_general(q, kk, (((1,), (1,)), ((), ())),
                                preferred_element_type=F32) * scale
            s = jnp.where(mask, s, -jnp.inf)
            sink = sink_ref[0:1, hq:hq + 1]
            mx = jnp.maximum(jnp.max(s, axis=-1, keepdims=True), sink)
            p = jnp.exp(s - mx)
            den = jnp.sum(p, axis=-1, keepdims=True) + jnp.exp(sink - mx)
            o = jnp.dot(p.astype(BF16), vv, preferred_element_type=F32)
            o_ref[:, qs] = o / den


def _swa_prompt(swa, sinks):
    B, T, _ = swa.shape
    w = WINDOW
    kcol = SWA_Q // SWA_KV
    prev = lambda n: jnp.maximum(n - 1, 0)
    return pl.pallas_call(
        _swa_prompt_kernel,
        grid=(B, T // w),
        in_specs=[pl.BlockSpec((None, w, SWA_Q), lambda b, n: (b, n, 0)),
                  pl.BlockSpec((None, w, SWA_KV), lambda b, n: (b, n, kcol)),
                  pl.BlockSpec((None, w, SWA_KV), lambda b, n: (b, n, kcol + 1)),
                  pl.BlockSpec((None, w, SWA_KV), lambda b, n: (b, prev(n), kcol)),
                  pl.BlockSpec((None, w, SWA_KV), lambda b, n: (b, prev(n), kcol + 1)),
                  _const_spec((1, LANES))],
        out_specs=pl.BlockSpec((None, w, SWA_Q), lambda b, n: (b, n, 0)),
        out_shape=jax.ShapeDtypeStruct((B, T, SWA_Q), F32),
        compiler_params=_cparams(("parallel", "parallel")),
        name="swa_prompt",
    )(swa, swa, swa, swa, swa, sinks)


def _swa_sample_kernel(swa_ref, kc_ref, vc_ref, sink_ref, o_ref, kn_ref, vn_ref):
    bb = swa_ref.shape[0]
    w = kc_ref.shape[1]
    scale = SWA_DH ** -0.5
    for i in range(bb):
        knew = swa_ref[i:i + 1, SWA_Q:SWA_Q + SWA_KV]
        vnew = swa_ref[i:i + 1, SWA_Q + SWA_KV:]
        kc = kc_ref[i]
        vc = vc_ref[i]
        kn_ref[i, 0:w - 1, :] = kc_ref[i, 1:w, :]
        kn_ref[i, w - 1:w, :] = knew
        vn_ref[i, 0:w - 1, :] = vc_ref[i, 1:w, :]
        vn_ref[i, w - 1:w, :] = vnew
        for j in range(SWA_KV_HEADS):
            ks = slice(j * SWA_DH, (j + 1) * SWA_DH)
            q4 = jnp.concatenate(
                [swa_ref[i:i + 1, (j * SWA_GROUP + g) * SWA_DH:(j * SWA_GROUP + g + 1) * SWA_DH]
                 for g in range(SWA_GROUP)] + [jnp.zeros((SUBLANES - SWA_GROUP, SWA_DH), F32)],
                axis=0)
            sink = jnp.concatenate(
                [sink_ref[0:1, j * SWA_GROUP + g:j * SWA_GROUP + g + 1] for g in range(SWA_GROUP)]
                + [jnp.zeros((SUBLANES - SWA_GROUP, 1), F32)], axis=0)
            sc = lax.dot_general(q4, kc[:, ks], (((1,), (1,)), ((), ())),
                                 preferred_element_type=F32) * scale
            sn = jnp.sum(q4 * knew[:, ks], axis=-1, keepdims=True) * scale
            mx = jnp.maximum(jnp.maximum(jnp.max(sc, axis=-1, keepdims=True), sn), sink)
            pc = jnp.exp(sc - mx)
            pn = jnp.exp(sn - mx)
            den = jnp.sum(pc, axis=-1, keepdims=True) + pn + jnp.exp(sink - mx)
            o = (jnp.dot(pc, vc[:, ks], preferred_element_type=F32) + pn * vnew[:, ks]) / den
            for g in range(SWA_GROUP):
                hq = j * SWA_GROUP + g
                o_ref[i:i + 1, hq * SWA_DH:(hq + 1) * SWA_DH] = o[g:g + 1, :]


def _swa_sample(swa, kc, vc, sinks, bb):
    R = swa.shape[0]
    w = kc.shape[1]
    cache_spec = pl.BlockSpec((bb, w, SWA_KV), lambda i: (i, 0, 0))
    return pl.pallas_call(
        _swa_sample_kernel,
        grid=(R // bb,),
        in_specs=[pl.BlockSpec((bb, SWA_OUT), lambda i: (i, 0)), cache_spec, cache_spec,
                  _const_spec((1, LANES))],
        out_specs=[pl.BlockSpec((bb, SWA_Q), lambda i: (i, 0)), cache_spec, cache_spec],
        out_shape=[jax.ShapeDtypeStruct((R, SWA_Q), F32),
                   jax.ShapeDtypeStruct(kc.shape, F32),
                   jax.ShapeDtypeStruct(vc.shape, F32)],
        compiler_params=_cparams(("parallel",)),
        name="swa_sample",
    )(swa, kc, vc, sinks)


def _outproj_ffn_kernel(x_ref, dn_ref, sw_ref, g5_ref, sh_ref, sc_ref, ga_ref, g_ref,
                        woa_ref, wob_ref, wgu_ref, wd_ref, o_ref):
    mix = (jnp.dot(dn_ref[...].astype(BF16), woa_ref[...], preferred_element_type=F32)
           + jnp.dot(sw_ref[...].astype(BF16), wob_ref[...], preferred_element_type=F32))
    x = x_ref[...] + g5_ref[...] * mix
    o_ref[...] = _ffn_body(x, sh_ref[...], sc_ref[...], ga_ref[...], g_ref[...], wgu_ref, wd_ref)


def _outproj_ffn(x, dn, sw, m, g, woa, wob, wgu, wd, tm):
    G, R, _ = x.shape
    g5 = _mod_specs(m, 1, tm)[2]
    return pl.pallas_call(
        _outproj_ffn_kernel,
        grid=(G, R // tm),
        in_specs=[_row_spec(tm, D_MODEL), _row_spec(tm, DN_V), _row_spec(tm, SWA_Q), g5]
        + _mod_specs(m, 2, tm) + [
            _const_spec((1, D_MODEL)),
            _const_spec((DN_V, D_MODEL)),
            _const_spec((SWA_Q, D_MODEL)),
            _const_spec((D_MODEL, 2 * D_FF)),
            _const_spec((D_FF, D_MODEL)),
        ],
        out_specs=_row_spec(tm, D_MODEL),
        out_shape=jax.ShapeDtypeStruct(x.shape, F32),
        compiler_params=_cparams(("parallel", "parallel")),
        name="outproj_ffn",
    )(x, dn, sw, m, m, m, m, g, woa, wob, wgu, wd)


def _rope_tables(pos):
    half = SWA_DH // 2
    inv = jnp.power(ROPE_THETA, -jnp.arange(half, dtype=F32) * 2.0 / SWA_DH)
    ang = pos.astype(F32)[:, None] * inv[None, :]
    cos, sin = jnp.cos(ang), jnp.sin(ang)
    reps = LANES // SWA_DH
    return (jnp.concatenate([cos, cos] * reps, axis=1),
            jnp.concatenate([-sin, sin] * reps, axis=1))


def _group_matrix(width):
    i = jnp.arange(width) // SWA_DH
    return jnp.where(i[:, None] == i[None, :], 1.0 / SWA_DH, 0.0).astype(BF16)


def _pack_w_in(w_in):
    conv = w_in[:, :DN_CONV_CH]
    o = DN_CONV_CH
    z = w_in[:, o:o + DN_V]
    o += DN_V
    ba = w_in[:, o:o + 2 * DN_HEADS]
    o += 2 * DN_HEADS
    sq = w_in[:, o:o + SWA_Q]
    o += SWA_Q
    sk = w_in[:, o:o + SWA_KV]
    o += SWA_KV
    sv = w_in[:, o:o + SWA_KV]
    pad = jnp.zeros((w_in.shape[0], LANES - 2 * DN_HEADS), w_in.dtype)
    return jnp.concatenate([conv, z, sq, sk, sv, ba, pad], axis=1).astype(BF16)


def _lane_row(vec, offset=0):
    return jnp.zeros((LANES,), F32).at[offset:offset + vec.shape[0]].set(vec.astype(F32))


def kernel(x_prompt, x_sample, c_prompt, c_sample, state_dn_conv, state_dn_S, cache_swa_k, cache_swa_v, w_ada, b_ada, g_ffn1, w_ffn1_gu, w_ffn1_down, g_mix, w_in, dn_conv_w, dn_A_log, dn_dt_bias, dn_norm_w, swa_q_norm, swa_k_norm, swa_sinks, w_out, g_ffn2, w_ffn2_gu, w_ffn2_down):
    L = w_ada.shape[0]
    B, T, _ = x_prompt.shape
    Bs = x_sample.shape[0]
    tm = min(512, T)
    bb = min(8, Bs)

    m_p, m_s = _adaln(c_prompt, c_sample, w_ada, b_ada)
    m_p = m_p.reshape(L, N_MOD, B, 1, D_MODEL)
    m_s = m_s.reshape(L, N_MOD, 1, Bs, D_MODEL)

    cos_p, sin_p = _rope_tables(jnp.arange(T))
    cos_s, sin_s = _rope_tables(PAST_LEN + jnp.arange(1))
    gq, gk = _group_matrix(SWA_Q), _group_matrix(SWA_KV)

    yp = x_prompt
    ys = x_sample.reshape(1, Bs, D_MODEL)
    outs = [[] for _ in range(8)]
    for l in range(L):
        wgu1, wd1 = w_ffn1_gu[l].astype(BF16), w_ffn1_down[l].astype(BF16)
        wgu2, wd2 = w_ffn2_gu[l].astype(BF16), w_ffn2_down[l].astype(BF16)
        win = _pack_w_in(w_in[l])
        woa, wob = w_out[l, :DN_V].astype(BF16), w_out[l, DN_V:].astype(BF16)
        g1, gm, g2 = (g_ffn1[l].reshape(1, D_MODEL), g_mix[l].reshape(1, D_MODEL),
                      g_ffn2[l].reshape(1, D_MODEL))
        cw = dn_conv_w[l]
        nw = dn_norm_w[l].reshape(1, DN_DV)
        prm = jnp.stack([
            _lane_row(-jnp.exp(dn_A_log[l].astype(F32)), DN_HEADS),
            _lane_row(dn_dt_bias[l], DN_HEADS),
            jnp.concatenate([swa_q_norm[l]] * (LANES // SWA_DH)).astype(F32),
            jnp.concatenate([swa_k_norm[l]] * (LANES // SWA_DH)).astype(F32),
        ] + [jnp.zeros((LANES,), F32)] * (SUBLANES - 4))
        sinks = _lane_row(swa_sinks[l]).reshape(1, LANES)

        yp = _ffn(yp, m_p[l], 0, g1, wgu1, wd1, tm)
        qkv, z, bg, swa, tail = _inproj_prompt(yp, m_p[l], gm, win, cw, prm, cos_p, sin_p, gq, gk, tm)
        dn_o, s_new = _delta_prompt(qkv, bg, z, nw, min(256, T))
        sw_o = _swa_prompt(swa, sinks)
        yp = _outproj_ffn(yp, dn_o, sw_o, m_p[l], g2, woa, wob, wgu2, wd2, tm)
        nkeep = min(WINDOW, T)
        outs[0].append(tail[:, HALO - (DN_CONV - 1):])
        outs[1].append(s_new)
        outs[2].append(swa[:, T - nkeep:, SWA_Q:SWA_Q + SWA_KV].reshape(B, nkeep, SWA_KV_HEADS, SWA_DH))
        outs[3].append(swa[:, T - nkeep:, SWA_Q + SWA_KV:].reshape(B, nkeep, SWA_KV_HEADS, SWA_DH))

        ys = _ffn(ys, m_s[l], 0, g1, wgu1, wd1, Bs)
        buf = jnp.transpose(state_dn_conv[l], (1, 0, 2))
        qkv, z, bg, swa, cnew = _inproj_sample(ys, m_s[l], gm, win, cw, prm, cos_s, sin_s, gq, gk, buf)
        dn_o, s_new = _delta_sample(qkv[0], bg[0], z[0], nw, state_dn_S[l], bb)
        w0 = cache_swa_k.shape[2]
        sw_o, k_new, v_new = _swa_sample(swa[0], cache_swa_k[l].reshape(Bs, w0, SWA_KV),
                                         cache_swa_v[l].reshape(Bs, w0, SWA_KV), sinks, bb)
        ys = _outproj_ffn(ys, dn_o[None], sw_o[None], m_s[l], g2, woa, wob, wgu2, wd2, Bs)
        outs[4].append(jnp.transpose(cnew, (1, 0, 2)))
        outs[5].append(s_new)
        outs[6].append(k_new.reshape(Bs, w0, SWA_KV_HEADS, SWA_DH))
        outs[7].append(v_new.reshape(Bs, w0, SWA_KV_HEADS, SWA_DH))

    return (yp, ys.reshape(Bs, 1, D_MODEL)) + tuple(jnp.stack(o) for o in outs)
```

```python
import functools
import math

import jax
import jax.numpy as jnp
from jax import lax
from jax.experimental import pallas as pl
from jax.experimental.pallas import tpu as pltpu

F32 = jnp.float32
BF16 = jnp.bfloat16

D_MODEL = 1024
D_FF = 2816
N_MOD = 9
EPS = 1e-6
DN_HEADS = 4
DN_DK = 128
DN_DV = 128
DN_QK = DN_HEADS * DN_DK
DN_V = DN_HEADS * DN_DV
DN_CONV = 4
DN_CONV_CH = 2 * DN_QK + DN_V
DN_CHUNK = 64
INV_BLOCK = 16
INTRA_ROWS = 256
SWA_DH = 64
SWA_HEADS = 8
SWA_KV_HEADS = 2
SWA_GROUP = SWA_HEADS // SWA_KV_HEADS
SWA_Q = SWA_HEADS * SWA_DH
SWA_KV = SWA_KV_HEADS * SWA_DH
WINDOW = 128
ROPE_THETA = 10000.0
PAST_LEN = 16384

LANES = 128
SUBLANES = 8
HALO = SUBLANES

C_CONV = 0
C_Z = C_CONV + DN_CONV_CH
C_SQ = C_Z + DN_V
C_SK = C_SQ + SWA_Q
C_SV = C_SK + SWA_KV
C_BG = C_SV + SWA_KV
IN_PACKED = C_BG + LANES
SWA_OUT = SWA_Q + 2 * SWA_KV

VMEM_LIMIT = 56 * 1024 * 1024


def _cparams(sem):
    return pltpu.CompilerParams(dimension_semantics=sem, vmem_limit_bytes=VMEM_LIMIT)


def _const_spec(shape):
    nd = len(shape)
    return pl.BlockSpec(shape, lambda *_: (0,) * nd, pipeline_mode=pl.Buffered(1))


def _sigmoid(x):
    return 1.0 / (1.0 + jnp.exp(-x))


def _silu(x):
    return x * _sigmoid(x)


def _adaln_kernel(cp_ref, cs_ref, w_ref, b_ref, mp_ref, ms_ref):
    w = w_ref[...].astype(BF16)
    b = b_ref[...]
    for c_ref, m_ref in ((cp_ref, mp_ref), (cs_ref, ms_ref)):
        c = c_ref[...]
        m_ref[...] = jnp.dot(_silu(c).astype(BF16), w, preferred_element_type=F32) + b


def _adaln(c_p, c_s, w_ada, b_ada):
    L = w_ada.shape[0]
    bp, bs = c_p.shape[0], c_s.shape[0]
    b4 = b_ada.reshape(L, N_MOD, 1, D_MODEL)
    return pl.pallas_call(
        _adaln_kernel,
        grid=(L, N_MOD),
        in_specs=[
            pl.BlockSpec((bp, D_MODEL), lambda l, j: (0, 0)),
            pl.BlockSpec((bs, D_MODEL), lambda l, j: (0, 0)),
            pl.BlockSpec((None, D_MODEL, D_MODEL), lambda l, j: (l, 0, j)),
            pl.BlockSpec((None, None, 1, D_MODEL), lambda l, j: (l, j, 0, 0)),
        ],
        out_specs=[
            pl.BlockSpec((None, None, bp, D_MODEL), lambda l, j: (l, j, 0, 0)),
            pl.BlockSpec((None, None, bs, D_MODEL), lambda l, j: (l, j, 0, 0)),
        ],
        out_shape=[
            jax.ShapeDtypeStruct((L, N_MOD, bp, D_MODEL), F32),
            jax.ShapeDtypeStruct((L, N_MOD, bs, D_MODEL), F32),
        ],
        compiler_params=_cparams(("parallel", "parallel")),
        name="adaln",
    )(c_p, c_s, w_ada, b4)


def _mod_norm(x, g, shift, scale):
    ms = jnp.mean(x * x, axis=-1, keepdims=True)
    y = x * lax.rsqrt(ms + EPS) * g
    return y * (1.0 + scale) + shift


def _ffn_body(x, shift, scale, gate, g, wgu_ref, wd_ref):
    h = _mod_norm(x, g, shift, scale).astype(BF16)
    gu = jnp.dot(h, wgu_ref[...], preferred_element_type=F32)
    a = gu[:, :D_FF]
    b = gu[:, D_FF:]
    act = (_silu(a) * b).astype(BF16)
    y = jnp.dot(act, wd_ref[...], preferred_element_type=F32)
    return x + (0.5 * gate) * y


def _mod_specs(m, sub, tm):
    rm = m.shape[2]
    blk = 1 if rm == 1 else tm

    def spec(i):
        if rm == 1:
            return pl.BlockSpec((None, None, 1, D_MODEL), lambda g, r: (i, g, 0, 0))
        return pl.BlockSpec((None, None, blk, D_MODEL), lambda g, r: (i, g, r, 0))

    return [spec(3 * sub), spec(3 * sub + 1), spec(3 * sub + 2)]


def _row_spec(tm, width):
    return pl.BlockSpec((None, tm, width), lambda g, r: (g, r, 0))


def _ffn_kernel(x_ref, sh_ref, sc_ref, ga_ref, g_ref, wgu_ref, wd_ref, o_ref):
    o_ref[...] = _ffn_body(x_ref[...], sh_ref[...], sc_ref[...], ga_ref[...], g_ref[...],
                           wgu_ref, wd_ref)


def _ffn(x, m, sub, g, wgu, wd, tm):
    G, R, _ = x.shape
    return pl.pallas_call(
        _ffn_kernel,
        grid=(G, R // tm),
        in_specs=[_row_spec(tm, D_MODEL)] + _mod_specs(m, sub, tm) + [
            _const_spec((1, D_MODEL)),
            _const_spec((D_MODEL, 2 * D_FF)),
            _const_spec((D_FF, D_MODEL)),
        ],
        out_specs=_row_spec(tm, D_MODEL),
        out_shape=jax.ShapeDtypeStruct(x.shape, F32),
        compiler_params=_cparams(("parallel", "parallel")),
        name="ffn",
    )(x, m, m, m, g, wgu, wd)


def _group_mean_sq(x, gmat_ref):
    sq = x * x
    hi = sq.astype(BF16)
    lo = (sq - hi.astype(F32)).astype(BF16)
    gm = gmat_ref[...]
    return (jnp.dot(hi, gm, preferred_element_type=F32)
            + jnp.dot(lo, gm, preferred_element_type=F32))


def _rope(x, cos, sin_signed):
    width = x.shape[-1]
    reps = width // LANES
    lane = lax.broadcasted_iota(jnp.int32, x.shape, 1)
    first_half = (lane & (SWA_DH - 1)) < (SWA_DH // 2)
    partner = jnp.where(first_half,
                        pltpu.roll(x, width - SWA_DH // 2, 1),
                        pltpu.roll(x, SWA_DH // 2, 1))
    if reps > 1:
        cos = jnp.concatenate([cos] * reps, axis=1)
        sin_signed = jnp.concatenate([sin_signed] * reps, axis=1)
    return x * cos + partner * sin_signed


def _mix_post(p, qkv_conv, prm_ref, cos, sin, gq_ref, gk_ref, qkv_ref, z_ref, bg_ref, swa_ref):
    qkv = _silu(qkv_conv)
    for hh in range(2 * DN_HEADS):
        xh = qkv[:, hh * DN_DK:(hh + 1) * DN_DK]
        s = lax.rsqrt(jnp.sum(xh * xh, axis=-1, keepdims=True) + EPS)
        if hh < DN_HEADS:
            s = s * (DN_DK ** -0.5)
        qkv_ref[:, hh * DN_DK:(hh + 1) * DN_DK] = xh * s
    qkv_ref[:, 2 * DN_QK:] = qkv[:, 2 * DN_QK:]
    z_ref[...] = p[:, C_Z:C_Z + DN_V]

    pb = p[:, C_BG:C_BG + LANES]
    neg_a = prm_ref[0:1, :]
    dt_b = prm_ref[1:2, :]
    xa = pb + dt_b
    softplus = jnp.maximum(xa, 0.0) + jnp.log(1.0 + jnp.exp(-jnp.abs(xa)))
    lane = lax.broadcasted_iota(jnp.int32, pb.shape, 1)
    bg_ref[...] = jnp.where(lane < DN_HEADS, _sigmoid(pb), neg_a * softplus)

    pq = p[:, C_SQ:C_SQ + SWA_Q]
    pk = p[:, C_SK:C_SK + SWA_KV]
    wq = jnp.concatenate([prm_ref[2:3, :]] * (SWA_Q // LANES), axis=1)
    wk = prm_ref[3:4, :]
    qn = pq * lax.rsqrt(_group_mean_sq(pq, gq_ref) + EPS) * wq
    kn = pk * lax.rsqrt(_group_mean_sq(pk, gk_ref) + EPS) * wk
    swa_ref[:, :SWA_Q] = _rope(qn, cos, sin)
    swa_ref[:, SWA_Q:SWA_Q + SWA_KV] = _rope(kn, cos, sin)
    swa_ref[:, SWA_Q + SWA_KV:] = p[:, C_SV:C_SV + SWA_KV]


def _inproj_prompt_kernel(x_ref, sh_ref, sc_ref, g_ref, win_ref, cw_ref, prm_ref, cos_ref, sin_ref,
                          gq_ref, gk_ref, qkv_ref, z_ref, bg_ref, swa_ref, tail_ref, ext_ref):
    tm = x_ref.shape[0]
    h = _mod_norm(x_ref[...], g_ref[...], sh_ref[...], sc_ref[...]).astype(BF16)
    p = jnp.dot(h, win_ref[...], preferred_element_type=F32)
    pc = p[:, :DN_CONV_CH]

    @pl.when(pl.program_id(1) == 0)
    def _():
        ext_ref[0:HALO, :] = jnp.zeros((HALO, DN_CONV_CH), F32)

    ext_ref[HALO:, :] = pc
    y = pc * cw_ref[DN_CONV - 1:DN_CONV, :]
    for j in range(DN_CONV - 1):
        off = HALO - (DN_CONV - 1) + j
        y = y + ext_ref[off:off + tm, :] * cw_ref[j:j + 1, :]
    ext_ref[0:HALO, :] = pc[tm - HALO:, :]
    tail_ref[...] = pc[tm - HALO:, :]
    _mix_post(p, y, prm_ref, cos_ref[...], sin_ref[...], gq_ref, gk_ref,
              qkv_ref, z_ref, bg_ref, swa_ref)


def _inproj_sample_kernel(x_ref, sh_ref, sc_ref, g_ref, win_ref, cw_ref, prm_ref, cos_ref, sin_ref,
                          gq_ref, gk_ref, buf_ref, qkv_ref, z_ref, bg_ref, swa_ref, cnew_ref):
    h = _mod_norm(x_ref[...], g_ref[...], sh_ref[...], sc_ref[...]).astype(BF16)
    p = jnp.dot(h, win_ref[...], preferred_element_type=F32)
    pc = p[:, :DN_CONV_CH]
    y = pc * cw_ref[DN_CONV - 1:DN_CONV, :]
    for j in range(DN_CONV - 1):
        y = y + buf_ref[j] * cw_ref[j:j + 1, :]
    for j in range(DN_CONV - 2):
        cnew_ref[j] = buf_ref[j + 1]
    cnew_ref[DN_CONV - 2] = pc
    _mix_post(p, y, prm_ref, cos_ref[...], sin_ref[...], gq_ref, gk_ref,
              qkv_ref, z_ref, bg_ref, swa_ref)


def _inproj_common_specs(m, tm):
    sh, sc, _ = _mod_specs(m, 1, tm)
    return [_row_spec(tm, D_MODEL), sh, sc,
            _const_spec((1, D_MODEL)),
            _const_spec((D_MODEL, IN_PACKED)),
            _const_spec((DN_CONV, DN_CONV_CH)),
            _const_spec((SUBLANES, LANES))]


def _inproj_prompt(x, m, g, win, cw, prm, cos, sin, gq, gk, tm):
    B, T, _ = x.shape
    outs = pl.pallas_call(
        _inproj_prompt_kernel,
        grid=(B, T // tm),
        in_specs=_inproj_common_specs(m, tm) + [
            pl.BlockSpec((tm, LANES), lambda b, r: (r, 0)),
            pl.BlockSpec((tm, LANES), lambda b, r: (r, 0)),
            _const_spec((SWA_Q, SWA_Q)),
            _const_spec((SWA_KV, SWA_KV)),
        ],
        out_specs=[
            _row_spec(tm, DN_CONV_CH), _row_spec(tm, DN_V), _row_spec(tm, LANES),
            _row_spec(tm, SWA_OUT),
            pl.BlockSpec((None, HALO, DN_CONV_CH), lambda b, r: (b, 0, 0)),
        ],
        out_shape=[
            jax.ShapeDtypeStruct((B, T, DN_CONV_CH), F32),
            jax.ShapeDtypeStruct((B, T, DN_V), F32),
            jax.ShapeDtypeStruct((B, T, LANES), F32),
            jax.ShapeDtypeStruct((B, T, SWA_OUT), F32),
            jax.ShapeDtypeStruct((B, HALO, DN_CONV_CH), F32),
        ],
        scratch_shapes=[pltpu.VMEM((tm + HALO, DN_CONV_CH), F32)],
        compiler_params=_cparams(("parallel", "arbitrary")),
        name="inproj_prompt",
    )(x, m, m, g, win, cw, prm, cos, sin, gq, gk)
    return outs


def _inproj_sample(x, m, g, win, cw, prm, cos, sin, gq, gk, buf):
    _, R, _ = x.shape
    nbuf = DN_CONV - 1
    return pl.pallas_call(
        _inproj_sample_kernel,
        grid=(1, 1),
        in_specs=_inproj_common_specs(m, R) + [
            _const_spec((1, LANES)), _const_spec((1, LANES)),
            _const_spec((SWA_Q, SWA_Q)), _const_spec((SWA_KV, SWA_KV)),
            _const_spec((nbuf, R, DN_CONV_CH)),
        ],
        out_specs=[
            _row_spec(R, DN_CONV_CH), _row_spec(R, DN_V), _row_spec(R, LANES), _row_spec(R, SWA_OUT),
            pl.BlockSpec((nbuf, R, DN_CONV_CH), lambda g_, r: (0, 0, 0)),
        ],
        out_shape=[
            jax.ShapeDtypeStruct((1, R, DN_CONV_CH), F32),
            jax.ShapeDtypeStruct((1, R, DN_V), F32),
            jax.ShapeDtypeStruct((1, R, LANES), F32),
            jax.ShapeDtypeStruct((1, R, SWA_OUT), F32),
            jax.ShapeDtypeStruct((nbuf, R, DN_CONV_CH), F32),
        ],
        compiler_params=_cparams(("arbitrary", "arbitrary")),
        name="inproj_sample",
    )(x, m, m, g, win, cw, prm, cos, sin, gq, gk, buf)


def _gated_norm(o, z, nw):
    on = o * lax.rsqrt(jnp.mean(o * o, axis=-1, keepdims=True) + EPS) * nw
    return on * _silu(z)


def _mm(a, b):
    return jnp.dot(a.astype(BF16), b.astype(BF16), preferred_element_type=F32)


def _unit_lower_inverses(mats, c):
    n = mats[0].shape[0]
    row = lax.broadcasted_iota(jnp.int32, (n, n), 0)
    col = lax.broadcasted_iota(jnp.int32, (n, n), 1)
    s = INV_BLOCK
    diag = (row // s) == (col // s)
    pws = [jnp.where(diag, a, 0.0) for a in mats]
    eye = jnp.where(row == col, 1.0, 0.0)
    xs = [eye - pw for pw in pws]
    k = 2
    while k < s:
        pws = [_mm(pw, pw) for pw in pws]
        xs = [x + _mm(x, pw) for x, pw in zip(xs, pws)]
        k *= 2
    while s < c:
        pair = ((row // (2 * s)) == (col // (2 * s))) & ((row // s) != (col // s))
        ts = [_mm(x, jnp.where(pair, a, 0.0)) for x, a in zip(xs, mats)]
        xs = [x - _mm(t, x) for x, t in zip(xs, ts)]
        s *= 2
    return xs


def _cumsum_rows(x):
    n = x.shape[0]
    row = lax.broadcasted_iota(jnp.int32, x.shape, 0)
    s = 1
    while s < n:
        x = x + jnp.where(row >= s, pltpu.roll(x, s, 0), 0.0)
        s *= 2
    return x


WQ_ROWS = 2 * DN_CHUNK
QKD_ROWS = DN_CHUNK + DN_DK


def _delta_intra_kernel(q_ref, k_ref, v_ref, bg_ref, u_ref, wq_ref, qkd_ref, eg_ref):
    c = DN_CHUNK
    r = INTRA_ROWS
    nc = r // c
    row = lax.broadcasted_iota(jnp.int32, (r, r), 0)
    col = lax.broadcasted_iota(jnp.int32, (r, r), 1)
    same = (row // c) == (col // c)
    incl = same & (row >= col)
    strict = same & (row > col)

    jobs = [(blk, h) for blk in range(q_ref.shape[0] // r) for h in range(DN_HEADS)]
    gcs, egcs, ekds, gcts, bgs = [], [], [], [], []
    for blk in range(q_ref.shape[0] // r):
        bg = bg_ref[blk * r:(blk + 1) * r, :]
        gc = jnp.concatenate([_cumsum_rows(bg[ci * c:(ci + 1) * c, :]) for ci in range(nc)], axis=0)
        g_end = [gc[(ci + 1) * c - 1:(ci + 1) * c, :] for ci in range(nc)]
        glast = jnp.concatenate([jnp.broadcast_to(g, (c, LANES)) for g in g_end], axis=0)
        for ci in range(nc):
            rows = [jnp.broadcast_to(jnp.exp(g_end[ci][:, DN_HEADS + h:DN_HEADS + h + 1]), (1, LANES))
                    for h in range(DN_HEADS)]
            eg_ref[blk * nc + ci] = jnp.concatenate(
                rows + [jnp.zeros((SUBLANES - DN_HEADS, LANES), F32)], axis=0)
        bgs.append(bg)
        gcs.append(gc)
        gcts.append(gc.T)
        egcs.append(jnp.exp(gc))
        ekds.append(jnp.exp(glast - gc))

    prep = []
    for blk, h in jobs:
        rs = slice(blk * r, (blk + 1) * r)
        sl = slice(h * DN_DK, (h + 1) * DN_DK)
        q = q_ref[rs, sl]
        k = k_ref[rs, sl]
        kb = k * bgs[blk][:, h:h + 1]
        kq = lax.dot_general(jnp.concatenate([kb, q], axis=0).astype(BF16), k.astype(BF16),
                             (((1,), (1,)), ((), ())), preferred_element_type=F32)
        prep.append((q, k, kb, kq))
    a_mats, qks = [], []
    for (blk, h), (q, k, kb, kq) in zip(jobs, prep):
        gl = slice(DN_HEADS + h, DN_HEADS + h + 1)
        decay = jnp.exp(jnp.where(incl, gcs[blk][:, gl] - gcts[blk][gl, :], -jnp.inf))
        a_mats.append(jnp.where(strict, kq[:r] * decay, 0.0))
        qks.append(kq[r:] * decay)
    tinvs = _unit_lower_inverses(a_mats, c)
    uws = []
    for (blk, h), (q, k, kb, kq), tinv in zip(jobs, prep, tinvs):
        rs = slice(blk * r, (blk + 1) * r)
        sl = slice(h * DN_DK, (h + 1) * DN_DK)
        gl = slice(DN_HEADS + h, DN_HEADS + h + 1)
        rhs = jnp.concatenate([v_ref[rs, sl] * bgs[blk][:, h:h + 1], kb * egcs[blk][:, gl]], axis=1)
        uws.append(_mm(tinv, rhs))
    for (blk, h), (q, k, kb, kq), uw, qk in zip(jobs, prep, uws, qks):
        rs = slice(blk * r, (blk + 1) * r)
        sl = slice(h * DN_DK, (h + 1) * DN_DK)
        gl = slice(DN_HEADS + h, DN_HEADS + h + 1)
        u_ref[rs, sl] = uw[:, :DN_DV]
        w_b = uw[:, DN_DV:].astype(BF16)
        qd_b = (q * egcs[blk][:, gl]).astype(BF16)
        for ci in range(nc):
            wq_ref[blk * nc + ci, h * WQ_ROWS:h * WQ_ROWS + c, :] = w_b[ci * c:(ci + 1) * c, :]
            wq_ref[blk * nc + ci, h * WQ_ROWS + c:(h + 1) * WQ_ROWS, :] = qd_b[ci * c:(ci + 1) * c, :]
        kdt = (k * ekds[blk][:, gl]).T
        for pj in range(nc // 2):
            ls = slice(pj * LANES, (pj + 1) * LANES)
            qk2 = qk[2 * pj * c:(2 * pj + 1) * c, ls] + qk[(2 * pj + 1) * c:(2 * pj + 2) * c, ls]
            pi = blk * (nc // 2) + pj
            qkd_ref[pi, h * QKD_ROWS:h * QKD_ROWS + c, :] = qk2.astype(BF16)
            qkd_ref[pi, h * QKD_ROWS + c:(h + 1) * QKD_ROWS, :] = kdt[:, ls].astype(BF16)


def _delta_scan_kernel(u_ref, wq_ref, qkd_ref, eg_ref, z_ref, nw_ref, o_ref, sout_ref, s_ref):
    n = pl.program_id(0)
    c = DN_CHUNK

    @pl.when(n == 0)
    def _():
        s_ref[...] = jnp.zeros_like(s_ref)

    nw = nw_ref[...]
    zeros = jnp.zeros((c, DN_DV), BF16)
    chains = [(b, h) for b in range(u_ref.shape[0]) for h in range(DN_HEADS)]
    for j in range(2):
        rs = slice(j * c, (j + 1) * c)
        ws = [jnp.dot(wq_ref[b, j, h * WQ_ROWS:(h + 1) * WQ_ROWS, :], s_ref[b, h].astype(BF16),
                      preferred_element_type=F32) for b, h in chains]
        rr = []
        for (b, h), w_h in zip(chains, ws):
            v_new = (u_ref[b, rs, h * DN_DV:(h + 1) * DN_DV] - w_h[:c]).astype(BF16)
            rhs = jnp.concatenate([v_new, zeros] if j == 0 else [zeros, v_new], axis=0)
            rr.append(jnp.dot(qkd_ref[b, h * QKD_ROWS:(h + 1) * QKD_ROWS, :], rhs,
                              preferred_element_type=F32))
        for (b, h), w_h, r_h in zip(chains, ws, rr):
            sl = slice(h * DN_DV, (h + 1) * DN_DV)
            s_ref[b, h] = s_ref[b, h] * eg_ref[b, j, h:h + 1, :] + r_h[c:]
            o_ref[b, rs, sl] = _gated_norm(w_h[c:] + r_h[:c], z_ref[b, rs, sl], nw)

    @pl.when(n == pl.num_programs(0) - 1)
    def _():
        sout_ref[...] = s_ref[...]


def _delta_prompt(qkv, bg, z, nw, ra):
    B, T, _ = qkv.shape
    c = DN_CHUNK
    n = T // c
    cpb = ra // c
    qkv_spec = lambda j: pl.BlockSpec((None, ra, DN_QK), lambda b, r: (b, r, j))
    u, wq, qkd, eg = pl.pallas_call(
        _delta_intra_kernel,
        grid=(B, T // ra),
        in_specs=[qkv_spec(0), qkv_spec(1), qkv_spec(2),
                  pl.BlockSpec((None, ra, LANES), lambda b, r: (b, r, 0))],
        out_specs=[pl.BlockSpec((None, ra, DN_V), lambda b, r: (b, r, 0)),
                   pl.BlockSpec((None, cpb, DN_HEADS * WQ_ROWS, DN_DK), lambda b, r: (b, r, 0, 0)),
                   pl.BlockSpec((None, cpb // 2, DN_HEADS * QKD_ROWS, LANES),
                                lambda b, r: (b, r, 0, 0)),
                   pl.BlockSpec((None, cpb, SUBLANES, LANES), lambda b, r: (b, r, 0, 0))],
        out_shape=[jax.ShapeDtypeStruct((B, T, DN_V), F32),
                   jax.ShapeDtypeStruct((B, n, DN_HEADS * WQ_ROWS, DN_DK), BF16),
                   jax.ShapeDtypeStruct((B, n // 2, DN_HEADS * QKD_ROWS, LANES), BF16),
                   jax.ShapeDtypeStruct((B, n, SUBLANES, LANES), F32)],
        compiler_params=_cparams(("parallel", "parallel")),
        name="delta_intra",
    )(qkv, qkv, qkv, bg)
    return pl.pallas_call(
        _delta_scan_kernel,
        grid=(n // 2,),
        in_specs=[pl.BlockSpec((B, 2 * c, DN_V), lambda i: (0, i, 0)),
                  pl.BlockSpec((B, 2, DN_HEADS * WQ_ROWS, DN_DK), lambda i: (0, i, 0, 0)),
                  pl.BlockSpec((B, None, DN_HEADS * QKD_ROWS, LANES), lambda i: (0, i, 0, 0)),
                  pl.BlockSpec((B, 2, SUBLANES, LANES), lambda i: (0, i, 0, 0)),
                  pl.BlockSpec((B, 2 * c, DN_V), lambda i: (0, i, 0)),
                  _const_spec((1, DN_DV))],
        out_specs=[pl.BlockSpec((B, 2 * c, DN_V), lambda i: (0, i, 0)),
                   pl.BlockSpec((B, DN_HEADS, DN_DK, DN_DV), lambda i: (0, 0, 0, 0))],
        out_shape=[jax.ShapeDtypeStruct((B, T, DN_V), F32),
                   jax.ShapeDtypeStruct((B, DN_HEADS, DN_DK, DN_DV), F32)],
        scratch_shapes=[pltpu.VMEM((B, DN_HEADS, DN_DK, DN_DV), F32)],
        compiler_params=_cparams(("arbitrary",)),
        name="delta_scan",
    )(u, wq, qkd, eg, z, nw)


def _delta_sample_kernel(qkv_ref, bg_ref, z_ref, nw_ref, s0_ref, o_ref, s_ref):
    bb = qkv_ref.shape[0]
    nw = nw_ref[...]
    zeros = jnp.zeros((SUBLANES - 2, DN_DK), F32)
    zeros7 = jnp.zeros((SUBLANES - 1, DN_DK), F32)
    chains = [(i, h) for i in range(bb) for h in range(DN_HEADS)]
    pre = []
    for i, h in chains:
        q = qkv_ref[i:i + 1, h * DN_DK:(h + 1) * DN_DK]
        k = qkv_ref[i:i + 1, DN_QK + h * DN_DK:DN_QK + (h + 1) * DN_DK]
        v = qkv_ref[i:i + 1, 2 * DN_QK + h * DN_DV:2 * DN_QK + (h + 1) * DN_DV]
        beta = bg_ref[i:i + 1, h:h + 1]
        eg = jnp.exp(bg_ref[i:i + 1, DN_HEADS + h:DN_HEADS + h + 1])
        lhs = jnp.concatenate([k * beta * eg, q * eg, zeros], axis=0)
        ws = jnp.dot(lhs, s0_ref[i, h], preferred_element_type=F32)
        pre.append((q, k, v, beta, eg, ws))
    upd = []
    for (i, h), (q, k, v, beta, eg, ws) in zip(chains, pre):
        v_new = v * beta - ws[0:1]
        kcol = jnp.concatenate([k, zeros7], axis=0).T
        vrow = jnp.concatenate([v_new, zeros7], axis=0)
        upd.append((v_new, jnp.dot(kcol, vrow, preferred_element_type=F32)))
    for (i, h), (q, k, v, beta, eg, ws), (v_new, kv) in zip(chains, pre, upd):
        sl = slice(h * DN_DK, (h + 1) * DN_DK)
        s_ref[i, h] = s0_ref[i, h] * eg + kv
        o = ws[1:2] + jnp.sum(q * k, axis=-1, keepdims=True) * v_new
        o_ref[i:i + 1, sl] = _gated_norm(o, z_ref[i:i + 1, sl], nw)


def _delta_sample(qkv, bg, z, nw, s0, bb):
    R = qkv.shape[0]
    return pl.pallas_call(
        _delta_sample_kernel,
        grid=(R // bb,),
        in_specs=[pl.BlockSpec((bb, DN_CONV_CH), lambda i: (i, 0)),
                  pl.BlockSpec((bb, LANES), lambda i: (i, 0)),
                  pl.BlockSpec((bb, DN_V), lambda i: (i, 0)),
                  _const_spec((1, DN_DV)),
                  pl.BlockSpec((bb, DN_HEADS, DN_DK, DN_DV), lambda i: (i, 0, 0, 0))],
        out_specs=[pl.BlockSpec((bb, DN_V), lambda i: (i, 0)),
                   pl.BlockSpec((bb, DN_HEADS, DN_DK, DN_DV), lambda i: (i, 0, 0, 0))],
        out_shape=[jax.ShapeDtypeStruct((R, DN_V), F32),
                   jax.ShapeDtypeStruct(s0.shape, F32)],
        compiler_params=_cparams(("parallel",)),
        name="delta_sample",
    )(qkv, bg, z, nw, s0)


def _swa_prompt_kernel(q_ref, kc_ref, vc_ref, kp_ref, vp_ref, sink_ref, o_ref):
    n = pl.program_id(1)
    w = WINDOW
    nq = q_ref.shape[0] // w
    half = SWA_DH
    row = lax.broadcasted_iota(jnp.int32, (w, 2 * w), 0)
    col = lax.broadcasted_iota(jnp.int32, (w, 2 * w), 1)
    band = (col >= row) & (col <= row + w)
    bias_mid = jnp.where(band, 0.0, -jnp.inf)
    bias_first = jnp.where(band & (col >= jnp.where(n == 0, w, 0)), 0.0, -jnp.inf)
    low = lax.broadcasted_iota(jnp.int32, (2 * w, LANES), 1) < half
    low_q = lax.broadcasted_iota(jnp.int32, (w, LANES), 1) < half
    kall = jnp.concatenate([kp_ref[...], kc_ref[...]], axis=0)
    vall = jnp.concatenate([vp_ref[...], vc_ref[...]], axis=0)
    kall_r = pltpu.roll(kall, half, 1)
    vall_r = pltpu.roll(vall, half, 1)
    scale = SWA_DH ** -0.5

    units = [(i, j) for i in range(nq) for j in range(SWA_KV_HEADS)]
    scores, vvs = [], []
    for i, j in units:
        rs = slice(i * w, (i + 2) * w)
        ka, kb = (kall[rs], kall_r[rs]) if j == 0 else (kall_r[rs], kall[rs])
        va, vb = (vall[rs], vall_r[rs]) if j == 0 else (vall_r[rs], vall[rs])
        kk = jnp.concatenate([jnp.where(low, ka, 0.0), jnp.where(low, 0.0, kb)], axis=0).astype(BF16)
        vvs.append(jnp.concatenate(
            [jnp.concatenate([jnp.where(low, va, 0.0), jnp.where(low, 1.0, 0.0)], axis=1),
             jnp.concatenate([jnp.where(low, 0.0, vb), jnp.where(low, 0.0, 1.0)], axis=1)],
            axis=0).astype(BF16))
        c0 = j * SWA_GROUP * SWA_DH
        q = jnp.concatenate([q_ref[i * w:(i + 1) * w, c0:c0 + LANES],
                             q_ref[i * w:(i + 1) * w, c0 + LANES:c0 + 2 * LANES]], axis=0)
        scores.append(lax.dot_general((q * scale).astype(BF16), kk, (((1,), (1,)), ((), ())),
                                      preferred_element_type=F32))
    keys = [(u, pr, hf) for u in range(len(units)) for pr in range(2) for hf in range(2)]
    sinks = {key: sink_ref[0:1, units[key[0]][1] * SWA_GROUP + 2 * key[1] + key[2]:
                           units[key[0]][1] * SWA_GROUP + 2 * key[1] + key[2] + 1] for key in keys}
    masked = {}
    for u, pr, hf in keys:
        bias = bias_first if units[u][0] == 0 else bias_mid
        masked[u, pr, hf] = scores[u][pr * w:(pr + 1) * w, hf * 2 * w:(hf + 1) * 2 * w] + bias
    mx = {key: jnp.maximum(jnp.max(masked[key], axis=-1, keepdims=True), sinks[key]) for key in keys}
    p = {key: jnp.exp(masked[key] - mx[key]).astype(BF16) for key in keys}
    outs = []
    for u in range(len(units)):
        pu = jnp.concatenate([jnp.concatenate([p[u, pr, 0], p[u, pr, 1]], axis=1)
                              for pr in range(2)], axis=0)
        outs.append(jnp.dot(pu, vvs[u], preferred_element_type=F32))
    for u, (i, j) in enumerate(units):
        c0 = j * SWA_GROUP * SWA_DH
        for pr in range(2):
            ou = outs[u][pr * w:(pr + 1) * w]
            sink_term = jnp.where(low_q, jnp.exp(sinks[u, pr, 0] - mx[u, pr, 0]),
                                  jnp.exp(sinks[u, pr, 1] - mx[u, pr, 1]))
            o_ref[i * w:(i + 1) * w, c0 + pr * LANES:c0 + (pr + 1) * LANES] = (
                ou[:, :LANES] / (ou[:, LANES:] + sink_term))


def _swa_prompt(swa, sinks, nq):
    B, T, _ = swa.shape
    w = WINDOW
    rq = nq * w
    kcol = SWA_Q // SWA_KV
    prev = lambda n: jnp.maximum(n * nq - 1, 0)
    return pl.pallas_call(
        _swa_prompt_kernel,
        grid=(B, T // rq),
        in_specs=[pl.BlockSpec((None, rq, SWA_Q), lambda b, n: (b, n, 0)),
                  pl.BlockSpec((None, rq, SWA_KV), lambda b, n: (b, n, kcol)),
                  pl.BlockSpec((None, rq, SWA_KV), lambda b, n: (b, n, kcol + 1)),
                  pl.BlockSpec((None, w, SWA_KV), lambda b, n: (b, prev(n), kcol)),
                  pl.BlockSpec((None, w, SWA_KV), lambda b, n: (b, prev(n), kcol + 1)),
                  _const_spec((1, LANES))],
        out_specs=pl.BlockSpec((None, rq, SWA_Q), lambda b, n: (b, n, 0)),
        out_shape=jax.ShapeDtypeStruct((B, T, SWA_Q), F32),
        compiler_params=_cparams(("parallel", "parallel")),
        name="swa_prompt",
    )(swa, swa, swa, swa, swa, sinks)


def _swa_sample_kernel(swa_ref, kc_ref, vc_ref, sink_ref, o_ref, kn_ref, vn_ref):
    bb = swa_ref.shape[0]
    w = kc_ref.shape[1]
    scale = SWA_DH ** -0.5
    for i in range(bb):
        kn_ref[i, 0:w - 1, :] = kc_ref[i, 1:w, :]
        kn_ref[i, w - 1:w, :] = swa_ref[i:i + 1, SWA_Q:SWA_Q + SWA_KV]
        vn_ref[i, 0:w - 1, :] = vc_ref[i, 1:w, :]
        vn_ref[i, w - 1:w, :] = swa_ref[i:i + 1, SWA_Q + SWA_KV:]
    units = [(i, j) for i in range(bb) for j in range(SWA_KV_HEADS)]
    sinks = [jnp.concatenate(
        [sink_ref[0:1, j * SWA_GROUP + g:j * SWA_GROUP + g + 1] for g in range(SWA_GROUP)]
        + [jnp.zeros((SUBLANES - SWA_GROUP, 1), F32)], axis=0) for j in range(SWA_KV_HEADS)]
    qs, scs = [], []
    for i, j in units:
        q4 = jnp.concatenate(
            [swa_ref[i:i + 1, (j * SWA_GROUP + g) * SWA_DH:(j * SWA_GROUP + g + 1) * SWA_DH]
             for g in range(SWA_GROUP)] + [jnp.zeros((SUBLANES - SWA_GROUP, SWA_DH), F32)], axis=0)
        qs.append(q4)
        scs.append(lax.dot_general(q4, kc_ref[i, :, j * SWA_DH:(j + 1) * SWA_DH],
                                   (((1,), (1,)), ((), ())), preferred_element_type=F32) * scale)
    soft = []
    for (i, j), q4, sc in zip(units, qs, scs):
        knew = swa_ref[i:i + 1, SWA_Q + j * SWA_DH:SWA_Q + (j + 1) * SWA_DH]
        sn = jnp.sum(q4 * knew, axis=-1, keepdims=True) * scale
        mx = jnp.maximum(jnp.maximum(jnp.max(sc, axis=-1, keepdims=True), sn), sinks[j])
        pc = jnp.exp(sc - mx)
        pn = jnp.exp(sn - mx)
        den = jnp.sum(pc, axis=-1, keepdims=True) + pn + jnp.exp(sinks[j] - mx)
        soft.append((pc, pn, den))
    pvs = [jnp.dot(pc, vc_ref[i, :, j * SWA_DH:(j + 1) * SWA_DH], preferred_element_type=F32)
           for (i, j), (pc, pn, den) in zip(units, soft)]
    for (i, j), (pc, pn, den), pv in zip(units, soft, pvs):
        vnew = swa_ref[i:i + 1, SWA_Q + SWA_KV + j * SWA_DH:SWA_Q + SWA_KV + (j + 1) * SWA_DH]
        o = (pv + pn * vnew) / den
        for g in range(SWA_GROUP):
            hq = j * SWA_GROUP + g
            o_ref[i:i + 1, hq * SWA_DH:(hq + 1) * SWA_DH] = o[g:g + 1, :]


def _swa_sample(swa, kc, vc, sinks, bb):
    R = swa.shape[0]
    w = kc.shape[1]
    cache_spec = pl.BlockSpec((bb, w, SWA_KV), lambda i: (i, 0, 0))
    return pl.pallas_call(
        _swa_sample_kernel,
        grid=(R // bb,),
        in_specs=[pl.BlockSpec((bb, SWA_OUT), lambda i: (i, 0)), cache_spec, cache_spec,
                  _const_spec((1, LANES))],
        out_specs=[pl.BlockSpec((bb, SWA_Q), lambda i: (i, 0)), cache_spec, cache_spec],
        out_shape=[jax.ShapeDtypeStruct((R, SWA_Q), F32),
                   jax.ShapeDtypeStruct(kc.shape, F32),
                   jax.ShapeDtypeStruct(vc.shape, F32)],
        compiler_params=_cparams(("parallel",)),
        name="swa_sample",
    )(swa, kc, vc, sinks)


def _outproj_ffn_kernel(x_ref, dn_ref, sw_ref, g5_ref, sh_ref, sc_ref, ga_ref, g_ref,
                        woa_ref, wob_ref, wgu_ref, wd_ref, o_ref):
    mix = (jnp.dot(dn_ref[...].astype(BF16), woa_ref[...], preferred_element_type=F32)
           + jnp.dot(sw_ref[...].astype(BF16), wob_ref[...], preferred_element_type=F32))
    x = x_ref[...] + g5_ref[...] * mix
    o_ref[...] = _ffn_body(x, sh_ref[...], sc_ref[...], ga_ref[...], g_ref[...], wgu_ref, wd_ref)


def _outproj_ffn(x, dn, sw, m, g, woa, wob, wgu, wd, tm):
    G, R, _ = x.shape
    g5 = _mod_specs(m, 1, tm)[2]
    return pl.pallas_call(
        _outproj_ffn_kernel,
        grid=(G, R // tm),
        in_specs=[_row_spec(tm, D_MODEL), _row_spec(tm, DN_V), _row_spec(tm, SWA_Q), g5]
        + _mod_specs(m, 2, tm) + [
            _const_spec((1, D_MODEL)),
            _const_spec((DN_V, D_MODEL)),
            _const_spec((SWA_Q, D_MODEL)),
            _const_spec((D_MODEL, 2 * D_FF)),
            _const_spec((D_FF, D_MODEL)),
        ],
        out_specs=_row_spec(tm, D_MODEL),
        out_shape=jax.ShapeDtypeStruct(x.shape, F32),
        compiler_params=_cparams(("parallel", "parallel")),
        name="outproj_ffn",
    )(x, dn, sw, m, m, m, m, g, woa, wob, wgu, wd)


def _rope_tables(pos):
    half = SWA_DH // 2
    inv = jnp.power(ROPE_THETA, -jnp.arange(half, dtype=F32) * 2.0 / SWA_DH)
    ang = pos.astype(F32)[:, None] * inv[None, :]
    cos, sin = jnp.cos(ang), jnp.sin(ang)
    reps = LANES // SWA_DH
    return (jnp.concatenate([cos, cos] * reps, axis=1),
            jnp.concatenate([-sin, sin] * reps, axis=1))


def _group_matrix(width):
    i = jnp.arange(width) // SWA_DH
    return jnp.where(i[:, None] == i[None, :], 1.0 / SWA_DH, 0.0).astype(BF16)


def _pack_w_in(w_in):
    conv = w_in[:, :DN_CONV_CH]
    o = DN_CONV_CH
    z = w_in[:, o:o + DN_V]
    o += DN_V
    ba = w_in[:, o:o + 2 * DN_HEADS]
    o += 2 * DN_HEADS
    sq = w_in[:, o:o + SWA_Q]
    o += SWA_Q
    sk = w_in[:, o:o + SWA_KV]
    o += SWA_KV
    sv = w_in[:, o:o + SWA_KV]
    pad = jnp.zeros((w_in.shape[0], LANES - 2 * DN_HEADS), w_in.dtype)
    return jnp.concatenate([conv, z, sq, sk, sv, ba, pad], axis=1).astype(BF16)


def _lane_row(vec, offset=0):
    return jnp.zeros((LANES,), F32).at[offset:offset + vec.shape[0]].set(vec.astype(F32))


def kernel(x_prompt, x_sample, c_prompt, c_sample, state_dn_conv, state_dn_S, cache_swa_k, cache_swa_v, w_ada, b_ada, g_ffn1, w_ffn1_gu, w_ffn1_down, g_mix, w_in, dn_conv_w, dn_A_log, dn_dt_bias, dn_norm_w, swa_q_norm, swa_k_norm, swa_sinks, w_out, g_ffn2, w_ffn2_gu, w_ffn2_down):
    L = w_ada.shape[0]
    B, T, _ = x_prompt.shape
    Bs = x_sample.shape[0]
    tm = min(512, T)
    bb = min(8, Bs)

    m_p, m_s = _adaln(c_prompt, c_sample, w_ada, b_ada)
    m_p = m_p.reshape(L, N_MOD, B, 1, D_MODEL)
    m_s = m_s.reshape(L, N_MOD, 1, Bs, D_MODEL)

    cos_p, sin_p = _rope_tables(jnp.arange(T))
    cos_s, sin_s = _rope_tables(PAST_LEN + jnp.arange(1))
    gq, gk = _group_matrix(SWA_Q), _group_matrix(SWA_KV)

    yp = x_prompt
    ys = x_sample.reshape(1, Bs, D_MODEL)
    outs = [[] for _ in range(8)]
    for l in range(L):
        wgu1, wd1 = w_ffn1_gu[l].astype(BF16), w_ffn1_down[l].astype(BF16)
        wgu2, wd2 = w_ffn2_gu[l].astype(BF16), w_ffn2_down[l].astype(BF16)
        win = _pack_w_in(w_in[l])
        woa, wob = w_out[l, :DN_V].astype(BF16), w_out[l, DN_V:].astype(BF16)
        g1, gm, g2 = (g_ffn1[l].reshape(1, D_MODEL), g_mix[l].reshape(1, D_MODEL),
                      g_ffn2[l].reshape(1, D_MODEL))
        cw = dn_conv_w[l]
        nw = dn_norm_w[l].reshape(1, DN_DV)
        prm = jnp.stack([
            _lane_row(-jnp.exp(dn_A_log[l].astype(F32)), DN_HEADS),
            _lane_row(dn_dt_bias[l], DN_HEADS),
            jnp.concatenate([swa_q_norm[l]] * (LANES // SWA_DH)).astype(F32),
            jnp.concatenate([swa_k_norm[l]] * (LANES // SWA_DH)).astype(F32),
        ] + [jnp.zeros((LANES,), F32)] * (SUBLANES - 4))
        sinks = _lane_row(swa_sinks[l]).reshape(1, LANES)

        yp = _ffn(yp, m_p[l], 0, g1, wgu1, wd1, tm)
        qkv, z, bg, swa, tail = _inproj_prompt(yp, m_p[l], gm, win, cw, prm, cos_p, sin_p, gq, gk, tm)
        dn_o, s_new = _delta_prompt(qkv, bg, z, nw, min(512, T))
        sw_o = _swa_prompt(swa, sinks, min(4, T // WINDOW))
        yp = _outproj_ffn(yp, dn_o, sw_o, m_p[l], g2, woa, wob, wgu2, wd2, tm)
        nkeep = min(WINDOW, T)
        outs[0].append(tail[:, HALO - (DN_CONV - 1):])
        outs[1].append(s_new)
        outs[2].append(swa[:, T - nkeep:, SWA_Q:SWA_Q + SWA_KV].reshape(B, nkeep, SWA_KV_HEADS, SWA_DH))
        outs[3].append(swa[:, T - nkeep:, SWA_Q + SWA_KV:].reshape(B, nkeep, SWA_KV_HEADS, SWA_DH))

        ys = _ffn(ys, m_s[l], 0, g1, wgu1, wd1, Bs)
        buf = jnp.transpose(state_dn_conv[l], (1, 0, 2))
        qkv, z, bg, swa, cnew = _inproj_sample(ys, m_s[l], gm, win, cw, prm, cos_s, sin_s, gq, gk, buf)
        dn_o, s_new = _delta_sample(qkv[0], bg[0], z[0], nw, state_dn_S[l], bb)
        w0 = cache_swa_k.shape[2]
        sw_o, k_new, v_new = _swa_sample(swa[0], cache_swa_k[l].reshape(Bs, w0, SWA_KV),
                                         cache_swa_v[l].reshape(Bs, w0, SWA_KV), sinks, bb)
        ys = _outproj_ffn(ys, dn_o[None], sw_o[None], m_s[l], g2, woa, wob, wgu2, wd2, Bs)
        outs[4].append(jnp.transpose(cnew, (1, 0, 2)))
        outs[5].append(s_new)
        outs[6].append(k_new.reshape(Bs, w0, SWA_KV_HEADS, SWA_DH))
        outs[7].append(v_new.reshape(Bs, w0, SWA_KV_HEADS, SWA_DH))

    return (yp, ys.reshape(Bs, 1, D_MODEL)) + tuple(jnp.stack(o) for o in outs)
```

```python
import functools
import math

import jax
import jax.numpy as jnp
from jax import lax
from jax.experimental import pallas as pl
from jax.experimental.pallas import tpu as pltpu

F32 = jnp.float32
BF16 = jnp.bfloat16

D_MODEL = 1024
D_FF = 2816
N_MOD = 9
EPS = 1e-6
DN_HEADS = 4
DN_DK = 128
DN_DV = 128
DN_QK = DN_HEADS * DN_DK
DN_V = DN_HEADS * DN_DV
DN_CONV = 4
DN_CONV_CH = 2 * DN_QK + DN_V
DN_CHUNK = 64
INV_BLOCK = 16
INTRA_ROWS = 256
SWA_DH = 64
SWA_HEADS = 8
SWA_KV_HEADS = 2
SWA_GROUP = SWA_HEADS // SWA_KV_HEADS
SWA_Q = SWA_HEADS * SWA_DH
SWA_KV = SWA_KV_HEADS * SWA_DH
WINDOW = 128
ROPE_THETA = 10000.0
PAST_LEN = 16384

LANES = 128
SUBLANES = 8
HALO = SUBLANES

C_CONV = 0
C_Z = C_CONV + DN_CONV_CH
C_SQ = C_Z + DN_V
C_SK = C_SQ + SWA_Q
C_SV = C_SK + SWA_KV
C_BG = C_SV + SWA_KV
IN_PACKED = C_BG + LANES
SWA_OUT = SWA_Q + 2 * SWA_KV

VMEM_LIMIT = 56 * 1024 * 1024


def _cparams(sem):
    return pltpu.CompilerParams(dimension_semantics=sem, vmem_limit_bytes=VMEM_LIMIT)


def _const_spec(shape):
    nd = len(shape)
    return pl.BlockSpec(shape, lambda *_: (0,) * nd, pipeline_mode=pl.Buffered(1))


def _layer_spec(shape, l):
    nd = len(shape)
    return pl.BlockSpec((None,) + tuple(shape), lambda *_: (l,) + (0,) * nd,
                        pipeline_mode=pl.Buffered(1))


def _sigmoid(x):
    return 1.0 / (1.0 + jnp.exp(-x))


def _silu(x):
    return x * _sigmoid(x)


def _adaln_kernel(cp_ref, cs_ref, w_ref, b_ref, mp_ref, ms_ref):
    w = w_ref[...].astype(BF16)
    b = b_ref[...]
    for c_ref, m_ref in ((cp_ref, mp_ref), (cs_ref, ms_ref)):
        c = c_ref[...]
        m_ref[...] = jnp.dot(_silu(c).astype(BF16), w, preferred_element_type=F32) + b


def _adaln(c_p, c_s, w_ada, b_ada):
    L = w_ada.shape[0]
    bp, bs = c_p.shape[0], c_s.shape[0]
    b4 = b_ada.reshape(L, N_MOD, 1, D_MODEL)
    return pl.pallas_call(
        _adaln_kernel,
        grid=(L, N_MOD),
        in_specs=[
            pl.BlockSpec((bp, D_MODEL), lambda l, j: (0, 0)),
            pl.BlockSpec((bs, D_MODEL), lambda l, j: (0, 0)),
            pl.BlockSpec((None, D_MODEL, D_MODEL), lambda l, j: (l, 0, j)),
            pl.BlockSpec((None, None, 1, D_MODEL), lambda l, j: (l, j, 0, 0)),
        ],
        out_specs=[
            pl.BlockSpec((None, None, bp, D_MODEL), lambda l, j: (l, j, 0, 0)),
            pl.BlockSpec((None, None, bs, D_MODEL), lambda l, j: (l, j, 0, 0)),
        ],
        out_shape=[
            jax.ShapeDtypeStruct((L, N_MOD, bp, D_MODEL), F32),
            jax.ShapeDtypeStruct((L, N_MOD, bs, D_MODEL), F32),
        ],
        compiler_params=_cparams(("parallel", "parallel")),
        name="adaln",
    )(c_p, c_s, w_ada, b4)


def _mod_norm(x, g, shift, scale):
    ms = jnp.mean(x * x, axis=-1, keepdims=True)
    y = x * lax.rsqrt(ms + EPS) * g
    return y * (1.0 + scale) + shift


def _ffn_body(x, shift, scale, gate, g, wgu_ref, wd_ref):
    h = _mod_norm(x, g, shift, scale).astype(BF16)
    gu = jnp.dot(h, wgu_ref[...], preferred_element_type=F32)
    a = gu[:, :D_FF]
    b = gu[:, D_FF:]
    act = (_silu(a) * b).astype(BF16)
    y = jnp.dot(act, wd_ref[...], preferred_element_type=F32)
    return x + (0.5 * gate) * y


def _mod_specs(m, sub, tm):
    rm = m.shape[2]
    blk = 1 if rm == 1 else tm

    def spec(i):
        if rm == 1:
            return pl.BlockSpec((None, None, 1, D_MODEL), lambda g, r: (i, g, 0, 0))
        return pl.BlockSpec((None, None, blk, D_MODEL), lambda g, r: (i, g, r, 0))

    return [spec(3 * sub), spec(3 * sub + 1), spec(3 * sub + 2)]


def _row_spec(tm, width):
    return pl.BlockSpec((None, tm, width), lambda g, r: (g, r, 0))


def _ffn_kernel(x_ref, sh_ref, sc_ref, ga_ref, g_ref, wgu_ref, wd_ref, o_ref):
    o_ref[...] = _ffn_body(x_ref[...], sh_ref[...], sc_ref[...], ga_ref[...], g_ref[...],
                           wgu_ref, wd_ref)


def _ffn(x, m, sub, g, wgu, wd, l, tm):
    G, R, _ = x.shape
    return pl.pallas_call(
        _ffn_kernel,
        grid=(G, R // tm),
        in_specs=[_row_spec(tm, D_MODEL)] + _mod_specs(m, sub, tm) + [
            _const_spec((1, D_MODEL)),
            _layer_spec((D_MODEL, 2 * D_FF), l),
            _layer_spec((D_FF, D_MODEL), l),
        ],
        out_specs=_row_spec(tm, D_MODEL),
        out_shape=jax.ShapeDtypeStruct(x.shape, F32),
        compiler_params=_cparams(("parallel", "parallel")),
        name="ffn",
    )(x, m, m, m, g, wgu, wd)


def _group_mean_sq(x, gmat_ref):
    sq = x * x
    hi = sq.astype(BF16)
    lo = (sq - hi.astype(F32)).astype(BF16)
    gm = gmat_ref[...]
    return (jnp.dot(hi, gm, preferred_element_type=F32)
            + jnp.dot(lo, gm, preferred_element_type=F32))


def _rope(x, cos, sin_signed):
    width = x.shape[-1]
    reps = width // LANES
    lane = lax.broadcasted_iota(jnp.int32, x.shape, 1)
    first_half = (lane & (SWA_DH - 1)) < (SWA_DH // 2)
    partner = jnp.where(first_half,
                        pltpu.roll(x, width - SWA_DH // 2, 1),
                        pltpu.roll(x, SWA_DH // 2, 1))
    if reps > 1:
        cos = jnp.concatenate([cos] * reps, axis=1)
        sin_signed = jnp.concatenate([sin_signed] * reps, axis=1)
    return x * cos + partner * sin_signed


def _swa_post(p, prm_ref, cos, sin, gq_ref, gk_ref, swa_ref, orows=slice(None)):
    pq = p[:, C_SQ:C_SQ + SWA_Q]
    pk = p[:, C_SK:C_SK + SWA_KV]
    wq = jnp.concatenate([prm_ref[2:3, :]] * (SWA_Q // LANES), axis=1)
    wk = prm_ref[3:4, :]
    qn = pq * lax.rsqrt(_group_mean_sq(pq, gq_ref) + EPS) * wq
    kn = pk * lax.rsqrt(_group_mean_sq(pk, gk_ref) + EPS) * wk
    swa_ref[orows, :SWA_Q] = _rope(qn, cos, sin)
    swa_ref[orows, SWA_Q:SWA_Q + SWA_KV] = _rope(kn, cos, sin)
    swa_ref[orows, SWA_Q + SWA_KV:] = p[:, C_SV:C_SV + SWA_KV]


def _dn_post(p, qkv_conv, prm_ref, qkv_ref, z_ref, bg_ref, orows=slice(None)):
    qkv = _silu(qkv_conv)
    for hh in range(2 * DN_HEADS):
        xh = qkv[:, hh * DN_DK:(hh + 1) * DN_DK]
        s = lax.rsqrt(jnp.sum(xh * xh, axis=-1, keepdims=True) + EPS)
        if hh < DN_HEADS:
            s = s * (DN_DK ** -0.5)
        qkv_ref[orows, hh * DN_DK:(hh + 1) * DN_DK] = xh * s
    qkv_ref[orows, 2 * DN_QK:] = qkv[:, 2 * DN_QK:]
    z_ref[orows, :] = p[:, C_Z:C_Z + DN_V]

    pb = p[:, C_BG:C_BG + LANES]
    neg_a = prm_ref[0:1, :]
    dt_b = prm_ref[1:2, :]
    xa = pb + dt_b
    softplus = jnp.maximum(xa, 0.0) + jnp.log(1.0 + jnp.exp(-jnp.abs(xa)))
    lane = lax.broadcasted_iota(jnp.int32, pb.shape, 1)
    bg_ref[orows, :] = jnp.where(lane < DN_HEADS, _sigmoid(pb), neg_a * softplus)


def _inproj_prompt_kernel(nt, x_ref, sh_ref, sc_ref, g_ref, win_ref, cw_ref, prm_ref, cos_ref,
                          sin_ref, gq_ref, gk_ref, qkv_ref, z_ref, bg_ref, swa_ref, tail_ref,
                          h0_ref, h1_ref, p0_ref, p1_ref):
    i = pl.program_id(0)
    tm = h0_ref.shape[0]

    @pl.when(i == 0)
    def _():
        for ref in (h0_ref, h1_ref, p0_ref, p1_ref):
            ref[...] = jnp.zeros_like(ref)

    def finish_conv(p_ref, orows):
        y = p_ref[HALO:, :DN_CONV_CH] * cw_ref[DN_CONV - 1:DN_CONV, :]
        for j in range(DN_CONV - 1):
            off = HALO - (DN_CONV - 1) + j
            y = y + p_ref[off:off + tm, :DN_CONV_CH] * cw_ref[j:j + 1, :]
        _dn_post(p_ref.at[HALO:], y, prm_ref, qkv_ref, z_ref, bg_ref, orows)

    g, sh, sc = g_ref[...], sh_ref[...], sc_ref[...]
    halves = (slice(0, tm), slice(tm, 2 * tm))
    carry0 = p0_ref[tm:tm + HALO, :DN_CONV_CH]
    p1_ref[0:HALO, :DN_CONV_CH] = carry0
    _swa_post(p0_ref.at[HALO:], prm_ref, cos_ref[halves[0], :], sin_ref[halves[0], :],
              gq_ref, gk_ref, swa_ref, halves[0])
    h0_ref[...] = _mod_norm(x_ref[halves[0], :], g, sh, sc).astype(BF16)
    p1_ref[HALO:, :] = jnp.dot(h1_ref[...], win_ref[...], preferred_element_type=F32)
    finish_conv(p0_ref, halves[0])
    carry1 = p1_ref[tm:tm + HALO, :DN_CONV_CH]
    p0_ref[0:HALO, :DN_CONV_CH] = jnp.where((2 * i) % nt == 0, 0.0, carry1)
    _swa_post(p1_ref.at[HALO:], prm_ref, cos_ref[halves[1], :], sin_ref[halves[1], :],
              gq_ref, gk_ref, swa_ref, halves[1])
    h1_ref[...] = _mod_norm(x_ref[halves[1], :], g, sh, sc).astype(BF16)
    p0_ref[HALO:, :] = jnp.dot(h0_ref[...], win_ref[...], preferred_element_type=F32)
    finish_conv(p1_ref, halves[1])
    tail_ref[...] = carry1


def _inproj_sample_kernel(x_ref, sh_ref, sc_ref, g_ref, win_ref, cw_ref, prm_ref, cos_ref, sin_ref,
                          gq_ref, gk_ref, buf_ref, qkv_ref, z_ref, bg_ref, swa_ref, cnew_ref):
    h = _mod_norm(x_ref[...], g_ref[...], sh_ref[...], sc_ref[...]).astype(BF16)
    p = jnp.dot(h, win_ref[...], preferred_element_type=F32)
    pc = p[:, :DN_CONV_CH]
    y = pc * cw_ref[DN_CONV - 1:DN_CONV, :]
    for j in range(DN_CONV - 1):
        y = y + buf_ref[j] * cw_ref[j:j + 1, :]
    for j in range(DN_CONV - 2):
        cnew_ref[j] = buf_ref[j + 1]
    cnew_ref[DN_CONV - 2] = pc
    _swa_post(p, prm_ref, cos_ref[...], sin_ref[...], gq_ref, gk_ref, swa_ref)
    _dn_post(p, y, prm_ref, qkv_ref, z_ref, bg_ref)


def _inproj_common_specs(m, tm, l):
    sh, sc, _ = _mod_specs(m, 1, tm)
    return [_row_spec(tm, D_MODEL), sh, sc,
            _const_spec((1, D_MODEL)),
            _layer_spec((D_MODEL, IN_PACKED), l),
            _const_spec((DN_CONV, DN_CONV_CH)),
            _const_spec((SUBLANES, LANES))]


def _inproj_prompt(x, m, g, win, cw, prm, cos, sin, gq, gk, l, tm):
    B, T, _ = x.shape
    nt = T // tm
    assert nt % 2 == 0
    npair = nt // 2
    n = B * npair
    src = lambda s: jnp.minimum(s, n - 1)
    dst = lambda s: jnp.maximum(s - 1, 0)
    mod = lambda i: pl.BlockSpec((None, None, 1, D_MODEL), lambda s: (i, src(s) // npair, 0, 0))
    out_rows = lambda width: pl.BlockSpec((None, 2 * tm, width),
                                          lambda s: (dst(s) // npair, dst(s) % npair, 0))
    table = pl.BlockSpec((2 * tm, LANES), lambda s: (dst(s) % npair, 0))
    return pl.pallas_call(
        functools.partial(_inproj_prompt_kernel, nt),
        grid=(n + 1,),
        in_specs=[
            pl.BlockSpec((None, 2 * tm, D_MODEL), lambda s: (src(s) // npair, src(s) % npair, 0)),
            mod(3), mod(4),
            _const_spec((1, D_MODEL)),
            _layer_spec((D_MODEL, IN_PACKED), l),
            _const_spec((DN_CONV, DN_CONV_CH)),
            _const_spec((SUBLANES, LANES)),
            table, table,
            _const_spec((SWA_Q, SWA_Q)),
            _const_spec((SWA_KV, SWA_KV)),
        ],
        out_specs=[
            out_rows(DN_CONV_CH), out_rows(DN_V), out_rows(LANES), out_rows(SWA_OUT),
            pl.BlockSpec((None, HALO, DN_CONV_CH), lambda s: (dst(s) // npair, 0, 0)),
        ],
        out_shape=[
            jax.ShapeDtypeStruct((B, T, DN_CONV_CH), F32),
            jax.ShapeDtypeStruct((B, T, DN_V), F32),
            jax.ShapeDtypeStruct((B, T, LANES), F32),
            jax.ShapeDtypeStruct((B, T, SWA_OUT), F32),
            jax.ShapeDtypeStruct((B, HALO, DN_CONV_CH), F32),
        ],
        scratch_shapes=[pltpu.VMEM((tm, D_MODEL), BF16), pltpu.VMEM((tm, D_MODEL), BF16),
                        pltpu.VMEM((HALO + tm, IN_PACKED), F32),
                        pltpu.VMEM((HALO + tm, IN_PACKED), F32)],
        compiler_params=_cparams(("arbitrary",)),
        name="inproj_prompt",
    )(x, m, m, g, win, cw, prm, cos, sin, gq, gk)


def _inproj_sample(x, m, g, win, cw, prm, cos, sin, gq, gk, buf, l):
    _, R, _ = x.shape
    nbuf = DN_CONV - 1
    return pl.pallas_call(
        _inproj_sample_kernel,
        grid=(1, 1),
        in_specs=_inproj_common_specs(m, R, l) + [
            _const_spec((1, LANES)), _const_spec((1, LANES)),
            _const_spec((SWA_Q, SWA_Q)), _const_spec((SWA_KV, SWA_KV)),
            _const_spec((nbuf, R, DN_CONV_CH)),
        ],
        out_specs=[
            _row_spec(R, DN_CONV_CH), _row_spec(R, DN_V), _row_spec(R, LANES), _row_spec(R, SWA_OUT),
            pl.BlockSpec((nbuf, R, DN_CONV_CH), lambda g_, r: (0, 0, 0)),
        ],
        out_shape=[
            jax.ShapeDtypeStruct((1, R, DN_CONV_CH), F32),
            jax.ShapeDtypeStruct((1, R, DN_V), F32),
            jax.ShapeDtypeStruct((1, R, LANES), F32),
            jax.ShapeDtypeStruct((1, R, SWA_OUT), F32),
            jax.ShapeDtypeStruct((nbuf, R, DN_CONV_CH), F32),
        ],
        compiler_params=_cparams(("arbitrary", "arbitrary")),
        name="inproj_sample",
    )(x, m, m, g, win, cw, prm, cos, sin, gq, gk, buf)


def _gated_norm(o, z, nw):
    on = o * lax.rsqrt(jnp.mean(o * o, axis=-1, keepdims=True) + EPS) * nw
    return on * _silu(z)


def _mm(a, b):
    return jnp.dot(a.astype(BF16), b.astype(BF16), preferred_element_type=F32)


def _unit_lower_inverses(mats, c):
    n = mats[0].shape[0]
    row = lax.broadcasted_iota(jnp.int32, (n, n), 0)
    col = lax.broadcasted_iota(jnp.int32, (n, n), 1)
    s = INV_BLOCK
    diag = (row // s) == (col // s)
    pws = [jnp.where(diag, a, 0.0) for a in mats]
    eye = jnp.where(row == col, 1.0, 0.0)
    xs = [eye - pw for pw in pws]
    k = 2
    while k < s:
        pws = [_mm(pw, pw) for pw in pws]
        xs = [x + _mm(x, pw) for x, pw in zip(xs, pws)]
        k *= 2
    while s < c:
        pair = ((row // (2 * s)) == (col // (2 * s))) & ((row // s) != (col // s))
        ts = [_mm(x, jnp.where(pair, a, 0.0)) for x, a in zip(xs, mats)]
        xs = [x - _mm(t, x) for x, t in zip(xs, ts)]
        s *= 2
    return xs


def _cumsum_rows(x):
    n = x.shape[0]
    row = lax.broadcasted_iota(jnp.int32, x.shape, 0)
    s = 1
    while s < n:
        x = x + jnp.where(row >= s, pltpu.roll(x, s, 0), 0.0)
        s *= 2
    return x


WQ_ROWS = 2 * DN_CHUNK
QKD_ROWS = DN_CHUNK + DN_DK


def _delta_intra_kernel(q_ref, k_ref, v_ref, bg_ref, u_ref, wq_ref, qkd_ref, eg_ref):
    c = DN_CHUNK
    r = INTRA_ROWS
    nc = r // c
    row = lax.broadcasted_iota(jnp.int32, (r, r), 0)
    col = lax.broadcasted_iota(jnp.int32, (r, r), 1)
    same = (row // c) == (col // c)
    incl = same & (row >= col)
    strict = same & (row > col)

    jobs = [(blk, h) for blk in range(q_ref.shape[0] // r) for h in range(DN_HEADS)]
    gcs, egcs, ekds, gcts, bgs = [], [], [], [], []
    for blk in range(q_ref.shape[0] // r):
        bg = bg_ref[blk * r:(blk + 1) * r, :]
        gc = jnp.concatenate([_cumsum_rows(bg[ci * c:(ci + 1) * c, :]) for ci in range(nc)], axis=0)
        g_end = [gc[(ci + 1) * c - 1:(ci + 1) * c, :] for ci in range(nc)]
        glast = jnp.concatenate([jnp.broadcast_to(g, (c, LANES)) for g in g_end], axis=0)
        for ci in range(nc):
            rows = [jnp.broadcast_to(jnp.exp(g_end[ci][:, DN_HEADS + h:DN_HEADS + h + 1]), (1, LANES))
                    for h in range(DN_HEADS)]
            eg_ref[blk * nc + ci] = jnp.concatenate(
                rows + [jnp.zeros((SUBLANES - DN_HEADS, LANES), F32)], axis=0)
        bgs.append(bg)
        gcs.append(gc)
        gcts.append(gc.T)
        egcs.append(jnp.exp(gc))
        ekds.append(jnp.exp(glast - gc))

    prep = []
    for blk, h in jobs:
        rs = slice(blk * r, (blk + 1) * r)
        sl = slice(h * DN_DK, (h + 1) * DN_DK)
        q = q_ref[rs, sl]
        k = k_ref[rs, sl]
        kb = k * bgs[blk][:, h:h + 1]
        kq = lax.dot_general(jnp.concatenate([kb, q], axis=0).astype(BF16), k.astype(BF16),
                             (((1,), (1,)), ((), ())), preferred_element_type=F32)
        prep.append((q, k, kb, kq))
    a_mats, qks = [], []
    for (blk, h), (q, k, kb, kq) in zip(jobs, prep):
        gl = slice(DN_HEADS + h, DN_HEADS + h + 1)
        decay = jnp.exp(jnp.where(incl, gcs[blk][:, gl] - gcts[blk][gl, :], -jnp.inf))
        a_mats.append(jnp.where(strict, kq[:r] * decay, 0.0))
        qks.append(kq[r:] * decay)
    tinvs = _unit_lower_inverses(a_mats, c)
    uws = []
    for (blk, h), (q, k, kb, kq), tinv in zip(jobs, prep, tinvs):
        rs = slice(blk * r, (blk + 1) * r)
        sl = slice(h * DN_DK, (h + 1) * DN_DK)
        gl = slice(DN_HEADS + h, DN_HEADS + h + 1)
        rhs = jnp.concatenate([v_ref[rs, sl] * bgs[blk][:, h:h + 1], kb * egcs[blk][:, gl]], axis=1)
        uws.append(_mm(tinv, rhs))
    for (blk, h), (q, k, kb, kq), uw, qk in zip(jobs, prep, uws, qks):
        rs = slice(blk * r, (blk + 1) * r)
        sl = slice(h * DN_DK, (h + 1) * DN_DK)
        gl = slice(DN_HEADS + h, DN_HEADS + h + 1)
        u_ref[rs, sl] = uw[:, :DN_DV]
        w_b = uw[:, DN_DV:].astype(BF16)
        qd_b = (q * egcs[blk][:, gl]).astype(BF16)
        for ci in range(nc):
            wq_ref[blk * nc + ci, h * WQ_ROWS:h * WQ_ROWS + c, :] = w_b[ci * c:(ci + 1) * c, :]
            wq_ref[blk * nc + ci, h * WQ_ROWS + c:(h + 1) * WQ_ROWS, :] = qd_b[ci * c:(ci + 1) * c, :]
        kdt = (k * ekds[blk][:, gl]).T
        for pj in range(nc // 2):
            ls = slice(pj * LANES, (pj + 1) * LANES)
            qk2 = qk[2 * pj * c:(2 * pj + 1) * c, ls] + qk[(2 * pj + 1) * c:(2 * pj + 2) * c, ls]
            pi = blk * (nc // 2) + pj
            qkd_ref[pi, h * QKD_ROWS:h * QKD_ROWS + c, :] = qk2.astype(BF16)
            qkd_ref[pi, h * QKD_ROWS + c:(h + 1) * QKD_ROWS, :] = kdt[:, ls].astype(BF16)


def _delta_scan_kernel(u_ref, wq_ref, qkd_ref, eg_ref, z_ref, nw_ref, o_ref, sout_ref, s_ref):
    n = pl.program_id(0)
    c = DN_CHUNK

    @pl.when(n == 0)
    def _():
        s_ref[...] = jnp.zeros_like(s_ref)

    nw = nw_ref[...]
    zeros = jnp.zeros((c, DN_DV), BF16)
    chains = [(b, h) for b in range(u_ref.shape[0]) for h in range(DN_HEADS)]
    for j in range(2):
        rs = slice(j * c, (j + 1) * c)
        ws = [jnp.dot(wq_ref[b, j, h * WQ_ROWS:(h + 1) * WQ_ROWS, :], s_ref[b, h].astype(BF16),
                      preferred_element_type=F32) for b, h in chains]
        rr = []
        for (b, h), w_h in zip(chains, ws):
            v_new = (u_ref[b, rs, h * DN_DV:(h + 1) * DN_DV] - w_h[:c]).astype(BF16)
            rhs = jnp.concatenate([v_new, zeros] if j == 0 else [zeros, v_new], axis=0)
            rr.append(jnp.dot(qkd_ref[b, h * QKD_ROWS:(h + 1) * QKD_ROWS, :], rhs,
                              preferred_element_type=F32))
        for (b, h), w_h, r_h in zip(chains, ws, rr):
            sl = slice(h * DN_DV, (h + 1) * DN_DV)
            s_ref[b, h] = s_ref[b, h] * eg_ref[b, j, h:h + 1, :] + r_h[c:]
            o_ref[b, rs, sl] = _gated_norm(w_h[c:] + r_h[:c], z_ref[b, rs, sl], nw)

    @pl.when(n == pl.num_programs(0) - 1)
    def _():
        sout_ref[...] = s_ref[...]


def _delta_prompt(qkv, bg, z, nw, ra):
    B, T, _ = qkv.shape
    c = DN_CHUNK
    n = T // c
    cpb = ra // c
    qkv_spec = lambda j: pl.BlockSpec((None, ra, DN_QK), lambda b, r: (b, r, j))
    u, wq, qkd, eg = pl.pallas_call(
        _delta_intra_kernel,
        grid=(B, T // ra),
        in_specs=[qkv_spec(0), qkv_spec(1), qkv_spec(2),
                  pl.BlockSpec((None, ra, LANES), lambda b, r: (b, r, 0))],
        out_specs=[pl.BlockSpec((None, ra, DN_V), lambda b, r: (b, r, 0)),
                   pl.BlockSpec((None, cpb, DN_HEADS * WQ_ROWS, DN_DK), lambda b, r: (b, r, 0, 0)),
                   pl.BlockSpec((None, cpb // 2, DN_HEADS * QKD_ROWS, LANES),
                                lambda b, r: (b, r, 0, 0)),
                   pl.BlockSpec((None, cpb, SUBLANES, LANES), lambda b, r: (b, r, 0, 0))],
        out_shape=[jax.ShapeDtypeStruct((B, T, DN_V), F32),
                   jax.ShapeDtypeStruct((B, n, DN_HEADS * WQ_ROWS, DN_DK), BF16),
                   jax.ShapeDtypeStruct((B, n // 2, DN_HEADS * QKD_ROWS, LANES), BF16),
                   jax.ShapeDtypeStruct((B, n, SUBLANES, LANES), F32)],
        compiler_params=_cparams(("parallel", "parallel")),
        name="delta_intra",
    )(qkv, qkv, qkv, bg)
    return pl.pallas_call(
        _delta_scan_kernel,
        grid=(n // 2,),
        in_specs=[pl.BlockSpec((B, 2 * c, DN_V), lambda i: (0, i, 0)),
                  pl.BlockSpec((B, 2, DN_HEADS * WQ_ROWS, DN_DK), lambda i: (0, i, 0, 0)),
                  pl.BlockSpec((B, None, DN_HEADS * QKD_ROWS, LANES), lambda i: (0, i, 0, 0)),
                  pl.BlockSpec((B, 2, SUBLANES, LANES), lambda i: (0, i, 0, 0)),
                  pl.BlockSpec((B, 2 * c, DN_V), lambda i: (0, i, 0)),
                  _const_spec((1, DN_DV))],
        out_specs=[pl.BlockSpec((B, 2 * c, DN_V), lambda i: (0, i, 0)),
                   pl.BlockSpec((B, DN_HEADS, DN_DK, DN_DV), lambda i: (0, 0, 0, 0))],
        out_shape=[jax.ShapeDtypeStruct((B, T, DN_V), F32),
                   jax.ShapeDtypeStruct((B, DN_HEADS, DN_DK, DN_DV), F32)],
        scratch_shapes=[pltpu.VMEM((B, DN_HEADS, DN_DK, DN_DV), F32)],
        compiler_params=_cparams(("arbitrary",)),
        name="delta_scan",
    )(u, wq, qkd, eg, z, nw)


def _delta_sample_kernel(qkv_ref, bg_ref, z_ref, nw_ref, s0_ref, o_ref, s_ref):
    bb = qkv_ref.shape[0]
    nw = nw_ref[...]
    zeros = jnp.zeros((SUBLANES - 2, DN_DK), F32)
    zeros7 = jnp.zeros((SUBLANES - 1, DN_DK), F32)
    chains = [(i, h) for i in range(bb) for h in range(DN_HEADS)]
    pre = []
    for i, h in chains:
        q = qkv_ref[i:i + 1, h * DN_DK:(h + 1) * DN_DK]
        k = qkv_ref[i:i + 1, DN_QK + h * DN_DK:DN_QK + (h + 1) * DN_DK]
        v = qkv_ref[i:i + 1, 2 * DN_QK + h * DN_DV:2 * DN_QK + (h + 1) * DN_DV]
        beta = bg_ref[i:i + 1, h:h + 1]
        eg = jnp.exp(bg_ref[i:i + 1, DN_HEADS + h:DN_HEADS + h + 1])
        lhs = jnp.concatenate([k * beta * eg, q * eg, zeros], axis=0)
        ws = jnp.dot(lhs, s0_ref[i, h], preferred_element_type=F32)
        pre.append((q, k, v, beta, eg, ws))
    upd = []
    for (i, h), (q, k, v, beta, eg, ws) in zip(chains, pre):
        v_new = v * beta - ws[0:1]
        kcol = jnp.concatenate([k, zeros7], axis=0).T
        vrow = jnp.concatenate([v_new, zeros7], axis=0)
        upd.append((v_new, jnp.dot(kcol, vrow, preferred_element_type=F32)))
    for (i, h), (q, k, v, beta, eg, ws), (v_new, kv) in zip(chains, pre, upd):
        sl = slice(h * DN_DK, (h + 1) * DN_DK)
        s_ref[i, h] = s0_ref[i, h] * eg + kv
        o = ws[1:2] + jnp.sum(q * k, axis=-1, keepdims=True) * v_new
        o_ref[i:i + 1, sl] = _gated_norm(o, z_ref[i:i + 1, sl], nw)


def _delta_sample(qkv, bg, z, nw, s0, l, bb):
    R = qkv.shape[0]
    return pl.pallas_call(
        _delta_sample_kernel,
        grid=(R // bb,),
        in_specs=[pl.BlockSpec((bb, DN_CONV_CH), lambda i: (i, 0)),
                  pl.BlockSpec((bb, LANES), lambda i: (i, 0)),
                  pl.BlockSpec((bb, DN_V), lambda i: (i, 0)),
                  _const_spec((1, DN_DV)),
                  pl.BlockSpec((None, bb, DN_HEADS, DN_DK, DN_DV), lambda i: (l, i, 0, 0, 0))],
        out_specs=[pl.BlockSpec((bb, DN_V), lambda i: (i, 0)),
                   pl.BlockSpec((bb, DN_HEADS, DN_DK, DN_DV), lambda i: (i, 0, 0, 0))],
        out_shape=[jax.ShapeDtypeStruct((R, DN_V), F32),
                   jax.ShapeDtypeStruct(s0.shape[1:], F32)],
        compiler_params=_cparams(("parallel",)),
        name="delta_sample",
    )(qkv, bg, z, nw, s0)


def _swa_prompt_kernel(q_ref, kc_ref, vc_ref, kp_ref, vp_ref, sink_ref, o_ref):
    n = pl.program_id(1)
    w = WINDOW
    nq = q_ref.shape[0] // w
    half = SWA_DH
    row = lax.broadcasted_iota(jnp.int32, (w, 2 * w), 0)
    col = lax.broadcasted_iota(jnp.int32, (w, 2 * w), 1)
    band = (col >= row) & (col <= row + w)
    bias_mid = jnp.where(band, 0.0, -jnp.inf)
    bias_first = jnp.where(band & (col >= jnp.where(n == 0, w, 0)), 0.0, -jnp.inf)
    low = lax.broadcasted_iota(jnp.int32, (2 * w, LANES), 1) < half
    low_q = lax.broadcasted_iota(jnp.int32, (w, LANES), 1) < half
    kall = jnp.concatenate([kp_ref[...], kc_ref[...]], axis=0)
    vall = jnp.concatenate([vp_ref[...], vc_ref[...]], axis=0)
    kall_r = pltpu.roll(kall, half, 1)
    vall_r = pltpu.roll(vall, half, 1)
    scale = SWA_DH ** -0.5

    units = [(i, j) for i in range(nq) for j in range(SWA_KV_HEADS)]
    scores, vvs = [], []
    for i, j in units:
        rs = slice(i * w, (i + 2) * w)
        ka, kb = (kall[rs], kall_r[rs]) if j == 0 else (kall_r[rs], kall[rs])
        va, vb = (vall[rs], vall_r[rs]) if j == 0 else (vall_r[rs], vall[rs])
        kk = jnp.concatenate([jnp.where(low, ka, 0.0), jnp.where(low, 0.0, kb)], axis=0).astype(BF16)
        vvs.append(jnp.concatenate(
            [jnp.concatenate([jnp.where(low, va, 0.0), jnp.where(low, 1.0, 0.0)], axis=1),
             jnp.concatenate([jnp.where(low, 0.0, vb), jnp.where(low, 0.0, 1.0)], axis=1)],
            axis=0).astype(BF16))
        c0 = j * SWA_GROUP * SWA_DH
        q = jnp.concatenate([q_ref[i * w:(i + 1) * w, c0:c0 + LANES],
                             q_ref[i * w:(i + 1) * w, c0 + LANES:c0 + 2 * LANES]], axis=0)
        scores.append(lax.dot_general((q * scale).astype(BF16), kk, (((1,), (1,)), ((), ())),
                                      preferred_element_type=F32))
    keys = [(u, pr, hf) for u in range(len(units)) for pr in range(2) for hf in range(2)]
    sinks = {key: sink_ref[0:1, units[key[0]][1] * SWA_GROUP + 2 * key[1] + key[2]:
                           units[key[0]][1] * SWA_GROUP + 2 * key[1] + key[2] + 1] for key in keys}
    masked = {}
    for u, pr, hf in keys:
        bias = bias_first if units[u][0] == 0 else bias_mid
        masked[u, pr, hf] = scores[u][pr * w:(pr + 1) * w, hf * 2 * w:(hf + 1) * 2 * w] + bias
    mx = {key: jnp.maximum(jnp.max(masked[key], axis=-1, keepdims=True), sinks[key]) for key in keys}
    p = {key: jnp.exp(masked[key] - mx[key]).astype(BF16) for key in keys}
    outs = []
    for u in range(len(units)):
        pu = jnp.concatenate([jnp.concatenate([p[u, pr, 0], p[u, pr, 1]], axis=1)
                              for pr in range(2)], axis=0)
        outs.append(jnp.dot(pu, vvs[u], preferred_element_type=F32))
    for u, (i, j) in enumerate(units):
        c0 = j * SWA_GROUP * SWA_DH
        for pr in range(2):
            ou = outs[u][pr * w:(pr + 1) * w]
            sink_term = jnp.where(low_q, jnp.exp(sinks[u, pr, 0] - mx[u, pr, 0]),
                                  jnp.exp(sinks[u, pr, 1] - mx[u, pr, 1]))
            o_ref[i * w:(i + 1) * w, c0 + pr * LANES:c0 + (pr + 1) * LANES] = (
                ou[:, :LANES] / (ou[:, LANES:] + sink_term))


def _swa_prompt(swa, sinks, nq):
    B, T, _ = swa.shape
    w = WINDOW
    rq = nq * w
    kcol = SWA_Q // SWA_KV
    prev = lambda n: jnp.maximum(n * nq - 1, 0)
    return pl.pallas_call(
        _swa_prompt_kernel,
        grid=(B, T // rq),
        in_specs=[pl.BlockSpec((None, rq, SWA_Q), lambda b, n: (b, n, 0)),
                  pl.BlockSpec((None, rq, SWA_KV), lambda b, n: (b, n, kcol)),
                  pl.BlockSpec((None, rq, SWA_KV), lambda b, n: (b, n, kcol + 1)),
                  pl.BlockSpec((None, w, SWA_KV), lambda b, n: (b, prev(n), kcol)),
                  pl.BlockSpec((None, w, SWA_KV), lambda b, n: (b, prev(n), kcol + 1)),
                  _const_spec((1, LANES))],
        out_specs=pl.BlockSpec((None, rq, SWA_Q), lambda b, n: (b, n, 0)),
        out_shape=jax.ShapeDtypeStruct((B, T, SWA_Q), F32),
        compiler_params=_cparams(("parallel", "parallel")),
        name="swa_prompt",
    )(swa, swa, swa, swa, swa, sinks)


def _swa_sample_kernel(swa_ref, kc_ref, vc_ref, sink_ref, o_ref, kn_ref, vn_ref):
    bb = swa_ref.shape[0]
    w = kc_ref.shape[1]
    scale = SWA_DH ** -0.5
    for i in range(bb):
        kn_ref[i, 0:w - 1, :] = kc_ref[i, 1:w, :]
        kn_ref[i, w - 1:w, :] = swa_ref[i:i + 1, SWA_Q:SWA_Q + SWA_KV]
        vn_ref[i, 0:w - 1, :] = vc_ref[i, 1:w, :]
        vn_ref[i, w - 1:w, :] = swa_ref[i:i + 1, SWA_Q + SWA_KV:]
    units = [(i, j) for i in range(bb) for j in range(SWA_KV_HEADS)]
    sinks = [jnp.concatenate(
        [sink_ref[0:1, j * SWA_GROUP + g:j * SWA_GROUP + g + 1] for g in range(SWA_GROUP)]
        + [jnp.zeros((SUBLANES - SWA_GROUP, 1), F32)], axis=0) for j in range(SWA_KV_HEADS)]
    qs, scs = [], []
    for i, j in units:
        q4 = jnp.concatenate(
            [swa_ref[i:i + 1, (j * SWA_GROUP + g) * SWA_DH:(j * SWA_GROUP + g + 1) * SWA_DH]
             for g in range(SWA_GROUP)] + [jnp.zeros((SUBLANES - SWA_GROUP, SWA_DH), F32)], axis=0)
        qs.append(q4)
        scs.append(lax.dot_general(q4, kc_ref[i, :, j * SWA_DH:(j + 1) * SWA_DH],
                                   (((1,), (1,)), ((), ())), preferred_element_type=F32) * scale)
    soft = []
    for (i, j), q4, sc in zip(units, qs, scs):
        knew = swa_ref[i:i + 1, SWA_Q + j * SWA_DH:SWA_Q + (j + 1) * SWA_DH]
        sn = jnp.sum(q4 * knew, axis=-1, keepdims=True) * scale
        mx = jnp.maximum(jnp.maximum(jnp.max(sc, axis=-1, keepdims=True), sn), sinks[j])
        pc = jnp.exp(sc - mx)
        pn = jnp.exp(sn - mx)
        den = jnp.sum(pc, axis=-1, keepdims=True) + pn + jnp.exp(sinks[j] - mx)
        soft.append((pc, pn, den))
    pvs = [jnp.dot(pc, vc_ref[i, :, j * SWA_DH:(j + 1) * SWA_DH], preferred_element_type=F32)
           for (i, j), (pc, pn, den) in zip(units, soft)]
    for (i, j), (pc, pn, den), pv in zip(units, soft, pvs):
        vnew = swa_ref[i:i + 1, SWA_Q + SWA_KV + j * SWA_DH:SWA_Q + SWA_KV + (j + 1) * SWA_DH]
        o = (pv + pn * vnew) / den
        for g in range(SWA_GROUP):
            hq = j * SWA_GROUP + g
            o_ref[i:i + 1, hq * SWA_DH:(hq + 1) * SWA_DH] = o[g:g + 1, :]


def _swa_sample(swa, kc, vc, sinks, bb):
    R = swa.shape[0]
    w = kc.shape[1]
    cache_spec = pl.BlockSpec((bb, w, SWA_KV), lambda i: (i, 0, 0))
    return pl.pallas_call(
        _swa_sample_kernel,
        grid=(R // bb,),
        in_specs=[pl.BlockSpec((bb, SWA_OUT), lambda i: (i, 0)), cache_spec, cache_spec,
                  _const_spec((1, LANES))],
        out_specs=[pl.BlockSpec((bb, SWA_Q), lambda i: (i, 0)), cache_spec, cache_spec],
        out_shape=[jax.ShapeDtypeStruct((R, SWA_Q), F32),
                   jax.ShapeDtypeStruct(kc.shape, F32),
                   jax.ShapeDtypeStruct(vc.shape, F32)],
        compiler_params=_cparams(("parallel",)),
        name="swa_sample",
    )(swa, kc, vc, sinks)


def _outproj_ffn_kernel(x_ref, dn_ref, sw_ref, g5_ref, sh_ref, sc_ref, ga_ref, g_ref,
                        woa_ref, wob_ref, wgu_ref, wd_ref, o_ref):
    mix = (jnp.dot(dn_ref[...].astype(BF16), woa_ref[...], preferred_element_type=F32)
           + jnp.dot(sw_ref[...].astype(BF16), wob_ref[...], preferred_element_type=F32))
    x = x_ref[...] + g5_ref[...] * mix
    o_ref[...] = _ffn_body(x, sh_ref[...], sc_ref[...], ga_ref[...], g_ref[...], wgu_ref, wd_ref)


def _outproj_ffn(x, dn, sw, m, g, woa, wob, wgu, wd, l, tm):
    G, R, _ = x.shape
    g5 = _mod_specs(m, 1, tm)[2]
    return pl.pallas_call(
        _outproj_ffn_kernel,
        grid=(G, R // tm),
        in_specs=[_row_spec(tm, D_MODEL), _row_spec(tm, DN_V), _row_spec(tm, SWA_Q), g5]
        + _mod_specs(m, 2, tm) + [
            _const_spec((1, D_MODEL)),
            _layer_spec((DN_V, D_MODEL), l),
            _layer_spec((SWA_Q, D_MODEL), l),
            _layer_spec((D_MODEL, 2 * D_FF), l),
            _layer_spec((D_FF, D_MODEL), l),
        ],
        out_specs=_row_spec(tm, D_MODEL),
        out_shape=jax.ShapeDtypeStruct(x.shape, F32),
        compiler_params=_cparams(("parallel", "parallel")),
        name="outproj_ffn",
    )(x, dn, sw, m, m, m, m, g, woa, wob, wgu, wd)


def _rope_tables(pos):
    half = SWA_DH // 2
    inv = jnp.power(ROPE_THETA, -jnp.arange(half, dtype=F32) * 2.0 / SWA_DH)
    ang = pos.astype(F32)[:, None] * inv[None, :]
    cos, sin = jnp.cos(ang), jnp.sin(ang)
    reps = LANES // SWA_DH
    return (jnp.concatenate([cos, cos] * reps, axis=1),
            jnp.concatenate([-sin, sin] * reps, axis=1))


def _group_matrix(width):
    i = jnp.arange(width) // SWA_DH
    return jnp.where(i[:, None] == i[None, :], 1.0 / SWA_DH, 0.0).astype(BF16)


def _pack_w_in(w_in):
    conv = w_in[..., :DN_CONV_CH]
    o = DN_CONV_CH
    z = w_in[..., o:o + DN_V]
    o += DN_V
    ba = w_in[..., o:o + 2 * DN_HEADS]
    o += 2 * DN_HEADS
    sq = w_in[..., o:o + SWA_Q]
    o += SWA_Q
    sk = w_in[..., o:o + SWA_KV]
    o += SWA_KV
    sv = w_in[..., o:o + SWA_KV]
    pad = jnp.zeros(w_in.shape[:-1] + (LANES - 2 * DN_HEADS,), w_in.dtype)
    return jnp.concatenate([conv, z, sq, sk, sv, ba, pad], axis=-1).astype(BF16)


def _lane_row(vec, offset=0):
    return jnp.zeros((LANES,), F32).at[offset:offset + vec.shape[0]].set(vec.astype(F32))


def kernel(x_prompt, x_sample, c_prompt, c_sample, state_dn_conv, state_dn_S, cache_swa_k, cache_swa_v, w_ada, b_ada, g_ffn1, w_ffn1_gu, w_ffn1_down, g_mix, w_in, dn_conv_w, dn_A_log, dn_dt_bias, dn_norm_w, swa_q_norm, swa_k_norm, swa_sinks, w_out, g_ffn2, w_ffn2_gu, w_ffn2_down):
    L = w_ada.shape[0]
    B, T, _ = x_prompt.shape
    Bs = x_sample.shape[0]
    tm = min(512, T)
    bb = min(8, Bs)

    m_p, m_s = _adaln(c_prompt, c_sample, w_ada, b_ada)
    m_p = m_p.reshape(L, N_MOD, B, 1, D_MODEL)
    m_s = m_s.reshape(L, N_MOD, 1, Bs, D_MODEL)

    cos_p, sin_p = _rope_tables(jnp.arange(T))
    cos_s, sin_s = _rope_tables(PAST_LEN + jnp.arange(1))
    gq, gk = _group_matrix(SWA_Q), _group_matrix(SWA_KV)

    wgu1, wd1 = w_ffn1_gu.astype(BF16), w_ffn1_down.astype(BF16)
    wgu2, wd2 = w_ffn2_gu.astype(BF16), w_ffn2_down.astype(BF16)
    win = _pack_w_in(w_in)
    woa, wob = w_out[:, :DN_V].astype(BF16), w_out[:, DN_V:].astype(BF16)

    yp = x_prompt
    ys = x_sample.reshape(1, Bs, D_MODEL)
    outs = [[] for _ in range(8)]
    for l in range(L):
        g1, gm, g2 = (g_ffn1[l].reshape(1, D_MODEL), g_mix[l].reshape(1, D_MODEL),
                      g_ffn2[l].reshape(1, D_MODEL))
        cw = dn_conv_w[l]
        nw = dn_norm_w[l].reshape(1, DN_DV)
        prm = jnp.stack([
            _lane_row(-jnp.exp(dn_A_log[l].astype(F32)), DN_HEADS),
            _lane_row(dn_dt_bias[l], DN_HEADS),
            jnp.concatenate([swa_q_norm[l]] * (LANES // SWA_DH)).astype(F32),
            jnp.concatenate([swa_k_norm[l]] * (LANES // SWA_DH)).astype(F32),
        ] + [jnp.zeros((LANES,), F32)] * (SUBLANES - 4))
        sinks = _lane_row(swa_sinks[l]).reshape(1, LANES)

        yp = _ffn(yp, m_p[l], 0, g1, wgu1, wd1, l, tm)
        qkv, z, bg, swa, tail = _inproj_prompt(yp, m_p[l], gm, win, cw, prm, cos_p, sin_p, gq, gk,
                                               l, min(256, T // 2))
        dn_o, s_new = _delta_prompt(qkv, bg, z, nw, min(512, T))
        sw_o = _swa_prompt(swa, sinks, min(4, T // WINDOW))
        yp = _outproj_ffn(yp, dn_o, sw_o, m_p[l], g2, woa, wob, wgu2, wd2, l, tm)
        nkeep = min(WINDOW, T)
        outs[0].append(tail[:, HALO - (DN_CONV - 1):])
        outs[1].append(s_new)
        outs[2].append(swa[:, T - nkeep:, SWA_Q:SWA_Q + SWA_KV].reshape(B, nkeep, SWA_KV_HEADS, SWA_DH))
        outs[3].append(swa[:, T - nkeep:, SWA_Q + SWA_KV:].reshape(B, nkeep, SWA_KV_HEADS, SWA_DH))

        ys = _ffn(ys, m_s[l], 0, g1, wgu1, wd1, l, Bs)
        buf = jnp.transpose(state_dn_conv[l], (1, 0, 2))
        qkv, z, bg, swa, cnew = _inproj_sample(ys, m_s[l], gm, win, cw, prm, cos_s, sin_s, gq, gk,
                                               buf, l)
        dn_o, s_new = _delta_sample(qkv[0], bg[0], z[0], nw, state_dn_S, l, bb)
        w0 = cache_swa_k.shape[2]
        sw_o, k_new, v_new = _swa_sample(swa[0], cache_swa_k[l].reshape(Bs, w0, SWA_KV),
                                         cache_swa_v[l].reshape(Bs, w0, SWA_KV), sinks, bb)
        ys = _outproj_ffn(ys, dn_o[None], sw_o[None], m_s[l], g2, woa, wob, wgu2, wd2, l, Bs)
        outs[4].append(jnp.transpose(cnew, (1, 0, 2)))
        outs[5].append(s_new)
        outs[6].append(k_new.reshape(Bs, w0, SWA_KV_HEADS, SWA_DH))
        outs[7].append(v_new.reshape(Bs, w0, SWA_KV_HEADS, SWA_DH))

    return (yp, ys.reshape(Bs, 1, D_MODEL)) + tuple(jnp.stack(o) for o in outs)
```

```python
import functools
import math

import jax
import jax.numpy as jnp
from jax import lax
from jax.experimental import pallas as pl
from jax.experimental.pallas import tpu as pltpu

F32 = jnp.float32
BF16 = jnp.bfloat16

D_MODEL = 1024
D_FF = 2816
N_MOD = 9
EPS = 1e-6
DN_HEADS = 4
DN_DK = 128
DN_DV = 128
DN_QK = DN_HEADS * DN_DK
DN_V = DN_HEADS * DN_DV
DN_CONV = 4
DN_CONV_CH = 2 * DN_QK + DN_V
DN_CHUNK = 64
INV_BLOCK = 16
INTRA_ROWS = 256
SWA_DH = 64
SWA_HEADS = 8
SWA_KV_HEADS = 2
SWA_GROUP = SWA_HEADS // SWA_KV_HEADS
SWA_Q = SWA_HEADS * SWA_DH
SWA_KV = SWA_KV_HEADS * SWA_DH
WINDOW = 128
ROPE_THETA = 10000.0
PAST_LEN = 16384

LANES = 128
SUBLANES = 8
HALO = SUBLANES

C_CONV = 0
C_Z = C_CONV + DN_CONV_CH
C_SQ = C_Z + DN_V
C_SK = C_SQ + SWA_Q
C_SV = C_SK + SWA_KV
C_BG = C_SV + SWA_KV
IN_PACKED = C_BG + LANES
SWA_OUT = SWA_Q + 2 * SWA_KV

VMEM_LIMIT = 56 * 1024 * 1024


def _cparams(sem):
    return pltpu.CompilerParams(dimension_semantics=sem, vmem_limit_bytes=VMEM_LIMIT)


def _const_spec(shape):
    nd = len(shape)
    return pl.BlockSpec(shape, lambda *_: (0,) * nd, pipeline_mode=pl.Buffered(1))


def _layer_spec(shape, l):
    nd = len(shape)
    return pl.BlockSpec((None,) + tuple(shape), lambda *_: (l,) + (0,) * nd,
                        pipeline_mode=pl.Buffered(1))


def _sigmoid(x):
    return 1.0 / (1.0 + jnp.exp(-x))


def _silu(x):
    return x * _sigmoid(x)


def _adaln_kernel(cp_ref, cs_ref, w_ref, b_ref, mp_ref, ms_ref):
    w = w_ref[...].astype(BF16)
    b = b_ref[...]
    for c_ref, m_ref in ((cp_ref, mp_ref), (cs_ref, ms_ref)):
        c = c_ref[...]
        m_ref[...] = jnp.dot(_silu(c).astype(BF16), w, preferred_element_type=F32) + b


def _adaln(c_p, c_s, w_ada, b_ada):
    L = w_ada.shape[0]
    bp, bs = c_p.shape[0], c_s.shape[0]
    b4 = b_ada.reshape(L, N_MOD, 1, D_MODEL)
    return pl.pallas_call(
        _adaln_kernel,
        grid=(L, N_MOD),
        in_specs=[
            pl.BlockSpec((bp, D_MODEL), lambda l, j: (0, 0)),
            pl.BlockSpec((bs, D_MODEL), lambda l, j: (0, 0)),
            pl.BlockSpec((None, D_MODEL, D_MODEL), lambda l, j: (l, 0, j)),
            pl.BlockSpec((None, None, 1, D_MODEL), lambda l, j: (l, j, 0, 0)),
        ],
        out_specs=[
            pl.BlockSpec((None, None, bp, D_MODEL), lambda l, j: (l, j, 0, 0)),
            pl.BlockSpec((None, None, bs, D_MODEL), lambda l, j: (l, j, 0, 0)),
        ],
        out_shape=[
            jax.ShapeDtypeStruct((L, N_MOD, bp, D_MODEL), F32),
            jax.ShapeDtypeStruct((L, N_MOD, bs, D_MODEL), F32),
        ],
        compiler_params=_cparams(("parallel", "parallel")),
        name="adaln",
    )(c_p, c_s, w_ada, b4)


def _mod_norm(x, g, shift, scale):
    ms = jnp.mean(x * x, axis=-1, keepdims=True)
    y = x * lax.rsqrt(ms + EPS) * g
    return y * (1.0 + scale) + shift


def _ffn_body(x, shift, scale, gate, g, wgu_ref, wd_ref):
    h = _mod_norm(x, g, shift, scale).astype(BF16)
    gu = jnp.dot(h, wgu_ref[...], preferred_element_type=F32)
    a = gu[:, :D_FF]
    b = gu[:, D_FF:]
    act = (_silu(a) * b).astype(BF16)
    y = jnp.dot(act, wd_ref[...], preferred_element_type=F32)
    return x + (0.5 * gate) * y


def _mod_specs(m, sub, tm):
    rm = m.shape[2]
    blk = 1 if rm == 1 else tm

    def spec(i):
        if rm == 1:
            return pl.BlockSpec((None, None, 1, D_MODEL), lambda g, r: (i, g, 0, 0))
        return pl.BlockSpec((None, None, blk, D_MODEL), lambda g, r: (i, g, r, 0))

    return [spec(3 * sub), spec(3 * sub + 1), spec(3 * sub + 2)]


def _row_spec(tm, width):
    return pl.BlockSpec((None, tm, width), lambda g, r: (g, r, 0))


def _ffn_kernel(x_ref, sh_ref, sc_ref, ga_ref, g_ref, wgu_ref, wd_ref, o_ref):
    o_ref[...] = _ffn_body(x_ref[...], sh_ref[...], sc_ref[...], ga_ref[...], g_ref[...],
                           wgu_ref, wd_ref)


def _ffn(x, m, sub, g, wgu, wd, l, tm):
    G, R, _ = x.shape
    return pl.pallas_call(
        _ffn_kernel,
        grid=(G, R // tm),
        in_specs=[_row_spec(tm, D_MODEL)] + _mod_specs(m, sub, tm) + [
            _const_spec((1, D_MODEL)),
            _layer_spec((D_MODEL, 2 * D_FF), l),
            _layer_spec((D_FF, D_MODEL), l),
        ],
        out_specs=_row_spec(tm, D_MODEL),
        out_shape=jax.ShapeDtypeStruct(x.shape, F32),
        compiler_params=_cparams(("parallel", "parallel")),
        name="ffn",
    )(x, m, m, m, g, wgu, wd)


def _group_mean_sq(x, gmat_ref):
    sq = x * x
    hi = sq.astype(BF16)
    lo = (sq - hi.astype(F32)).astype(BF16)
    gm = gmat_ref[...]
    return (jnp.dot(hi, gm, preferred_element_type=F32)
            + jnp.dot(lo, gm, preferred_element_type=F32))


def _rope(x, cos, sin_signed):
    width = x.shape[-1]
    reps = width // LANES
    lane = lax.broadcasted_iota(jnp.int32, x.shape, 1)
    first_half = (lane & (SWA_DH - 1)) < (SWA_DH // 2)
    partner = jnp.where(first_half,
                        pltpu.roll(x, width - SWA_DH // 2, 1),
                        pltpu.roll(x, SWA_DH // 2, 1))
    if reps > 1:
        cos = jnp.concatenate([cos] * reps, axis=1)
        sin_signed = jnp.concatenate([sin_signed] * reps, axis=1)
    return x * cos + partner * sin_signed


def _swa_post(p, prm_ref, cos, sin, gq_ref, gk_ref, swa_ref, orows=slice(None)):
    pq = p[:, C_SQ:C_SQ + SWA_Q]
    pk = p[:, C_SK:C_SK + SWA_KV]
    wq = jnp.concatenate([prm_ref[2:3, :]] * (SWA_Q // LANES), axis=1)
    wk = prm_ref[3:4, :]
    qn = pq * lax.rsqrt(_group_mean_sq(pq, gq_ref) + EPS) * wq
    kn = pk * lax.rsqrt(_group_mean_sq(pk, gk_ref) + EPS) * wk
    swa_ref[orows, :SWA_Q] = _rope(qn, cos, sin)
    swa_ref[orows, SWA_Q:SWA_Q + SWA_KV] = _rope(kn, cos, sin)
    swa_ref[orows, SWA_Q + SWA_KV:] = p[:, C_SV:C_SV + SWA_KV]


def _dn_post(p, qkv_conv, prm_ref, qkv_ref, z_ref, bg_ref, orows=slice(None)):
    qkv = _silu(qkv_conv)
    for hh in range(2 * DN_HEADS):
        xh = qkv[:, hh * DN_DK:(hh + 1) * DN_DK]
        s = lax.rsqrt(jnp.sum(xh * xh, axis=-1, keepdims=True) + EPS)
        if hh < DN_HEADS:
            s = s * (DN_DK ** -0.5)
        qkv_ref[orows, hh * DN_DK:(hh + 1) * DN_DK] = xh * s
    qkv_ref[orows, 2 * DN_QK:] = qkv[:, 2 * DN_QK:]
    z_ref[orows, :] = p[:, C_Z:C_Z + DN_V].astype(z_ref.dtype)

    pb = p[:, C_BG:C_BG + LANES]
    neg_a = prm_ref[0:1, :]
    dt_b = prm_ref[1:2, :]
    xa = pb + dt_b
    softplus = jnp.maximum(xa, 0.0) + jnp.log(1.0 + jnp.exp(-jnp.abs(xa)))
    lane = lax.broadcasted_iota(jnp.int32, pb.shape, 1)
    bg_ref[orows, :] = jnp.where(lane < DN_HEADS, _sigmoid(pb), neg_a * softplus)


def _inproj_prompt_kernel(nt, x_ref, sh_ref, sc_ref, g_ref, win_ref, cw_ref, prm_ref, cos_ref,
                          sin_ref, gq_ref, gk_ref, qkv_ref, z_ref, bg_ref, swa_ref, tail_ref,
                          h0_ref, h1_ref, p0_ref, p1_ref):
    i = pl.program_id(0)
    tm = h0_ref.shape[0]

    @pl.when(i == 0)
    def _():
        for ref in (h0_ref, h1_ref, p0_ref, p1_ref):
            ref[...] = jnp.zeros_like(ref)

    def finish_conv(p_ref, orows):
        y = p_ref[HALO:, :DN_CONV_CH] * cw_ref[DN_CONV - 1:DN_CONV, :]
        for j in range(DN_CONV - 1):
            off = HALO - (DN_CONV - 1) + j
            y = y + p_ref[off:off + tm, :DN_CONV_CH] * cw_ref[j:j + 1, :]
        _dn_post(p_ref.at[HALO:], y, prm_ref, qkv_ref, z_ref, bg_ref, orows)

    g, sh, sc = g_ref[...], sh_ref[...], sc_ref[...]
    halves = (slice(0, tm), slice(tm, 2 * tm))
    carry0 = p0_ref[tm:tm + HALO, :DN_CONV_CH]
    p1_ref[0:HALO, :DN_CONV_CH] = carry0
    _swa_post(p0_ref.at[HALO:], prm_ref, cos_ref[halves[0], :], sin_ref[halves[0], :],
              gq_ref, gk_ref, swa_ref, halves[0])
    h0_ref[...] = _mod_norm(x_ref[halves[0], :], g, sh, sc).astype(BF16)
    p1_ref[HALO:, :] = jnp.dot(h1_ref[...], win_ref[...], preferred_element_type=F32)
    finish_conv(p0_ref, halves[0])
    carry1 = p1_ref[tm:tm + HALO, :DN_CONV_CH]
    p0_ref[0:HALO, :DN_CONV_CH] = jnp.where((2 * i) % nt == 0, 0.0, carry1)
    _swa_post(p1_ref.at[HALO:], prm_ref, cos_ref[halves[1], :], sin_ref[halves[1], :],
              gq_ref, gk_ref, swa_ref, halves[1])
    h1_ref[...] = _mod_norm(x_ref[halves[1], :], g, sh, sc).astype(BF16)
    p0_ref[HALO:, :] = jnp.dot(h0_ref[...], win_ref[...], preferred_element_type=F32)
    finish_conv(p1_ref, halves[1])
    tail_ref[...] = carry1


def _inproj_sample_kernel(x_ref, sh_ref, sc_ref, g_ref, win_ref, cw_ref, prm_ref, cos_ref, sin_ref,
                          gq_ref, gk_ref, buf_ref, qkv_ref, z_ref, bg_ref, swa_ref, cnew_ref):
    h = _mod_norm(x_ref[...], g_ref[...], sh_ref[...], sc_ref[...]).astype(BF16)
    p = jnp.dot(h, win_ref[...], preferred_element_type=F32)
    pc = p[:, :DN_CONV_CH]
    y = pc * cw_ref[DN_CONV - 1:DN_CONV, :]
    for j in range(DN_CONV - 1):
        y = y + buf_ref[j] * cw_ref[j:j + 1, :]
    for j in range(DN_CONV - 2):
        cnew_ref[j] = buf_ref[j + 1]
    cnew_ref[DN_CONV - 2] = pc
    _swa_post(p, prm_ref, cos_ref[...], sin_ref[...], gq_ref, gk_ref, swa_ref)
    _dn_post(p, y, prm_ref, qkv_ref, z_ref, bg_ref)


def _inproj_common_specs(m, tm, l):
    sh, sc, _ = _mod_specs(m, 1, tm)
    return [_row_spec(tm, D_MODEL), sh, sc,
            _const_spec((1, D_MODEL)),
            _layer_spec((D_MODEL, IN_PACKED), l),
            _const_spec((DN_CONV, DN_CONV_CH)),
            _const_spec((SUBLANES, LANES))]


def _inproj_prompt(x, m, g, win, cw, prm, cos, sin, gq, gk, l, tm):
    B, T, _ = x.shape
    nt = T // tm
    assert nt % 2 == 0
    npair = nt // 2
    n = B * npair
    src = lambda s: jnp.minimum(s, n - 1)
    dst = lambda s: jnp.maximum(s - 1, 0)
    mod = lambda i: pl.BlockSpec((None, None, 1, D_MODEL), lambda s: (i, src(s) // npair, 0, 0))
    out_rows = lambda width: pl.BlockSpec((None, 2 * tm, width),
                                          lambda s: (dst(s) // npair, dst(s) % npair, 0))
    table = pl.BlockSpec((2 * tm, LANES), lambda s: (dst(s) % npair, 0))
    return pl.pallas_call(
        functools.partial(_inproj_prompt_kernel, nt),
        grid=(n + 1,),
        in_specs=[
            pl.BlockSpec((None, 2 * tm, D_MODEL), lambda s: (src(s) // npair, src(s) % npair, 0)),
            mod(3), mod(4),
            _const_spec((1, D_MODEL)),
            _layer_spec((D_MODEL, IN_PACKED), l),
            _const_spec((DN_CONV, DN_CONV_CH)),
            _const_spec((SUBLANES, LANES)),
            table, table,
            _const_spec((SWA_Q, SWA_Q)),
            _const_spec((SWA_KV, SWA_KV)),
        ],
        out_specs=[
            out_rows(DN_CONV_CH), out_rows(DN_V), out_rows(LANES), out_rows(SWA_OUT),
            pl.BlockSpec((None, HALO, DN_CONV_CH), lambda s: (dst(s) // npair, 0, 0)),
        ],
        out_shape=[
            jax.ShapeDtypeStruct((B, T, DN_CONV_CH), F32),
            jax.ShapeDtypeStruct((B, T, DN_V), BF16),
            jax.ShapeDtypeStruct((B, T, LANES), F32),
            jax.ShapeDtypeStruct((B, T, SWA_OUT), F32),
            jax.ShapeDtypeStruct((B, HALO, DN_CONV_CH), F32),
        ],
        scratch_shapes=[pltpu.VMEM((tm, D_MODEL), BF16), pltpu.VMEM((tm, D_MODEL), BF16),
                        pltpu.VMEM((HALO + tm, IN_PACKED), F32),
                        pltpu.VMEM((HALO + tm, IN_PACKED), F32)],
        compiler_params=_cparams(("arbitrary",)),
        name="inproj_prompt",
    )(x, m, m, g, win, cw, prm, cos, sin, gq, gk)


def _inproj_sample(x, m, g, win, cw, prm, cos, sin, gq, gk, buf, l):
    _, R, _ = x.shape
    nbuf = DN_CONV - 1
    return pl.pallas_call(
        _inproj_sample_kernel,
        grid=(1, 1),
        in_specs=_inproj_common_specs(m, R, l) + [
            _const_spec((1, LANES)), _const_spec((1, LANES)),
            _const_spec((SWA_Q, SWA_Q)), _const_spec((SWA_KV, SWA_KV)),
            _const_spec((nbuf, R, DN_CONV_CH)),
        ],
        out_specs=[
            _row_spec(R, DN_CONV_CH), _row_spec(R, DN_V), _row_spec(R, LANES), _row_spec(R, SWA_OUT),
            pl.BlockSpec((nbuf, R, DN_CONV_CH), lambda g_, r: (0, 0, 0)),
        ],
        out_shape=[
            jax.ShapeDtypeStruct((1, R, DN_CONV_CH), F32),
            jax.ShapeDtypeStruct((1, R, DN_V), F32),
            jax.ShapeDtypeStruct((1, R, LANES), F32),
            jax.ShapeDtypeStruct((1, R, SWA_OUT), F32),
            jax.ShapeDtypeStruct((nbuf, R, DN_CONV_CH), F32),
        ],
        compiler_params=_cparams(("arbitrary", "arbitrary")),
        name="inproj_sample",
    )(x, m, m, g, win, cw, prm, cos, sin, gq, gk, buf)


def _gated_norm(o, z, nw):
    on = o * lax.rsqrt(jnp.mean(o * o, axis=-1, keepdims=True) + EPS) * nw
    return on * _silu(z.astype(F32))


def _mm(a, b):
    return jnp.dot(a.astype(BF16), b.astype(BF16), preferred_element_type=F32)


def _wide_to_blockdiag(wide):
    c, n = wide.shape[0], wide.shape[1] // wide.shape[0]
    cblk = lax.broadcasted_iota(jnp.int32, wide.shape, 1) // c
    return jnp.concatenate([jnp.where(cblk == i, wide, 0.0) for i in range(n)], axis=0)


def _blockdiag_to_wide(full, c):
    n = full.shape[0] // c
    cblk = lax.broadcasted_iota(jnp.int32, (c, full.shape[1]), 1) // c
    wide = full[0:c]
    for i in range(1, n):
        wide = jnp.where(cblk == i, full[i * c:(i + 1) * c], wide)
    return wide


def _unit_lower_inverses(mats):
    c = mats[0].shape[0]
    row = lax.broadcasted_iota(jnp.int32, mats[0].shape, 0)
    col = lax.broadcasted_iota(jnp.int32, mats[0].shape, 1) % c
    s = INV_BLOCK
    diag = (row // s) == (col // s)
    pws = [jnp.where(diag, a, 0.0) for a in mats]
    eye = jnp.where(row == col, 1.0, 0.0)
    xs = [eye - pw for pw in pws]
    k = 2
    while k < s:
        pws = [_mm(pw, _wide_to_blockdiag(pw)) for pw in pws]
        xs = [x + _mm(x, _wide_to_blockdiag(pw)) for x, pw in zip(xs, pws)]
        k *= 2
    while s < c:
        pair = ((row // (2 * s)) == (col // (2 * s))) & ((row // s) != (col // s))
        ts = [_mm(x, _wide_to_blockdiag(jnp.where(pair, a, 0.0))) for x, a in zip(xs, mats)]
        xs = [x - _mm(t, _wide_to_blockdiag(x)) for x, t in zip(xs, ts)]
        s *= 2
    return xs


def _cumsum_rows(x):
    n = x.shape[0]
    row = lax.broadcasted_iota(jnp.int32, x.shape, 0)
    s = 1
    while s < n:
        x = x + jnp.where(row >= s, pltpu.roll(x, s, 0), 0.0)
        s *= 2
    return x


WQ_ROWS = 2 * DN_CHUNK
QKD_ROWS = DN_CHUNK + DN_DK


def _delta_intra_kernel(q_ref, k_ref, v_ref, bg_ref, u_ref, wq_ref, qkd_ref, eg_ref):
    c = DN_CHUNK
    r = INTRA_ROWS
    nc = r // c
    row = lax.broadcasted_iota(jnp.int32, (c, r), 0)
    col = lax.broadcasted_iota(jnp.int32, (c, r), 1) % c
    incl = row >= col
    strict = row > col

    jobs = [(blk, h) for blk in range(q_ref.shape[0] // r) for h in range(DN_HEADS)]
    gcs, egcs, ekds, gcts, bgs = [], [], [], [], []
    for blk in range(q_ref.shape[0] // r):
        bg = bg_ref[blk * r:(blk + 1) * r, :]
        gc = jnp.concatenate([_cumsum_rows(bg[ci * c:(ci + 1) * c, :]) for ci in range(nc)], axis=0)
        g_end = [gc[(ci + 1) * c - 1:(ci + 1) * c, :] for ci in range(nc)]
        glast = jnp.concatenate([jnp.broadcast_to(g, (c, LANES)) for g in g_end], axis=0)
        for ci in range(nc):
            rows = [jnp.broadcast_to(jnp.exp(g_end[ci][:, DN_HEADS + h:DN_HEADS + h + 1]), (1, LANES))
                    for h in range(DN_HEADS)]
            eg_ref[blk * nc + ci] = jnp.concatenate(
                rows + [jnp.zeros((SUBLANES - DN_HEADS, LANES), F32)], axis=0)
        bgs.append(bg)
        gcs.append(gc)
        gcts.append(gc.T)
        egcs.append(jnp.exp(gc))
        ekds.append(jnp.exp(glast - gc))

    prep = []
    for blk, h in jobs:
        rs = slice(blk * r, (blk + 1) * r)
        sl = slice(h * DN_DK, (h + 1) * DN_DK)
        q = q_ref[rs, sl]
        k = k_ref[rs, sl]
        kb = k * bgs[blk][:, h:h + 1]
        kq = lax.dot_general(jnp.concatenate([kb, q], axis=0).astype(BF16), k.astype(BF16),
                             (((1,), (1,)), ((), ())), preferred_element_type=F32)
        prep.append((q, k, kb, kq))
    a_mats, qks = [], []
    for (blk, h), (q, k, kb, kq) in zip(jobs, prep):
        gl = slice(DN_HEADS + h, DN_HEADS + h + 1)
        gcol = jnp.concatenate([jnp.broadcast_to(gcs[blk][ci * c:(ci + 1) * c, gl], (c, c))
                                for ci in range(nc)], axis=1)
        decay = jnp.exp(jnp.where(incl, gcol - gcts[blk][gl, :], -jnp.inf))
        a_mats.append(jnp.where(strict, _blockdiag_to_wide(kq[:r], c) * decay, 0.0))
        qks.append(_blockdiag_to_wide(kq[r:], c) * decay)
    tinvs = _unit_lower_inverses(a_mats)
    uws = []
    for (blk, h), (q, k, kb, kq), tinv in zip(jobs, prep, tinvs):
        rs = slice(blk * r, (blk + 1) * r)
        sl = slice(h * DN_DK, (h + 1) * DN_DK)
        gl = slice(DN_HEADS + h, DN_HEADS + h + 1)
        rhs = jnp.concatenate([v_ref[rs, sl] * bgs[blk][:, h:h + 1], kb * egcs[blk][:, gl]], axis=1)
        uws.append(_mm(_wide_to_blockdiag(tinv), rhs))
    for (blk, h), (q, k, kb, kq), uw, qk in zip(jobs, prep, uws, qks):
        rs = slice(blk * r, (blk + 1) * r)
        sl = slice(h * DN_DK, (h + 1) * DN_DK)
        gl = slice(DN_HEADS + h, DN_HEADS + h + 1)
        u_ref[rs, sl] = uw[:, :DN_DV]
        w_b = uw[:, DN_DV:].astype(BF16)
        qd_b = (q * egcs[blk][:, gl]).astype(BF16)
        for ci in range(nc):
            wq_ref[blk * nc + ci, h * WQ_ROWS:h * WQ_ROWS + c, :] = w_b[ci * c:(ci + 1) * c, :]
            wq_ref[blk * nc + ci, h * WQ_ROWS + c:(h + 1) * WQ_ROWS, :] = qd_b[ci * c:(ci + 1) * c, :]
        kdt = (k * ekds[blk][:, gl]).T
        for pj in range(nc // 2):
            ls = slice(pj * LANES, (pj + 1) * LANES)
            pi = blk * (nc // 2) + pj
            qkd_ref[pi, h * QKD_ROWS:h * QKD_ROWS + c, :] = qk[:, ls].astype(BF16)
            qkd_ref[pi, h * QKD_ROWS + c:(h + 1) * QKD_ROWS, :] = kdt[:, ls].astype(BF16)


def _delta_scan_kernel(u_ref, wq_ref, qkd_ref, eg_ref, z_ref, nw_ref, o_ref, sout_ref, s_ref):
    n = pl.program_id(0)
    c = DN_CHUNK

    @pl.when(n == 0)
    def _():
        s_ref[...] = jnp.zeros_like(s_ref)

    nw = nw_ref[...]
    zeros = jnp.zeros((c, DN_DV), BF16)
    chains = [(b, h) for b in range(u_ref.shape[0]) for h in range(DN_HEADS)]
    for j in range(2):
        rs = slice(j * c, (j + 1) * c)
        ws = [jnp.dot(wq_ref[b, j, h * WQ_ROWS:(h + 1) * WQ_ROWS, :], s_ref[b, h].astype(BF16),
                      preferred_element_type=F32) for b, h in chains]
        rr = []
        for (b, h), w_h in zip(chains, ws):
            v_new = (u_ref[b, rs, h * DN_DV:(h + 1) * DN_DV] - w_h[:c]).astype(BF16)
            rhs = jnp.concatenate([v_new, zeros] if j == 0 else [zeros, v_new], axis=0)
            rr.append(jnp.dot(qkd_ref[b, h * QKD_ROWS:(h + 1) * QKD_ROWS, :], rhs,
                              preferred_element_type=F32))
        for (b, h), w_h, r_h in zip(chains, ws, rr):
            sl = slice(h * DN_DV, (h + 1) * DN_DV)
            s_ref[b, h] = s_ref[b, h] * eg_ref[b, j, h:h + 1, :] + r_h[c:]
            o_ref[b, rs, sl] = _gated_norm(w_h[c:] + r_h[:c], z_ref[b, rs, sl], nw).astype(o_ref.dtype)

    @pl.when(n == pl.num_programs(0) - 1)
    def _():
        sout_ref[...] = s_ref[...]


def _delta_prompt(qkv, bg, z, nw, ra):
    B, T, _ = qkv.shape
    c = DN_CHUNK
    n = T // c
    cpb = ra // c
    qkv_spec = lambda j: pl.BlockSpec((None, ra, DN_QK), lambda b, r: (b, r, j))
    u, wq, qkd, eg = pl.pallas_call(
        _delta_intra_kernel,
        grid=(B, T // ra),
        in_specs=[qkv_spec(0), qkv_spec(1), qkv_spec(2),
                  pl.BlockSpec((None, ra, LANES), lambda b, r: (b, r, 0))],
        out_specs=[pl.BlockSpec((None, ra, DN_V), lambda b, r: (b, r, 0)),
                   pl.BlockSpec((None, cpb, DN_HEADS * WQ_ROWS, DN_DK), lambda b, r: (b, r, 0, 0)),
                   pl.BlockSpec((None, cpb // 2, DN_HEADS * QKD_ROWS, LANES),
                                lambda b, r: (b, r, 0, 0)),
                   pl.BlockSpec((None, cpb, SUBLANES, LANES), lambda b, r: (b, r, 0, 0))],
        out_shape=[jax.ShapeDtypeStruct((B, T, DN_V), F32),
                   jax.ShapeDtypeStruct((B, n, DN_HEADS * WQ_ROWS, DN_DK), BF16),
                   jax.ShapeDtypeStruct((B, n // 2, DN_HEADS * QKD_ROWS, LANES), BF16),
                   jax.ShapeDtypeStruct((B, n, SUBLANES, LANES), F32)],
        compiler_params=_cparams(("parallel", "parallel")),
        name="delta_intra",
    )(qkv, qkv, qkv, bg)
    return pl.pallas_call(
        _delta_scan_kernel,
        grid=(n // 2,),
        in_specs=[pl.BlockSpec((B, 2 * c, DN_V), lambda i: (0, i, 0)),
                  pl.BlockSpec((B, 2, DN_HEADS * WQ_ROWS, DN_DK), lambda i: (0, i, 0, 0)),
                  pl.BlockSpec((B, None, DN_HEADS * QKD_ROWS, LANES), lambda i: (0, i, 0, 0)),
                  pl.BlockSpec((B, 2, SUBLANES, LANES), lambda i: (0, i, 0, 0)),
                  pl.BlockSpec((B, 2 * c, DN_V), lambda i: (0, i, 0)),
                  _const_spec((1, DN_DV))],
        out_specs=[pl.BlockSpec((B, 2 * c, DN_V), lambda i: (0, i, 0)),
                   pl.BlockSpec((B, DN_HEADS, DN_DK, DN_DV), lambda i: (0, 0, 0, 0))],
        out_shape=[jax.ShapeDtypeStruct((B, T, DN_V), BF16),
                   jax.ShapeDtypeStruct((B, DN_HEADS, DN_DK, DN_DV), F32)],
        scratch_shapes=[pltpu.VMEM((B, DN_HEADS, DN_DK, DN_DV), F32)],
        compiler_params=_cparams(("arbitrary",)),
        name="delta_scan",
    )(u, wq, qkd, eg, z, nw)


def _delta_sample_kernel(qkv_ref, bg_ref, z_ref, nw_ref, s0_ref, o_ref, s_ref):
    bb = qkv_ref.shape[0]
    nw = nw_ref[...]
    zeros = jnp.zeros((SUBLANES - 2, DN_DK), F32)
    zeros7 = jnp.zeros((SUBLANES - 1, DN_DK), F32)
    chains = [(i, h) for i in range(bb) for h in range(DN_HEADS)]
    pre = []
    for i, h in chains:
        q = qkv_ref[i:i + 1, h * DN_DK:(h + 1) * DN_DK]
        k = qkv_ref[i:i + 1, DN_QK + h * DN_DK:DN_QK + (h + 1) * DN_DK]
        v = qkv_ref[i:i + 1, 2 * DN_QK + h * DN_DV:2 * DN_QK + (h + 1) * DN_DV]
        beta = bg_ref[i:i + 1, h:h + 1]
        eg = jnp.exp(bg_ref[i:i + 1, DN_HEADS + h:DN_HEADS + h + 1])
        lhs = jnp.concatenate([k * beta * eg, q * eg, zeros], axis=0)
        ws = jnp.dot(lhs, s0_ref[i, h], preferred_element_type=F32)
        pre.append((q, k, v, beta, eg, ws))
    upd = []
    for (i, h), (q, k, v, beta, eg, ws) in zip(chains, pre):
        v_new = v * beta - ws[0:1]
        kcol = jnp.concatenate([k, zeros7], axis=0).T
        vrow = jnp.concatenate([v_new, zeros7], axis=0)
        upd.append((v_new, jnp.dot(kcol, vrow, preferred_element_type=F32)))
    for (i, h), (q, k, v, beta, eg, ws), (v_new, kv) in zip(chains, pre, upd):
        sl = slice(h * DN_DK, (h + 1) * DN_DK)
        s_ref[i, h] = s0_ref[i, h] * eg + kv
        o = ws[1:2] + jnp.sum(q * k, axis=-1, keepdims=True) * v_new
        o_ref[i:i + 1, sl] = _gated_norm(o, z_ref[i:i + 1, sl], nw)


def _delta_sample(qkv, bg, z, nw, s0, l, bb):
    R = qkv.shape[0]
    return pl.pallas_call(
        _delta_sample_kernel,
        grid=(R // bb,),
        in_specs=[pl.BlockSpec((bb, DN_CONV_CH), lambda i: (i, 0)),
                  pl.BlockSpec((bb, LANES), lambda i: (i, 0)),
                  pl.BlockSpec((bb, DN_V), lambda i: (i, 0)),
                  _const_spec((1, DN_DV)),
                  pl.BlockSpec((None, bb, DN_HEADS, DN_DK, DN_DV), lambda i: (l, i, 0, 0, 0))],
        out_specs=[pl.BlockSpec((bb, DN_V), lambda i: (i, 0)),
                   pl.BlockSpec((bb, DN_HEADS, DN_DK, DN_DV), lambda i: (i, 0, 0, 0))],
        out_shape=[jax.ShapeDtypeStruct((R, DN_V), F32),
                   jax.ShapeDtypeStruct(s0.shape[1:], F32)],
        compiler_params=_cparams(("parallel",)),
        name="delta_sample",
    )(qkv, bg, z, nw, s0)


def _swa_prompt_kernel(q_ref, kc_ref, vc_ref, kp_ref, vp_ref, sink_ref, o_ref):
    n = pl.program_id(1)
    w = WINDOW
    nq = q_ref.shape[0] // w
    half = SWA_DH
    row = lax.broadcasted_iota(jnp.int32, (w, 2 * w), 0)
    col = lax.broadcasted_iota(jnp.int32, (w, 2 * w), 1)
    band = (col >= row) & (col <= row + w)
    bias_mid = jnp.where(band, 0.0, -jnp.inf)
    bias_first = jnp.where(band & (col >= jnp.where(n == 0, w, 0)), 0.0, -jnp.inf)
    low = lax.broadcasted_iota(jnp.int32, (2 * w, LANES), 1) < half
    low_q = lax.broadcasted_iota(jnp.int32, (w, LANES), 1) < half
    kall = jnp.concatenate([kp_ref[...], kc_ref[...]], axis=0)
    vall = jnp.concatenate([vp_ref[...], vc_ref[...]], axis=0)
    kall_r = pltpu.roll(kall, half, 1)
    vall_r = pltpu.roll(vall, half, 1)
    scale = SWA_DH ** -0.5

    units = [(i, j) for i in range(nq) for j in range(SWA_KV_HEADS)]
    scores, vvs = [], []
    for i, j in units:
        rs = slice(i * w, (i + 2) * w)
        ka, kb = (kall[rs], kall_r[rs]) if j == 0 else (kall_r[rs], kall[rs])
        va, vb = (vall[rs], vall_r[rs]) if j == 0 else (vall_r[rs], vall[rs])
        kk = jnp.concatenate([jnp.where(low, ka, 0.0), jnp.where(low, 0.0, kb)], axis=0).astype(BF16)
        vvs.append(jnp.concatenate(
            [jnp.concatenate([jnp.where(low, va, 0.0), jnp.where(low, 1.0, 0.0)], axis=1),
             jnp.concatenate([jnp.where(low, 0.0, vb), jnp.where(low, 0.0, 1.0)], axis=1)],
            axis=0).astype(BF16))
        c0 = j * SWA_GROUP * SWA_DH
        q = jnp.concatenate([q_ref[i * w:(i + 1) * w, c0:c0 + LANES],
                             q_ref[i * w:(i + 1) * w, c0 + LANES:c0 + 2 * LANES]], axis=0)
        scores.append(lax.dot_general((q * scale).astype(BF16), kk, (((1,), (1,)), ((), ())),
                                      preferred_element_type=F32))
    keys = [(u, pr, hf) for u in range(len(units)) for pr in range(2) for hf in range(2)]
    sinks = {key: sink_ref[0:1, units[key[0]][1] * SWA_GROUP + 2 * key[1] + key[2]:
                           units[key[0]][1] * SWA_GROUP + 2 * key[1] + key[2] + 1] for key in keys}
    masked = {}
    for u, pr, hf in keys:
        bias = bias_first if units[u][0] == 0 else bias_mid
        masked[u, pr, hf] = scores[u][pr * w:(pr + 1) * w, hf * 2 * w:(hf + 1) * 2 * w] + bias
    mx = {key: jnp.maximum(jnp.max(masked[key], axis=-1, keepdims=True), sinks[key]) for key in keys}
    p = {key: jnp.exp(masked[key] - mx[key]).astype(BF16) for key in keys}
    outs = []
    for u in range(len(units)):
        pu = jnp.concatenate([jnp.concatenate([p[u, pr, 0], p[u, pr, 1]], axis=1)
                              for pr in range(2)], axis=0)
        outs.append(jnp.dot(pu, vvs[u], preferred_element_type=F32))
    for u, (i, j) in enumerate(units):
        c0 = j * SWA_GROUP * SWA_DH
        for pr in range(2):
            ou = outs[u][pr * w:(pr + 1) * w]
            sink_term = jnp.where(low_q, jnp.exp(sinks[u, pr, 0] - mx[u, pr, 0]),
                                  jnp.exp(sinks[u, pr, 1] - mx[u, pr, 1]))
            o_ref[i * w:(i + 1) * w, c0 + pr * LANES:c0 + (pr + 1) * LANES] = (
                ou[:, :LANES] / (ou[:, LANES:] + sink_term)).astype(o_ref.dtype)


def _swa_prompt(swa, sinks, nq):
    B, T, _ = swa.shape
    w = WINDOW
    rq = nq * w
    kcol = SWA_Q // SWA_KV
    prev = lambda n: jnp.maximum(n * nq - 1, 0)
    return pl.pallas_call(
        _swa_prompt_kernel,
        grid=(B, T // rq),
        in_specs=[pl.BlockSpec((None, rq, SWA_Q), lambda b, n: (b, n, 0)),
                  pl.BlockSpec((None, rq, SWA_KV), lambda b, n: (b, n, kcol)),
                  pl.BlockSpec((None, rq, SWA_KV), lambda b, n: (b, n, kcol + 1)),
                  pl.BlockSpec((None, w, SWA_KV), lambda b, n: (b, prev(n), kcol)),
                  pl.BlockSpec((None, w, SWA_KV), lambda b, n: (b, prev(n), kcol + 1)),
                  _const_spec((1, LANES))],
        out_specs=pl.BlockSpec((None, rq, SWA_Q), lambda b, n: (b, n, 0)),
        out_shape=jax.ShapeDtypeStruct((B, T, SWA_Q), BF16),
        compiler_params=_cparams(("parallel", "parallel")),
        name="swa_prompt",
    )(swa, swa, swa, swa, swa, sinks)


def _swa_sample_kernel(swa_ref, kc_ref, vc_ref, sink_ref, o_ref, kn_ref, vn_ref):
    bb = swa_ref.shape[0]
    w = kc_ref.shape[1]
    scale = SWA_DH ** -0.5
    for i in range(bb):
        kn_ref[i, 0:w - 1, :] = kc_ref[i, 1:w, :]
        kn_ref[i, w - 1:w, :] = swa_ref[i:i + 1, SWA_Q:SWA_Q + SWA_KV]
        vn_ref[i, 0:w - 1, :] = vc_ref[i, 1:w, :]
        vn_ref[i, w - 1:w, :] = swa_ref[i:i + 1, SWA_Q + SWA_KV:]
    units = [(i, j) for i in range(bb) for j in range(SWA_KV_HEADS)]
    sinks = [jnp.concatenate(
        [sink_ref[0:1, j * SWA_GROUP + g:j * SWA_GROUP + g + 1] for g in range(SWA_GROUP)]
        + [jnp.zeros((SUBLANES - SWA_GROUP, 1), F32)], axis=0) for j in range(SWA_KV_HEADS)]
    qs, scs = [], []
    for i, j in units:
        q4 = jnp.concatenate(
            [swa_ref[i:i + 1, (j * SWA_GROUP + g) * SWA_DH:(j * SWA_GROUP + g + 1) * SWA_DH]
             for g in range(SWA_GROUP)] + [jnp.zeros((SUBLANES - SWA_GROUP, SWA_DH), F32)], axis=0)
        qs.append(q4)
        scs.append(lax.dot_general(q4, kc_ref[i, :, j * SWA_DH:(j + 1) * SWA_DH],
                                   (((1,), (1,)), ((), ())), preferred_element_type=F32) * scale)
    soft = []
    for (i, j), q4, sc in zip(units, qs, scs):
        knew = swa_ref[i:i + 1, SWA_Q + j * SWA_DH:SWA_Q + (j + 1) * SWA_DH]
        sn = jnp.sum(q4 * knew, axis=-1, keepdims=True) * scale
        mx = jnp.maximum(jnp.maximum(jnp.max(sc, axis=-1, keepdims=True), sn), sinks[j])
        pc = jnp.exp(sc - mx)
        pn = jnp.exp(sn - mx)
        den = jnp.sum(pc, axis=-1, keepdims=True) + pn + jnp.exp(sinks[j] - mx)
        soft.append((pc, pn, den))
    pvs = [jnp.dot(pc, vc_ref[i, :, j * SWA_DH:(j + 1) * SWA_DH], preferred_element_type=F32)
           for (i, j), (pc, pn, den) in zip(units, soft)]
    for (i, j), (pc, pn, den), pv in zip(units, soft, pvs):
        vnew = swa_ref[i:i + 1, SWA_Q + SWA_KV + j * SWA_DH:SWA_Q + SWA_KV + (j + 1) * SWA_DH]
        o = (pv + pn * vnew) / den
        for g in range(SWA_GROUP):
            hq = j * SWA_GROUP + g
            o_ref[i:i + 1, hq * SWA_DH:(hq + 1) * SWA_DH] = o[g:g + 1, :]


def _swa_sample(swa, kc, vc, sinks, bb):
    R = swa.shape[0]
    w = kc.shape[1]
    cache_spec = pl.BlockSpec((bb, w, SWA_KV), lambda i: (i, 0, 0))
    return pl.pallas_call(
        _swa_sample_kernel,
        grid=(R // bb,),
        in_specs=[pl.BlockSpec((bb, SWA_OUT), lambda i: (i, 0)), cache_spec, cache_spec,
                  _const_spec((1, LANES))],
        out_specs=[pl.BlockSpec((bb, SWA_Q), lambda i: (i, 0)), cache_spec, cache_spec],
        out_shape=[jax.ShapeDtypeStruct((R, SWA_Q), F32),
                   jax.ShapeDtypeStruct(kc.shape, F32),
                   jax.ShapeDtypeStruct(vc.shape, F32)],
        compiler_params=_cparams(("parallel",)),
        name="swa_sample",
    )(swa, kc, vc, sinks)


def _outproj_ffn_kernel(x_ref, dn_ref, sw_ref, g5_ref, sh_ref, sc_ref, ga_ref, g_ref,
                        woa_ref, wob_ref, wgu_ref, wd_ref, o_ref):
    mix = (jnp.dot(dn_ref[...].astype(BF16), woa_ref[...], preferred_element_type=F32)
           + jnp.dot(sw_ref[...].astype(BF16), wob_ref[...], preferred_element_type=F32))
    x = x_ref[...] + g5_ref[...] * mix
    o_ref[...] = _ffn_body(x, sh_ref[...], sc_ref[...], ga_ref[...], g_ref[...], wgu_ref, wd_ref)


def _outproj_ffn(x, dn, sw, m, g, woa, wob, wgu, wd, l, tm):
    G, R, _ = x.shape
    g5 = _mod_specs(m, 1, tm)[2]
    return pl.pallas_call(
        _outproj_ffn_kernel,
        grid=(G, R // tm),
        in_specs=[_row_spec(tm, D_MODEL), _row_spec(tm, DN_V), _row_spec(tm, SWA_Q), g5]
        + _mod_specs(m, 2, tm) + [
            _const_spec((1, D_MODEL)),
            _layer_spec((DN_V, D_MODEL), l),
            _layer_spec((SWA_Q, D_MODEL), l),
            _layer_spec((D_MODEL, 2 * D_FF), l),
            _layer_spec((D_FF, D_MODEL), l),
        ],
        out_specs=_row_spec(tm, D_MODEL),
        out_shape=jax.ShapeDtypeStruct(x.shape, F32),
        compiler_params=_cparams(("parallel", "parallel")),
        name="outproj_ffn",
    )(x, dn, sw, m, m, m, m, g, woa, wob, wgu, wd)


def _rope_tables(pos):
    half = SWA_DH // 2
    inv = jnp.power(ROPE_THETA, -jnp.arange(half, dtype=F32) * 2.0 / SWA_DH)
    ang = pos.astype(F32)[:, None] * inv[None, :]
    cos, sin = jnp.cos(ang), jnp.sin(ang)
    reps = LANES // SWA_DH
    return (jnp.concatenate([cos, cos] * reps, axis=1),
            jnp.concatenate([-sin, sin] * reps, axis=1))


def _group_matrix(width):
    i = jnp.arange(width) // SWA_DH
    return jnp.where(i[:, None] == i[None, :], 1.0 / SWA_DH, 0.0).astype(BF16)


def _pack_w_in(w_in):
    conv = w_in[..., :DN_CONV_CH]
    o = DN_CONV_CH
    z = w_in[..., o:o + DN_V]
    o += DN_V
    ba = w_in[..., o:o + 2 * DN_HEADS]
    o += 2 * DN_HEADS
    sq = w_in[..., o:o + SWA_Q]
    o += SWA_Q
    sk = w_in[..., o:o + SWA_KV]
    o += SWA_KV
    sv = w_in[..., o:o + SWA_KV]
    pad = jnp.zeros(w_in.shape[:-1] + (LANES - 2 * DN_HEADS,), w_in.dtype)
    return jnp.concatenate([conv, z, sq, sk, sv, ba, pad], axis=-1).astype(BF16)


def _lane_row(vec, offset=0):
    return jnp.zeros((LANES,), F32).at[offset:offset + vec.shape[0]].set(vec.astype(F32))


def kernel(x_prompt, x_sample, c_prompt, c_sample, state_dn_conv, state_dn_S, cache_swa_k, cache_swa_v, w_ada, b_ada, g_ffn1, w_ffn1_gu, w_ffn1_down, g_mix, w_in, dn_conv_w, dn_A_log, dn_dt_bias, dn_norm_w, swa_q_norm, swa_k_norm, swa_sinks, w_out, g_ffn2, w_ffn2_gu, w_ffn2_down):
    L = w_ada.shape[0]
    B, T, _ = x_prompt.shape
    Bs = x_sample.shape[0]
    tm = min(512, T)
    bb = min(8, Bs)

    m_p, m_s = _adaln(c_prompt, c_sample, w_ada, b_ada)
    m_p = m_p.reshape(L, N_MOD, B, 1, D_MODEL)
    m_s = m_s.reshape(L, N_MOD, 1, Bs, D_MODEL)

    cos_p, sin_p = _rope_tables(jnp.arange(T))
    cos_s, sin_s = _rope_tables(PAST_LEN + jnp.arange(1))
    gq, gk = _group_matrix(SWA_Q), _group_matrix(SWA_KV)

    wgu1, wd1 = w_ffn1_gu.astype(BF16), w_ffn1_down.astype(BF16)
    wgu2, wd2 = w_ffn2_gu.astype(BF16), w_ffn2_down.astype(BF16)
    win = _pack_w_in(w_in)
    woa, wob = w_out[:, :DN_V].astype(BF16), w_out[:, DN_V:].astype(BF16)

    yp = x_prompt
    ys = x_sample.reshape(1, Bs, D_MODEL)
    outs = [[] for _ in range(8)]
    for l in range(L):
        g1, gm, g2 = (g_ffn1[l].reshape(1, D_MODEL), g_mix[l].reshape(1, D_MODEL),
                      g_ffn2[l].reshape(1, D_MODEL))
        cw = dn_conv_w[l]
        nw = dn_norm_w[l].reshape(1, DN_DV)
        prm = jnp.stack([
            _lane_row(-jnp.exp(dn_A_log[l].astype(F32)), DN_HEADS),
            _lane_row(dn_dt_bias[l], DN_HEADS),
            jnp.concatenate([swa_q_norm[l]] * (LANES // SWA_DH)).astype(F32),
            jnp.concatenate([swa_k_norm[l]] * (LANES // SWA_DH)).astype(F32),
        ] + [jnp.zeros((LANES,), F32)] * (SUBLANES - 4))
        sinks = _lane_row(swa_sinks[l]).reshape(1, LANES)

        yp = _ffn(yp, m_p[l], 0, g1, wgu1, wd1, l, tm)
        qkv, z, bg, swa, tail = _inproj_prompt(yp, m_p[l], gm, win, cw, prm, cos_p, sin_p, gq, gk,
                                               l, min(256, T // 2))
        dn_o, s_new = _delta_prompt(qkv, bg, z, nw, min(512, T))
        sw_o = _swa_prompt(swa, sinks, min(4, T // WINDOW))
        yp = _outproj_ffn(yp, dn_o, sw_o, m_p[l], g2, woa, wob, wgu2, wd2, l, tm)
        nkeep = min(WINDOW, T)
        outs[0].append(tail[:, HALO - (DN_CONV - 1):])
        outs[1].append(s_new)
        outs[2].append(swa[:, T - nkeep:, SWA_Q:SWA_Q + SWA_KV].reshape(B, nkeep, SWA_KV_HEADS, SWA_DH))
        outs[3].append(swa[:, T - nkeep:, SWA_Q + SWA_KV:].reshape(B, nkeep, SWA_KV_HEADS, SWA_DH))

        ys = _ffn(ys, m_s[l], 0, g1, wgu1, wd1, l, Bs)
        buf = jnp.transpose(state_dn_conv[l], (1, 0, 2))
        qkv, z, bg, swa, cnew = _inproj_sample(ys, m_s[l], gm, win, cw, prm, cos_s, sin_s, gq, gk,
                                               buf, l)
        dn_o, s_new = _delta_sample(qkv[0], bg[0], z[0], nw, state_dn_S, l, bb)
        w0 = cache_swa_k.shape[2]
        sw_o, k_new, v_new = _swa_sample(swa[0], cache_swa_k[l].reshape(Bs, w0, SWA_KV),
                                         cache_swa_v[l].reshape(Bs, w0, SWA_KV), sinks, bb)
        ys = _outproj_ffn(ys, dn_o[None], sw_o[None], m_s[l], g2, woa, wob, wgu2, wd2, l, Bs)
        outs[4].append(jnp.transpose(cnew, (1, 0, 2)))
        outs[5].append(s_new)
        outs[6].append(k_new.reshape(Bs, w0, SWA_KV_HEADS, SWA_DH))
        outs[7].append(v_new.reshape(Bs, w0, SWA_KV_HEADS, SWA_DH))

    return (yp, ys.reshape(Bs, 1, D_MODEL)) + tuple(jnp.stack(o) for o in outs)
```

```python
import functools
import math

import jax
import jax.numpy as jnp
from jax import lax
from jax.experimental import pallas as pl
from jax.experimental.pallas import tpu as pltpu

F32 = jnp.float32
BF16 = jnp.bfloat16

D_MODEL = 1024
D_FF = 2816
N_MOD = 9
EPS = 1e-6
DN_HEADS = 4
DN_DK = 128
DN_DV = 128
DN_QK = DN_HEADS * DN_DK
DN_V = DN_HEADS * DN_DV
DN_CONV = 4
DN_CONV_CH = 2 * DN_QK + DN_V
DN_CHUNK = 64
INV_BLOCK = 16
INTRA_ROWS = 256
SWA_DH = 64
SWA_HEADS = 8
SWA_KV_HEADS = 2
SWA_GROUP = SWA_HEADS // SWA_KV_HEADS
SWA_Q = SWA_HEADS * SWA_DH
SWA_KV = SWA_KV_HEADS * SWA_DH
WINDOW = 128
ROPE_THETA = 10000.0
PAST_LEN = 16384

LANES = 128
SUBLANES = 8
HALO = SUBLANES

C_CONV = 0
C_Z = C_CONV + DN_CONV_CH
C_SQ = C_Z + DN_V
C_SK = C_SQ + SWA_Q
C_SV = C_SK + SWA_KV
C_BG = C_SV + SWA_KV
IN_PACKED = C_BG + LANES
SWA_OUT = SWA_Q + 2 * SWA_KV

VMEM_LIMIT = 56 * 1024 * 1024


def _cparams(sem):
    return pltpu.CompilerParams(dimension_semantics=sem, vmem_limit_bytes=VMEM_LIMIT)


def _const_spec(shape):
    nd = len(shape)
    return pl.BlockSpec(shape, lambda *_: (0,) * nd, pipeline_mode=pl.Buffered(1))


def _layer_spec(shape, l):
    nd = len(shape)
    return pl.BlockSpec((None,) + tuple(shape), lambda *_: (l,) + (0,) * nd,
                        pipeline_mode=pl.Buffered(1))


def _sigmoid(x):
    return 1.0 / (1.0 + jnp.exp(-x))


def _silu(x):
    return x * _sigmoid(x)


def _adaln_kernel(cp_ref, cs_ref, w_ref, b_ref, mp_ref, ms_ref):
    w = w_ref[...].astype(BF16)
    b = b_ref[...]
    for c_ref, m_ref in ((cp_ref, mp_ref), (cs_ref, ms_ref)):
        c = c_ref[...]
        m_ref[...] = jnp.dot(_silu(c).astype(BF16), w, preferred_element_type=F32) + b


def _adaln(c_p, c_s, w_ada, b_ada):
    L = w_ada.shape[0]
    bp, bs = c_p.shape[0], c_s.shape[0]
    b4 = b_ada.reshape(L, N_MOD, 1, D_MODEL)
    return pl.pallas_call(
        _adaln_kernel,
        grid=(L, N_MOD),
        in_specs=[
            pl.BlockSpec((bp, D_MODEL), lambda l, j: (0, 0)),
            pl.BlockSpec((bs, D_MODEL), lambda l, j: (0, 0)),
            pl.BlockSpec((None, D_MODEL, D_MODEL), lambda l, j: (l, 0, j)),
            pl.BlockSpec((None, None, 1, D_MODEL), lambda l, j: (l, j, 0, 0)),
        ],
        out_specs=[
            pl.BlockSpec((None, None, bp, D_MODEL), lambda l, j: (l, j, 0, 0)),
            pl.BlockSpec((None, None, bs, D_MODEL), lambda l, j: (l, j, 0, 0)),
        ],
        out_shape=[
            jax.ShapeDtypeStruct((L, N_MOD, bp, D_MODEL), F32),
            jax.ShapeDtypeStruct((L, N_MOD, bs, D_MODEL), F32),
        ],
        compiler_params=_cparams(("parallel", "parallel")),
        name="adaln",
    )(c_p, c_s, w_ada, b4)


def _mod_norm(x, g, shift, scale):
    ms = jnp.mean(x * x, axis=-1, keepdims=True)
    return (x * lax.rsqrt(ms + EPS)) * (g * (1.0 + scale)) + shift


def _ffn_body(x, shift, scale, gate, g, wgu_ref, wd_ref):
    h = _mod_norm(x, g, shift, scale).astype(BF16)
    gu = jnp.dot(h, wgu_ref[...], preferred_element_type=F32)
    a = gu[:, :D_FF]
    b = gu[:, D_FF:]
    act = (_silu(a) * b).astype(BF16)
    y = jnp.dot(act, wd_ref[...], preferred_element_type=F32)
    return x + (0.5 * gate) * y


def _mod_specs(m, sub, tm):
    rm = m.shape[2]
    blk = 1 if rm == 1 else tm

    def spec(i):
        if rm == 1:
            return pl.BlockSpec((None, None, 1, D_MODEL), lambda g, r: (i, g, 0, 0))
        return pl.BlockSpec((None, None, blk, D_MODEL), lambda g, r: (i, g, r, 0))

    return [spec(3 * sub), spec(3 * sub + 1), spec(3 * sub + 2)]


def _row_spec(tm, width):
    return pl.BlockSpec((None, tm, width), lambda g, r: (g, r, 0))


def _ffn_kernel(x_ref, sh_ref, sc_ref, ga_ref, g_ref, wgu_ref, wd_ref, o_ref):
    o_ref[...] = _ffn_body(x_ref[...], sh_ref[...], sc_ref[...], ga_ref[...], g_ref[...],
                           wgu_ref, wd_ref)


def _ffn(x, m, sub, g, wgu, wd, l, tm):
    G, R, _ = x.shape
    return pl.pallas_call(
        _ffn_kernel,
        grid=(G, R // tm),
        in_specs=[_row_spec(tm, D_MODEL)] + _mod_specs(m, sub, tm) + [
            _const_spec((1, D_MODEL)),
            _layer_spec((D_MODEL, 2 * D_FF), l),
            _layer_spec((D_FF, D_MODEL), l),
        ],
        out_specs=_row_spec(tm, D_MODEL),
        out_shape=jax.ShapeDtypeStruct(x.shape, F32),
        compiler_params=_cparams(("parallel", "parallel")),
        name="ffn",
    )(x, m, m, m, g, wgu, wd)


def _group_mean_sq(x, gmat_ref):
    sq = x * x
    hi = sq.astype(BF16)
    lo = (sq - hi.astype(F32)).astype(BF16)
    gm = gmat_ref[...]
    return (jnp.dot(hi, gm, preferred_element_type=F32)
            + jnp.dot(lo, gm, preferred_element_type=F32))


def _rope(x, cos, sin_signed):
    width = x.shape[-1]
    reps = width // LANES
    lane = lax.broadcasted_iota(jnp.int32, x.shape, 1)
    first_half = (lane & (SWA_DH - 1)) < (SWA_DH // 2)
    partner = jnp.where(first_half,
                        pltpu.roll(x, width - SWA_DH // 2, 1),
                        pltpu.roll(x, SWA_DH // 2, 1))
    if reps > 1:
        cos = jnp.concatenate([cos] * reps, axis=1)
        sin_signed = jnp.concatenate([sin_signed] * reps, axis=1)
    return x * cos + partner * sin_signed


def _swa_post(p, prm_ref, cos, sin, gq_ref, gk_ref, swa_ref, orows=slice(None)):
    pq = p[:, C_SQ:C_SQ + SWA_Q]
    pk = p[:, C_SK:C_SK + SWA_KV]
    wq = jnp.concatenate([prm_ref[2:3, :]] * (SWA_Q // LANES), axis=1)
    wk = prm_ref[3:4, :]
    qn = pq * lax.rsqrt(_group_mean_sq(pq, gq_ref) + EPS) * wq
    kn = pk * lax.rsqrt(_group_mean_sq(pk, gk_ref) + EPS) * wk
    swa_ref[orows, :SWA_Q] = _rope(qn, cos, sin)
    swa_ref[orows, SWA_Q:SWA_Q + SWA_KV] = _rope(kn, cos, sin)
    swa_ref[orows, SWA_Q + SWA_KV:] = p[:, C_SV:C_SV + SWA_KV]


def _dn_group(a, gi, qkv_ref):
    for hh in range(DN_HEADS):
        cols = slice(gi * DN_QK + hh * DN_DK, gi * DN_QK + (hh + 1) * DN_DK)
        xh = a[:, hh * DN_DK:(hh + 1) * DN_DK]
        if gi < 2:
            s = lax.rsqrt(jnp.sum(xh * xh, axis=-1, keepdims=True) + EPS)
            if gi == 0:
                s = s * (DN_DK ** -0.5)
            xh = xh * s
        qkv_ref[:, cols] = xh.astype(qkv_ref.dtype)


def _bg_post(pb, prm_ref, bg_ref):
    neg_a = prm_ref[0:1, :]
    dt_b = prm_ref[1:2, :]
    xa = pb + dt_b
    softplus = jnp.maximum(xa, 0.0) + jnp.log(1.0 + jnp.exp(-jnp.abs(xa)))
    lane = lax.broadcasted_iota(jnp.int32, pb.shape, 1)
    bg_ref[...] = jnp.where(lane < DN_HEADS, _sigmoid(pb), neg_a * softplus)


def _dn_post(p, qkv_conv, prm_ref, qkv_ref, z_ref, bg_ref):
    qkv = _silu(qkv_conv)
    for gi in range(3):
        _dn_group(qkv[:, gi * DN_QK:(gi + 1) * DN_QK], gi, qkv_ref)
    z_ref[...] = p[:, C_Z:C_Z + DN_V].astype(z_ref.dtype)
    _bg_post(p[:, C_BG:C_BG + LANES], prm_ref, bg_ref)


class _Cols:
    def __init__(self, segments):
        self.segments = segments

    def __getitem__(self, idx):
        cols = idx[1]
        for start, arr in self.segments:
            if start <= cols.start and cols.stop <= start + arr.shape[1]:
                return arr[:, cols.start - start:cols.stop - start]
        raise KeyError(cols)


def _inproj_prompt_kernel(x_ref, sh_ref, sc_ref, g_ref, win_ref, cw_ref, prm_ref, cos_ref, sin_ref,
                          gq_ref, gk_ref, qkv_ref, z_ref, bg_ref, swa_ref, tail_ref, ext_ref):
    tm = x_ref.shape[0]
    first = pl.program_id(1) == 0

    @pl.when(first)
    def _():
        ext_ref[0:HALO, :] = jnp.zeros((HALO, DN_CONV_CH), F32)

    @pl.when(jnp.logical_not(first))
    def _():
        ext_ref[0:HALO, :] = ext_ref[tm:tm + HALO, :]

    h = _mod_norm(x_ref[...], g_ref[...], sh_ref[...], sc_ref[...]).astype(BF16)
    proj = lambda a, b: jnp.dot(h, win_ref[:, a:b], preferred_element_type=F32)
    p_att = _Cols([(C_SQ, proj(C_SQ, C_SK)), (C_SK, proj(C_SK, IN_PACKED))])
    _swa_post(p_att, prm_ref, cos_ref[...], sin_ref[...], gq_ref, gk_ref, swa_ref)
    _bg_post(p_att[:, C_BG:C_BG + LANES], prm_ref, bg_ref)
    for gi in range(3):
        cs = slice(gi * DN_QK, (gi + 1) * DN_QK)
        pc = proj(cs.start, cs.stop)
        ext_ref[HALO:, cs] = pc
        y = pc * cw_ref[DN_CONV - 1:DN_CONV, cs]
        for j in range(DN_CONV - 1):
            off = HALO - (DN_CONV - 1) + j
            y = y + ext_ref[off:off + tm, cs] * cw_ref[j:j + 1, cs]
        _dn_group(_silu(y), gi, qkv_ref)
    tail_ref[...] = ext_ref[tm:tm + HALO, :]
    z_ref[...] = proj(C_Z, C_Z + DN_V).astype(z_ref.dtype)


def _inproj_sample_kernel(x_ref, sh_ref, sc_ref, g_ref, win_ref, cw_ref, prm_ref, cos_ref, sin_ref,
                          gq_ref, gk_ref, buf_ref, qkv_ref, z_ref, bg_ref, swa_ref, cnew_ref):
    h = _mod_norm(x_ref[...], g_ref[...], sh_ref[...], sc_ref[...]).astype(BF16)
    p = jnp.dot(h, win_ref[...], preferred_element_type=F32)
    pc = p[:, :DN_CONV_CH]
    y = pc * cw_ref[DN_CONV - 1:DN_CONV, :]
    for j in range(DN_CONV - 1):
        y = y + buf_ref[j] * cw_ref[j:j + 1, :]
    for j in range(DN_CONV - 2):
        cnew_ref[j] = buf_ref[j + 1]
    cnew_ref[DN_CONV - 2] = pc
    _swa_post(p, prm_ref, cos_ref[...], sin_ref[...], gq_ref, gk_ref, swa_ref)
    _dn_post(p, y, prm_ref, qkv_ref, z_ref, bg_ref)


def _inproj_common_specs(m, tm, l):
    sh, sc, _ = _mod_specs(m, 1, tm)
    return [_row_spec(tm, D_MODEL), sh, sc,
            _const_spec((1, D_MODEL)),
            _layer_spec((D_MODEL, IN_PACKED), l),
            _const_spec((DN_CONV, DN_CONV_CH)),
            _const_spec((SUBLANES, LANES))]


def _inproj_prompt(x, m, g, win, cw, prm, cos, sin, gq, gk, l, tm):
    B, T, _ = x.shape
    table = pl.BlockSpec((tm, LANES), lambda b, r: (r, 0))
    return pl.pallas_call(
        _inproj_prompt_kernel,
        grid=(B, T // tm),
        in_specs=_inproj_common_specs(m, tm, l) + [
            table, table,
            _const_spec((SWA_Q, SWA_Q)),
            _const_spec((SWA_KV, SWA_KV)),
        ],
        out_specs=[
            _row_spec(tm, DN_CONV_CH), _row_spec(tm, DN_V), _row_spec(tm, LANES),
            _row_spec(tm, SWA_OUT),
            pl.BlockSpec((None, HALO, DN_CONV_CH), lambda b, r: (b, 0, 0)),
        ],
        out_shape=[
            jax.ShapeDtypeStruct((B, T, DN_CONV_CH), BF16),
            jax.ShapeDtypeStruct((B, T, DN_V), BF16),
            jax.ShapeDtypeStruct((B, T, LANES), F32),
            jax.ShapeDtypeStruct((B, T, SWA_OUT), F32),
            jax.ShapeDtypeStruct((B, HALO, DN_CONV_CH), F32),
        ],
        scratch_shapes=[pltpu.VMEM((HALO + tm, DN_CONV_CH), F32)],
        compiler_params=_cparams(("parallel", "arbitrary")),
        name="inproj_prompt",
    )(x, m, m, g, win, cw, prm, cos, sin, gq, gk)


def _inproj_sample(x, m, g, win, cw, prm, cos, sin, gq, gk, buf, l):
    _, R, _ = x.shape
    nbuf = DN_CONV - 1
    return pl.pallas_call(
        _inproj_sample_kernel,
        grid=(1, 1),
        in_specs=_inproj_common_specs(m, R, l) + [
            _const_spec((1, LANES)), _const_spec((1, LANES)),
            _const_spec((SWA_Q, SWA_Q)), _const_spec((SWA_KV, SWA_KV)),
            _const_spec((nbuf, R, DN_CONV_CH)),
        ],
        out_specs=[
            _row_spec(R, DN_CONV_CH), _row_spec(R, DN_V), _row_spec(R, LANES), _row_spec(R, SWA_OUT),
            pl.BlockSpec((nbuf, R, DN_CONV_CH), lambda g_, r: (0, 0, 0)),
        ],
        out_shape=[
            jax.ShapeDtypeStruct((1, R, DN_CONV_CH), F32),
            jax.ShapeDtypeStruct((1, R, DN_V), F32),
            jax.ShapeDtypeStruct((1, R, LANES), F32),
            jax.ShapeDtypeStruct((1, R, SWA_OUT), F32),
            jax.ShapeDtypeStruct((nbuf, R, DN_CONV_CH), F32),
        ],
        compiler_params=_cparams(("arbitrary", "arbitrary")),
        name="inproj_sample",
    )(x, m, m, g, win, cw, prm, cos, sin, gq, gk, buf)


def _gated_norm(o, z, nw):
    on = o * lax.rsqrt(jnp.mean(o * o, axis=-1, keepdims=True) + EPS) * nw
    return on * _silu(z.astype(F32))


def _mm(a, b):
    return jnp.dot(a.astype(BF16), b.astype(BF16), preferred_element_type=F32)


def _wide_to_blockdiag(wide):
    c, n = wide.shape[0], wide.shape[1] // wide.shape[0]
    cblk = lax.broadcasted_iota(jnp.int32, wide.shape, 1) // c
    return jnp.concatenate([jnp.where(cblk == i, wide, 0.0) for i in range(n)], axis=0)


def _blockdiag_to_wide(full, c):
    n = full.shape[0] // c
    cblk = lax.broadcasted_iota(jnp.int32, (c, full.shape[1]), 1) // c
    wide = full[0:c]
    for i in range(1, n):
        wide = jnp.where(cblk == i, full[i * c:(i + 1) * c], wide)
    return wide


def _unit_lower_inverses(mats):
    c = mats[0].shape[0]
    row = lax.broadcasted_iota(jnp.int32, mats[0].shape, 0)
    col = lax.broadcasted_iota(jnp.int32, mats[0].shape, 1) % c
    s = INV_BLOCK
    diag = (row // s) == (col // s)
    pws = [jnp.where(diag, a, 0.0) for a in mats]
    eye = jnp.where(row == col, 1.0, 0.0)
    xs = [eye - pw for pw in pws]
    k = 2
    while k < s:
        pws = [_mm(pw, _wide_to_blockdiag(pw)) for pw in pws]
        xs = [x + _mm(x, _wide_to_blockdiag(pw)) for x, pw in zip(xs, pws)]
        k *= 2
    while s < c:
        pair = ((row // (2 * s)) == (col // (2 * s))) & ((row // s) != (col // s))
        ts = [_mm(x, _wide_to_blockdiag(jnp.where(pair, a, 0.0))) for x, a in zip(xs, mats)]
        xs = [x - _mm(t, _wide_to_blockdiag(x)) for x, t in zip(xs, ts)]
        s *= 2
    return xs


def _cumsum_rows(x):
    n = x.shape[0]
    row = lax.broadcasted_iota(jnp.int32, x.shape, 0)
    s = 1
    while s < n:
        x = x + jnp.where(row >= s, pltpu.roll(x, s, 0), 0.0)
        s *= 2
    return x


WQ_ROWS = 2 * DN_CHUNK
QKD_ROWS = DN_CHUNK + DN_DK


def _delta_intra_kernel(q_ref, k_ref, v_ref, bg_ref, u_ref, wq_ref, qkd_ref, eg_ref):
    c = DN_CHUNK
    r = INTRA_ROWS
    nc = r // c
    row = lax.broadcasted_iota(jnp.int32, (c, r), 0)
    col = lax.broadcasted_iota(jnp.int32, (c, r), 1) % c
    incl = row >= col
    strict = row > col

    jobs = [(blk, h) for blk in range(q_ref.shape[0] // r) for h in range(DN_HEADS)]
    gcs, egcs, ekds, gcts, bgs = [], [], [], [], []
    for blk in range(q_ref.shape[0] // r):
        bg = bg_ref[blk * r:(blk + 1) * r, :]
        gc = jnp.concatenate([_cumsum_rows(bg[ci * c:(ci + 1) * c, :]) for ci in range(nc)], axis=0)
        g_end = [gc[(ci + 1) * c - 1:(ci + 1) * c, :] for ci in range(nc)]
        glast = jnp.concatenate([jnp.broadcast_to(g, (c, LANES)) for g in g_end], axis=0)
        for ci in range(nc):
            rows = [jnp.broadcast_to(jnp.exp(g_end[ci][:, DN_HEADS + h:DN_HEADS + h + 1]), (1, LANES))
                    for h in range(DN_HEADS)]
            eg_ref[blk * nc + ci] = jnp.concatenate(
                rows + [jnp.zeros((SUBLANES - DN_HEADS, LANES), F32)], axis=0)
        bgs.append(bg)
        gcs.append(gc)
        gcts.append(gc.T)
        egcs.append(jnp.exp(gc))
        ekds.append(jnp.exp(glast - gc))

    prep = []
    for blk, h in jobs:
        rs = slice(blk * r, (blk + 1) * r)
        sl = slice(h * DN_DK, (h + 1) * DN_DK)
        q = q_ref[rs, sl]
        k = k_ref[rs, sl]
        kb = k * bgs[blk][:, h:h + 1]
        kq = lax.dot_general(jnp.concatenate([kb, q], axis=0).astype(BF16), k.astype(BF16),
                             (((1,), (1,)), ((), ())), preferred_element_type=F32)
        prep.append((q, k, kb, kq))
    a_mats, qks = [], []
    for (blk, h), (q, k, kb, kq) in zip(jobs, prep):
        gl = slice(DN_HEADS + h, DN_HEADS + h + 1)
        gcol = jnp.concatenate([jnp.broadcast_to(gcs[blk][ci * c:(ci + 1) * c, gl], (c, c))
                                for ci in range(nc)], axis=1)
        decay = jnp.exp(jnp.where(incl, gcol - gcts[blk][gl, :], -jnp.inf))
        a_mats.append(jnp.where(strict, _blockdiag_to_wide(kq[:r], c) * decay, 0.0))
        qks.append(_blockdiag_to_wide(kq[r:], c) * decay)
    tinvs = _unit_lower_inverses(a_mats)
    uws = []
    for (blk, h), (q, k, kb, kq), tinv in zip(jobs, prep, tinvs):
        rs = slice(blk * r, (blk + 1) * r)
        sl = slice(h * DN_DK, (h + 1) * DN_DK)
        gl = slice(DN_HEADS + h, DN_HEADS + h + 1)
        rhs = jnp.concatenate([v_ref[rs, sl] * bgs[blk][:, h:h + 1], kb * egcs[blk][:, gl]], axis=1)
        uws.append(_mm(_wide_to_blockdiag(tinv), rhs))
    for (blk, h), (q, k, kb, kq), uw, qk in zip(jobs, prep, uws, qks):
        rs = slice(blk * r, (blk + 1) * r)
        sl = slice(h * DN_DK, (h + 1) * DN_DK)
        gl = slice(DN_HEADS + h, DN_HEADS + h + 1)
        u_ref[rs, sl] = uw[:, :DN_DV]
        w_b = uw[:, DN_DV:].astype(BF16)
        qd_b = (q * egcs[blk][:, gl]).astype(BF16)
        for ci in range(nc):
            wq_ref[blk * nc + ci, h * WQ_ROWS:h * WQ_ROWS + c, :] = w_b[ci * c:(ci + 1) * c, :]
            wq_ref[blk * nc + ci, h * WQ_ROWS + c:(h + 1) * WQ_ROWS, :] = qd_b[ci * c:(ci + 1) * c, :]
        kdt = (k * ekds[blk][:, gl]).T
        for pj in range(nc // 2):
            ls = slice(pj * LANES, (pj + 1) * LANES)
            pi = blk * (nc // 2) + pj
            qkd_ref[pi, h * QKD_ROWS:h * QKD_ROWS + c, :] = qk[:, ls].astype(BF16)
            qkd_ref[pi, h * QKD_ROWS + c:(h + 1) * QKD_ROWS, :] = kdt[:, ls].astype(BF16)


def _delta_scan_kernel(u_ref, wq_ref, qkd_ref, eg_ref, z_ref, nw_ref, o_ref, sout_ref, s_ref):
    n = pl.program_id(0)
    c = DN_CHUNK

    @pl.when(n == 0)
    def _():
        s_ref[...] = jnp.zeros_like(s_ref)

    nw = nw_ref[...]
    zeros = jnp.zeros((c, DN_DV), BF16)
    chains = [(b, h) for b in range(u_ref.shape[0]) for h in range(DN_HEADS)]
    for j in range(2):
        rs = slice(j * c, (j + 1) * c)
        ws = [jnp.dot(wq_ref[b, j, h * WQ_ROWS:(h + 1) * WQ_ROWS, :], s_ref[b, h].astype(BF16),
                      preferred_element_type=F32) for b, h in chains]
        rr = []
        for (b, h), w_h in zip(chains, ws):
            v_new = (u_ref[b, rs, h * DN_DV:(h + 1) * DN_DV] - w_h[:c]).astype(BF16)
            rhs = jnp.concatenate([v_new, zeros] if j == 0 else [zeros, v_new], axis=0)
            rr.append(jnp.dot(qkd_ref[b, h * QKD_ROWS:(h + 1) * QKD_ROWS, :], rhs,
                              preferred_element_type=F32))
        for (b, h), w_h, r_h in zip(chains, ws, rr):
            sl = slice(h * DN_DV, (h + 1) * DN_DV)
            s_ref[b, h] = s_ref[b, h] * eg_ref[b, j, h:h + 1, :] + r_h[c:]
            o_ref[b, rs, sl] = _gated_norm(w_h[c:] + r_h[:c], z_ref[b, rs, sl], nw).astype(o_ref.dtype)

    @pl.when(n == pl.num_programs(0) - 1)
    def _():
        sout_ref[...] = s_ref[...]


def _delta_prompt(qkv, bg, z, nw, ra):
    B, T, _ = qkv.shape
    c = DN_CHUNK
    n = T // c
    cpb = ra // c
    qkv_spec = lambda j: pl.BlockSpec((None, ra, DN_QK), lambda b, r: (b, r, j))
    u, wq, qkd, eg = pl.pallas_call(
        _delta_intra_kernel,
        grid=(B, T // ra),
        in_specs=[qkv_spec(0), qkv_spec(1), qkv_spec(2),
                  pl.BlockSpec((None, ra, LANES), lambda b, r: (b, r, 0))],
        out_specs=[pl.BlockSpec((None, ra, DN_V), lambda b, r: (b, r, 0)),
                   pl.BlockSpec((None, cpb, DN_HEADS * WQ_ROWS, DN_DK), lambda b, r: (b, r, 0, 0)),
                   pl.BlockSpec((None, cpb // 2, DN_HEADS * QKD_ROWS, LANES),
                                lambda b, r: (b, r, 0, 0)),
                   pl.BlockSpec((None, cpb, SUBLANES, LANES), lambda b, r: (b, r, 0, 0))],
        out_shape=[jax.ShapeDtypeStruct((B, T, DN_V), F32),
                   jax.ShapeDtypeStruct((B, n, DN_HEADS * WQ_ROWS, DN_DK), BF16),
                   jax.ShapeDtypeStruct((B, n // 2, DN_HEADS * QKD_ROWS, LANES), BF16),
                   jax.ShapeDtypeStruct((B, n, SUBLANES, LANES), F32)],
        compiler_params=_cparams(("parallel", "parallel")),
        name="delta_intra",
    )(qkv, qkv, qkv, bg)
    return pl.pallas_call(
        _delta_scan_kernel,
        grid=(n // 2,),
        in_specs=[pl.BlockSpec((B, 2 * c, DN_V), lambda i: (0, i, 0)),
                  pl.BlockSpec((B, 2, DN_HEADS * WQ_ROWS, DN_DK), lambda i: (0, i, 0, 0)),
                  pl.BlockSpec((B, None, DN_HEADS * QKD_ROWS, LANES), lambda i: (0, i, 0, 0)),
                  pl.BlockSpec((B, 2, SUBLANES, LANES), lambda i: (0, i, 0, 0)),
                  pl.BlockSpec((B, 2 * c, DN_V), lambda i: (0, i, 0)),
                  _const_spec((1, DN_DV))],
        out_specs=[pl.BlockSpec((B, 2 * c, DN_V), lambda i: (0, i, 0)),
                   pl.BlockSpec((B, DN_HEADS, DN_DK, DN_DV), lambda i: (0, 0, 0, 0))],
        out_shape=[jax.ShapeDtypeStruct((B, T, DN_V), BF16),
                   jax.ShapeDtypeStruct((B, DN_HEADS, DN_DK, DN_DV), F32)],
        scratch_shapes=[pltpu.VMEM((B, DN_HEADS, DN_DK, DN_DV), F32)],
        compiler_params=_cparams(("arbitrary",)),
        name="delta_scan",
    )(u, wq, qkd, eg, z, nw)


def _delta_sample_kernel(qkv_ref, bg_ref, z_ref, nw_ref, s0_ref, o_ref, s_ref):
    bb = qkv_ref.shape[0]
    nw = nw_ref[...]
    zeros = jnp.zeros((SUBLANES - 2, DN_DK), F32)
    zeros7 = jnp.zeros((SUBLANES - 1, DN_DK), F32)
    chains = [(i, h) for i in range(bb) for h in range(DN_HEADS)]
    pre = []
    for i, h in chains:
        q = qkv_ref[i:i + 1, h * DN_DK:(h + 1) * DN_DK]
        k = qkv_ref[i:i + 1, DN_QK + h * DN_DK:DN_QK + (h + 1) * DN_DK]
        v = qkv_ref[i:i + 1, 2 * DN_QK + h * DN_DV:2 * DN_QK + (h + 1) * DN_DV]
        beta = bg_ref[i:i + 1, h:h + 1]
        eg = jnp.exp(bg_ref[i:i + 1, DN_HEADS + h:DN_HEADS + h + 1])
        lhs = jnp.concatenate([k * beta * eg, q * eg, zeros], axis=0)
        ws = jnp.dot(lhs, s0_ref[i, h], preferred_element_type=F32)
        pre.append((q, k, v, beta, eg, ws))
    upd = []
    for (i, h), (q, k, v, beta, eg, ws) in zip(chains, pre):
        v_new = v * beta - ws[0:1]
        kcol = jnp.concatenate([k, zeros7], axis=0).T
        vrow = jnp.concatenate([v_new, zeros7], axis=0)
        upd.append((v_new, jnp.dot(kcol, vrow, preferred_element_type=F32)))
    for (i, h), (q, k, v, beta, eg, ws), (v_new, kv) in zip(chains, pre, upd):
        sl = slice(h * DN_DK, (h + 1) * DN_DK)
        s_ref[i, h] = s0_ref[i, h] * eg + kv
        o = ws[1:2] + jnp.sum(q * k, axis=-1, keepdims=True) * v_new
        o_ref[i:i + 1, sl] = _gated_norm(o, z_ref[i:i + 1, sl], nw)


def _delta_sample(qkv, bg, z, nw, s0, l, bb):
    R = qkv.shape[0]
    return pl.pallas_call(
        _delta_sample_kernel,
        grid=(R // bb,),
        in_specs=[pl.BlockSpec((bb, DN_CONV_CH), lambda i: (i, 0)),
                  pl.BlockSpec((bb, LANES), lambda i: (i, 0)),
                  pl.BlockSpec((bb, DN_V), lambda i: (i, 0)),
                  _const_spec((1, DN_DV)),
                  pl.BlockSpec((None, bb, DN_HEADS, DN_DK, DN_DV), lambda i: (l, i, 0, 0, 0))],
        out_specs=[pl.BlockSpec((bb, DN_V), lambda i: (i, 0)),
                   pl.BlockSpec((bb, DN_HEADS, DN_DK, DN_DV), lambda i: (i, 0, 0, 0))],
        out_shape=[jax.ShapeDtypeStruct((R, DN_V), F32),
                   jax.ShapeDtypeStruct(s0.shape[1:], F32)],
        compiler_params=_cparams(("parallel",)),
        name="delta_sample",
    )(qkv, bg, z, nw, s0)


def _swa_prompt_kernel(q_ref, kc_ref, vc_ref, kp_ref, vp_ref, sink_ref, o_ref):
    n = pl.program_id(1)
    w = WINDOW
    nq = q_ref.shape[0] // w
    half = SWA_DH
    row = lax.broadcasted_iota(jnp.int32, (w, 2 * w), 0)
    col = lax.broadcasted_iota(jnp.int32, (w, 2 * w), 1)
    band = (col >= row) & (col <= row + w)
    bias_mid = jnp.where(band, 0.0, -jnp.inf)
    bias_first = jnp.where(band & (col >= jnp.where(n == 0, w, 0)), 0.0, -jnp.inf)
    low = lax.broadcasted_iota(jnp.int32, (2 * w, LANES), 1) < half
    low_q = lax.broadcasted_iota(jnp.int32, (w, LANES), 1) < half
    kall = jnp.concatenate([kp_ref[...], kc_ref[...]], axis=0)
    vall = jnp.concatenate([vp_ref[...], vc_ref[...]], axis=0)
    kall_r = pltpu.roll(kall, half, 1)
    vall_r = pltpu.roll(vall, half, 1)
    log2e = math.log2(math.e)
    scale = SWA_DH ** -0.5 * log2e

    units = [(i, j) for i in range(nq) for j in range(SWA_KV_HEADS)]
    scores, vvs = [], []
    for i, j in units:
        rs = slice(i * w, (i + 2) * w)
        ka, kb = (kall[rs], kall_r[rs]) if j == 0 else (kall_r[rs], kall[rs])
        va, vb = (vall[rs], vall_r[rs]) if j == 0 else (vall_r[rs], vall[rs])
        kk = jnp.concatenate([jnp.where(low, ka, 0.0), jnp.where(low, 0.0, kb)], axis=0).astype(BF16)
        vvs.append(jnp.concatenate(
            [jnp.concatenate([jnp.where(low, va, 0.0), jnp.where(low, 1.0, 0.0)], axis=1),
             jnp.concatenate([jnp.where(low, 0.0, vb), jnp.where(low, 0.0, 1.0)], axis=1)],
            axis=0).astype(BF16))
        c0 = j * SWA_GROUP * SWA_DH
        q = jnp.concatenate([q_ref[i * w:(i + 1) * w, c0:c0 + LANES],
                             q_ref[i * w:(i + 1) * w, c0 + LANES:c0 + 2 * LANES]], axis=0)
        scores.append(lax.dot_general((q * scale).astype(BF16), kk, (((1,), (1,)), ((), ())),
                                      preferred_element_type=F32))
    keys = [(u, pr, hf) for u in range(len(units)) for pr in range(2) for hf in range(2)]
    sinks = {key: log2e * sink_ref[0:1, units[key[0]][1] * SWA_GROUP + 2 * key[1] + key[2]:
                                   units[key[0]][1] * SWA_GROUP + 2 * key[1] + key[2] + 1]
             for key in keys}
    masked = {}
    for u, pr, hf in keys:
        bias = bias_first if units[u][0] == 0 else bias_mid
        masked[u, pr, hf] = scores[u][pr * w:(pr + 1) * w, hf * 2 * w:(hf + 1) * 2 * w] + bias
    mx = {key: jnp.maximum(jnp.max(masked[key], axis=-1, keepdims=True), sinks[key]) for key in keys}
    p = {key: jnp.exp2(masked[key] - mx[key]).astype(BF16) for key in keys}
    outs = []
    for u in range(len(units)):
        pu = jnp.concatenate([jnp.concatenate([p[u, pr, 0], p[u, pr, 1]], axis=1)
                              for pr in range(2)], axis=0)
        outs.append(jnp.dot(pu, vvs[u], preferred_element_type=F32))
    for u, (i, j) in enumerate(units):
        c0 = j * SWA_GROUP * SWA_DH
        for pr in range(2):
            ou = outs[u][pr * w:(pr + 1) * w]
            sink_term = jnp.where(low_q, jnp.exp2(sinks[u, pr, 0] - mx[u, pr, 0]),
                                  jnp.exp2(sinks[u, pr, 1] - mx[u, pr, 1]))
            o_ref[i * w:(i + 1) * w, c0 + pr * LANES:c0 + (pr + 1) * LANES] = (
                ou[:, :LANES] / (ou[:, LANES:] + sink_term)).astype(o_ref.dtype)


def _swa_prompt(swa, sinks, nq):
    B, T, _ = swa.shape
    w = WINDOW
    rq = nq * w
    kcol = SWA_Q // SWA_KV
    prev = lambda n: jnp.maximum(n * nq - 1, 0)
    return pl.pallas_call(
        _swa_prompt_kernel,
        grid=(B, T // rq),
        in_specs=[pl.BlockSpec((None, rq, SWA_Q), lambda b, n: (b, n, 0)),
                  pl.BlockSpec((None, rq, SWA_KV), lambda b, n: (b, n, kcol)),
                  pl.BlockSpec((None, rq, SWA_KV), lambda b, n: (b, n, kcol + 1)),
                  pl.BlockSpec((None, w, SWA_KV), lambda b, n: (b, prev(n), kcol)),
                  pl.BlockSpec((None, w, SWA_KV), lambda b, n: (b, prev(n), kcol + 1)),
                  _const_spec((1, LANES))],
        out_specs=pl.BlockSpec((None, rq, SWA_Q), lambda b, n: (b, n, 0)),
        out_shape=jax.ShapeDtypeStruct((B, T, SWA_Q), BF16),
        compiler_params=_cparams(("parallel", "parallel")),
        name="swa_prompt",
    )(swa, swa, swa, swa, swa, sinks)


def _swa_sample_kernel(swa_ref, kc_ref, vc_ref, sink_ref, o_ref, kn_ref, vn_ref):
    bb = swa_ref.shape[0]
    w = kc_ref.shape[1]
    scale = SWA_DH ** -0.5
    for i in range(bb):
        kn_ref[i, 0:w - 1, :] = kc_ref[i, 1:w, :]
        kn_ref[i, w - 1:w, :] = swa_ref[i:i + 1, SWA_Q:SWA_Q + SWA_KV]
        vn_ref[i, 0:w - 1, :] = vc_ref[i, 1:w, :]
        vn_ref[i, w - 1:w, :] = swa_ref[i:i + 1, SWA_Q + SWA_KV:]
    units = [(i, j) for i in range(bb) for j in range(SWA_KV_HEADS)]
    sinks = [jnp.concatenate(
        [sink_ref[0:1, j * SWA_GROUP + g:j * SWA_GROUP + g + 1] for g in range(SWA_GROUP)]
        + [jnp.zeros((SUBLANES - SWA_GROUP, 1), F32)], axis=0) for j in range(SWA_KV_HEADS)]
    qs, scs = [], []
    for i, j in units:
        q4 = jnp.concatenate(
            [swa_ref[i:i + 1, (j * SWA_GROUP + g) * SWA_DH:(j * SWA_GROUP + g + 1) * SWA_DH]
             for g in range(SWA_GROUP)] + [jnp.zeros((SUBLANES - SWA_GROUP, SWA_DH), F32)], axis=0)
        qs.append(q4)
        scs.append(lax.dot_general(q4, kc_ref[i, :, j * SWA_DH:(j + 1) * SWA_DH],
                                   (((1,), (1,)), ((), ())), preferred_element_type=F32) * scale)
    soft = []
    for (i, j), q4, sc in zip(units, qs, scs):
        knew = swa_ref[i:i + 1, SWA_Q + j * SWA_DH:SWA_Q + (j + 1) * SWA_DH]
        sn = jnp.sum(q4 * knew, axis=-1, keepdims=True) * scale
        mx = jnp.maximum(jnp.maximum(jnp.max(sc, axis=-1, keepdims=True), sn), sinks[j])
        pc = jnp.exp(sc - mx)
        pn = jnp.exp(sn - mx)
        den = jnp.sum(pc, axis=-1, keepdims=True) + pn + jnp.exp(sinks[j] - mx)
        soft.append((pc, pn, den))
    pvs = [jnp.dot(pc, vc_ref[i, :, j * SWA_DH:(j + 1) * SWA_DH], preferred_element_type=F32)
           for (i, j), (pc, pn, den) in zip(units, soft)]
    for (i, j), (pc, pn, den), pv in zip(units, soft, pvs):
        vnew = swa_ref[i:i + 1, SWA_Q + SWA_KV + j * SWA_DH:SWA_Q + SWA_KV + (j + 1) * SWA_DH]
        o = (pv + pn * vnew) / den
        for g in range(SWA_GROUP):
            hq = j * SWA_GROUP + g
            o_ref[i:i + 1, hq * SWA_DH:(hq + 1) * SWA_DH] = o[g:g + 1, :]


def _swa_sample(swa, kc, vc, sinks, bb):
    R = swa.shape[0]
    w = kc.shape[1]
    cache_spec = pl.BlockSpec((bb, w, SWA_KV), lambda i: (i, 0, 0))
    return pl.pallas_call(
        _swa_sample_kernel,
        grid=(R // bb,),
        in_specs=[pl.BlockSpec((bb, SWA_OUT), lambda i: (i, 0)), cache_spec, cache_spec,
                  _const_spec((1, LANES))],
        out_specs=[pl.BlockSpec((bb, SWA_Q), lambda i: (i, 0)), cache_spec, cache_spec],
        out_shape=[jax.ShapeDtypeStruct((R, SWA_Q), F32),
                   jax.ShapeDtypeStruct(kc.shape, F32),
                   jax.ShapeDtypeStruct(vc.shape, F32)],
        compiler_params=_cparams(("parallel",)),
        name="swa_sample",
    )(swa, kc, vc, sinks)


def _outproj_ffn_kernel(x_ref, dn_ref, sw_ref, g5_ref, sh_ref, sc_ref, ga_ref, g_ref,
                        woa_ref, wob_ref, wgu_ref, wd_ref, o_ref):
    mix = (jnp.dot(dn_ref[...].astype(BF16), woa_ref[...], preferred_element_type=F32)
           + jnp.dot(sw_ref[...].astype(BF16), wob_ref[...], preferred_element_type=F32))
    x = x_ref[...] + g5_ref[...] * mix
    o_ref[...] = _ffn_body(x, sh_ref[...], sc_ref[...], ga_ref[...], g_ref[...], wgu_ref, wd_ref)


def _outproj_ffn(x, dn, sw, m, g, woa, wob, wgu, wd, l, tm):
    G, R, _ = x.shape
    g5 = _mod_specs(m, 1, tm)[2]
    return pl.pallas_call(
        _outproj_ffn_kernel,
        grid=(G, R // tm),
        in_specs=[_row_spec(tm, D_MODEL), _row_spec(tm, DN_V), _row_spec(tm, SWA_Q), g5]
        + _mod_specs(m, 2, tm) + [
            _const_spec((1, D_MODEL)),
            _layer_spec((DN_V, D_MODEL), l),
            _layer_spec((SWA_Q, D_MODEL), l),
            _layer_spec((D_MODEL, 2 * D_FF), l),
            _layer_spec((D_FF, D_MODEL), l),
        ],
        out_specs=_row_spec(tm, D_MODEL),
        out_shape=jax.ShapeDtypeStruct(x.shape, F32),
        compiler_params=_cparams(("parallel", "parallel")),
        name="outproj_ffn",
    )(x, dn, sw, m, m, m, m, g, woa, wob, wgu, wd)


def _rope_tables(pos):
    half = SWA_DH // 2
    inv = jnp.power(ROPE_THETA, -jnp.arange(half, dtype=F32) * 2.0 / SWA_DH)
    ang = pos.astype(F32)[:, None] * inv[None, :]
    cos, sin = jnp.cos(ang), jnp.sin(ang)
    reps = LANES // SWA_DH
    return (jnp.concatenate([cos, cos] * reps, axis=1),
            jnp.concatenate([-sin, sin] * reps, axis=1))


def _group_matrix(width):
    i = jnp.arange(width) // SWA_DH
    return jnp.where(i[:, None] == i[None, :], 1.0 / SWA_DH, 0.0).astype(BF16)


def _pack_w_in(w_in):
    conv = w_in[..., :DN_CONV_CH]
    o = DN_CONV_CH
    z = w_in[..., o:o + DN_V]
    o += DN_V
    ba = w_in[..., o:o + 2 * DN_HEADS]
    o += 2 * DN_HEADS
    sq = w_in[..., o:o + SWA_Q]
    o += SWA_Q
    sk = w_in[..., o:o + SWA_KV]
    o += SWA_KV
    sv = w_in[..., o:o + SWA_KV]
    pad = jnp.zeros(w_in.shape[:-1] + (LANES - 2 * DN_HEADS,), w_in.dtype)
    return jnp.concatenate([conv, z, sq, sk, sv, ba, pad], axis=-1).astype(BF16)


def _lane_row(vec, offset=0):
    return jnp.zeros((LANES,), F32).at[offset:offset + vec.shape[0]].set(vec.astype(F32))


def kernel(x_prompt, x_sample, c_prompt, c_sample, state_dn_conv, state_dn_S, cache_swa_k, cache_swa_v, w_ada, b_ada, g_ffn1, w_ffn1_gu, w_ffn1_down, g_mix, w_in, dn_conv_w, dn_A_log, dn_dt_bias, dn_norm_w, swa_q_norm, swa_k_norm, swa_sinks, w_out, g_ffn2, w_ffn2_gu, w_ffn2_down):
    L = w_ada.shape[0]
    B, T, _ = x_prompt.shape
    Bs = x_sample.shape[0]
    tm = min(512, T)
    bb = min(8, Bs)

    m_p, m_s = _adaln(c_prompt, c_sample, w_ada, b_ada)
    m_p = m_p.reshape(L, N_MOD, B, 1, D_MODEL)
    m_s = m_s.reshape(L, N_MOD, 1, Bs, D_MODEL)

    cos_p, sin_p = _rope_tables(jnp.arange(T))
    cos_s, sin_s = _rope_tables(PAST_LEN + jnp.arange(1))
    gq, gk = _group_matrix(SWA_Q), _group_matrix(SWA_KV)

    wgu1, wd1 = w_ffn1_gu.astype(BF16), w_ffn1_down.astype(BF16)
    wgu2, wd2 = w_ffn2_gu.astype(BF16), w_ffn2_down.astype(BF16)
    win = _pack_w_in(w_in)
    woa, wob = w_out[:, :DN_V].astype(BF16), w_out[:, DN_V:].astype(BF16)

    yp = x_prompt
    ys = x_sample.reshape(1, Bs, D_MODEL)
    outs = [[] for _ in range(8)]
    for l in range(L):
        g1, gm, g2 = (g_ffn1[l].reshape(1, D_MODEL), g_mix[l].reshape(1, D_MODEL),
                      g_ffn2[l].reshape(1, D_MODEL))
        cw = dn_conv_w[l]
        nw = dn_norm_w[l].reshape(1, DN_DV)
        prm = jnp.stack([
            _lane_row(-jnp.exp(dn_A_log[l].astype(F32)), DN_HEADS),
            _lane_row(dn_dt_bias[l], DN_HEADS),
            jnp.concatenate([swa_q_norm[l]] * (LANES // SWA_DH)).astype(F32),
            jnp.concatenate([swa_k_norm[l]] * (LANES // SWA_DH)).astype(F32),
        ] + [jnp.zeros((LANES,), F32)] * (SUBLANES - 4))
        sinks = _lane_row(swa_sinks[l]).reshape(1, LANES)

        yp = _ffn(yp, m_p[l], 0, g1, wgu1, wd1, l, tm)
        qkv, z, bg, swa, tail = _inproj_prompt(yp, m_p[l], gm, win, cw, prm, cos_p, sin_p, gq, gk,
                                               l, tm)
        dn_o, s_new = _delta_prompt(qkv, bg, z, nw, min(512, T))
        sw_o = _swa_prompt(swa, sinks, min(4, T // WINDOW))
        yp = _outproj_ffn(yp, dn_o, sw_o, m_p[l], g2, woa, wob, wgu2, wd2, l, tm)
        nkeep = min(WINDOW, T)
        outs[0].append(tail[:, HALO - (DN_CONV - 1):])
        outs[1].append(s_new)
        outs[2].append(swa[:, T - nkeep:, SWA_Q:SWA_Q + SWA_KV].reshape(B, nkeep, SWA_KV_HEADS, SWA_DH))
        outs[3].append(swa[:, T - nkeep:, SWA_Q + SWA_KV:].reshape(B, nkeep, SWA_KV_HEADS, SWA_DH))

        ys = _ffn(ys, m_s[l], 0, g1, wgu1, wd1, l, Bs)
        buf = jnp.transpose(state_dn_conv[l], (1, 0, 2))
        qkv, z, bg, swa, cnew = _inproj_sample(ys, m_s[l], gm, win, cw, prm, cos_s, sin_s, gq, gk,
                                               buf, l)
        dn_o, s_new = _delta_sample(qkv[0], bg[0], z[0], nw, state_dn_S, l, bb)
        w0 = cache_swa_k.shape[2]
        sw_o, k_new, v_new = _swa_sample(swa[0], cache_swa_k[l].reshape(Bs, w0, SWA_KV),
                                         cache_swa_v[l].reshape(Bs, w0, SWA_KV), sinks, bb)
        ys = _outproj_ffn(ys, dn_o[None], sw_o[None], m_s[l], g2, woa, wob, wgu2, wd2, l, Bs)
        outs[4].append(jnp.transpose(cnew, (1, 0, 2)))
        outs[5].append(s_new)
        outs[6].append(k_new.reshape(Bs, w0, SWA_KV_HEADS, SWA_DH))
        outs[7].append(v_new.reshape(Bs, w0, SWA_KV_HEADS, SWA_DH))

    return (yp, ys.reshape(Bs, 1, D_MODEL)) + tuple(jnp.stack(o) for o in outs)
```

```python
import functools
import math

import jax
import jax.numpy as jnp
from jax import lax
from jax.experimental import pallas as pl
from jax.experimental.pallas import tpu as pltpu

F32 = jnp.float32
BF16 = jnp.bfloat16

D_MODEL = 1024
D_FF = 2816
N_MOD = 9
EPS = 1e-6
DN_HEADS = 4
DN_DK = 128
DN_DV = 128
DN_QK = DN_HEADS * DN_DK
DN_V = DN_HEADS * DN_DV
DN_CONV = 4
DN_CONV_CH = 2 * DN_QK + DN_V
DN_CHUNK = 64
INV_BLOCK = 16
INTRA_ROWS = 256
SWA_DH = 64
SWA_HEADS = 8
SWA_KV_HEADS = 2
SWA_GROUP = SWA_HEADS // SWA_KV_HEADS
SWA_Q = SWA_HEADS * SWA_DH
SWA_KV = SWA_KV_HEADS * SWA_DH
WINDOW = 128
ROPE_THETA = 10000.0
PAST_LEN = 16384

LANES = 128
SUBLANES = 8
HALO = SUBLANES

C_CONV = 0
C_Z = C_CONV + DN_CONV_CH
C_SQ = C_Z + DN_V
C_SK = C_SQ + SWA_Q
C_SV = C_SK + SWA_KV
C_BG = C_SV + SWA_KV
IN_PACKED = C_BG + LANES
SWA_OUT = SWA_Q + 2 * SWA_KV

VMEM_LIMIT = 56 * 1024 * 1024


def _cparams(sem):
    return pltpu.CompilerParams(dimension_semantics=sem, vmem_limit_bytes=VMEM_LIMIT)


def _const_spec(shape):
    nd = len(shape)
    return pl.BlockSpec(shape, lambda *_: (0,) * nd, pipeline_mode=pl.Buffered(1))


def _layer_spec(shape, l):
    nd = len(shape)
    return pl.BlockSpec((None,) + tuple(shape), lambda *_: (l,) + (0,) * nd,
                        pipeline_mode=pl.Buffered(1))


def _sigmoid(x):
    return 1.0 / (1.0 + jnp.exp(-x))


def _silu(x):
    return x * _sigmoid(x)


def _adaln_kernel(cp_ref, cs_ref, w_ref, b_ref, mp_ref, ms_ref):
    w = w_ref[...].astype(BF16)
    b = b_ref[...]
    for c_ref, m_ref in ((cp_ref, mp_ref), (cs_ref, ms_ref)):
        c = c_ref[...]
        m_ref[...] = jnp.dot(_silu(c).astype(BF16), w, preferred_element_type=F32) + b


def _adaln(c_p, c_s, w_ada, b_ada):
    L = w_ada.shape[0]
    bp, bs = c_p.shape[0], c_s.shape[0]
    b4 = b_ada.reshape(L, N_MOD, 1, D_MODEL)
    return pl.pallas_call(
        _adaln_kernel,
        grid=(L, N_MOD),
        in_specs=[
            pl.BlockSpec((bp, D_MODEL), lambda l, j: (0, 0)),
            pl.BlockSpec((bs, D_MODEL), lambda l, j: (0, 0)),
            pl.BlockSpec((None, D_MODEL, D_MODEL), lambda l, j: (l, 0, j)),
            pl.BlockSpec((None, None, 1, D_MODEL), lambda l, j: (l, j, 0, 0)),
        ],
        out_specs=[
            pl.BlockSpec((None, None, bp, D_MODEL), lambda l, j: (l, j, 0, 0)),
            pl.BlockSpec((None, None, bs, D_MODEL), lambda l, j: (l, j, 0, 0)),
        ],
        out_shape=[
            jax.ShapeDtypeStruct((L, N_MOD, bp, D_MODEL), F32),
            jax.ShapeDtypeStruct((L, N_MOD, bs, D_MODEL), F32),
        ],
        compiler_params=_cparams(("parallel", "parallel")),
        name="adaln",
    )(c_p, c_s, w_ada, b4)


def _mod_norm(x, g, shift, scale):
    ms = jnp.mean(x * x, axis=-1, keepdims=True)
    return (x * lax.rsqrt(ms + EPS)) * (g * (1.0 + scale)) + shift


def _ffn_body(x, shift, scale, gate, g, wgu_ref, wd_ref):
    h = _mod_norm(x, g, shift, scale).astype(BF16)
    gu = jnp.dot(h, wgu_ref[...], preferred_element_type=F32)
    a = gu[:, :D_FF]
    b = gu[:, D_FF:]
    act = (_silu(a) * b).astype(BF16)
    y = jnp.dot(act, wd_ref[...], preferred_element_type=F32)
    return x + (0.5 * gate) * y


def _mod_specs(m, sub, tm):
    rm = m.shape[2]
    blk = 1 if rm == 1 else tm

    def spec(i):
        if rm == 1:
            return pl.BlockSpec((None, None, 1, D_MODEL), lambda g, r: (i, g, 0, 0))
        return pl.BlockSpec((None, None, blk, D_MODEL), lambda g, r: (i, g, r, 0))

    return [spec(3 * sub), spec(3 * sub + 1), spec(3 * sub + 2)]


def _row_spec(tm, width):
    return pl.BlockSpec((None, tm, width), lambda g, r: (g, r, 0))


def _ffn_kernel(x_ref, sh_ref, sc_ref, ga_ref, g_ref, wgu_ref, wd_ref, o_ref):
    o_ref[...] = _ffn_body(x_ref[...], sh_ref[...], sc_ref[...], ga_ref[...], g_ref[...],
                           wgu_ref, wd_ref)


def _ffn(x, m, sub, g, wgu, wd, l, tm):
    G, R, _ = x.shape
    return pl.pallas_call(
        _ffn_kernel,
        grid=(G, R // tm),
        in_specs=[_row_spec(tm, D_MODEL)] + _mod_specs(m, sub, tm) + [
            _const_spec((1, D_MODEL)),
            _layer_spec((D_MODEL, 2 * D_FF), l),
            _layer_spec((D_FF, D_MODEL), l),
        ],
        out_specs=_row_spec(tm, D_MODEL),
        out_shape=jax.ShapeDtypeStruct(x.shape, F32),
        compiler_params=_cparams(("parallel", "parallel")),
        name="ffn",
    )(x, m, m, m, g, wgu, wd)


def _group_mean_sq(x, gmat_ref):
    sq = x * x
    hi = sq.astype(BF16)
    lo = (sq - hi.astype(F32)).astype(BF16)
    gm = gmat_ref[...]
    return (jnp.dot(hi, gm, preferred_element_type=F32)
            + jnp.dot(lo, gm, preferred_element_type=F32))


def _rope(x, cos, sin_signed):
    width = x.shape[-1]
    reps = width // LANES
    lane = lax.broadcasted_iota(jnp.int32, x.shape, 1)
    first_half = (lane & (SWA_DH - 1)) < (SWA_DH // 2)
    partner = jnp.where(first_half,
                        pltpu.roll(x, width - SWA_DH // 2, 1),
                        pltpu.roll(x, SWA_DH // 2, 1))
    if reps > 1:
        cos = jnp.concatenate([cos] * reps, axis=1)
        sin_signed = jnp.concatenate([sin_signed] * reps, axis=1)
    return x * cos + partner * sin_signed


def _swa_post(p, prm_ref, cos, sin, gq_ref, gk_ref, swa_ref, orows=slice(None)):
    pq = p[:, C_SQ:C_SQ + SWA_Q]
    pk = p[:, C_SK:C_SK + SWA_KV]
    wq = jnp.concatenate([prm_ref[2:3, :]] * (SWA_Q // LANES), axis=1)
    wk = prm_ref[3:4, :]
    qn = pq * lax.rsqrt(_group_mean_sq(pq, gq_ref) + EPS) * wq
    kn = pk * lax.rsqrt(_group_mean_sq(pk, gk_ref) + EPS) * wk
    swa_ref[orows, :SWA_Q] = _rope(qn, cos, sin)
    swa_ref[orows, SWA_Q:SWA_Q + SWA_KV] = _rope(kn, cos, sin)
    swa_ref[orows, SWA_Q + SWA_KV:] = p[:, C_SV:C_SV + SWA_KV]


def _dn_group(a, gi, qkv_ref):
    for hh in range(DN_HEADS):
        cols = slice(gi * DN_QK + hh * DN_DK, gi * DN_QK + (hh + 1) * DN_DK)
        xh = a[:, hh * DN_DK:(hh + 1) * DN_DK]
        if gi < 2:
            s = lax.rsqrt(jnp.sum(xh * xh, axis=-1, keepdims=True) + EPS)
            if gi == 0:
                s = s * (DN_DK ** -0.5)
            xh = xh * s
        qkv_ref[:, cols] = xh.astype(qkv_ref.dtype)


def _bg_post(pb, prm_ref, bg_ref):
    neg_a = prm_ref[0:1, :]
    dt_b = prm_ref[1:2, :]
    xa = pb + dt_b
    softplus = jnp.maximum(xa, 0.0) + jnp.log(1.0 + jnp.exp(-jnp.abs(xa)))
    lane = lax.broadcasted_iota(jnp.int32, pb.shape, 1)
    bg_ref[...] = jnp.where(lane < DN_HEADS, _sigmoid(pb), neg_a * softplus)


def _dn_post(p, qkv_conv, prm_ref, qkv_ref, z_ref, bg_ref):
    qkv = _silu(qkv_conv)
    for gi in range(3):
        _dn_group(qkv[:, gi * DN_QK:(gi + 1) * DN_QK], gi, qkv_ref)
    z_ref[...] = p[:, C_Z:C_Z + DN_V].astype(z_ref.dtype)
    _bg_post(p[:, C_BG:C_BG + LANES], prm_ref, bg_ref)


class _Cols:
    def __init__(self, segments):
        self.segments = segments

    def __getitem__(self, idx):
        cols = idx[1]
        for start, arr in self.segments:
            if start <= cols.start and cols.stop <= start + arr.shape[1]:
                return arr[:, cols.start - start:cols.stop - start]
        raise KeyError(cols)


def _inproj_prompt_kernel(x_ref, sh_ref, sc_ref, g_ref, win_ref, cw_ref, prm_ref, cos_ref, sin_ref,
                          gq_ref, gk_ref, qkv_ref, z_ref, bg_ref, swa_ref, tail_ref, ext_ref):
    tm = x_ref.shape[0]
    first = pl.program_id(1) == 0

    @pl.when(first)
    def _():
        ext_ref[0:HALO, :] = jnp.zeros((HALO, DN_CONV_CH), F32)

    @pl.when(jnp.logical_not(first))
    def _():
        ext_ref[0:HALO, :] = ext_ref[tm:tm + HALO, :]

    h = _mod_norm(x_ref[...], g_ref[...], sh_ref[...], sc_ref[...]).astype(BF16)
    proj = lambda a, b: jnp.dot(h, win_ref[:, a:b], preferred_element_type=F32)
    p_att = _Cols([(C_SQ, proj(C_SQ, C_SK)), (C_SK, proj(C_SK, IN_PACKED))])
    _swa_post(p_att, prm_ref, cos_ref[...], sin_ref[...], gq_ref, gk_ref, swa_ref)
    _bg_post(p_att[:, C_BG:C_BG + LANES], prm_ref, bg_ref)
    for gi in range(3):
        cs = slice(gi * DN_QK, (gi + 1) * DN_QK)
        pc = proj(cs.start, cs.stop)
        ext_ref[HALO:, cs] = pc
        y = pc * cw_ref[DN_CONV - 1:DN_CONV, cs]
        for j in range(DN_CONV - 1):
            off = HALO - (DN_CONV - 1) + j
            y = y + ext_ref[off:off + tm, cs] * cw_ref[j:j + 1, cs]
        _dn_group(_silu(y), gi, qkv_ref)
    tail_ref[...] = ext_ref[tm:tm + HALO, :]
    z_ref[...] = proj(C_Z, C_Z + DN_V).astype(z_ref.dtype)


def _inproj_sample_kernel(x_ref, sh_ref, sc_ref, g_ref, win_ref, cw_ref, prm_ref, cos_ref, sin_ref,
                          gq_ref, gk_ref, buf_ref, qkv_ref, z_ref, bg_ref, swa_ref, cnew_ref):
    h = _mod_norm(x_ref[...], g_ref[...], sh_ref[...], sc_ref[...]).astype(BF16)
    p = jnp.dot(h, win_ref[...], preferred_element_type=F32)
    pc = p[:, :DN_CONV_CH]
    y = pc * cw_ref[DN_CONV - 1:DN_CONV, :]
    for j in range(DN_CONV - 1):
        y = y + buf_ref[j] * cw_ref[j:j + 1, :]
    for j in range(DN_CONV - 2):
        cnew_ref[j] = buf_ref[j + 1]
    cnew_ref[DN_CONV - 2] = pc
    _swa_post(p, prm_ref, cos_ref[...], sin_ref[...], gq_ref, gk_ref, swa_ref)
    _dn_post(p, y, prm_ref, qkv_ref, z_ref, bg_ref)


def _inproj_common_specs(m, tm, l):
    sh, sc, _ = _mod_specs(m, 1, tm)
    return [_row_spec(tm, D_MODEL), sh, sc,
            _const_spec((1, D_MODEL)),
            _layer_spec((D_MODEL, IN_PACKED), l),
            _const_spec((DN_CONV, DN_CONV_CH)),
            _const_spec((SUBLANES, LANES))]


def _inproj_prompt(x, m, g, win, cw, prm, cos, sin, gq, gk, l, tm):
    B, T, _ = x.shape
    table = pl.BlockSpec((tm, LANES), lambda b, r: (r, 0))
    return pl.pallas_call(
        _inproj_prompt_kernel,
        grid=(B, T // tm),
        in_specs=_inproj_common_specs(m, tm, l) + [
            table, table,
            _const_spec((SWA_Q, SWA_Q)),
            _const_spec((SWA_KV, SWA_KV)),
        ],
        out_specs=[
            _row_spec(tm, DN_CONV_CH), _row_spec(tm, DN_V), _row_spec(tm, LANES),
            _row_spec(tm, SWA_OUT),
            pl.BlockSpec((None, HALO, DN_CONV_CH), lambda b, r: (b, 0, 0)),
        ],
        out_shape=[
            jax.ShapeDtypeStruct((B, T, DN_CONV_CH), BF16),
            jax.ShapeDtypeStruct((B, T, DN_V), BF16),
            jax.ShapeDtypeStruct((B, T, LANES), F32),
            jax.ShapeDtypeStruct((B, T, SWA_OUT), F32),
            jax.ShapeDtypeStruct((B, HALO, DN_CONV_CH), F32),
        ],
        scratch_shapes=[pltpu.VMEM((HALO + tm, DN_CONV_CH), F32)],
        compiler_params=_cparams(("parallel", "arbitrary")),
        name="inproj_prompt",
    )(x, m, m, g, win, cw, prm, cos, sin, gq, gk)


def _inproj_sample(x, m, g, win, cw, prm, cos, sin, gq, gk, buf, l):
    _, R, _ = x.shape
    nbuf = DN_CONV - 1
    return pl.pallas_call(
        _inproj_sample_kernel,
        grid=(1, 1),
        in_specs=_inproj_common_specs(m, R, l) + [
            _const_spec((1, LANES)), _const_spec((1, LANES)),
            _const_spec((SWA_Q, SWA_Q)), _const_spec((SWA_KV, SWA_KV)),
            _const_spec((nbuf, R, DN_CONV_CH)),
        ],
        out_specs=[
            _row_spec(R, DN_CONV_CH), _row_spec(R, DN_V), _row_spec(R, LANES), _row_spec(R, SWA_OUT),
            pl.BlockSpec((nbuf, R, DN_CONV_CH), lambda g_, r: (0, 0, 0)),
        ],
        out_shape=[
            jax.ShapeDtypeStruct((1, R, DN_CONV_CH), F32),
            jax.ShapeDtypeStruct((1, R, DN_V), F32),
            jax.ShapeDtypeStruct((1, R, LANES), F32),
            jax.ShapeDtypeStruct((1, R, SWA_OUT), F32),
            jax.ShapeDtypeStruct((nbuf, R, DN_CONV_CH), F32),
        ],
        compiler_params=_cparams(("arbitrary", "arbitrary")),
        name="inproj_sample",
    )(x, m, m, g, win, cw, prm, cos, sin, gq, gk, buf)


def _gated_norm(o, z, nw):
    on = o * lax.rsqrt(jnp.mean(o * o, axis=-1, keepdims=True) + EPS) * nw
    return on * _silu(z.astype(F32))


def _mm(a, b):
    return jnp.dot(a.astype(BF16), b.astype(BF16), preferred_element_type=F32)


def _wide_to_blockdiag(wide):
    c, n = wide.shape[0], wide.shape[1] // wide.shape[0]
    cblk = lax.broadcasted_iota(jnp.int32, wide.shape, 1) // c
    return jnp.concatenate([jnp.where(cblk == i, wide, 0.0) for i in range(n)], axis=0)


def _blockdiag_to_wide(full, c):
    n = full.shape[0] // c
    cblk = lax.broadcasted_iota(jnp.int32, (c, full.shape[1]), 1) // c
    wide = full[0:c]
    for i in range(1, n):
        wide = jnp.where(cblk == i, full[i * c:(i + 1) * c], wide)
    return wide


def _unit_lower_inverses(mats):
    c = mats[0].shape[0]
    row = lax.broadcasted_iota(jnp.int32, mats[0].shape, 0)
    col = lax.broadcasted_iota(jnp.int32, mats[0].shape, 1) % c
    s = INV_BLOCK
    diag = (row // s) == (col // s)
    pws = [jnp.where(diag, a, 0.0) for a in mats]
    eye = jnp.where(row == col, 1.0, 0.0)
    xs = [eye - pw for pw in pws]
    k = 2
    while k < s:
        pws = [_mm(pw, _wide_to_blockdiag(pw)) for pw in pws]
        xs = [x + _mm(x, _wide_to_blockdiag(pw)) for x, pw in zip(xs, pws)]
        k *= 2
    while s < c:
        pair = ((row // (2 * s)) == (col // (2 * s))) & ((row // s) != (col // s))
        ts = [_mm(x, _wide_to_blockdiag(jnp.where(pair, a, 0.0))) for x, a in zip(xs, mats)]
        xs = [x - _mm(t, _wide_to_blockdiag(x)) for x, t in zip(xs, ts)]
        s *= 2
    return xs


def _cumsum_rows(x):
    n = x.shape[0]
    row = lax.broadcasted_iota(jnp.int32, x.shape, 0)
    s = 1
    while s < n:
        x = x + jnp.where(row >= s, pltpu.roll(x, s, 0), 0.0)
        s *= 2
    return x


WQ_ROWS = 2 * DN_CHUNK
QKD_ROWS = DN_CHUNK + DN_DK


def _delta_intra_kernel(q_ref, k_ref, v_ref, bg_ref, u_ref, wq_ref, qkd_ref, eg_ref):
    c = DN_CHUNK
    r = INTRA_ROWS
    nc = r // c
    row = lax.broadcasted_iota(jnp.int32, (c, r), 0)
    col = lax.broadcasted_iota(jnp.int32, (c, r), 1) % c
    incl = row >= col
    strict = row > col

    jobs = [(blk, h) for blk in range(q_ref.shape[0] // r) for h in range(DN_HEADS)]
    gcs, egcs, ekds, gcts, bgs = [], [], [], [], []
    for blk in range(q_ref.shape[0] // r):
        bg = bg_ref[blk * r:(blk + 1) * r, :]
        gc = jnp.concatenate([_cumsum_rows(bg[ci * c:(ci + 1) * c, :]) for ci in range(nc)], axis=0)
        g_end = [gc[(ci + 1) * c - 1:(ci + 1) * c, :] for ci in range(nc)]
        glast = jnp.concatenate([jnp.broadcast_to(g, (c, LANES)) for g in g_end], axis=0)
        for ci in range(nc):
            rows = [jnp.broadcast_to(jnp.exp(g_end[ci][:, DN_HEADS + h:DN_HEADS + h + 1]), (1, LANES))
                    for h in range(DN_HEADS)]
            eg_ref[blk * nc + ci] = jnp.concatenate(
                rows + [jnp.zeros((SUBLANES - DN_HEADS, LANES), F32)], axis=0)
        bgs.append(bg)
        gcs.append(gc)
        gcts.append(gc.T)
        egcs.append(jnp.exp(gc))
        ekds.append(jnp.exp(glast - gc))

    prep = []
    for blk, h in jobs:
        rs = slice(blk * r, (blk + 1) * r)
        sl = slice(h * DN_DK, (h + 1) * DN_DK)
        q = q_ref[rs, sl]
        k = k_ref[rs, sl]
        kb = k * bgs[blk][:, h:h + 1]
        kq = lax.dot_general(jnp.concatenate([kb, q], axis=0).astype(BF16), k.astype(BF16),
                             (((1,), (1,)), ((), ())), preferred_element_type=F32)
        prep.append((q, k, kb, kq))
    a_mats, qks = [], []
    for (blk, h), (q, k, kb, kq) in zip(jobs, prep):
        gl = slice(DN_HEADS + h, DN_HEADS + h + 1)
        gcol = jnp.concatenate([jnp.broadcast_to(gcs[blk][ci * c:(ci + 1) * c, gl], (c, c))
                                for ci in range(nc)], axis=1)
        decay = jnp.exp(jnp.where(incl, gcol - gcts[blk][gl, :], -jnp.inf))
        a_mats.append(jnp.where(strict, _blockdiag_to_wide(kq[:r], c) * decay, 0.0))
        qks.append(_blockdiag_to_wide(kq[r:], c) * decay)
    tinvs = _unit_lower_inverses(a_mats)
    uws = []
    for (blk, h), (q, k, kb, kq), tinv in zip(jobs, prep, tinvs):
        rs = slice(blk * r, (blk + 1) * r)
        sl = slice(h * DN_DK, (h + 1) * DN_DK)
        gl = slice(DN_HEADS + h, DN_HEADS + h + 1)
        rhs = jnp.concatenate([v_ref[rs, sl] * bgs[blk][:, h:h + 1], kb * egcs[blk][:, gl]], axis=1)
        uws.append(_mm(_wide_to_blockdiag(tinv), rhs))
    for (blk, h), (q, k, kb, kq), uw, qk in zip(jobs, prep, uws, qks):
        rs = slice(blk * r, (blk + 1) * r)
        sl = slice(h * DN_DK, (h + 1) * DN_DK)
        gl = slice(DN_HEADS + h, DN_HEADS + h + 1)
        u_ref[rs, sl] = uw[:, :DN_DV]
        w_b = uw[:, DN_DV:].astype(BF16)
        qd_b = (q * egcs[blk][:, gl]).astype(BF16)
        for ci in range(nc):
            wq_ref[blk * nc + ci, h * WQ_ROWS:h * WQ_ROWS + c, :] = w_b[ci * c:(ci + 1) * c, :]
            wq_ref[blk * nc + ci, h * WQ_ROWS + c:(h + 1) * WQ_ROWS, :] = qd_b[ci * c:(ci + 1) * c, :]
        kdt = (k * ekds[blk][:, gl]).T
        for pj in range(nc // 2):
            ls = slice(pj * LANES, (pj + 1) * LANES)
            pi = blk * (nc // 2) + pj
            qkd_ref[pi, h * QKD_ROWS:h * QKD_ROWS + c, :] = qk[:, ls].astype(BF16)
            qkd_ref[pi, h * QKD_ROWS + c:(h + 1) * QKD_ROWS, :] = kdt[:, ls].astype(BF16)


def _delta_scan_kernel(u_ref, wq_ref, qkd_ref, eg_ref, z_ref, nw_ref, o_ref, sout_ref, s_ref):
    n = pl.program_id(0)
    c = DN_CHUNK

    @pl.when(n == 0)
    def _():
        s_ref[...] = jnp.zeros_like(s_ref)

    nw = nw_ref[...]
    zeros = jnp.zeros((c, DN_DV), BF16)
    chains = [(b, h) for b in range(u_ref.shape[0]) for h in range(DN_HEADS)]
    for j in range(2):
        rs = slice(j * c, (j + 1) * c)
        ws = [jnp.dot(wq_ref[b, j, h * WQ_ROWS:(h + 1) * WQ_ROWS, :], s_ref[b, h].astype(BF16),
                      preferred_element_type=F32) for b, h in chains]
        rr = []
        for (b, h), w_h in zip(chains, ws):
            v_new = (u_ref[b, rs, h * DN_DV:(h + 1) * DN_DV] - w_h[:c]).astype(BF16)
            rhs = jnp.concatenate([v_new, zeros] if j == 0 else [zeros, v_new], axis=0)
            rr.append(jnp.dot(qkd_ref[b, h * QKD_ROWS:(h + 1) * QKD_ROWS, :], rhs,
                              preferred_element_type=F32))
        for (b, h), w_h, r_h in zip(chains, ws, rr):
            sl = slice(h * DN_DV, (h + 1) * DN_DV)
            s_ref[b, h] = s_ref[b, h] * eg_ref[b, j, h:h + 1, :] + r_h[c:]
            o_ref[b, rs, sl] = _gated_norm(w_h[c:] + r_h[:c], z_ref[b, rs, sl], nw).astype(o_ref.dtype)

    @pl.when(n == pl.num_programs(0) - 1)
    def _():
        sout_ref[...] = s_ref[...]


def _delta_prompt(qkv, bg, z, nw, ra):
    B, T, _ = qkv.shape
    c = DN_CHUNK
    n = T // c
    cpb = ra // c
    qkv_spec = lambda j: pl.BlockSpec((None, ra, DN_QK), lambda b, r: (b, r, j))
    u, wq, qkd, eg = pl.pallas_call(
        _delta_intra_kernel,
        grid=(B, T // ra),
        in_specs=[qkv_spec(0), qkv_spec(1), qkv_spec(2),
                  pl.BlockSpec((None, ra, LANES), lambda b, r: (b, r, 0))],
        out_specs=[pl.BlockSpec((None, ra, DN_V), lambda b, r: (b, r, 0)),
                   pl.BlockSpec((None, cpb, DN_HEADS * WQ_ROWS, DN_DK), lambda b, r: (b, r, 0, 0)),
                   pl.BlockSpec((None, cpb // 2, DN_HEADS * QKD_ROWS, LANES),
                                lambda b, r: (b, r, 0, 0)),
                   pl.BlockSpec((None, cpb, SUBLANES, LANES), lambda b, r: (b, r, 0, 0))],
        out_shape=[jax.ShapeDtypeStruct((B, T, DN_V), F32),
                   jax.ShapeDtypeStruct((B, n, DN_HEADS * WQ_ROWS, DN_DK), BF16),
                   jax.ShapeDtypeStruct((B, n // 2, DN_HEADS * QKD_ROWS, LANES), BF16),
                   jax.ShapeDtypeStruct((B, n, SUBLANES, LANES), F32)],
        compiler_params=_cparams(("parallel", "parallel")),
        name="delta_intra",
    )(qkv, qkv, qkv, bg)
    return pl.pallas_call(
        _delta_scan_kernel,
        grid=(n // 2,),
        in_specs=[pl.BlockSpec((B, 2 * c, DN_V), lambda i: (0, i, 0)),
                  pl.BlockSpec((B, 2, DN_HEADS * WQ_ROWS, DN_DK), lambda i: (0, i, 0, 0)),
                  pl.BlockSpec((B, None, DN_HEADS * QKD_ROWS, LANES), lambda i: (0, i, 0, 0)),
                  pl.BlockSpec((B, 2, SUBLANES, LANES), lambda i: (0, i, 0, 0)),
                  pl.BlockSpec((B, 2 * c, DN_V), lambda i: (0, i, 0)),
                  _const_spec((1, DN_DV))],
        out_specs=[pl.BlockSpec((B, 2 * c, DN_V), lambda i: (0, i, 0)),
                   pl.BlockSpec((B, DN_HEADS, DN_DK, DN_DV), lambda i: (0, 0, 0, 0))],
        out_shape=[jax.ShapeDtypeStruct((B, T, DN_V), BF16),
                   jax.ShapeDtypeStruct((B, DN_HEADS, DN_DK, DN_DV), F32)],
        scratch_shapes=[pltpu.VMEM((B, DN_HEADS, DN_DK, DN_DV), F32)],
        compiler_params=_cparams(("arbitrary",)),
        name="delta_scan",
    )(u, wq, qkd, eg, z, nw)


def _delta_sample_kernel(qkv_ref, bg_ref, z_ref, nw_ref, s0_ref, o_ref, s_ref):
    bb = qkv_ref.shape[0]
    nw = nw_ref[...]
    zeros = jnp.zeros((SUBLANES - 2, DN_DK), F32)
    row = lax.broadcasted_iota(jnp.int32, (bb, DN_DV), 0)
    heads = []
    for h in range(DN_HEADS):
        q = qkv_ref[:, h * DN_DK:(h + 1) * DN_DK]
        k = qkv_ref[:, DN_QK + h * DN_DK:DN_QK + (h + 1) * DN_DK]
        v = qkv_ref[:, 2 * DN_QK + h * DN_DV:2 * DN_QK + (h + 1) * DN_DV]
        beta = bg_ref[:, h:h + 1]
        eg = jnp.exp(bg_ref[:, DN_HEADS + h:DN_HEADS + h + 1])
        w = k * beta * eg
        qd = q * eg
        ws = [jnp.dot(jnp.concatenate([w[i:i + 1], qd[i:i + 1], zeros], axis=0), s0_ref[i, h],
                      preferred_element_type=F32) for i in range(bb)]
        heads.append((q, k, v, beta, eg, ws))
    updates = []
    for q, k, v, beta, eg, ws in heads:
        w_s = jnp.concatenate([ws[i][0:1] for i in range(bb)], axis=0)
        q_s = jnp.concatenate([ws[i][1:2] for i in range(bb)], axis=0)
        v_new = v * beta - w_s
        kt = k.T
        kv = [jnp.dot(kt, jnp.where(row == i, v_new, 0.0), preferred_element_type=F32)
              for i in range(bb)]
        updates.append((v_new, q_s, kv))
    for h, ((q, k, v, beta, eg, ws), (v_new, q_s, kv)) in enumerate(zip(heads, updates)):
        sl = slice(h * DN_DK, (h + 1) * DN_DK)
        for i in range(bb):
            s_ref[i, h] = s0_ref[i, h] * eg[i:i + 1] + kv[i]
        o = q_s + jnp.sum(q * k, axis=-1, keepdims=True) * v_new
        o_ref[:, sl] = _gated_norm(o, z_ref[:, sl], nw)


def _delta_sample(qkv, bg, z, nw, s0, l, bb):
    R = qkv.shape[0]
    return pl.pallas_call(
        _delta_sample_kernel,
        grid=(R // bb,),
        in_specs=[pl.BlockSpec((bb, DN_CONV_CH), lambda i: (i, 0)),
                  pl.BlockSpec((bb, LANES), lambda i: (i, 0)),
                  pl.BlockSpec((bb, DN_V), lambda i: (i, 0)),
                  _const_spec((1, DN_DV)),
                  pl.BlockSpec((None, bb, DN_HEADS, DN_DK, DN_DV), lambda i: (l, i, 0, 0, 0))],
        out_specs=[pl.BlockSpec((bb, DN_V), lambda i: (i, 0)),
                   pl.BlockSpec((bb, DN_HEADS, DN_DK, DN_DV), lambda i: (i, 0, 0, 0))],
        out_shape=[jax.ShapeDtypeStruct((R, DN_V), F32),
                   jax.ShapeDtypeStruct(s0.shape[1:], F32)],
        compiler_params=_cparams(("parallel",)),
        name="delta_sample",
    )(qkv, bg, z, nw, s0)


def _swa_prompt_kernel(q_ref, kc_ref, vc_ref, kp_ref, vp_ref, sink_ref, o_ref):
    n = pl.program_id(1)
    w = WINDOW
    nq = q_ref.shape[0] // w
    half = SWA_DH
    key = lax.broadcasted_iota(jnp.int32, (2 * w, w), 0)
    qry = lax.broadcasted_iota(jnp.int32, (2 * w, w), 1)
    band = (key >= qry) & (key <= qry + w)
    bias_mid = jnp.where(band, 0.0, -jnp.inf)
    bias_first = jnp.where(band & (key >= jnp.where(n == 0, w, 0)), 0.0, -jnp.inf)
    kall = jnp.concatenate([kp_ref[...], kc_ref[...]], axis=0)
    vall = jnp.concatenate([vp_ref[...], vc_ref[...]], axis=0)
    low = lax.broadcasted_iota(jnp.int32, kall.shape, 1) < half
    kall_r = pltpu.roll(kall, half, 1)
    k_lo = [jnp.where(low, kall, 0.0).astype(BF16), jnp.where(low, kall_r, 0.0).astype(BF16)]
    k_hi = [jnp.where(low, 0.0, kall_r).astype(BF16), jnp.where(low, 0.0, kall).astype(BF16)]
    v_t = vall.T.astype(BF16)
    zeros_v = jnp.zeros((half, 2 * w), BF16)
    sum_rows = 2 * SUBLANES
    ones8 = jnp.ones((sum_rows, 2 * w), BF16)
    zeros8 = jnp.zeros((sum_rows, 2 * w), BF16)
    first_pair = lax.broadcasted_iota(jnp.int32, (1, 2 * w), 1) < w
    log2e = math.log2(math.e)
    scale = SWA_DH ** -0.5 * log2e

    units = [(i, j) for i in range(nq) for j in range(SWA_KV_HEADS)]
    scores = []
    for i, j in units:
        rs = slice(i * w, (i + 2) * w)
        kk = jnp.concatenate([k_lo[j][rs], k_hi[j][rs]], axis=0)
        c0 = j * SWA_GROUP * SWA_DH
        q = jnp.concatenate([q_ref[i * w:(i + 1) * w, c0:c0 + LANES],
                             q_ref[i * w:(i + 1) * w, c0 + LANES:c0 + 2 * LANES]], axis=0)
        scores.append(lax.dot_general(kk, (q * scale).astype(BF16), (((1,), (1,)), ((), ())),
                                      preferred_element_type=F32))
    probs, sink_rows, maxima = [], [], []
    for (i, j), s in zip(units, scores):
        bias = bias_first if i == 0 else bias_mid
        bias2 = jnp.concatenate([bias, bias], axis=1)
        p_hf, sk_hf, mx_hf = [], [], []
        for hf in range(2):
            heads = [j * SWA_GROUP + 2 * pr + hf for pr in range(2)]
            sink = log2e * jnp.where(first_pair, sink_ref[0:1, heads[0]:heads[0] + 1],
                                     sink_ref[0:1, heads[1]:heads[1] + 1])
            masked = s[hf * 2 * w:(hf + 1) * 2 * w, :] + bias2
            mx = jnp.maximum(jnp.max(masked, axis=0, keepdims=True), sink)
            p_hf.append(jnp.exp2(masked - mx).astype(BF16))
            sk_hf.append(sink)
            mx_hf.append(mx)
        probs.append(jnp.concatenate(p_hf, axis=0))
        sink_rows.append(sk_hf)
        maxima.append(mx_hf)
    outs = []
    for (i, j), p in zip(units, probs):
        vt = v_t[j * half:(j + 1) * half, i * w:(i + 2) * w]
        lhs = jnp.concatenate([jnp.concatenate([vt, zeros_v], axis=1),
                               jnp.concatenate([zeros_v, vt], axis=1),
                               jnp.concatenate([ones8, zeros8], axis=1),
                               jnp.concatenate([zeros8, ones8], axis=1)], axis=0)
        outs.append(jnp.dot(lhs, p, preferred_element_type=F32))
    for (i, j), o_t, sk_hf, mx_hf in zip(units, outs, sink_rows, maxima):
        c0 = j * SWA_GROUP * SWA_DH
        parts = []
        for hf in range(2):
            den = (o_t[2 * half + hf * sum_rows:2 * half + hf * sum_rows + 1, :]
                   + jnp.exp2(sk_hf[hf] - mx_hf[hf]))
            parts.append(o_t[hf * half:(hf + 1) * half, :] / den)
        o = jnp.concatenate(parts, axis=0).T
        for pr in range(2):
            o_ref[i * w:(i + 1) * w, c0 + pr * LANES:c0 + (pr + 1) * LANES] = (
                o[pr * w:(pr + 1) * w, :].astype(o_ref.dtype))


def _swa_prompt(swa, sinks, nq):
    B, T, _ = swa.shape
    w = WINDOW
    rq = nq * w
    kcol = SWA_Q // SWA_KV
    prev = lambda n: jnp.maximum(n * nq - 1, 0)
    return pl.pallas_call(
        _swa_prompt_kernel,
        grid=(B, T // rq),
        in_specs=[pl.BlockSpec((None, rq, SWA_Q), lambda b, n: (b, n, 0)),
                  pl.BlockSpec((None, rq, SWA_KV), lambda b, n: (b, n, kcol)),
                  pl.BlockSpec((None, rq, SWA_KV), lambda b, n: (b, n, kcol + 1)),
                  pl.BlockSpec((None, w, SWA_KV), lambda b, n: (b, prev(n), kcol)),
                  pl.BlockSpec((None, w, SWA_KV), lambda b, n: (b, prev(n), kcol + 1)),
                  _const_spec((1, LANES))],
        out_specs=pl.BlockSpec((None, rq, SWA_Q), lambda b, n: (b, n, 0)),
        out_shape=jax.ShapeDtypeStruct((B, T, SWA_Q), BF16),
        compiler_params=_cparams(("parallel", "parallel")),
        name="swa_prompt",
    )(swa, swa, swa, swa, swa, sinks)


def _swa_sample_kernel(swa_ref, kc_ref, vc_ref, sink_ref, o_ref, kn_ref, vn_ref):
    bb = swa_ref.shape[0]
    w = kc_ref.shape[1]
    scale = SWA_DH ** -0.5
    for i in range(bb):
        kn_ref[i, 0:w - 1, :] = kc_ref[i, 1:w, :]
        kn_ref[i, w - 1:w, :] = swa_ref[i:i + 1, SWA_Q:SWA_Q + SWA_KV]
        vn_ref[i, 0:w - 1, :] = vc_ref[i, 1:w, :]
        vn_ref[i, w - 1:w, :] = swa_ref[i:i + 1, SWA_Q + SWA_KV:]
    units = [(i, j) for i in range(bb) for j in range(SWA_KV_HEADS)]
    sinks = [jnp.concatenate(
        [sink_ref[0:1, j * SWA_GROUP + g:j * SWA_GROUP + g + 1] for g in range(SWA_GROUP)]
        + [jnp.zeros((SUBLANES - SWA_GROUP, 1), F32)], axis=0) for j in range(SWA_KV_HEADS)]
    qs, scs = [], []
    for i, j in units:
        q4 = jnp.concatenate(
            [swa_ref[i:i + 1, (j * SWA_GROUP + g) * SWA_DH:(j * SWA_GROUP + g + 1) * SWA_DH]
             for g in range(SWA_GROUP)] + [jnp.zeros((SUBLANES - SWA_GROUP, SWA_DH), F32)], axis=0)
        qs.append(q4)
        scs.append(lax.dot_general(q4, kc_ref[i, :, j * SWA_DH:(j + 1) * SWA_DH],
                                   (((1,), (1,)), ((), ())), preferred_element_type=F32) * scale)
    soft = []
    for (i, j), q4, sc in zip(units, qs, scs):
        knew = swa_ref[i:i + 1, SWA_Q + j * SWA_DH:SWA_Q + (j + 1) * SWA_DH]
        sn = jnp.sum(q4 * knew, axis=-1, keepdims=True) * scale
        mx = jnp.maximum(jnp.maximum(jnp.max(sc, axis=-1, keepdims=True), sn), sinks[j])
        pc = jnp.exp(sc - mx)
        pn = jnp.exp(sn - mx)
        den = jnp.sum(pc, axis=-1, keepdims=True) + pn + jnp.exp(sinks[j] - mx)
        soft.append((pc, pn, den))
    pvs = [jnp.dot(pc, vc_ref[i, :, j * SWA_DH:(j + 1) * SWA_DH], preferred_element_type=F32)
           for (i, j), (pc, pn, den) in zip(units, soft)]
    for (i, j), (pc, pn, den), pv in zip(units, soft, pvs):
        vnew = swa_ref[i:i + 1, SWA_Q + SWA_KV + j * SWA_DH:SWA_Q + SWA_KV + (j + 1) * SWA_DH]
        o = (pv + pn * vnew) / den
        for g in range(SWA_GROUP):
            hq = j * SWA_GROUP + g
            o_ref[i:i + 1, hq * SWA_DH:(hq + 1) * SWA_DH] = o[g:g + 1, :]


def _swa_sample(swa, kc, vc, sinks, bb):
    R = swa.shape[0]
    w = kc.shape[1]
    cache_spec = pl.BlockSpec((bb, w, SWA_KV), lambda i: (i, 0, 0))
    return pl.pallas_call(
        _swa_sample_kernel,
        grid=(R // bb,),
        in_specs=[pl.BlockSpec((bb, SWA_OUT), lambda i: (i, 0)), cache_spec, cache_spec,
                  _const_spec((1, LANES))],
        out_specs=[pl.BlockSpec((bb, SWA_Q), lambda i: (i, 0)), cache_spec, cache_spec],
        out_shape=[jax.ShapeDtypeStruct((R, SWA_Q), F32),
                   jax.ShapeDtypeStruct(kc.shape, F32),
                   jax.ShapeDtypeStruct(vc.shape, F32)],
        compiler_params=_cparams(("parallel",)),
        name="swa_sample",
    )(swa, kc, vc, sinks)


def _outproj_ffn_kernel(x_ref, dn_ref, sw_ref, g5_ref, sh_ref, sc_ref, ga_ref, g_ref,
                        woa_ref, wob_ref, wgu_ref, wd_ref, o_ref):
    mix = (jnp.dot(dn_ref[...].astype(BF16), woa_ref[...], preferred_element_type=F32)
           + jnp.dot(sw_ref[...].astype(BF16), wob_ref[...], preferred_element_type=F32))
    x = x_ref[...] + g5_ref[...] * mix
    o_ref[...] = _ffn_body(x, sh_ref[...], sc_ref[...], ga_ref[...], g_ref[...], wgu_ref, wd_ref)


def _outproj_ffn(x, dn, sw, m, g, woa, wob, wgu, wd, l, tm):
    G, R, _ = x.shape
    g5 = _mod_specs(m, 1, tm)[2]
    return pl.pallas_call(
        _outproj_ffn_kernel,
        grid=(G, R // tm),
        in_specs=[_row_spec(tm, D_MODEL), _row_spec(tm, DN_V), _row_spec(tm, SWA_Q), g5]
        + _mod_specs(m, 2, tm) + [
            _const_spec((1, D_MODEL)),
            _layer_spec((DN_V, D_MODEL), l),
            _layer_spec((SWA_Q, D_MODEL), l),
            _layer_spec((D_MODEL, 2 * D_FF), l),
            _layer_spec((D_FF, D_MODEL), l),
        ],
        out_specs=_row_spec(tm, D_MODEL),
        out_shape=jax.ShapeDtypeStruct(x.shape, F32),
        compiler_params=_cparams(("parallel", "parallel")),
        name="outproj_ffn",
    )(x, dn, sw, m, m, m, m, g, woa, wob, wgu, wd)


def _rope_tables(pos):
    half = SWA_DH // 2
    inv = jnp.power(ROPE_THETA, -jnp.arange(half, dtype=F32) * 2.0 / SWA_DH)
    ang = pos.astype(F32)[:, None] * inv[None, :]
    cos, sin = jnp.cos(ang), jnp.sin(ang)
    reps = LANES // SWA_DH
    return (jnp.concatenate([cos, cos] * reps, axis=1),
            jnp.concatenate([-sin, sin] * reps, axis=1))


def _group_matrix(width):
    i = jnp.arange(width) // SWA_DH
    return jnp.where(i[:, None] == i[None, :], 1.0 / SWA_DH, 0.0).astype(BF16)


def _pack_w_in(w_in):
    conv = w_in[..., :DN_CONV_CH]
    o = DN_CONV_CH
    z = w_in[..., o:o + DN_V]
    o += DN_V
    ba = w_in[..., o:o + 2 * DN_HEADS]
    o += 2 * DN_HEADS
    sq = w_in[..., o:o + SWA_Q]
    o += SWA_Q
    sk = w_in[..., o:o + SWA_KV]
    o += SWA_KV
    sv = w_in[..., o:o + SWA_KV]
    pad = jnp.zeros(w_in.shape[:-1] + (LANES - 2 * DN_HEADS,), w_in.dtype)
    return jnp.concatenate([conv, z, sq, sk, sv, ba, pad], axis=-1).astype(BF16)


def _lane_row(vec, offset=0):
    return jnp.zeros((LANES,), F32).at[offset:offset + vec.shape[0]].set(vec.astype(F32))


def kernel(x_prompt, x_sample, c_prompt, c_sample, state_dn_conv, state_dn_S, cache_swa_k, cache_swa_v, w_ada, b_ada, g_ffn1, w_ffn1_gu, w_ffn1_down, g_mix, w_in, dn_conv_w, dn_A_log, dn_dt_bias, dn_norm_w, swa_q_norm, swa_k_norm, swa_sinks, w_out, g_ffn2, w_ffn2_gu, w_ffn2_down):
    L = w_ada.shape[0]
    B, T, _ = x_prompt.shape
    Bs = x_sample.shape[0]
    tm = min(512, T)
    bb = min(8, Bs)

    m_p, m_s = _adaln(c_prompt, c_sample, w_ada, b_ada)
    m_p = m_p.reshape(L, N_MOD, B, 1, D_MODEL)
    m_s = m_s.reshape(L, N_MOD, 1, Bs, D_MODEL)

    cos_p, sin_p = _rope_tables(jnp.arange(T))
    cos_s, sin_s = _rope_tables(PAST_LEN + jnp.arange(1))
    gq, gk = _group_matrix(SWA_Q), _group_matrix(SWA_KV)

    wgu1, wd1 = w_ffn1_gu.astype(BF16), w_ffn1_down.astype(BF16)
    wgu2, wd2 = w_ffn2_gu.astype(BF16), w_ffn2_down.astype(BF16)
    win = _pack_w_in(w_in)
    woa, wob = w_out[:, :DN_V].astype(BF16), w_out[:, DN_V:].astype(BF16)

    yp = x_prompt
    ys = x_sample.reshape(1, Bs, D_MODEL)
    outs = [[] for _ in range(8)]
    for l in range(L):
        g1, gm, g2 = (g_ffn1[l].reshape(1, D_MODEL), g_mix[l].reshape(1, D_MODEL),
                      g_ffn2[l].reshape(1, D_MODEL))
        cw = dn_conv_w[l]
        nw = dn_norm_w[l].reshape(1, DN_DV)
        prm = jnp.stack([
            _lane_row(-jnp.exp(dn_A_log[l].astype(F32)), DN_HEADS),
            _lane_row(dn_dt_bias[l], DN_HEADS),
            jnp.concatenate([swa_q_norm[l]] * (LANES // SWA_DH)).astype(F32),
            jnp.concatenate([swa_k_norm[l]] * (LANES // SWA_DH)).astype(F32),
        ] + [jnp.zeros((LANES,), F32)] * (SUBLANES - 4))
        sinks = _lane_row(swa_sinks[l]).reshape(1, LANES)

        yp = _ffn(yp, m_p[l], 0, g1, wgu1, wd1, l, tm)
        qkv, z, bg, swa, tail = _inproj_prompt(yp, m_p[l], gm, win, cw, prm, cos_p, sin_p, gq, gk,
                                               l, tm)
        dn_o, s_new = _delta_prompt(qkv, bg, z, nw, min(512, T))
        sw_o = _swa_prompt(swa, sinks, min(4, T // WINDOW))
        yp = _outproj_ffn(yp, dn_o, sw_o, m_p[l], g2, woa, wob, wgu2, wd2, l, tm)
        nkeep = min(WINDOW, T)
        outs[0].append(tail[:, HALO - (DN_CONV - 1):])
        outs[1].append(s_new)
        outs[2].append(swa[:, T - nkeep:, SWA_Q:SWA_Q + SWA_KV].reshape(B, nkeep, SWA_KV_HEADS, SWA_DH))
        outs[3].append(swa[:, T - nkeep:, SWA_Q + SWA_KV:].reshape(B, nkeep, SWA_KV_HEADS, SWA_DH))

        ys = _ffn(ys, m_s[l], 0, g1, wgu1, wd1, l, Bs)
        buf = jnp.transpose(state_dn_conv[l], (1, 0, 2))
        qkv, z, bg, swa, cnew = _inproj_sample(ys, m_s[l], gm, win, cw, prm, cos_s, sin_s, gq, gk,
                                               buf, l)
        dn_o, s_new = _delta_sample(qkv[0], bg[0], z[0], nw, state_dn_S, l, bb)
        w0 = cache_swa_k.shape[2]
        sw_o, k_new, v_new = _swa_sample(swa[0], cache_swa_k[l].reshape(Bs, w0, SWA_KV),
                                         cache_swa_v[l].reshape(Bs, w0, SWA_KV), sinks, bb)
        ys = _outproj_ffn(ys, dn_o[None], sw_o[None], m_s[l], g2, woa, wob, wgu2, wd2, l, Bs)
        outs[4].append(jnp.transpose(cnew, (1, 0, 2)))
        outs[5].append(s_new)
        outs[6].append(k_new.reshape(Bs, w0, SWA_KV_HEADS, SWA_DH))
        outs[7].append(v_new.reshape(Bs, w0, SWA_KV_HEADS, SWA_DH))

    return (yp, ys.reshape(Bs, 1, D_MODEL)) + tuple(jnp.stack(o) for o in outs)
```

```python
import functools
import math

import jax
import jax.numpy as jnp
from jax import lax
from jax.experimental import pallas as pl
from jax.experimental.pallas import tpu as pltpu

F32 = jnp.float32
BF16 = jnp.bfloat16

D_MODEL = 1024
D_FF = 2816
N_MOD = 9
EPS = 1e-6
DN_HEADS = 4
DN_DK = 128
DN_DV = 128
DN_QK = DN_HEADS * DN_DK
DN_V = DN_HEADS * DN_DV
DN_CONV = 4
DN_CONV_CH = 2 * DN_QK + DN_V
DN_CHUNK = 64
INV_BLOCK = 16
INTRA_ROWS = 256
SWA_DH = 64
SWA_HEADS = 8
SWA_KV_HEADS = 2
SWA_GROUP = SWA_HEADS // SWA_KV_HEADS
SWA_Q = SWA_HEADS * SWA_DH
SWA_KV = SWA_KV_HEADS * SWA_DH
WINDOW = 128
ROPE_THETA = 10000.0
PAST_LEN = 16384

LANES = 128
SUBLANES = 8
HALO = SUBLANES

C_CONV = 0
C_Z = C_CONV + DN_CONV_CH
C_SQ = C_Z + DN_V
C_SK = C_SQ + SWA_Q
C_SV = C_SK + SWA_KV
C_BG = C_SV + SWA_KV
IN_PACKED = C_BG + LANES
SWA_OUT = SWA_Q + 2 * SWA_KV

VMEM_LIMIT = 56 * 1024 * 1024


def _cparams(sem):
    return pltpu.CompilerParams(dimension_semantics=sem, vmem_limit_bytes=VMEM_LIMIT)


def _const_spec(shape):
    nd = len(shape)
    return pl.BlockSpec(shape, lambda *_: (0,) * nd, pipeline_mode=pl.Buffered(1))


def _layer_spec(shape, l):
    nd = len(shape)
    return pl.BlockSpec((None,) + tuple(shape), lambda *_: (l,) + (0,) * nd,
                        pipeline_mode=pl.Buffered(1))


def _sigmoid(x):
    return 1.0 / (1.0 + jnp.exp(-x))


def _silu(x):
    return x * _sigmoid(x)


def _adaln_kernel(cp_ref, cs_ref, w_ref, b_ref, mp_ref, ms_ref):
    w = w_ref[...].astype(BF16)
    b = b_ref[...]
    for c_ref, m_ref in ((cp_ref, mp_ref), (cs_ref, ms_ref)):
        c = c_ref[...]
        m_ref[...] = jnp.dot(_silu(c).astype(BF16), w, preferred_element_type=F32) + b


def _adaln(c_p, c_s, w_ada, b_ada):
    L = w_ada.shape[0]
    bp, bs = c_p.shape[0], c_s.shape[0]
    b4 = b_ada.reshape(L, N_MOD, 1, D_MODEL)
    return pl.pallas_call(
        _adaln_kernel,
        grid=(L, N_MOD),
        in_specs=[
            pl.BlockSpec((bp, D_MODEL), lambda l, j: (0, 0)),
            pl.BlockSpec((bs, D_MODEL), lambda l, j: (0, 0)),
            pl.BlockSpec((None, D_MODEL, D_MODEL), lambda l, j: (l, 0, j)),
            pl.BlockSpec((None, None, 1, D_MODEL), lambda l, j: (l, j, 0, 0)),
        ],
        out_specs=[
            pl.BlockSpec((None, None, bp, D_MODEL), lambda l, j: (l, j, 0, 0)),
            pl.BlockSpec((None, None, bs, D_MODEL), lambda l, j: (l, j, 0, 0)),
        ],
        out_shape=[
            jax.ShapeDtypeStruct((L, N_MOD, bp, D_MODEL), F32),
            jax.ShapeDtypeStruct((L, N_MOD, bs, D_MODEL), F32),
        ],
        compiler_params=_cparams(("parallel", "parallel")),
        name="adaln",
    )(c_p, c_s, w_ada, b4)


def _mod_norm(x, g, shift, scale):
    ms = jnp.mean(x * x, axis=-1, keepdims=True)
    return (x * lax.rsqrt(ms + EPS)) * (g * (1.0 + scale)) + shift


def _ffn_body(x, shift, scale, gate, g, wgu_ref, wd_ref):
    h = _mod_norm(x, g, shift, scale).astype(BF16)
    gu = jnp.dot(h, wgu_ref[...], preferred_element_type=F32)
    a = gu[:, :D_FF]
    b = gu[:, D_FF:]
    act = (_silu(a) * b).astype(BF16)
    y = jnp.dot(act, wd_ref[...], preferred_element_type=F32)
    return x + (0.5 * gate) * y


def _mod_specs(m, sub, tm):
    rm = m.shape[2]
    blk = 1 if rm == 1 else tm

    def spec(i):
        if rm == 1:
            return pl.BlockSpec((None, None, 1, D_MODEL), lambda g, r: (i, g, 0, 0))
        return pl.BlockSpec((None, None, blk, D_MODEL), lambda g, r: (i, g, r, 0))

    return [spec(3 * sub), spec(3 * sub + 1), spec(3 * sub + 2)]


def _row_spec(tm, width):
    return pl.BlockSpec((None, tm, width), lambda g, r: (g, r, 0))


def _ffn_kernel(x_ref, sh_ref, sc_ref, ga_ref, g_ref, wgu_ref, wd_ref, o_ref):
    o_ref[...] = _ffn_body(x_ref[...], sh_ref[...], sc_ref[...], ga_ref[...], g_ref[...],
                           wgu_ref, wd_ref)


def _ffn(x, m, sub, g, wgu, wd, l, tm):
    G, R, _ = x.shape
    return pl.pallas_call(
        _ffn_kernel,
        grid=(G, R // tm),
        in_specs=[_row_spec(tm, D_MODEL)] + _mod_specs(m, sub, tm) + [
            _const_spec((1, D_MODEL)),
            _layer_spec((D_MODEL, 2 * D_FF), l),
            _layer_spec((D_FF, D_MODEL), l),
        ],
        out_specs=_row_spec(tm, D_MODEL),
        out_shape=jax.ShapeDtypeStruct(x.shape, F32),
        compiler_params=_cparams(("parallel", "parallel")),
        name="ffn",
    )(x, m, m, m, g, wgu, wd)


def _group_mean_sq(x, gmat_ref):
    return jnp.dot((x * x).astype(BF16), gmat_ref[...], preferred_element_type=F32)


def _rope(x, cos, sin_signed):
    width = x.shape[-1]
    reps = width // LANES
    lane = lax.broadcasted_iota(jnp.int32, x.shape, 1)
    first_half = (lane & (SWA_DH - 1)) < (SWA_DH // 2)
    partner = jnp.where(first_half,
                        pltpu.roll(x, width - SWA_DH // 2, 1),
                        pltpu.roll(x, SWA_DH // 2, 1))
    if reps > 1:
        cos = jnp.concatenate([cos] * reps, axis=1)
        sin_signed = jnp.concatenate([sin_signed] * reps, axis=1)
    return x * cos + partner * sin_signed


def _swa_post(p, prm_ref, cos, sin, gq_ref, gk_ref, swa_ref, orows=slice(None)):
    pq = p[:, C_SQ:C_SQ + SWA_Q]
    pk = p[:, C_SK:C_SK + SWA_KV]
    wq = jnp.concatenate([prm_ref[2:3, :]] * (SWA_Q // LANES), axis=1)
    wk = prm_ref[3:4, :]
    qn = pq * lax.rsqrt(_group_mean_sq(pq, gq_ref) + EPS) * wq
    kn = pk * lax.rsqrt(_group_mean_sq(pk, gk_ref) + EPS) * wk
    swa_ref[orows, :SWA_Q] = _rope(qn, cos, sin)
    swa_ref[orows, SWA_Q:SWA_Q + SWA_KV] = _rope(kn, cos, sin)
    swa_ref[orows, SWA_Q + SWA_KV:] = p[:, C_SV:C_SV + SWA_KV]


def _dn_group(a, gi, qkv_ref):
    for hh in range(DN_HEADS):
        cols = slice(gi * DN_QK + hh * DN_DK, gi * DN_QK + (hh + 1) * DN_DK)
        xh = a[:, hh * DN_DK:(hh + 1) * DN_DK]
        if gi < 2:
            s = lax.rsqrt(jnp.sum(xh * xh, axis=-1, keepdims=True) + EPS)
            if gi == 0:
                s = s * (DN_DK ** -0.5)
            xh = xh * s
        qkv_ref[:, cols] = xh.astype(qkv_ref.dtype)


def _bg_post(pb, prm_ref, bg_ref):
    neg_a = prm_ref[0:1, :]
    dt_b = prm_ref[1:2, :]
    xa = pb + dt_b
    softplus = jnp.maximum(xa, 0.0) + jnp.log(1.0 + jnp.exp(-jnp.abs(xa)))
    lane = lax.broadcasted_iota(jnp.int32, pb.shape, 1)
    bg_ref[...] = jnp.where(lane < DN_HEADS, _sigmoid(pb), neg_a * softplus)


def _dn_post(p, qkv_conv, prm_ref, qkv_ref, z_ref, bg_ref):
    qkv = _silu(qkv_conv)
    for gi in range(3):
        _dn_group(qkv[:, gi * DN_QK:(gi + 1) * DN_QK], gi, qkv_ref)
    z_ref[...] = p[:, C_Z:C_Z + DN_V].astype(z_ref.dtype)
    _bg_post(p[:, C_BG:C_BG + LANES], prm_ref, bg_ref)


class _Cols:
    def __init__(self, segments):
        self.segments = segments

    def __getitem__(self, idx):
        cols = idx[1]
        for start, arr in self.segments:
            if start <= cols.start and cols.stop <= start + arr.shape[1]:
                return arr[:, cols.start - start:cols.stop - start]
        raise KeyError(cols)


def _inproj_prompt_kernel(x_ref, sh_ref, sc_ref, g_ref, win_ref, cw_ref, prm_ref, cos_ref, sin_ref,
                          gq_ref, gk_ref, qkv_ref, z_ref, bg_ref, swa_ref, tail_ref, ext_ref):
    tm = x_ref.shape[0]
    first = pl.program_id(1) == 0

    @pl.when(first)
    def _():
        ext_ref[0:HALO, :] = jnp.zeros((HALO, DN_CONV_CH), F32)

    @pl.when(jnp.logical_not(first))
    def _():
        ext_ref[0:HALO, :] = ext_ref[tm:tm + HALO, :]

    h = _mod_norm(x_ref[...], g_ref[...], sh_ref[...], sc_ref[...]).astype(BF16)
    proj = lambda a, b: jnp.dot(h, win_ref[:, a:b], preferred_element_type=F32)
    p_att = _Cols([(C_SQ, proj(C_SQ, C_SK)), (C_SK, proj(C_SK, IN_PACKED))])
    _swa_post(p_att, prm_ref, cos_ref[...], sin_ref[...], gq_ref, gk_ref, swa_ref)
    _bg_post(p_att[:, C_BG:C_BG + LANES], prm_ref, bg_ref)
    for gi in range(3):
        cs = slice(gi * DN_QK, (gi + 1) * DN_QK)
        pc = proj(cs.start, cs.stop)
        ext_ref[HALO:, cs] = pc
        y = pc * cw_ref[DN_CONV - 1:DN_CONV, cs]
        for j in range(DN_CONV - 1):
            off = HALO - (DN_CONV - 1) + j
            y = y + ext_ref[off:off + tm, cs] * cw_ref[j:j + 1, cs]
        _dn_group(_silu(y), gi, qkv_ref)
    tail_ref[...] = ext_ref[tm:tm + HALO, :]
    z_ref[...] = proj(C_Z, C_Z + DN_V).astype(z_ref.dtype)


def _inproj_sample_kernel(x_ref, sh_ref, sc_ref, g_ref, win_ref, cw_ref, prm_ref, cos_ref, sin_ref,
                          gq_ref, gk_ref, buf_ref, qkv_ref, z_ref, bg_ref, swa_ref, cnew_ref):
    h = _mod_norm(x_ref[...], g_ref[...], sh_ref[...], sc_ref[...]).astype(BF16)
    p = jnp.dot(h, win_ref[...], preferred_element_type=F32)
    pc = p[:, :DN_CONV_CH]
    y = pc * cw_ref[DN_CONV - 1:DN_CONV, :]
    for j in range(DN_CONV - 1):
        y = y + buf_ref[j] * cw_ref[j:j + 1, :]
    for j in range(DN_CONV - 2):
        cnew_ref[j] = buf_ref[j + 1]
    cnew_ref[DN_CONV - 2] = pc
    _swa_post(p, prm_ref, cos_ref[...], sin_ref[...], gq_ref, gk_ref, swa_ref)
    _dn_post(p, y, prm_ref, qkv_ref, z_ref, bg_ref)


def _inproj_common_specs(m, tm, l):
    sh, sc, _ = _mod_specs(m, 1, tm)
    return [_row_spec(tm, D_MODEL), sh, sc,
            _const_spec((1, D_MODEL)),
            _layer_spec((D_MODEL, IN_PACKED), l),
            _const_spec((DN_CONV, DN_CONV_CH)),
            _const_spec((SUBLANES, LANES))]


def _inproj_prompt(x, m, g, win, cw, prm, cos, sin, gq, gk, l, tm):
    B, T, _ = x.shape
    table = pl.BlockSpec((tm, LANES), lambda b, r: (r, 0))
    return pl.pallas_call(
        _inproj_prompt_kernel,
        grid=(B, T // tm),
        in_specs=_inproj_common_specs(m, tm, l) + [
            table, table,
            _const_spec((SWA_Q, SWA_Q)),
            _const_spec((SWA_KV, SWA_KV)),
        ],
        out_specs=[
            _row_spec(tm, DN_CONV_CH), _row_spec(tm, DN_V), _row_spec(tm, LANES),
            _row_spec(tm, SWA_OUT),
            pl.BlockSpec((None, HALO, DN_CONV_CH), lambda b, r: (b, 0, 0)),
        ],
        out_shape=[
            jax.ShapeDtypeStruct((B, T, DN_CONV_CH), BF16),
            jax.ShapeDtypeStruct((B, T, DN_V), BF16),
            jax.ShapeDtypeStruct((B, T, LANES), F32),
            jax.ShapeDtypeStruct((B, T, SWA_OUT), F32),
            jax.ShapeDtypeStruct((B, HALO, DN_CONV_CH), F32),
        ],
        scratch_shapes=[pltpu.VMEM((HALO + tm, DN_CONV_CH), F32)],
        compiler_params=_cparams(("parallel", "arbitrary")),
        name="inproj_prompt",
    )(x, m, m, g, win, cw, prm, cos, sin, gq, gk)


def _inproj_sample(x, m, g, win, cw, prm, cos, sin, gq, gk, buf, l):
    _, R, _ = x.shape
    nbuf = DN_CONV - 1
    return pl.pallas_call(
        _inproj_sample_kernel,
        grid=(1, 1),
        in_specs=_inproj_common_specs(m, R, l) + [
            _const_spec((1, LANES)), _const_spec((1, LANES)),
            _const_spec((SWA_Q, SWA_Q)), _const_spec((SWA_KV, SWA_KV)),
            _const_spec((nbuf, R, DN_CONV_CH)),
        ],
        out_specs=[
            _row_spec(R, DN_CONV_CH), _row_spec(R, DN_V), _row_spec(R, LANES), _row_spec(R, SWA_OUT),
            pl.BlockSpec((nbuf, R, DN_CONV_CH), lambda g_, r: (0, 0, 0)),
        ],
        out_shape=[
            jax.ShapeDtypeStruct((1, R, DN_CONV_CH), F32),
            jax.ShapeDtypeStruct((1, R, DN_V), F32),
            jax.ShapeDtypeStruct((1, R, LANES), F32),
            jax.ShapeDtypeStruct((1, R, SWA_OUT), F32),
            jax.ShapeDtypeStruct((nbuf, R, DN_CONV_CH), F32),
        ],
        compiler_params=_cparams(("arbitrary", "arbitrary")),
        name="inproj_sample",
    )(x, m, m, g, win, cw, prm, cos, sin, gq, gk, buf)


def _gated_norm(o, z, nw):
    on = o * lax.rsqrt(jnp.mean(o * o, axis=-1, keepdims=True) + EPS) * nw
    return on * _silu(z.astype(F32))


def _mm(a, b):
    return jnp.dot(a.astype(BF16), b.astype(BF16), preferred_element_type=F32)


def _wide_to_blockdiag(wide):
    c, n = wide.shape[0], wide.shape[1] // wide.shape[0]
    cblk = lax.broadcasted_iota(jnp.int32, wide.shape, 1) // c
    return jnp.concatenate([jnp.where(cblk == i, wide, 0.0) for i in range(n)], axis=0)


def _blockdiag_to_wide(full, c):
    n = full.shape[0] // c
    cblk = lax.broadcasted_iota(jnp.int32, (c, full.shape[1]), 1) // c
    wide = full[0:c]
    for i in range(1, n):
        wide = jnp.where(cblk == i, full[i * c:(i + 1) * c], wide)
    return wide


def _unit_lower_inverses(mats):
    c = mats[0].shape[0]
    row = lax.broadcasted_iota(jnp.int32, mats[0].shape, 0)
    col = lax.broadcasted_iota(jnp.int32, mats[0].shape, 1) % c
    s = INV_BLOCK
    diag = (row // s) == (col // s)
    pws = [jnp.where(diag, a, 0.0) for a in mats]
    eye = jnp.where(row == col, 1.0, 0.0)
    xs = [eye - pw for pw in pws]
    k = 2
    while k < s:
        pws = [_mm(pw, _wide_to_blockdiag(pw)) for pw in pws]
        xs = [x + _mm(x, _wide_to_blockdiag(pw)) for x, pw in zip(xs, pws)]
        k *= 2
    while s < c:
        pair = ((row // (2 * s)) == (col // (2 * s))) & ((row // s) != (col // s))
        ts = [_mm(x, _wide_to_blockdiag(jnp.where(pair, a, 0.0))) for x, a in zip(xs, mats)]
        xs = [x - _mm(t, _wide_to_blockdiag(x)) for x, t in zip(xs, ts)]
        s *= 2
    return xs


def _cumsum_rows(x):
    n = x.shape[0]
    row = lax.broadcasted_iota(jnp.int32, x.shape, 0)
    s = 1
    while s < n:
        x = x + jnp.where(row >= s, pltpu.roll(x, s, 0), 0.0)
        s *= 2
    return x


WQ_ROWS = 2 * DN_CHUNK
QKD_ROWS = DN_CHUNK + DN_DK


def _delta_intra_kernel(q_ref, k_ref, v_ref, bg_ref, u_ref, wq_ref, qkd_ref, eg_ref):
    c = DN_CHUNK
    r = INTRA_ROWS
    nc = r // c
    row = lax.broadcasted_iota(jnp.int32, (c, r), 0)
    col = lax.broadcasted_iota(jnp.int32, (c, r), 1) % c
    incl = row >= col
    strict = row > col

    jobs = [(blk, h) for blk in range(q_ref.shape[0] // r) for h in range(DN_HEADS)]
    gcs, egcs, ekds, gcts, bgs = [], [], [], [], []
    for blk in range(q_ref.shape[0] // r):
        bg = bg_ref[blk * r:(blk + 1) * r, :]
        gc = jnp.concatenate([_cumsum_rows(bg[ci * c:(ci + 1) * c, :]) for ci in range(nc)], axis=0)
        g_end = [gc[(ci + 1) * c - 1:(ci + 1) * c, :] for ci in range(nc)]
        glast = jnp.concatenate([jnp.broadcast_to(g, (c, LANES)) for g in g_end], axis=0)
        for ci in range(nc):
            rows = [jnp.broadcast_to(jnp.exp(g_end[ci][:, DN_HEADS + h:DN_HEADS + h + 1]), (1, LANES))
                    for h in range(DN_HEADS)]
            eg_ref[blk * nc + ci] = jnp.concatenate(
                rows + [jnp.zeros((SUBLANES - DN_HEADS, LANES), F32)], axis=0)
        bgs.append(bg)
        gcs.append(gc)
        gcts.append(gc.T)
        egcs.append(jnp.exp(gc))
        ekds.append(jnp.exp(glast - gc))

    prep = []
    for blk, h in jobs:
        rs = slice(blk * r, (blk + 1) * r)
        sl = slice(h * DN_DK, (h + 1) * DN_DK)
        q = q_ref[rs, sl]
        k = k_ref[rs, sl]
        kb = k * bgs[blk][:, h:h + 1]
        kq = lax.dot_general(jnp.concatenate([kb, q], axis=0).astype(BF16), k.astype(BF16),
                             (((1,), (1,)), ((), ())), preferred_element_type=F32)
        prep.append((q, k, kb, kq))
    a_mats, qks = [], []
    for (blk, h), (q, k, kb, kq) in zip(jobs, prep):
        gl = slice(DN_HEADS + h, DN_HEADS + h + 1)
        gcol = jnp.concatenate([jnp.broadcast_to(gcs[blk][ci * c:(ci + 1) * c, gl], (c, c))
                                for ci in range(nc)], axis=1)
        decay = jnp.exp(jnp.where(incl, gcol - gcts[blk][gl, :], -jnp.inf))
        a_mats.append(jnp.where(strict, _blockdiag_to_wide(kq[:r], c) * decay, 0.0))
        qks.append(_blockdiag_to_wide(kq[r:], c) * decay)
    tinvs = _unit_lower_inverses(a_mats)
    uws = []
    for (blk, h), (q, k, kb, kq), tinv in zip(jobs, prep, tinvs):
        rs = slice(blk * r, (blk + 1) * r)
        sl = slice(h * DN_DK, (h + 1) * DN_DK)
        gl = slice(DN_HEADS + h, DN_HEADS + h + 1)
        rhs = jnp.concatenate([v_ref[rs, sl] * bgs[blk][:, h:h + 1], kb * egcs[blk][:, gl]], axis=1)
        uws.append(_mm(_wide_to_blockdiag(tinv), rhs))
    for (blk, h), (q, k, kb, kq), uw, qk in zip(jobs, prep, uws, qks):
        rs = slice(blk * r, (blk + 1) * r)
        sl = slice(h * DN_DK, (h + 1) * DN_DK)
        gl = slice(DN_HEADS + h, DN_HEADS + h + 1)
        u_ref[rs, sl] = uw[:, :DN_DV]
        w_b = uw[:, DN_DV:].astype(BF16)
        qd_b = (q * egcs[blk][:, gl]).astype(BF16)
        for ci in range(nc):
            wq_ref[blk * nc + ci, h * WQ_ROWS:h * WQ_ROWS + c, :] = w_b[ci * c:(ci + 1) * c, :]
            wq_ref[blk * nc + ci, h * WQ_ROWS + c:(h + 1) * WQ_ROWS, :] = qd_b[ci * c:(ci + 1) * c, :]
        kdt = (k * ekds[blk][:, gl]).T
        for pj in range(nc // 2):
            ls = slice(pj * LANES, (pj + 1) * LANES)
            pi = blk * (nc // 2) + pj
            qkd_ref[pi, h * QKD_ROWS:h * QKD_ROWS + c, :] = qk[:, ls].astype(BF16)
            qkd_ref[pi, h * QKD_ROWS + c:(h + 1) * QKD_ROWS, :] = kdt[:, ls].astype(BF16)


def _delta_scan_kernel(u_ref, wq_ref, qkd_ref, eg_ref, o_ref, sout_ref, s_ref):
    n = pl.program_id(0)
    c = DN_CHUNK

    @pl.when(n == 0)
    def _():
        s_ref[...] = jnp.zeros_like(s_ref)

    zeros = jnp.zeros((c, DN_DV), BF16)
    chains = [(b, h) for b in range(u_ref.shape[0]) for h in range(DN_HEADS)]
    for j in range(2):
        rs = slice(j * c, (j + 1) * c)
        ws = [jnp.dot(wq_ref[b, j, h * WQ_ROWS:(h + 1) * WQ_ROWS, :], s_ref[b, h].astype(BF16),
                      preferred_element_type=F32) for b, h in chains]
        rr = []
        for (b, h), w_h in zip(chains, ws):
            v_new = (u_ref[b, rs, h * DN_DV:(h + 1) * DN_DV] - w_h[:c]).astype(BF16)
            rhs = jnp.concatenate([v_new, zeros] if j == 0 else [zeros, v_new], axis=0)
            rr.append(jnp.dot(qkd_ref[b, h * QKD_ROWS:(h + 1) * QKD_ROWS, :], rhs,
                              preferred_element_type=F32))
        for (b, h), w_h, r_h in zip(chains, ws, rr):
            sl = slice(h * DN_DV, (h + 1) * DN_DV)
            s_ref[b, h] = s_ref[b, h] * eg_ref[b, j, h:h + 1, :] + r_h[c:]
            o_ref[b, rs, sl] = (w_h[c:] + r_h[:c]).astype(o_ref.dtype)

    @pl.when(n == pl.num_programs(0) - 1)
    def _():
        sout_ref[...] = s_ref[...]


def _delta_prompt(qkv, bg, ra):
    B, T, _ = qkv.shape
    c = DN_CHUNK
    n = T // c
    cpb = ra // c
    qkv_spec = lambda j: pl.BlockSpec((None, ra, DN_QK), lambda b, r: (b, r, j))
    u, wq, qkd, eg = pl.pallas_call(
        _delta_intra_kernel,
        grid=(B, T // ra),
        in_specs=[qkv_spec(0), qkv_spec(1), qkv_spec(2),
                  pl.BlockSpec((None, ra, LANES), lambda b, r: (b, r, 0))],
        out_specs=[pl.BlockSpec((None, ra, DN_V), lambda b, r: (b, r, 0)),
                   pl.BlockSpec((None, cpb, DN_HEADS * WQ_ROWS, DN_DK), lambda b, r: (b, r, 0, 0)),
                   pl.BlockSpec((None, cpb // 2, DN_HEADS * QKD_ROWS, LANES),
                                lambda b, r: (b, r, 0, 0)),
                   pl.BlockSpec((None, cpb, SUBLANES, LANES), lambda b, r: (b, r, 0, 0))],
        out_shape=[jax.ShapeDtypeStruct((B, T, DN_V), F32),
                   jax.ShapeDtypeStruct((B, n, DN_HEADS * WQ_ROWS, DN_DK), BF16),
                   jax.ShapeDtypeStruct((B, n // 2, DN_HEADS * QKD_ROWS, LANES), BF16),
                   jax.ShapeDtypeStruct((B, n, SUBLANES, LANES), F32)],
        compiler_params=_cparams(("parallel", "parallel")),
        name="delta_intra",
    )(qkv, qkv, qkv, bg)
    return pl.pallas_call(
        _delta_scan_kernel,
        grid=(n // 2,),
        in_specs=[pl.BlockSpec((B, 2 * c, DN_V), lambda i: (0, i, 0)),
                  pl.BlockSpec((B, 2, DN_HEADS * WQ_ROWS, DN_DK), lambda i: (0, i, 0, 0)),
                  pl.BlockSpec((B, None, DN_HEADS * QKD_ROWS, LANES), lambda i: (0, i, 0, 0)),
                  pl.BlockSpec((B, 2, SUBLANES, LANES), lambda i: (0, i, 0, 0))],
        out_specs=[pl.BlockSpec((B, 2 * c, DN_V), lambda i: (0, i, 0)),
                   pl.BlockSpec((B, DN_HEADS, DN_DK, DN_DV), lambda i: (0, 0, 0, 0))],
        out_shape=[jax.ShapeDtypeStruct((B, T, DN_V), BF16),
                   jax.ShapeDtypeStruct((B, DN_HEADS, DN_DK, DN_DV), F32)],
        scratch_shapes=[pltpu.VMEM((B, DN_HEADS, DN_DK, DN_DV), F32)],
        compiler_params=_cparams(("arbitrary",)),
        name="delta_scan",
    )(u, wq, qkd, eg)


def _delta_sample_kernel(qkv_ref, bg_ref, z_ref, nw_ref, s0_ref, o_ref, s_ref):
    bb = qkv_ref.shape[0]
    nw = nw_ref[...]
    zeros = jnp.zeros((SUBLANES - 2, DN_DK), F32)
    row = lax.broadcasted_iota(jnp.int32, (bb, DN_DV), 0)
    heads = []
    for h in range(DN_HEADS):
        q = qkv_ref[:, h * DN_DK:(h + 1) * DN_DK]
        k = qkv_ref[:, DN_QK + h * DN_DK:DN_QK + (h + 1) * DN_DK]
        v = qkv_ref[:, 2 * DN_QK + h * DN_DV:2 * DN_QK + (h + 1) * DN_DV]
        beta = bg_ref[:, h:h + 1]
        eg = jnp.exp(bg_ref[:, DN_HEADS + h:DN_HEADS + h + 1])
        w = k * beta * eg
        qd = q * eg
        ws = [jnp.dot(jnp.concatenate([w[i:i + 1], qd[i:i + 1], zeros], axis=0), s0_ref[i, h],
                      preferred_element_type=F32) for i in range(bb)]
        heads.append((q, k, v, beta, eg, ws))
    updates = []
    for q, k, v, beta, eg, ws in heads:
        w_s = jnp.concatenate([ws[i][0:1] for i in range(bb)], axis=0)
        q_s = jnp.concatenate([ws[i][1:2] for i in range(bb)], axis=0)
        v_new = v * beta - w_s
        kt = k.T
        kv = [jnp.dot(kt, jnp.where(row == i, v_new, 0.0), preferred_element_type=F32)
              for i in range(bb)]
        updates.append((v_new, q_s, kv))
    for h, ((q, k, v, beta, eg, ws), (v_new, q_s, kv)) in enumerate(zip(heads, updates)):
        sl = slice(h * DN_DK, (h + 1) * DN_DK)
        for i in range(bb):
            s_ref[i, h] = s0_ref[i, h] * eg[i:i + 1] + kv[i]
        o = q_s + jnp.sum(q * k, axis=-1, keepdims=True) * v_new
        o_ref[:, sl] = _gated_norm(o, z_ref[:, sl], nw)


def _delta_sample(qkv, bg, z, nw, s0, l, bb):
    R = qkv.shape[0]
    return pl.pallas_call(
        _delta_sample_kernel,
        grid=(R // bb,),
        in_specs=[pl.BlockSpec((bb, DN_CONV_CH), lambda i: (i, 0)),
                  pl.BlockSpec((bb, LANES), lambda i: (i, 0)),
                  pl.BlockSpec((bb, DN_V), lambda i: (i, 0)),
                  _const_spec((1, DN_DV)),
                  pl.BlockSpec((None, bb, DN_HEADS, DN_DK, DN_DV), lambda i: (l, i, 0, 0, 0))],
        out_specs=[pl.BlockSpec((bb, DN_V), lambda i: (i, 0)),
                   pl.BlockSpec((bb, DN_HEADS, DN_DK, DN_DV), lambda i: (i, 0, 0, 0))],
        out_shape=[jax.ShapeDtypeStruct((R, DN_V), F32),
                   jax.ShapeDtypeStruct(s0.shape[1:], F32)],
        compiler_params=_cparams(("parallel",)),
        name="delta_sample",
    )(qkv, bg, z, nw, s0)


def _swa_prompt_kernel(q_ref, kc_ref, vc_ref, kp_ref, vp_ref, sink_ref, o_ref):
    n = pl.program_id(1)
    w = WINDOW
    nq = q_ref.shape[0] // w
    half = SWA_DH
    key = lax.broadcasted_iota(jnp.int32, (2 * w, w), 0)
    qry = lax.broadcasted_iota(jnp.int32, (2 * w, w), 1)
    band = (key >= qry) & (key <= qry + w)
    bias_mid = jnp.where(band, 0.0, -jnp.inf)
    bias_first = jnp.where(band & (key >= jnp.where(n == 0, w, 0)), 0.0, -jnp.inf)
    kall = jnp.concatenate([kp_ref[...], kc_ref[...]], axis=0)
    vall = jnp.concatenate([vp_ref[...], vc_ref[...]], axis=0)
    low = lax.broadcasted_iota(jnp.int32, kall.shape, 1) < half
    kall_r = pltpu.roll(kall, half, 1)
    k_lo = [jnp.where(low, kall, 0.0).astype(BF16), jnp.where(low, kall_r, 0.0).astype(BF16)]
    k_hi = [jnp.where(low, 0.0, kall_r).astype(BF16), jnp.where(low, 0.0, kall).astype(BF16)]
    v_t = vall.T.astype(BF16)
    zeros_v = jnp.zeros((half, 2 * w), BF16)
    sum_rows = 2 * SUBLANES
    ones8 = jnp.ones((sum_rows, 2 * w), BF16)
    zeros8 = jnp.zeros((sum_rows, 2 * w), BF16)
    first_pair = lax.broadcasted_iota(jnp.int32, (1, 2 * w), 1) < w
    log2e = math.log2(math.e)
    scale = SWA_DH ** -0.5 * log2e

    units = [(i, j) for i in range(nq) for j in range(SWA_KV_HEADS)]
    scores = []
    for i, j in units:
        rs = slice(i * w, (i + 2) * w)
        kk = jnp.concatenate([k_lo[j][rs], k_hi[j][rs]], axis=0)
        c0 = j * SWA_GROUP * SWA_DH
        q = jnp.concatenate([q_ref[i * w:(i + 1) * w, c0:c0 + LANES],
                             q_ref[i * w:(i + 1) * w, c0 + LANES:c0 + 2 * LANES]], axis=0)
        scores.append(lax.dot_general(kk, (q * scale).astype(BF16), (((1,), (1,)), ((), ())),
                                      preferred_element_type=F32))
    probs, sink_rows, maxima = [], [], []
    for (i, j), s in zip(units, scores):
        bias = bias_first if i == 0 else bias_mid
        bias2 = jnp.concatenate([bias, bias], axis=1)
        p_hf, sk_hf, mx_hf = [], [], []
        for hf in range(2):
            heads = [j * SWA_GROUP + 2 * pr + hf for pr in range(2)]
            sink = log2e * jnp.where(first_pair, sink_ref[0:1, heads[0]:heads[0] + 1],
                                     sink_ref[0:1, heads[1]:heads[1] + 1])
            masked = s[hf * 2 * w:(hf + 1) * 2 * w, :] + bias2
            mx = jnp.maximum(jnp.max(masked, axis=0, keepdims=True), sink)
            p_hf.append(jnp.exp2(masked - mx).astype(BF16))
            sk_hf.append(sink)
            mx_hf.append(mx)
        probs.append(jnp.concatenate(p_hf, axis=0))
        sink_rows.append(sk_hf)
        maxima.append(mx_hf)
    outs = []
    for (i, j), p in zip(units, probs):
        vt = v_t[j * half:(j + 1) * half, i * w:(i + 2) * w]
        lhs = jnp.concatenate([jnp.concatenate([vt, zeros_v], axis=1),
                               jnp.concatenate([zeros_v, vt], axis=1),
                               jnp.concatenate([ones8, zeros8], axis=1),
                               jnp.concatenate([zeros8, ones8], axis=1)], axis=0)
        outs.append(jnp.dot(lhs, p, preferred_element_type=F32))
    for (i, j), o_t, sk_hf, mx_hf in zip(units, outs, sink_rows, maxima):
        c0 = j * SWA_GROUP * SWA_DH
        parts = []
        for hf in range(2):
            den = (o_t[2 * half + hf * sum_rows:2 * half + hf * sum_rows + 1, :]
                   + jnp.exp2(sk_hf[hf] - mx_hf[hf]))
            parts.append(o_t[hf * half:(hf + 1) * half, :] / den)
        o = jnp.concatenate(parts, axis=0).T
        for pr in range(2):
            o_ref[i * w:(i + 1) * w, c0 + pr * LANES:c0 + (pr + 1) * LANES] = (
                o[pr * w:(pr + 1) * w, :].astype(o_ref.dtype))


def _swa_prompt(swa, sinks, nq):
    B, T, _ = swa.shape
    w = WINDOW
    rq = nq * w
    kcol = SWA_Q // SWA_KV
    prev = lambda n: jnp.maximum(n * nq - 1, 0)
    return pl.pallas_call(
        _swa_prompt_kernel,
        grid=(B, T // rq),
        in_specs=[pl.BlockSpec((None, rq, SWA_Q), lambda b, n: (b, n, 0)),
                  pl.BlockSpec((None, rq, SWA_KV), lambda b, n: (b, n, kcol)),
                  pl.BlockSpec((None, rq, SWA_KV), lambda b, n: (b, n, kcol + 1)),
                  pl.BlockSpec((None, w, SWA_KV), lambda b, n: (b, prev(n), kcol)),
                  pl.BlockSpec((None, w, SWA_KV), lambda b, n: (b, prev(n), kcol + 1)),
                  _const_spec((1, LANES))],
        out_specs=pl.BlockSpec((None, rq, SWA_Q), lambda b, n: (b, n, 0)),
        out_shape=jax.ShapeDtypeStruct((B, T, SWA_Q), BF16),
        compiler_params=_cparams(("parallel", "parallel")),
        name="swa_prompt",
    )(swa, swa, swa, swa, swa, sinks)


def _swa_sample_kernel(swa_ref, kc_ref, vc_ref, sink_ref, o_ref, kn_ref, vn_ref):
    bb = swa_ref.shape[0]
    w = kc_ref.shape[1]
    scale = SWA_DH ** -0.5
    for i in range(bb):
        kn_ref[i, 0:w - 1, :] = kc_ref[i, 1:w, :]
        kn_ref[i, w - 1:w, :] = swa_ref[i:i + 1, SWA_Q:SWA_Q + SWA_KV]
        vn_ref[i, 0:w - 1, :] = vc_ref[i, 1:w, :]
        vn_ref[i, w - 1:w, :] = swa_ref[i:i + 1, SWA_Q + SWA_KV:]
    units = [(i, j) for i in range(bb) for j in range(SWA_KV_HEADS)]
    sinks = [jnp.concatenate(
        [sink_ref[0:1, j * SWA_GROUP + g:j * SWA_GROUP + g + 1] for g in range(SWA_GROUP)]
        + [jnp.zeros((SUBLANES - SWA_GROUP, 1), F32)], axis=0) for j in range(SWA_KV_HEADS)]
    qs, scs = [], []
    for i, j in units:
        q4 = jnp.concatenate(
            [swa_ref[i:i + 1, (j * SWA_GROUP + g) * SWA_DH:(j * SWA_GROUP + g + 1) * SWA_DH]
             for g in range(SWA_GROUP)] + [jnp.zeros((SUBLANES - SWA_GROUP, SWA_DH), F32)], axis=0)
        qs.append(q4)
        scs.append(lax.dot_general(q4, kc_ref[i, :, j * SWA_DH:(j + 1) * SWA_DH],
                                   (((1,), (1,)), ((), ())), preferred_element_type=F32) * scale)
    soft = []
    for (i, j), q4, sc in zip(units, qs, scs):
        knew = swa_ref[i:i + 1, SWA_Q + j * SWA_DH:SWA_Q + (j + 1) * SWA_DH]
        sn = jnp.sum(q4 * knew, axis=-1, keepdims=True) * scale
        mx = jnp.maximum(jnp.maximum(jnp.max(sc, axis=-1, keepdims=True), sn), sinks[j])
        pc = jnp.exp(sc - mx)
        pn = jnp.exp(sn - mx)
        den = jnp.sum(pc, axis=-1, keepdims=True) + pn + jnp.exp(sinks[j] - mx)
        soft.append((pc, pn, den))
    pvs = [jnp.dot(pc, vc_ref[i, :, j * SWA_DH:(j + 1) * SWA_DH], preferred_element_type=F32)
           for (i, j), (pc, pn, den) in zip(units, soft)]
    for (i, j), (pc, pn, den), pv in zip(units, soft, pvs):
        vnew = swa_ref[i:i + 1, SWA_Q + SWA_KV + j * SWA_DH:SWA_Q + SWA_KV + (j + 1) * SWA_DH]
        o = (pv + pn * vnew) / den
        for g in range(SWA_GROUP):
            hq = j * SWA_GROUP + g
            o_ref[i:i + 1, hq * SWA_DH:(hq + 1) * SWA_DH] = o[g:g + 1, :]


def _swa_sample(swa, kc, vc, sinks, bb):
    R = swa.shape[0]
    w = kc.shape[1]
    cache_spec = pl.BlockSpec((bb, w, SWA_KV), lambda i: (i, 0, 0))
    return pl.pallas_call(
        _swa_sample_kernel,
        grid=(R // bb,),
        in_specs=[pl.BlockSpec((bb, SWA_OUT), lambda i: (i, 0)), cache_spec, cache_spec,
                  _const_spec((1, LANES))],
        out_specs=[pl.BlockSpec((bb, SWA_Q), lambda i: (i, 0)), cache_spec, cache_spec],
        out_shape=[jax.ShapeDtypeStruct((R, SWA_Q), F32),
                   jax.ShapeDtypeStruct(kc.shape, F32),
                   jax.ShapeDtypeStruct(vc.shape, F32)],
        compiler_params=_cparams(("parallel",)),
        name="swa_sample",
    )(swa, kc, vc, sinks)


def _outproj_ffn_kernel(gate_dn, x_ref, dn_ref, sw_ref, *refs):
    g5_ref, sh_ref, sc_ref, ga_ref, g_ref, woa_ref, wob_ref, wgu_ref, wd_ref, o_ref = refs[-10:]
    if gate_dn:
        z_ref, nw_ref = refs[:2]
        nw = nw_ref[...]
        dn = jnp.concatenate(
            [_gated_norm(dn_ref[:, h * DN_DV:(h + 1) * DN_DV].astype(F32),
                         z_ref[:, h * DN_DV:(h + 1) * DN_DV], nw).astype(BF16)
             for h in range(DN_HEADS)], axis=1)
    else:
        dn = dn_ref[...].astype(BF16)
    mix = (jnp.dot(dn, woa_ref[...], preferred_element_type=F32)
           + jnp.dot(sw_ref[...].astype(BF16), wob_ref[...], preferred_element_type=F32))
    x = x_ref[...] + g5_ref[...] * mix
    o_ref[...] = _ffn_body(x, sh_ref[...], sc_ref[...], ga_ref[...], g_ref[...], wgu_ref, wd_ref)


def _outproj_ffn(x, dn, sw, m, g, woa, wob, wgu, wd, l, tm, z=None, nw=None):
    G, R, _ = x.shape
    g5 = _mod_specs(m, 1, tm)[2]
    gate_dn = z is not None
    gate_specs = [_row_spec(tm, DN_V), _const_spec((1, DN_DV))] if gate_dn else []
    gate_args = (z, nw) if gate_dn else ()
    return pl.pallas_call(
        functools.partial(_outproj_ffn_kernel, gate_dn),
        grid=(G, R // tm),
        in_specs=[_row_spec(tm, D_MODEL), _row_spec(tm, DN_V), _row_spec(tm, SWA_Q)]
        + gate_specs + [g5] + _mod_specs(m, 2, tm) + [
            _const_spec((1, D_MODEL)),
            _layer_spec((DN_V, D_MODEL), l),
            _layer_spec((SWA_Q, D_MODEL), l),
            _layer_spec((D_MODEL, 2 * D_FF), l),
            _layer_spec((D_FF, D_MODEL), l),
        ],
        out_specs=_row_spec(tm, D_MODEL),
        out_shape=jax.ShapeDtypeStruct(x.shape, F32),
        compiler_params=_cparams(("parallel", "parallel")),
        name="outproj_ffn",
    )(x, dn, sw, *gate_args, m, m, m, m, g, woa, wob, wgu, wd)


def _rope_tables(pos):
    half = SWA_DH // 2
    inv = jnp.power(ROPE_THETA, -jnp.arange(half, dtype=F32) * 2.0 / SWA_DH)
    ang = pos.astype(F32)[:, None] * inv[None, :]
    cos, sin = jnp.cos(ang), jnp.sin(ang)
    reps = LANES // SWA_DH
    return (jnp.concatenate([cos, cos] * reps, axis=1),
            jnp.concatenate([-sin, sin] * reps, axis=1))


def _group_matrix(width):
    i = jnp.arange(width) // SWA_DH
    return jnp.where(i[:, None] == i[None, :], 1.0 / SWA_DH, 0.0).astype(BF16)


def _pack_w_in(w_in):
    conv = w_in[..., :DN_CONV_CH]
    o = DN_CONV_CH
    z = w_in[..., o:o + DN_V]
    o += DN_V
    ba = w_in[..., o:o + 2 * DN_HEADS]
    o += 2 * DN_HEADS
    sq = w_in[..., o:o + SWA_Q]
    o += SWA_Q
    sk = w_in[..., o:o + SWA_KV]
    o += SWA_KV
    sv = w_in[..., o:o + SWA_KV]
    pad = jnp.zeros(w_in.shape[:-1] + (LANES - 2 * DN_HEADS,), w_in.dtype)
    return jnp.concatenate([conv, z, sq, sk, sv, ba, pad], axis=-1).astype(BF16)


def _lane_row(vec, offset=0):
    return jnp.zeros((LANES,), F32).at[offset:offset + vec.shape[0]].set(vec.astype(F32))


def kernel(x_prompt, x_sample, c_prompt, c_sample, state_dn_conv, state_dn_S, cache_swa_k, cache_swa_v, w_ada, b_ada, g_ffn1, w_ffn1_gu, w_ffn1_down, g_mix, w_in, dn_conv_w, dn_A_log, dn_dt_bias, dn_norm_w, swa_q_norm, swa_k_norm, swa_sinks, w_out, g_ffn2, w_ffn2_gu, w_ffn2_down):
    L = w_ada.shape[0]
    B, T, _ = x_prompt.shape
    Bs = x_sample.shape[0]
    tm = min(512, T)
    bb = min(8, Bs)

    m_p, m_s = _adaln(c_prompt, c_sample, w_ada, b_ada)
    m_p = m_p.reshape(L, N_MOD, B, 1, D_MODEL)
    m_s = m_s.reshape(L, N_MOD, 1, Bs, D_MODEL)

    cos_p, sin_p = _rope_tables(jnp.arange(T))
    cos_s, sin_s = _rope_tables(PAST_LEN + jnp.arange(1))
    gq, gk = _group_matrix(SWA_Q), _group_matrix(SWA_KV)

    wgu1, wd1 = w_ffn1_gu.astype(BF16), w_ffn1_down.astype(BF16)
    wgu2, wd2 = w_ffn2_gu.astype(BF16), w_ffn2_down.astype(BF16)
    win = _pack_w_in(w_in)
    woa, wob = w_out[:, :DN_V].astype(BF16), w_out[:, DN_V:].astype(BF16)

    yp = x_prompt
    ys = x_sample.reshape(1, Bs, D_MODEL)
    outs = [[] for _ in range(8)]
    for l in range(L):
        g1, gm, g2 = (g_ffn1[l].reshape(1, D_MODEL), g_mix[l].reshape(1, D_MODEL),
                      g_ffn2[l].reshape(1, D_MODEL))
        cw = dn_conv_w[l]
        nw = dn_norm_w[l].reshape(1, DN_DV)
        prm = jnp.stack([
            _lane_row(-jnp.exp(dn_A_log[l].astype(F32)), DN_HEADS),
            _lane_row(dn_dt_bias[l], DN_HEADS),
            jnp.concatenate([swa_q_norm[l]] * (LANES // SWA_DH)).astype(F32),
            jnp.concatenate([swa_k_norm[l]] * (LANES // SWA_DH)).astype(F32),
        ] + [jnp.zeros((LANES,), F32)] * (SUBLANES - 4))
        sinks = _lane_row(swa_sinks[l]).reshape(1, LANES)

        yp = _ffn(yp, m_p[l], 0, g1, wgu1, wd1, l, tm)
        qkv, z, bg, swa, tail = _inproj_prompt(yp, m_p[l], gm, win, cw, prm, cos_p, sin_p, gq, gk,
                                               l, tm)
        dn_o, s_new = _delta_prompt(qkv, bg, min(1024, T))
        sw_o = _swa_prompt(swa, sinks, min(4, T // WINDOW))
        yp = _outproj_ffn(yp, dn_o, sw_o, m_p[l], g2, woa, wob, wgu2, wd2, l, tm, z, nw)
        nkeep = min(WINDOW, T)
        outs[0].append(tail[:, HALO - (DN_CONV - 1):])
        outs[1].append(s_new)
        outs[2].append(swa[:, T - nkeep:, SWA_Q:SWA_Q + SWA_KV].reshape(B, nkeep, SWA_KV_HEADS, SWA_DH))
        outs[3].append(swa[:, T - nkeep:, SWA_Q + SWA_KV:].reshape(B, nkeep, SWA_KV_HEADS, SWA_DH))

        ys = _ffn(ys, m_s[l], 0, g1, wgu1, wd1, l, Bs)
        buf = jnp.transpose(state_dn_conv[l], (1, 0, 2))
        qkv, z, bg, swa, cnew = _inproj_sample(ys, m_s[l], gm, win, cw, prm, cos_s, sin_s, gq, gk,
                                               buf, l)
        dn_o, s_new = _delta_sample(qkv[0], bg[0], z[0], nw, state_dn_S, l, bb)
        w0 = cache_swa_k.shape[2]
        sw_o, k_new, v_new = _swa_sample(swa[0], cache_swa_k[l].reshape(Bs, w0, SWA_KV),
                                         cache_swa_v[l].reshape(Bs, w0, SWA_KV), sinks, bb)
        ys = _outproj_ffn(ys, dn_o[None], sw_o[None], m_s[l], g2, woa, wob, wgu2, wd2, l, Bs)
        outs[4].append(jnp.transpose(cnew, (1, 0, 2)))
        outs[5].append(s_new)
        outs[6].append(k_new.reshape(Bs, w0, SWA_KV_HEADS, SWA_DH))
        outs[7].append(v_new.reshape(Bs, w0, SWA_KV_HEADS, SWA_DH))

    return (yp, ys.reshape(Bs, 1, D_MODEL)) + tuple(jnp.stack(o) for o in outs)
```

```python
import functools
import math

import jax
import jax.numpy as jnp
from jax import lax
from jax.experimental import pallas as pl
from jax.experimental.pallas import tpu as pltpu

F32 = jnp.float32
BF16 = jnp.bfloat16

D_MODEL = 1024
D_FF = 2816
N_MOD = 9
EPS = 1e-6
DN_HEADS = 4
DN_DK = 128
DN_DV = 128
DN_QK = DN_HEADS * DN_DK
DN_V = DN_HEADS * DN_DV
DN_CONV = 4
DN_CONV_CH = 2 * DN_QK + DN_V
DN_CHUNK = 64
INV_BLOCK = 16
INTRA_ROWS = 256
SWA_DH = 64
SWA_HEADS = 8
SWA_KV_HEADS = 2
SWA_GROUP = SWA_HEADS // SWA_KV_HEADS
SWA_Q = SWA_HEADS * SWA_DH
SWA_KV = SWA_KV_HEADS * SWA_DH
WINDOW = 128
ROPE_THETA = 10000.0
PAST_LEN = 16384

LANES = 128
SUBLANES = 8
HALO = SUBLANES

C_CONV = 0
C_Z = C_CONV + DN_CONV_CH
C_SQ = C_Z + DN_V
C_SK = C_SQ + SWA_Q
C_SV = C_SK + SWA_KV
C_BG = C_SV + SWA_KV
IN_PACKED = C_BG + LANES
SWA_OUT = SWA_Q + 2 * SWA_KV

VMEM_LIMIT = 56 * 1024 * 1024

ROW_TILE = 512
INTRA_STEP_ROWS = 1024
SWA_QBLOCKS = 8
SAMPLE_SEQS = SUBLANES


def _cparams(sem):
    return pltpu.CompilerParams(dimension_semantics=sem, vmem_limit_bytes=VMEM_LIMIT)


def _const_spec(shape):
    nd = len(shape)
    return pl.BlockSpec(shape, lambda *_: (0,) * nd, pipeline_mode=pl.Buffered(1))


def _layer_spec(shape, l):
    nd = len(shape)
    return pl.BlockSpec((None,) + tuple(shape), lambda *_: (l,) + (0,) * nd,
                        pipeline_mode=pl.Buffered(1))


def _sigmoid(x):
    return 1.0 / (1.0 + jnp.exp(-x))


def _silu(x):
    return x * _sigmoid(x)


def _adaln_kernel(cp_ref, cs_ref, w_ref, b_ref, mp_ref, ms_ref):
    w = w_ref[...].astype(BF16)
    b = b_ref[...]
    for c_ref, m_ref in ((cp_ref, mp_ref), (cs_ref, ms_ref)):
        c = c_ref[...]
        m_ref[...] = jnp.dot(_silu(c).astype(BF16), w, preferred_element_type=F32) + b


def _adaln(c_p, c_s, w_ada, b_ada):
    L = w_ada.shape[0]
    bp, bs = c_p.shape[0], c_s.shape[0]
    b4 = b_ada.reshape(L, N_MOD, 1, D_MODEL)
    return pl.pallas_call(
        _adaln_kernel,
        grid=(L, N_MOD),
        in_specs=[
            pl.BlockSpec((bp, D_MODEL), lambda l, j: (0, 0)),
            pl.BlockSpec((bs, D_MODEL), lambda l, j: (0, 0)),
            pl.BlockSpec((None, D_MODEL, D_MODEL), lambda l, j: (l, 0, j)),
            pl.BlockSpec((None, None, 1, D_MODEL), lambda l, j: (l, j, 0, 0)),
        ],
        out_specs=[
            pl.BlockSpec((None, None, bp, D_MODEL), lambda l, j: (l, j, 0, 0)),
            pl.BlockSpec((None, None, bs, D_MODEL), lambda l, j: (l, j, 0, 0)),
        ],
        out_shape=[
            jax.ShapeDtypeStruct((L, N_MOD, bp, D_MODEL), F32),
            jax.ShapeDtypeStruct((L, N_MOD, bs, D_MODEL), F32),
        ],
        compiler_params=_cparams(("parallel", "parallel")),
        name="adaln",
    )(c_p, c_s, w_ada, b4)


def _mod_norm(x, g, shift, scale):
    ms = jnp.mean(x * x, axis=-1, keepdims=True)
    return (x * lax.rsqrt(ms + EPS)) * (g * (1.0 + scale)) + shift


def _ffn_body(x, shift, scale, gate, g, wgu_ref, wd_ref):
    h = _mod_norm(x, g, shift, scale).astype(BF16)
    gu = jnp.dot(h, wgu_ref[...], preferred_element_type=F32)
    a = gu[:, :D_FF]
    b = gu[:, D_FF:]
    act = (_silu(a) * b).astype(BF16)
    y = jnp.dot(act, wd_ref[...], preferred_element_type=F32)
    return x + (0.5 * gate) * y


def _mod_specs(m, sub, tm):
    rm = m.shape[2]
    blk = 1 if rm == 1 else tm

    def spec(i):
        if rm == 1:
            return pl.BlockSpec((None, None, 1, D_MODEL), lambda g, r: (i, g, 0, 0))
        return pl.BlockSpec((None, None, blk, D_MODEL), lambda g, r: (i, g, r, 0))

    return [spec(3 * sub), spec(3 * sub + 1), spec(3 * sub + 2)]


def _row_spec(tm, width):
    return pl.BlockSpec((None, tm, width), lambda g, r: (g, r, 0))


def _ffn_kernel(x_ref, sh_ref, sc_ref, ga_ref, g_ref, wgu_ref, wd_ref, o_ref):
    o_ref[...] = _ffn_body(x_ref[...], sh_ref[...], sc_ref[...], ga_ref[...], g_ref[...],
                           wgu_ref, wd_ref)


def _ffn(x, m, sub, g, wgu, wd, l, tm):
    G, R, _ = x.shape
    return pl.pallas_call(
        _ffn_kernel,
        grid=(G, R // tm),
        in_specs=[_row_spec(tm, D_MODEL)] + _mod_specs(m, sub, tm) + [
            _const_spec((1, D_MODEL)),
            _layer_spec((D_MODEL, 2 * D_FF), l),
            _layer_spec((D_FF, D_MODEL), l),
        ],
        out_specs=_row_spec(tm, D_MODEL),
        out_shape=jax.ShapeDtypeStruct(x.shape, F32),
        compiler_params=_cparams(("parallel", "parallel")),
        name="ffn",
    )(x, m, m, m, g, wgu, wd)


def _group_mean_sq(x, gmat_ref):
    return jnp.dot((x * x).astype(BF16), gmat_ref[...], preferred_element_type=F32)


def _rope(x, cos, sin_signed):
    width = x.shape[-1]
    reps = width // LANES
    lane = lax.broadcasted_iota(jnp.int32, x.shape, 1)
    first_half = (lane & (SWA_DH - 1)) < (SWA_DH // 2)
    partner = jnp.where(first_half,
                        pltpu.roll(x, width - SWA_DH // 2, 1),
                        pltpu.roll(x, SWA_DH // 2, 1))
    if reps > 1:
        cos = jnp.concatenate([cos] * reps, axis=1)
        sin_signed = jnp.concatenate([sin_signed] * reps, axis=1)
    return x * cos + partner * sin_signed


def _swa_post(p, prm_ref, cos, sin, gq_ref, gk_ref, swa_ref):
    pq = p[:, C_SQ:C_SQ + SWA_Q]
    pk = p[:, C_SK:C_SK + SWA_KV]
    wq = jnp.concatenate([prm_ref[2:3, :]] * (SWA_Q // LANES), axis=1)
    wk = prm_ref[3:4, :]
    qn = pq * lax.rsqrt(_group_mean_sq(pq, gq_ref) + EPS) * wq
    kn = pk * lax.rsqrt(_group_mean_sq(pk, gk_ref) + EPS) * wk
    swa_ref[:, :SWA_Q] = _rope(qn, cos, sin)
    swa_ref[:, SWA_Q:SWA_Q + SWA_KV] = _rope(kn, cos, sin)
    swa_ref[:, SWA_Q + SWA_KV:] = p[:, C_SV:C_SV + SWA_KV]


def _dn_group(a, gi, qkv_ref):
    for hh in range(DN_HEADS):
        cols = slice(gi * DN_QK + hh * DN_DK, gi * DN_QK + (hh + 1) * DN_DK)
        xh = a[:, hh * DN_DK:(hh + 1) * DN_DK]
        if gi < 2:
            s = lax.rsqrt(jnp.sum(xh * xh, axis=-1, keepdims=True) + EPS)
            if gi == 0:
                s = s * (DN_DK ** -0.5)
            xh = xh * s
        qkv_ref[:, cols] = xh.astype(qkv_ref.dtype)


def _bg_post(pb, prm_ref, bg_ref):
    neg_a = prm_ref[0:1, :]
    dt_b = prm_ref[1:2, :]
    xa = pb + dt_b
    softplus = jnp.maximum(xa, 0.0) + jnp.log(1.0 + jnp.exp(-jnp.abs(xa)))
    lane = lax.broadcasted_iota(jnp.int32, pb.shape, 1)
    bg_ref[...] = jnp.where(lane < DN_HEADS, _sigmoid(pb), neg_a * softplus)


def _dn_post(p, qkv_conv, prm_ref, qkv_ref, z_ref, bg_ref):
    qkv = _silu(qkv_conv)
    for gi in range(3):
        _dn_group(qkv[:, gi * DN_QK:(gi + 1) * DN_QK], gi, qkv_ref)
    z_ref[...] = p[:, C_Z:C_Z + DN_V].astype(z_ref.dtype)
    _bg_post(p[:, C_BG:C_BG + LANES], prm_ref, bg_ref)


class _Cols:
    def __init__(self, segments):
        self.segments = segments

    def __getitem__(self, idx):
        cols = idx[1]
        for start, arr in self.segments:
            if start <= cols.start and cols.stop <= start + arr.shape[1]:
                return arr[:, cols.start - start:cols.stop - start]
        raise KeyError(cols)


def _inproj_prompt_kernel(x_ref, sh_ref, sc_ref, g_ref, win_ref, cw_ref, prm_ref, cos_ref, sin_ref,
                          gq_ref, gk_ref, qkv_ref, z_ref, bg_ref, swa_ref, tail_ref, ext_ref):
    tm = x_ref.shape[0]
    first = pl.program_id(1) == 0

    @pl.when(first)
    def _():
        ext_ref[0:HALO, :] = jnp.zeros((HALO, DN_CONV_CH), F32)

    @pl.when(jnp.logical_not(first))
    def _():
        ext_ref[0:HALO, :] = ext_ref[tm:tm + HALO, :]

    h = _mod_norm(x_ref[...], g_ref[...], sh_ref[...], sc_ref[...]).astype(BF16)
    proj = lambda a, b: jnp.dot(h, win_ref[:, a:b], preferred_element_type=F32)
    p_att = _Cols([(C_SQ, proj(C_SQ, C_SK)), (C_SK, proj(C_SK, IN_PACKED))])
    _swa_post(p_att, prm_ref, cos_ref[...], sin_ref[...], gq_ref, gk_ref, swa_ref)
    _bg_post(p_att[:, C_BG:C_BG + LANES], prm_ref, bg_ref)
    for gi in range(3):
        cs = slice(gi * DN_QK, (gi + 1) * DN_QK)
        pc = proj(cs.start, cs.stop)
        ext_ref[HALO:, cs] = pc
        y = pc * cw_ref[DN_CONV - 1:DN_CONV, cs]
        for j in range(DN_CONV - 1):
            off = HALO - (DN_CONV - 1) + j
            y = y + ext_ref[off:off + tm, cs] * cw_ref[j:j + 1, cs]
        _dn_group(_silu(y), gi, qkv_ref)
    tail_ref[...] = ext_ref[tm:tm + HALO, :]
    z_ref[...] = proj(C_Z, C_Z + DN_V).astype(z_ref.dtype)


def _inproj_sample_kernel(x_ref, sh_ref, sc_ref, g_ref, win_ref, cw_ref, prm_ref, cos_ref, sin_ref,
                          gq_ref, gk_ref, buf_ref, qkv_ref, z_ref, bg_ref, swa_ref, cnew_ref):
    h = _mod_norm(x_ref[...], g_ref[...], sh_ref[...], sc_ref[...]).astype(BF16)
    p = jnp.dot(h, win_ref[...], preferred_element_type=F32)
    pc = p[:, :DN_CONV_CH]
    y = pc * cw_ref[DN_CONV - 1:DN_CONV, :]
    for j in range(DN_CONV - 1):
        y = y + buf_ref[j] * cw_ref[j:j + 1, :]
    for j in range(DN_CONV - 2):
        cnew_ref[j] = buf_ref[j + 1]
    cnew_ref[DN_CONV - 2] = pc
    _swa_post(p, prm_ref, cos_ref[...], sin_ref[...], gq_ref, gk_ref, swa_ref)
    _dn_post(p, y, prm_ref, qkv_ref, z_ref, bg_ref)


def _inproj_common_specs(m, tm, l):
    sh, sc, _ = _mod_specs(m, 1, tm)
    return [_row_spec(tm, D_MODEL), sh, sc,
            _const_spec((1, D_MODEL)),
            _layer_spec((D_MODEL, IN_PACKED), l),
            _const_spec((DN_CONV, DN_CONV_CH)),
            _const_spec((SUBLANES, LANES))]


def _inproj_prompt(x, m, g, win, cw, prm, cos, sin, gq, gk, l, tm):
    B, T, _ = x.shape
    table = pl.BlockSpec((tm, LANES), lambda b, r: (r, 0))
    return pl.pallas_call(
        _inproj_prompt_kernel,
        grid=(B, T // tm),
        in_specs=_inproj_common_specs(m, tm, l) + [
            table, table,
            _const_spec((SWA_Q, SWA_Q)),
            _const_spec((SWA_KV, SWA_KV)),
        ],
        out_specs=[
            _row_spec(tm, DN_CONV_CH), _row_spec(tm, DN_V), _row_spec(tm, LANES),
            _row_spec(tm, SWA_OUT),
            pl.BlockSpec((None, HALO, DN_CONV_CH), lambda b, r: (b, 0, 0)),
        ],
        out_shape=[
            jax.ShapeDtypeStruct((B, T, DN_CONV_CH), BF16),
            jax.ShapeDtypeStruct((B, T, DN_V), BF16),
            jax.ShapeDtypeStruct((B, T, LANES), F32),
            jax.ShapeDtypeStruct((B, T, SWA_OUT), F32),
            jax.ShapeDtypeStruct((B, HALO, DN_CONV_CH), F32),
        ],
        scratch_shapes=[pltpu.VMEM((HALO + tm, DN_CONV_CH), F32)],
        compiler_params=_cparams(("parallel", "arbitrary")),
        name="inproj_prompt",
    )(x, m, m, g, win, cw, prm, cos, sin, gq, gk)


def _inproj_sample(x, m, g, win, cw, prm, cos, sin, gq, gk, buf, l):
    _, R, _ = x.shape
    nbuf = DN_CONV - 1
    return pl.pallas_call(
        _inproj_sample_kernel,
        grid=(1, 1),
        in_specs=_inproj_common_specs(m, R, l) + [
            _const_spec((1, LANES)), _const_spec((1, LANES)),
            _const_spec((SWA_Q, SWA_Q)), _const_spec((SWA_KV, SWA_KV)),
            _const_spec((nbuf, R, DN_CONV_CH)),
        ],
        out_specs=[
            _row_spec(R, DN_CONV_CH), _row_spec(R, DN_V), _row_spec(R, LANES), _row_spec(R, SWA_OUT),
            pl.BlockSpec((nbuf, R, DN_CONV_CH), lambda g_, r: (0, 0, 0)),
        ],
        out_shape=[
            jax.ShapeDtypeStruct((1, R, DN_CONV_CH), F32),
            jax.ShapeDtypeStruct((1, R, DN_V), F32),
            jax.ShapeDtypeStruct((1, R, LANES), F32),
            jax.ShapeDtypeStruct((1, R, SWA_OUT), F32),
            jax.ShapeDtypeStruct((nbuf, R, DN_CONV_CH), F32),
        ],
        compiler_params=_cparams(("arbitrary", "arbitrary")),
        name="inproj_sample",
    )(x, m, m, g, win, cw, prm, cos, sin, gq, gk, buf)


def _gated_norm(o, z, nw):
    on = o * lax.rsqrt(jnp.mean(o * o, axis=-1, keepdims=True) + EPS) * nw
    return on * _silu(z.astype(F32))


def _mm(a, b):
    return jnp.dot(a.astype(BF16), b.astype(BF16), preferred_element_type=F32)


def _wide_to_blockdiag(wide):
    c, n = wide.shape[0], wide.shape[1] // wide.shape[0]
    cblk = lax.broadcasted_iota(jnp.int32, wide.shape, 1) // c
    return jnp.concatenate([jnp.where(cblk == i, wide, 0.0) for i in range(n)], axis=0)


def _blockdiag_to_wide(full, c):
    n = full.shape[0] // c
    cblk = lax.broadcasted_iota(jnp.int32, (c, full.shape[1]), 1) // c
    wide = full[0:c]
    for i in range(1, n):
        wide = jnp.where(cblk == i, full[i * c:(i + 1) * c], wide)
    return wide


def _unit_lower_inverses(mats):
    c = mats[0].shape[0]
    row = lax.broadcasted_iota(jnp.int32, mats[0].shape, 0)
    col = lax.broadcasted_iota(jnp.int32, mats[0].shape, 1) % c
    s = INV_BLOCK
    diag = (row // s) == (col // s)
    pws = [jnp.where(diag, a, 0.0) for a in mats]
    eye = jnp.where(row == col, 1.0, 0.0)
    xs = [eye - pw for pw in pws]
    k = 2
    while k < s:
        pws = [_mm(pw, _wide_to_blockdiag(pw)) for pw in pws]
        xs = [x + _mm(x, _wide_to_blockdiag(pw)) for x, pw in zip(xs, pws)]
        k *= 2
    while s < c:
        pair = ((row // (2 * s)) == (col // (2 * s))) & ((row // s) != (col // s))
        ts = [_mm(x, _wide_to_blockdiag(jnp.where(pair, a, 0.0))) for x, a in zip(xs, mats)]
        xs = [x - _mm(t, _wide_to_blockdiag(x)) for x, t in zip(xs, ts)]
        s *= 2
    return xs


def _cumsum_rows(x):
    n = x.shape[0]
    row = lax.broadcasted_iota(jnp.int32, x.shape, 0)
    s = 1
    while s < n:
        x = x + jnp.where(row >= s, pltpu.roll(x, s, 0), 0.0)
        s *= 2
    return x


WQ_ROWS = 2 * DN_CHUNK
QKD_ROWS = DN_CHUNK + DN_DK


def _delta_intra_kernel(q_ref, k_ref, v_ref, bg_ref, u_ref, wq_ref, qkd_ref, eg_ref):
    c = DN_CHUNK
    r = INTRA_ROWS
    nc = r // c
    row = lax.broadcasted_iota(jnp.int32, (c, r), 0)
    col = lax.broadcasted_iota(jnp.int32, (c, r), 1) % c
    incl = row >= col
    strict = row > col

    jobs = [(blk, h) for blk in range(q_ref.shape[0] // r) for h in range(DN_HEADS)]
    gcs, egcs, ekds, gcts, bgs = [], [], [], [], []
    for blk in range(q_ref.shape[0] // r):
        bg = bg_ref[blk * r:(blk + 1) * r, :]
        gc = jnp.concatenate([_cumsum_rows(bg[ci * c:(ci + 1) * c, :]) for ci in range(nc)], axis=0)
        g_end = [gc[(ci + 1) * c - 1:(ci + 1) * c, :] for ci in range(nc)]
        glast = jnp.concatenate([jnp.broadcast_to(g, (c, LANES)) for g in g_end], axis=0)
        for ci in range(nc):
            rows = [jnp.broadcast_to(jnp.exp(g_end[ci][:, DN_HEADS + h:DN_HEADS + h + 1]), (1, LANES))
                    for h in range(DN_HEADS)]
            eg_ref[blk * nc + ci] = jnp.concatenate(
                rows + [jnp.zeros((SUBLANES - DN_HEADS, LANES), F32)], axis=0)
        bgs.append(bg)
        gcs.append(gc)
        gcts.append(gc.T)
        egcs.append(jnp.exp(gc))
        ekds.append(jnp.exp(glast - gc))

    prep = []
    for blk, h in jobs:
        rs = slice(blk * r, (blk + 1) * r)
        sl = slice(h * DN_DK, (h + 1) * DN_DK)
        q = q_ref[rs, sl]
        k = k_ref[rs, sl]
        kb = k * bgs[blk][:, h:h + 1]
        kq = lax.dot_general(jnp.concatenate([kb, q], axis=0).astype(BF16), k.astype(BF16),
                             (((1,), (1,)), ((), ())), preferred_element_type=F32)
        prep.append((q, k, kb, kq))
    a_mats, qks = [], []
    for (blk, h), (q, k, kb, kq) in zip(jobs, prep):
        gl = slice(DN_HEADS + h, DN_HEADS + h + 1)
        gcol = jnp.concatenate([jnp.broadcast_to(gcs[blk][ci * c:(ci + 1) * c, gl], (c, c))
                                for ci in range(nc)], axis=1)
        decay = jnp.exp(jnp.where(incl, gcol - gcts[blk][gl, :], -jnp.inf))
        a_mats.append(jnp.where(strict, _blockdiag_to_wide(kq[:r], c) * decay, 0.0))
        qks.append(_blockdiag_to_wide(kq[r:], c) * decay)
    tinvs = _unit_lower_inverses(a_mats)
    uws = []
    for (blk, h), (q, k, kb, kq), tinv in zip(jobs, prep, tinvs):
        rs = slice(blk * r, (blk + 1) * r)
        sl = slice(h * DN_DK, (h + 1) * DN_DK)
        gl = slice(DN_HEADS + h, DN_HEADS + h + 1)
        rhs = jnp.concatenate([v_ref[rs, sl] * bgs[blk][:, h:h + 1], kb * egcs[blk][:, gl]], axis=1)
        uws.append(_mm(_wide_to_blockdiag(tinv), rhs))
    for (blk, h), (q, k, kb, kq), uw, qk in zip(jobs, prep, uws, qks):
        rs = slice(blk * r, (blk + 1) * r)
        sl = slice(h * DN_DK, (h + 1) * DN_DK)
        gl = slice(DN_HEADS + h, DN_HEADS + h + 1)
        u_ref[rs, sl] = uw[:, :DN_DV]
        w_b = uw[:, DN_DV:].astype(BF16)
        qd_b = (q * egcs[blk][:, gl]).astype(BF16)
        for ci in range(nc):
            wq_ref[blk * nc + ci, h * WQ_ROWS:h * WQ_ROWS + c, :] = w_b[ci * c:(ci + 1) * c, :]
            wq_ref[blk * nc + ci, h * WQ_ROWS + c:(h + 1) * WQ_ROWS, :] = qd_b[ci * c:(ci + 1) * c, :]
        kdt = (k * ekds[blk][:, gl]).T
        for pj in range(nc // 2):
            ls = slice(pj * LANES, (pj + 1) * LANES)
            pi = blk * (nc // 2) + pj
            qkd_ref[pi, h * QKD_ROWS:h * QKD_ROWS + c, :] = qk[:, ls].astype(BF16)
            qkd_ref[pi, h * QKD_ROWS + c:(h + 1) * QKD_ROWS, :] = kdt[:, ls].astype(BF16)


def _delta_scan_kernel(u_ref, wq_ref, qkd_ref, eg_ref, o_ref, sout_ref, s_ref):
    n = pl.program_id(0)
    c = DN_CHUNK

    @pl.when(n == 0)
    def _():
        s_ref[...] = jnp.zeros_like(s_ref)

    zeros = jnp.zeros((c, DN_DV), BF16)
    chains = [(b, h) for b in range(u_ref.shape[0]) for h in range(DN_HEADS)]
    for j in range(2):
        rs = slice(j * c, (j + 1) * c)
        ws = [jnp.dot(wq_ref[b, j, h * WQ_ROWS:(h + 1) * WQ_ROWS, :], s_ref[b, h].astype(BF16),
                      preferred_element_type=F32) for b, h in chains]
        rr = []
        for (b, h), w_h in zip(chains, ws):
            v_new = (u_ref[b, rs, h * DN_DV:(h + 1) * DN_DV] - w_h[:c]).astype(BF16)
            rhs = jnp.concatenate([v_new, zeros] if j == 0 else [zeros, v_new], axis=0)
            rr.append(jnp.dot(qkd_ref[b, h * QKD_ROWS:(h + 1) * QKD_ROWS, :], rhs,
                              preferred_element_type=F32))
        for (b, h), w_h, r_h in zip(chains, ws, rr):
            sl = slice(h * DN_DV, (h + 1) * DN_DV)
            s_ref[b, h] = s_ref[b, h] * eg_ref[b, j, h:h + 1, :] + r_h[c:]
            o_ref[b, rs, sl] = (w_h[c:] + r_h[:c]).astype(o_ref.dtype)

    @pl.when(n == pl.num_programs(0) - 1)
    def _():
        sout_ref[...] = s_ref[...]


def _delta_prompt(qkv, bg, ra):
    B, T, _ = qkv.shape
    c = DN_CHUNK
    n = T // c
    cpb = ra // c
    qkv_spec = lambda j: pl.BlockSpec((None, ra, DN_QK), lambda b, r: (b, r, j))
    u, wq, qkd, eg = pl.pallas_call(
        _delta_intra_kernel,
        grid=(B, T // ra),
        in_specs=[qkv_spec(0), qkv_spec(1), qkv_spec(2),
                  pl.BlockSpec((None, ra, LANES), lambda b, r: (b, r, 0))],
        out_specs=[pl.BlockSpec((None, ra, DN_V), lambda b, r: (b, r, 0)),
                   pl.BlockSpec((None, cpb, DN_HEADS * WQ_ROWS, DN_DK), lambda b, r: (b, r, 0, 0)),
                   pl.BlockSpec((None, cpb // 2, DN_HEADS * QKD_ROWS, LANES),
                                lambda b, r: (b, r, 0, 0)),
                   pl.BlockSpec((None, cpb, SUBLANES, LANES), lambda b, r: (b, r, 0, 0))],
        out_shape=[jax.ShapeDtypeStruct((B, T, DN_V), F32),
                   jax.ShapeDtypeStruct((B, n, DN_HEADS * WQ_ROWS, DN_DK), BF16),
                   jax.ShapeDtypeStruct((B, n // 2, DN_HEADS * QKD_ROWS, LANES), BF16),
                   jax.ShapeDtypeStruct((B, n, SUBLANES, LANES), F32)],
        compiler_params=_cparams(("parallel", "parallel")),
        name="delta_intra",
    )(qkv, qkv, qkv, bg)
    return pl.pallas_call(
        _delta_scan_kernel,
        grid=(n // 2,),
        in_specs=[pl.BlockSpec((B, 2 * c, DN_V), lambda i: (0, i, 0)),
                  pl.BlockSpec((B, 2, DN_HEADS * WQ_ROWS, DN_DK), lambda i: (0, i, 0, 0)),
                  pl.BlockSpec((B, None, DN_HEADS * QKD_ROWS, LANES), lambda i: (0, i, 0, 0)),
                  pl.BlockSpec((B, 2, SUBLANES, LANES), lambda i: (0, i, 0, 0))],
        out_specs=[pl.BlockSpec((B, 2 * c, DN_V), lambda i: (0, i, 0)),
                   pl.BlockSpec((B, DN_HEADS, DN_DK, DN_DV), lambda i: (0, 0, 0, 0))],
        out_shape=[jax.ShapeDtypeStruct((B, T, DN_V), BF16),
                   jax.ShapeDtypeStruct((B, DN_HEADS, DN_DK, DN_DV), F32)],
        scratch_shapes=[pltpu.VMEM((B, DN_HEADS, DN_DK, DN_DV), F32)],
        compiler_params=_cparams(("arbitrary",)),
        name="delta_scan",
    )(u, wq, qkd, eg)


def _delta_sample_kernel(qkv_ref, bg_ref, z_ref, nw_ref, s0_ref, o_ref, s_ref):
    bb = qkv_ref.shape[0]
    nw = nw_ref[...]
    zeros = jnp.zeros((SUBLANES - 2, DN_DK), F32)
    row = lax.broadcasted_iota(jnp.int32, (bb, DN_DV), 0)
    heads = []
    for h in range(DN_HEADS):
        q = qkv_ref[:, h * DN_DK:(h + 1) * DN_DK]
        k = qkv_ref[:, DN_QK + h * DN_DK:DN_QK + (h + 1) * DN_DK]
        v = qkv_ref[:, 2 * DN_QK + h * DN_DV:2 * DN_QK + (h + 1) * DN_DV]
        beta = bg_ref[:, h:h + 1]
        eg = jnp.exp(bg_ref[:, DN_HEADS + h:DN_HEADS + h + 1])
        w = k * beta * eg
        qd = q * eg
        ws = [jnp.dot(jnp.concatenate([w[i:i + 1], qd[i:i + 1], zeros], axis=0), s0_ref[i, h],
                      preferred_element_type=F32) for i in range(bb)]
        heads.append((q, k, v, beta, eg, ws))
    updates = []
    for q, k, v, beta, eg, ws in heads:
        w_s = jnp.concatenate([ws[i][0:1] for i in range(bb)], axis=0)
        q_s = jnp.concatenate([ws[i][1:2] for i in range(bb)], axis=0)
        v_new = v * beta - w_s
        kt = k.T
        kv = [jnp.dot(kt, jnp.where(row == i, v_new, 0.0), preferred_element_type=F32)
              for i in range(bb)]
        updates.append((v_new, q_s, kv))
    for h, ((q, k, v, beta, eg, ws), (v_new, q_s, kv)) in enumerate(zip(heads, updates)):
        sl = slice(h * DN_DK, (h + 1) * DN_DK)
        for i in range(bb):
            s_ref[i, h] = s0_ref[i, h] * eg[i:i + 1] + kv[i]
        o = q_s + jnp.sum(q * k, axis=-1, keepdims=True) * v_new
        o_ref[:, sl] = _gated_norm(o, z_ref[:, sl], nw)


def _delta_sample(qkv, bg, z, nw, s0, l, bb):
    R = qkv.shape[0]
    return pl.pallas_call(
        _delta_sample_kernel,
        grid=(R // bb,),
        in_specs=[pl.BlockSpec((bb, DN_CONV_CH), lambda i: (i, 0)),
                  pl.BlockSpec((bb, LANES), lambda i: (i, 0)),
                  pl.BlockSpec((bb, DN_V), lambda i: (i, 0)),
                  _const_spec((1, DN_DV)),
                  pl.BlockSpec((None, bb, DN_HEADS, DN_DK, DN_DV), lambda i: (l, i, 0, 0, 0))],
        out_specs=[pl.BlockSpec((bb, DN_V), lambda i: (i, 0)),
                   pl.BlockSpec((bb, DN_HEADS, DN_DK, DN_DV), lambda i: (i, 0, 0, 0))],
        out_shape=[jax.ShapeDtypeStruct((R, DN_V), F32),
                   jax.ShapeDtypeStruct(s0.shape[1:], F32)],
        compiler_params=_cparams(("parallel",)),
        name="delta_sample",
    )(qkv, bg, z, nw, s0)


def _swa_prompt_kernel(q_ref, kc_ref, vc_ref, kp_ref, vp_ref, sink_ref, o_ref):
    n = pl.program_id(1)
    w = WINDOW
    nq = q_ref.shape[0] // w
    half = SWA_DH
    key = lax.broadcasted_iota(jnp.int32, (2 * w, w), 0)
    qry = lax.broadcasted_iota(jnp.int32, (2 * w, w), 1)
    band = (key >= qry) & (key <= qry + w)
    bias_mid = jnp.where(band, 0.0, -jnp.inf)
    bias_first = jnp.where(band & (key >= jnp.where(n == 0, w, 0)), 0.0, -jnp.inf)
    kall = jnp.concatenate([kp_ref[...], kc_ref[...]], axis=0)
    vall = jnp.concatenate([vp_ref[...], vc_ref[...]], axis=0)
    low = lax.broadcasted_iota(jnp.int32, kall.shape, 1) < half
    kall_r = pltpu.roll(kall, half, 1)
    k_lo = [jnp.where(low, kall, 0.0).astype(BF16), jnp.where(low, kall_r, 0.0).astype(BF16)]
    k_hi = [jnp.where(low, 0.0, kall_r).astype(BF16), jnp.where(low, 0.0, kall).astype(BF16)]
    v_t = vall.T.astype(BF16)
    zeros_v = jnp.zeros((half, 2 * w), BF16)
    sum_rows = 2 * SUBLANES
    ones8 = jnp.ones((sum_rows, 2 * w), BF16)
    zeros8 = jnp.zeros((sum_rows, 2 * w), BF16)
    first_pair = lax.broadcasted_iota(jnp.int32, (1, 2 * w), 1) < w
    log2e = math.log2(math.e)
    scale = SWA_DH ** -0.5 * log2e

    units = [(i, j) for i in range(nq) for j in range(SWA_KV_HEADS)]
    scores = []
    for i, j in units:
        rs = slice(i * w, (i + 2) * w)
        kk = jnp.concatenate([k_lo[j][rs], k_hi[j][rs]], axis=0)
        c0 = j * SWA_GROUP * SWA_DH
        q = jnp.concatenate([q_ref[i * w:(i + 1) * w, c0:c0 + LANES],
                             q_ref[i * w:(i + 1) * w, c0 + LANES:c0 + 2 * LANES]], axis=0)
        scores.append(lax.dot_general(kk, (q * scale).astype(BF16), (((1,), (1,)), ((), ())),
                                      preferred_element_type=F32))
    probs, sink_rows, maxima = [], [], []
    for (i, j), s in zip(units, scores):
        bias = bias_first if i == 0 else bias_mid
        bias2 = jnp.concatenate([bias, bias], axis=1)
        p_hf, sk_hf, mx_hf = [], [], []
        for hf in range(2):
            heads = [j * SWA_GROUP + 2 * pr + hf for pr in range(2)]
            sink = log2e * jnp.where(first_pair, sink_ref[0:1, heads[0]:heads[0] + 1],
                                     sink_ref[0:1, heads[1]:heads[1] + 1])
            masked = s[hf * 2 * w:(hf + 1) * 2 * w, :] + bias2
            mx = jnp.maximum(jnp.max(masked, axis=0, keepdims=True), sink)
            p_hf.append(jnp.exp2(masked - mx).astype(BF16))
            sk_hf.append(sink)
            mx_hf.append(mx)
        probs.append(jnp.concatenate(p_hf, axis=0))
        sink_rows.append(sk_hf)
        maxima.append(mx_hf)
    outs = []
    for (i, j), p in zip(units, probs):
        vt = v_t[j * half:(j + 1) * half, i * w:(i + 2) * w]
        lhs = jnp.concatenate([jnp.concatenate([vt, zeros_v], axis=1),
                               jnp.concatenate([zeros_v, vt], axis=1),
                               jnp.concatenate([ones8, zeros8], axis=1),
                               jnp.concatenate([zeros8, ones8], axis=1)], axis=0)
        outs.append(jnp.dot(lhs, p, preferred_element_type=F32))
    for (i, j), o_t, sk_hf, mx_hf in zip(units, outs, sink_rows, maxima):
        c0 = j * SWA_GROUP * SWA_DH
        parts = []
        for hf in range(2):
            den = (o_t[2 * half + hf * sum_rows:2 * half + hf * sum_rows + 1, :]
                   + jnp.exp2(sk_hf[hf] - mx_hf[hf]))
            parts.append(o_t[hf * half:(hf + 1) * half, :] / den)
        o = jnp.concatenate(parts, axis=0).T
        for pr in range(2):
            o_ref[i * w:(i + 1) * w, c0 + pr * LANES:c0 + (pr + 1) * LANES] = (
                o[pr * w:(pr + 1) * w, :].astype(o_ref.dtype))


def _swa_prompt(swa, sinks, nq):
    B, T, _ = swa.shape
    w = WINDOW
    rq = nq * w
    kcol = SWA_Q // SWA_KV
    prev = lambda n: jnp.maximum(n * nq - 1, 0)
    return pl.pallas_call(
        _swa_prompt_kernel,
        grid=(B, T // rq),
        in_specs=[pl.BlockSpec((None, rq, SWA_Q), lambda b, n: (b, n, 0)),
                  pl.BlockSpec((None, rq, SWA_KV), lambda b, n: (b, n, kcol)),
                  pl.BlockSpec((None, rq, SWA_KV), lambda b, n: (b, n, kcol + 1)),
                  pl.BlockSpec((None, w, SWA_KV), lambda b, n: (b, prev(n), kcol)),
                  pl.BlockSpec((None, w, SWA_KV), lambda b, n: (b, prev(n), kcol + 1)),
                  _const_spec((1, LANES))],
        out_specs=pl.BlockSpec((None, rq, SWA_Q), lambda b, n: (b, n, 0)),
        out_shape=jax.ShapeDtypeStruct((B, T, SWA_Q), BF16),
        compiler_params=_cparams(("parallel", "parallel")),
        name="swa_prompt",
    )(swa, swa, swa, swa, swa, sinks)


def _swa_sample_kernel(swa_ref, kc_ref, vc_ref, sink_ref, o_ref, kn_ref, vn_ref):
    bb = swa_ref.shape[0]
    w = kc_ref.shape[1]
    scale = SWA_DH ** -0.5
    for i in range(bb):
        kn_ref[i, 0:w - 1, :] = kc_ref[i, 1:w, :]
        kn_ref[i, w - 1:w, :] = swa_ref[i:i + 1, SWA_Q:SWA_Q + SWA_KV]
        vn_ref[i, 0:w - 1, :] = vc_ref[i, 1:w, :]
        vn_ref[i, w - 1:w, :] = swa_ref[i:i + 1, SWA_Q + SWA_KV:]
    units = [(i, j) for i in range(bb) for j in range(SWA_KV_HEADS)]
    sinks = [jnp.concatenate(
        [sink_ref[0:1, j * SWA_GROUP + g:j * SWA_GROUP + g + 1] for g in range(SWA_GROUP)]
        + [jnp.zeros((SUBLANES - SWA_GROUP, 1), F32)], axis=0) for j in range(SWA_KV_HEADS)]
    qs, scs = [], []
    for i, j in units:
        q4 = jnp.concatenate(
            [swa_ref[i:i + 1, (j * SWA_GROUP + g) * SWA_DH:(j * SWA_GROUP + g + 1) * SWA_DH]
             for g in range(SWA_GROUP)] + [jnp.zeros((SUBLANES - SWA_GROUP, SWA_DH), F32)], axis=0)
        qs.append(q4)
        scs.append(lax.dot_general(q4, kc_ref[i, :, j * SWA_DH:(j + 1) * SWA_DH],
                                   (((1,), (1,)), ((), ())), preferred_element_type=F32) * scale)
    soft = []
    for (i, j), q4, sc in zip(units, qs, scs):
        knew = swa_ref[i:i + 1, SWA_Q + j * SWA_DH:SWA_Q + (j + 1) * SWA_DH]
        sn = jnp.sum(q4 * knew, axis=-1, keepdims=True) * scale
        mx = jnp.maximum(jnp.maximum(jnp.max(sc, axis=-1, keepdims=True), sn), sinks[j])
        pc = jnp.exp(sc - mx)
        pn = jnp.exp(sn - mx)
        den = jnp.sum(pc, axis=-1, keepdims=True) + pn + jnp.exp(sinks[j] - mx)
        soft.append((pc, pn, den))
    pvs = [jnp.dot(pc, vc_ref[i, :, j * SWA_DH:(j + 1) * SWA_DH], preferred_element_type=F32)
           for (i, j), (pc, pn, den) in zip(units, soft)]
    for (i, j), (pc, pn, den), pv in zip(units, soft, pvs):
        vnew = swa_ref[i:i + 1, SWA_Q + SWA_KV + j * SWA_DH:SWA_Q + SWA_KV + (j + 1) * SWA_DH]
        o = (pv + pn * vnew) / den
        for g in range(SWA_GROUP):
            hq = j * SWA_GROUP + g
            o_ref[i:i + 1, hq * SWA_DH:(hq + 1) * SWA_DH] = o[g:g + 1, :]


def _swa_sample(swa, kc, vc, sinks, bb):
    R = swa.shape[0]
    w = kc.shape[1]
    cache_spec = pl.BlockSpec((bb, w, SWA_KV), lambda i: (i, 0, 0))
    return pl.pallas_call(
        _swa_sample_kernel,
        grid=(R // bb,),
        in_specs=[pl.BlockSpec((bb, SWA_OUT), lambda i: (i, 0)), cache_spec, cache_spec,
                  _const_spec((1, LANES))],
        out_specs=[pl.BlockSpec((bb, SWA_Q), lambda i: (i, 0)), cache_spec, cache_spec],
        out_shape=[jax.ShapeDtypeStruct((R, SWA_Q), F32),
                   jax.ShapeDtypeStruct(kc.shape, F32),
                   jax.ShapeDtypeStruct(vc.shape, F32)],
        compiler_params=_cparams(("parallel",)),
        name="swa_sample",
    )(swa, kc, vc, sinks)


def _outproj_ffn_kernel(gate_dn, x_ref, dn_ref, sw_ref, *refs):
    g5_ref, sh_ref, sc_ref, ga_ref, g_ref, woa_ref, wob_ref, wgu_ref, wd_ref, o_ref = refs[-10:]
    if gate_dn:
        z_ref, nw_ref = refs[:2]
        nw = nw_ref[...]
        dn = jnp.concatenate(
            [_gated_norm(dn_ref[:, h * DN_DV:(h + 1) * DN_DV].astype(F32),
                         z_ref[:, h * DN_DV:(h + 1) * DN_DV], nw).astype(BF16)
             for h in range(DN_HEADS)], axis=1)
    else:
        dn = dn_ref[...].astype(BF16)
    mix = (jnp.dot(dn, woa_ref[...], preferred_element_type=F32)
           + jnp.dot(sw_ref[...].astype(BF16), wob_ref[...], preferred_element_type=F32))
    x = x_ref[...] + g5_ref[...] * mix
    o_ref[...] = _ffn_body(x, sh_ref[...], sc_ref[...], ga_ref[...], g_ref[...], wgu_ref, wd_ref)


def _outproj_ffn(x, dn, sw, m, g, woa, wob, wgu, wd, l, tm, z=None, nw=None):
    G, R, _ = x.shape
    g5 = _mod_specs(m, 1, tm)[2]
    gate_dn = z is not None
    gate_specs = [_row_spec(tm, DN_V), _const_spec((1, DN_DV))] if gate_dn else []
    gate_args = (z, nw) if gate_dn else ()
    return pl.pallas_call(
        functools.partial(_outproj_ffn_kernel, gate_dn),
        grid=(G, R // tm),
        in_specs=[_row_spec(tm, D_MODEL), _row_spec(tm, DN_V), _row_spec(tm, SWA_Q)]
        + gate_specs + [g5] + _mod_specs(m, 2, tm) + [
            _const_spec((1, D_MODEL)),
            _layer_spec((DN_V, D_MODEL), l),
            _layer_spec((SWA_Q, D_MODEL), l),
            _layer_spec((D_MODEL, 2 * D_FF), l),
            _layer_spec((D_FF, D_MODEL), l),
        ],
        out_specs=_row_spec(tm, D_MODEL),
        out_shape=jax.ShapeDtypeStruct(x.shape, F32),
        compiler_params=_cparams(("parallel", "parallel")),
        name="outproj_ffn",
    )(x, dn, sw, *gate_args, m, m, m, m, g, woa, wob, wgu, wd)


def _rope_tables(pos):
    half = SWA_DH // 2
    inv = jnp.power(ROPE_THETA, -jnp.arange(half, dtype=F32) * 2.0 / SWA_DH)
    ang = pos.astype(F32)[:, None] * inv[None, :]
    cos, sin = jnp.cos(ang), jnp.sin(ang)
    reps = LANES // SWA_DH
    return (jnp.concatenate([cos, cos] * reps, axis=1),
            jnp.concatenate([-sin, sin] * reps, axis=1))


def _group_matrix(width):
    i = jnp.arange(width) // SWA_DH
    return jnp.where(i[:, None] == i[None, :], 1.0 / SWA_DH, 0.0).astype(BF16)


def _pack_w_in(w_in):
    conv = w_in[..., :DN_CONV_CH]
    o = DN_CONV_CH
    z = w_in[..., o:o + DN_V]
    o += DN_V
    ba = w_in[..., o:o + 2 * DN_HEADS]
    o += 2 * DN_HEADS
    sq = w_in[..., o:o + SWA_Q]
    o += SWA_Q
    sk = w_in[..., o:o + SWA_KV]
    o += SWA_KV
    sv = w_in[..., o:o + SWA_KV]
    pad = jnp.zeros(w_in.shape[:-1] + (LANES - 2 * DN_HEADS,), w_in.dtype)
    return jnp.concatenate([conv, z, sq, sk, sv, ba, pad], axis=-1).astype(BF16)


def _lane_row(vec, offset=0):
    return jnp.zeros((LANES,), F32).at[offset:offset + vec.shape[0]].set(vec.astype(F32))


def kernel(x_prompt, x_sample, c_prompt, c_sample, state_dn_conv, state_dn_S, cache_swa_k, cache_swa_v, w_ada, b_ada, g_ffn1, w_ffn1_gu, w_ffn1_down, g_mix, w_in, dn_conv_w, dn_A_log, dn_dt_bias, dn_norm_w, swa_q_norm, swa_k_norm, swa_sinks, w_out, g_ffn2, w_ffn2_gu, w_ffn2_down):
    L = w_ada.shape[0]
    B, T, _ = x_prompt.shape
    Bs = x_sample.shape[0]
    tm = min(ROW_TILE, T)
    bb = min(SAMPLE_SEQS, Bs)

    m_p, m_s = _adaln(c_prompt, c_sample, w_ada, b_ada)
    m_p = m_p.reshape(L, N_MOD, B, 1, D_MODEL)
    m_s = m_s.reshape(L, N_MOD, 1, Bs, D_MODEL)

    cos_p, sin_p = _rope_tables(jnp.arange(T))
    cos_s, sin_s = _rope_tables(PAST_LEN + jnp.arange(1))
    gq, gk = _group_matrix(SWA_Q), _group_matrix(SWA_KV)

    wgu1, wd1 = w_ffn1_gu.astype(BF16), w_ffn1_down.astype(BF16)
    wgu2, wd2 = w_ffn2_gu.astype(BF16), w_ffn2_down.astype(BF16)
    win = _pack_w_in(w_in)
    woa, wob = w_out[:, :DN_V].astype(BF16), w_out[:, DN_V:].astype(BF16)

    yp = x_prompt
    ys = x_sample.reshape(1, Bs, D_MODEL)
    outs = [[] for _ in range(8)]
    for l in range(L):
        g1, gm, g2 = (g_ffn1[l].reshape(1, D_MODEL), g_mix[l].reshape(1, D_MODEL),
                      g_ffn2[l].reshape(1, D_MODEL))
        cw = dn_conv_w[l]
        nw = dn_norm_w[l].reshape(1, DN_DV)
        prm = jnp.stack([
            _lane_row(-jnp.exp(dn_A_log[l].astype(F32)), DN_HEADS),
            _lane_row(dn_dt_bias[l], DN_HEADS),
            jnp.concatenate([swa_q_norm[l]] * (LANES // SWA_DH)).astype(F32),
            jnp.concatenate([swa_k_norm[l]] * (LANES // SWA_DH)).astype(F32),
        ] + [jnp.zeros((LANES,), F32)] * (SUBLANES - 4))
        sinks = _lane_row(swa_sinks[l]).reshape(1, LANES)

        yp = _ffn(yp, m_p[l], 0, g1, wgu1, wd1, l, tm)
        qkv, z, bg, swa, tail = _inproj_prompt(yp, m_p[l], gm, win, cw, prm, cos_p, sin_p, gq, gk,
                                               l, tm)
        dn_o, s_new = _delta_prompt(qkv, bg, min(INTRA_STEP_ROWS, T))
        sw_o = _swa_prompt(swa, sinks, min(SWA_QBLOCKS, T // WINDOW))
        yp = _outproj_ffn(yp, dn_o, sw_o, m_p[l], g2, woa, wob, wgu2, wd2, l, tm, z, nw)
        nkeep = min(WINDOW, T)
        outs[0].append(tail[:, HALO - (DN_CONV - 1):])
        outs[1].append(s_new)
        outs[2].append(swa[:, T - nkeep:, SWA_Q:SWA_Q + SWA_KV].reshape(B, nkeep, SWA_KV_HEADS, SWA_DH))
        outs[3].append(swa[:, T - nkeep:, SWA_Q + SWA_KV:].reshape(B, nkeep, SWA_KV_HEADS, SWA_DH))

        ys = _ffn(ys, m_s[l], 0, g1, wgu1, wd1, l, Bs)
        buf = jnp.transpose(state_dn_conv[l], (1, 0, 2))
        qkv, z, bg, swa, cnew = _inproj_sample(ys, m_s[l], gm, win, cw, prm, cos_s, sin_s, gq, gk,
                                               buf, l)
        dn_o, s_new = _delta_sample(qkv[0], bg[0], z[0], nw, state_dn_S, l, bb)
        w0 = cache_swa_k.shape[2]
        sw_o, k_new, v_new = _swa_sample(swa[0], cache_swa_k[l].reshape(Bs, w0, SWA_KV),
                                         cache_swa_v[l].reshape(Bs, w0, SWA_KV), sinks, bb)
        ys = _outproj_ffn(ys, dn_o[None], sw_o[None], m_s[l], g2, woa, wob, wgu2, wd2, l, Bs)
        outs[4].append(jnp.transpose(cnew, (1, 0, 2)))
        outs[5].append(s_new)
        outs[6].append(k_new.reshape(Bs, w0, SWA_KV_HEADS, SWA_DH))
        outs[7].append(v_new.reshape(Bs, w0, SWA_KV_HEADS, SWA_DH))

    return (yp, ys.reshape(Bs, 1, D_MODEL)) + tuple(jnp.stack(o) for o in outs)
```

```python
import functools
import math

import jax
import jax.numpy as jnp
from jax import lax
from jax.experimental import pallas as pl
from jax.experimental.pallas import tpu as pltpu

F32 = jnp.float32
BF16 = jnp.bfloat16

D_MODEL = 1024
D_FF = 2816
N_MOD = 9
EPS = 1e-6
DN_HEADS = 4
DN_DK = 128
DN_DV = 128
DN_QK = DN_HEADS * DN_DK
DN_V = DN_HEADS * DN_DV
DN_CONV = 4
DN_CONV_CH = 2 * DN_QK + DN_V
DN_CHUNK = 64
INV_BLOCK = 16
INTRA_ROWS = 256
SWA_DH = 64
SWA_HEADS = 8
SWA_KV_HEADS = 2
SWA_GROUP = SWA_HEADS // SWA_KV_HEADS
SWA_Q = SWA_HEADS * SWA_DH
SWA_KV = SWA_KV_HEADS * SWA_DH
WINDOW = 128
ROPE_THETA = 10000.0
PAST_LEN = 16384

LANES = 128
SUBLANES = 8
HALO = SUBLANES

C_CONV = 0
C_Z = C_CONV + DN_CONV_CH
C_SQ = C_Z + DN_V
C_SK = C_SQ + SWA_Q
C_SV = C_SK + SWA_KV
C_BG = C_SV + SWA_KV
IN_PACKED = C_BG + LANES
SWA_OUT = SWA_Q + 2 * SWA_KV

VMEM_LIMIT = 56 * 1024 * 1024

ROW_TILE = 512
INTRA_STEP_ROWS = 1024
SWA_QBLOCKS = 8
SAMPLE_SEQS = SUBLANES


def _cparams(sem):
    return pltpu.CompilerParams(dimension_semantics=sem, vmem_limit_bytes=VMEM_LIMIT)


def _const_spec(shape):
    nd = len(shape)
    return pl.BlockSpec(shape, lambda *_: (0,) * nd, pipeline_mode=pl.Buffered(1))


def _layer_spec(shape, l):
    nd = len(shape)
    return pl.BlockSpec((None,) + tuple(shape), lambda *_: (l,) + (0,) * nd,
                        pipeline_mode=pl.Buffered(1))


def _sigmoid(x):
    return 1.0 / (1.0 + jnp.exp(-x))


def _silu(x):
    return x * _sigmoid(x)


def _adaln_kernel(cp_ref, cs_ref, w_ref, b_ref, mp_ref, ms_ref):
    w = w_ref[...].astype(BF16)
    b = b_ref[...]
    for c_ref, m_ref in ((cp_ref, mp_ref), (cs_ref, ms_ref)):
        c = c_ref[...]
        m_ref[...] = jnp.dot(_silu(c).astype(BF16), w, preferred_element_type=F32) + b


def _adaln(c_p, c_s, w_ada, b_ada):
    L = w_ada.shape[0]
    bp, bs = c_p.shape[0], c_s.shape[0]
    b4 = b_ada.reshape(L, N_MOD, 1, D_MODEL)
    return pl.pallas_call(
        _adaln_kernel,
        grid=(L, N_MOD),
        in_specs=[
            pl.BlockSpec((bp, D_MODEL), lambda l, j: (0, 0)),
            pl.BlockSpec((bs, D_MODEL), lambda l, j: (0, 0)),
            pl.BlockSpec((None, D_MODEL, D_MODEL), lambda l, j: (l, 0, j)),
            pl.BlockSpec((None, None, 1, D_MODEL), lambda l, j: (l, j, 0, 0)),
        ],
        out_specs=[
            pl.BlockSpec((None, None, bp, D_MODEL), lambda l, j: (l, j, 0, 0)),
            pl.BlockSpec((None, None, bs, D_MODEL), lambda l, j: (l, j, 0, 0)),
        ],
        out_shape=[
            jax.ShapeDtypeStruct((L, N_MOD, bp, D_MODEL), F32),
            jax.ShapeDtypeStruct((L, N_MOD, bs, D_MODEL), F32),
        ],
        compiler_params=_cparams(("parallel", "parallel")),
        name="adaln",
    )(c_p, c_s, w_ada, b4)


def _mod_norm(x, g, shift, scale):
    ms = jnp.mean(x * x, axis=-1, keepdims=True)
    return (x * lax.rsqrt(ms + EPS)) * (g * (1.0 + scale)) + shift


def _ffn_body(x, shift, scale, gate, g, wgu_ref, wd_ref):
    h = _mod_norm(x, g, shift, scale).astype(BF16)
    gu = jnp.dot(h, wgu_ref[...], preferred_element_type=F32)
    a = gu[:, :D_FF]
    b = gu[:, D_FF:]
    act = (_silu(a) * b).astype(BF16)
    y = jnp.dot(act, wd_ref[...], preferred_element_type=F32)
    return x + (0.5 * gate) * y


def _mod_specs(m, sub, tm):
    rm = m.shape[2]
    blk = 1 if rm == 1 else tm

    def spec(i):
        if rm == 1:
            return pl.BlockSpec((None, None, 1, D_MODEL), lambda g, r: (i, g, 0, 0))
        return pl.BlockSpec((None, None, blk, D_MODEL), lambda g, r: (i, g, r, 0))

    return [spec(3 * sub), spec(3 * sub + 1), spec(3 * sub + 2)]


def _row_spec(tm, width):
    return pl.BlockSpec((None, tm, width), lambda g, r: (g, r, 0))


def _ffn_kernel(x_ref, sh_ref, sc_ref, ga_ref, g_ref, wgu_ref, wd_ref, o_ref):
    o_ref[...] = _ffn_body(x_ref[...], sh_ref[...], sc_ref[...], ga_ref[...], g_ref[...],
                           wgu_ref, wd_ref)


def _ffn(x, m, sub, g, wgu, wd, l, tm):
    G, R, _ = x.shape
    return pl.pallas_call(
        _ffn_kernel,
        grid=(G, R // tm),
        in_specs=[_row_spec(tm, D_MODEL)] + _mod_specs(m, sub, tm) + [
            _const_spec((1, D_MODEL)),
            _layer_spec((D_MODEL, 2 * D_FF), l),
            _layer_spec((D_FF, D_MODEL), l),
        ],
        out_specs=_row_spec(tm, D_MODEL),
        out_shape=jax.ShapeDtypeStruct(x.shape, F32),
        compiler_params=_cparams(("parallel", "parallel")),
        name="ffn",
    )(x, m, m, m, g, wgu, wd)


def _group_mean_sq(x, gmat_ref):
    return jnp.dot((x * x).astype(BF16), gmat_ref[...], preferred_element_type=F32)


def _rope(x, cos, sin_signed):
    width = x.shape[-1]
    reps = width // LANES
    lane = lax.broadcasted_iota(jnp.int32, x.shape, 1)
    first_half = (lane & (SWA_DH - 1)) < (SWA_DH // 2)
    partner = jnp.where(first_half,
                        pltpu.roll(x, width - SWA_DH // 2, 1),
                        pltpu.roll(x, SWA_DH // 2, 1))
    if reps > 1:
        cos = jnp.concatenate([cos] * reps, axis=1)
        sin_signed = jnp.concatenate([sin_signed] * reps, axis=1)
    return x * cos + partner * sin_signed


def _swa_post(p, prm_ref, cos, sin, gq_ref, gk_ref, swa_ref):
    pq = p[:, C_SQ:C_SQ + SWA_Q]
    pk = p[:, C_SK:C_SK + SWA_KV]
    wq = jnp.concatenate([prm_ref[2:3, :]] * (SWA_Q // LANES), axis=1)
    wk = prm_ref[3:4, :]
    qn = pq * lax.rsqrt(_group_mean_sq(pq, gq_ref) + EPS) * wq
    kn = pk * lax.rsqrt(_group_mean_sq(pk, gk_ref) + EPS) * wk
    swa_ref[:, :SWA_Q] = _rope(qn, cos, sin)
    swa_ref[:, SWA_Q:SWA_Q + SWA_KV] = _rope(kn, cos, sin)
    swa_ref[:, SWA_Q + SWA_KV:] = p[:, C_SV:C_SV + SWA_KV]


def _dn_group(a, gi, qkv_ref):
    for hh in range(DN_HEADS):
        cols = slice(gi * DN_QK + hh * DN_DK, gi * DN_QK + (hh + 1) * DN_DK)
        xh = a[:, hh * DN_DK:(hh + 1) * DN_DK]
        if gi < 2:
            s = lax.rsqrt(jnp.sum(xh * xh, axis=-1, keepdims=True) + EPS)
            if gi == 0:
                s = s * (DN_DK ** -0.5)
            xh = xh * s
        qkv_ref[:, cols] = xh.astype(qkv_ref.dtype)


def _bg_post(pb, prm_ref, bg_ref):
    neg_a = prm_ref[0:1, :]
    dt_b = prm_ref[1:2, :]
    xa = pb + dt_b
    softplus = jnp.maximum(xa, 0.0) + jnp.log(1.0 + jnp.exp(-jnp.abs(xa)))
    lane = lax.broadcasted_iota(jnp.int32, pb.shape, 1)
    bg_ref[...] = jnp.where(lane < DN_HEADS, _sigmoid(pb), neg_a * softplus)


def _dn_post(p, qkv_conv, prm_ref, qkv_ref, z_ref, bg_ref):
    qkv = _silu(qkv_conv)
    for gi in range(3):
        _dn_group(qkv[:, gi * DN_QK:(gi + 1) * DN_QK], gi, qkv_ref)
    z_ref[...] = p[:, C_Z:C_Z + DN_V].astype(z_ref.dtype)
    _bg_post(p[:, C_BG:C_BG + LANES], prm_ref, bg_ref)


class _Cols:
    def __init__(self, segments):
        self.segments = segments

    def __getitem__(self, idx):
        cols = idx[1]
        for start, arr in self.segments:
            if start <= cols.start and cols.stop <= start + arr.shape[1]:
                return arr[:, cols.start - start:cols.stop - start]
        raise KeyError(cols)


def _inproj_prompt_kernel(x_ref, sh_ref, sc_ref, g_ref, win_ref, cw_ref, prm_ref, cos_ref, sin_ref,
                          gq_ref, gk_ref, qkv_ref, z_ref, bg_ref, swa_ref, tail_ref, ext_ref):
    tm = x_ref.shape[0]
    first = pl.program_id(1) == 0

    @pl.when(first)
    def _():
        ext_ref[0:HALO, :] = jnp.zeros((HALO, DN_CONV_CH), F32)

    @pl.when(jnp.logical_not(first))
    def _():
        ext_ref[0:HALO, :] = ext_ref[tm:tm + HALO, :]

    h = _mod_norm(x_ref[...], g_ref[...], sh_ref[...], sc_ref[...]).astype(BF16)
    proj = lambda a, b: jnp.dot(h, win_ref[:, a:b], preferred_element_type=F32)
    p_att = _Cols([(C_SQ, proj(C_SQ, C_SK)), (C_SK, proj(C_SK, IN_PACKED))])
    _swa_post(p_att, prm_ref, cos_ref[...], sin_ref[...], gq_ref, gk_ref, swa_ref)
    _bg_post(p_att[:, C_BG:C_BG + LANES], prm_ref, bg_ref)
    for gi in range(3):
        cs = slice(gi * DN_QK, (gi + 1) * DN_QK)
        pc = proj(cs.start, cs.stop)
        ext_ref[HALO:, cs] = pc
        y = pc * cw_ref[DN_CONV - 1:DN_CONV, cs]
        for j in range(DN_CONV - 1):
            off = HALO - (DN_CONV - 1) + j
            y = y + ext_ref[off:off + tm, cs] * cw_ref[j:j + 1, cs]
        _dn_group(_silu(y), gi, qkv_ref)
    tail_ref[...] = ext_ref[tm:tm + HALO, :]
    z_ref[...] = proj(C_Z, C_Z + DN_V).astype(z_ref.dtype)


def _inproj_sample_kernel(x_ref, sh_ref, sc_ref, g_ref, win_ref, cw_ref, prm_ref, cos_ref, sin_ref,
                          gq_ref, gk_ref, buf_ref, qkv_ref, z_ref, bg_ref, swa_ref, cnew_ref):
    h = _mod_norm(x_ref[...], g_ref[...], sh_ref[...], sc_ref[...]).astype(BF16)
    p = jnp.dot(h, win_ref[...], preferred_element_type=F32)
    pc = p[:, :DN_CONV_CH]
    y = pc * cw_ref[DN_CONV - 1:DN_CONV, :]
    for j in range(DN_CONV - 1):
        y = y + buf_ref[j] * cw_ref[j:j + 1, :]
    for j in range(DN_CONV - 2):
        cnew_ref[j] = buf_ref[j + 1]
    cnew_ref[DN_CONV - 2] = pc
    _swa_post(p, prm_ref, cos_ref[...], sin_ref[...], gq_ref, gk_ref, swa_ref)
    _dn_post(p, y, prm_ref, qkv_ref, z_ref, bg_ref)


def _inproj_common_specs(m, tm, l):
    sh, sc, _ = _mod_specs(m, 1, tm)
    return [_row_spec(tm, D_MODEL), sh, sc,
            _const_spec((1, D_MODEL)),
            _layer_spec((D_MODEL, IN_PACKED), l),
            _const_spec((DN_CONV, DN_CONV_CH)),
            _const_spec((SUBLANES, LANES))]


def _inproj_prompt(x, m, g, win, cw, prm, cos, sin, gq, gk, l, tm):
    B, T, _ = x.shape
    table = pl.BlockSpec((tm, LANES), lambda b, r: (r, 0))
    return pl.pallas_call(
        _inproj_prompt_kernel,
        grid=(B, T // tm),
        in_specs=_inproj_common_specs(m, tm, l) + [
            table, table,
            _const_spec((SWA_Q, SWA_Q)),
            _const_spec((SWA_KV, SWA_KV)),
        ],
        out_specs=[
            _row_spec(tm, DN_CONV_CH), _row_spec(tm, DN_V), _row_spec(tm, LANES),
            _row_spec(tm, SWA_OUT),
            pl.BlockSpec((None, HALO, DN_CONV_CH), lambda b, r: (b, 0, 0)),
        ],
        out_shape=[
            jax.ShapeDtypeStruct((B, T, DN_CONV_CH), BF16),
            jax.ShapeDtypeStruct((B, T, DN_V), BF16),
            jax.ShapeDtypeStruct((B, T, LANES), F32),
            jax.ShapeDtypeStruct((B, T, SWA_OUT), F32),
            jax.ShapeDtypeStruct((B, HALO, DN_CONV_CH), F32),
        ],
        scratch_shapes=[pltpu.VMEM((HALO + tm, DN_CONV_CH), F32)],
        compiler_params=_cparams(("parallel", "arbitrary")),
        name="inproj_prompt",
    )(x, m, m, g, win, cw, prm, cos, sin, gq, gk)


def _inproj_sample(x, m, g, win, cw, prm, cos, sin, gq, gk, buf, l):
    _, R, _ = x.shape
    nbuf = DN_CONV - 1
    return pl.pallas_call(
        _inproj_sample_kernel,
        grid=(1, 1),
        in_specs=_inproj_common_specs(m, R, l) + [
            _const_spec((1, LANES)), _const_spec((1, LANES)),
            _const_spec((SWA_Q, SWA_Q)), _const_spec((SWA_KV, SWA_KV)),
            _const_spec((nbuf, R, DN_CONV_CH)),
        ],
        out_specs=[
            _row_spec(R, DN_CONV_CH), _row_spec(R, DN_V), _row_spec(R, LANES), _row_spec(R, SWA_OUT),
            pl.BlockSpec((nbuf, R, DN_CONV_CH), lambda g_, r: (0, 0, 0)),
        ],
        out_shape=[
            jax.ShapeDtypeStruct((1, R, DN_CONV_CH), F32),
            jax.ShapeDtypeStruct((1, R, DN_V), F32),
            jax.ShapeDtypeStruct((1, R, LANES), F32),
            jax.ShapeDtypeStruct((1, R, SWA_OUT), F32),
            jax.ShapeDtypeStruct((nbuf, R, DN_CONV_CH), F32),
        ],
        compiler_params=_cparams(("arbitrary", "arbitrary")),
        name="inproj_sample",
    )(x, m, m, g, win, cw, prm, cos, sin, gq, gk, buf)


def _gated_norm(o, z, nw):
    on = o * lax.rsqrt(jnp.mean(o * o, axis=-1, keepdims=True) + EPS) * nw
    return on * _silu(z.astype(F32))


def _mm(a, b):
    return jnp.dot(a.astype(BF16), b.astype(BF16), preferred_element_type=F32)


def _wide_to_blockdiag(wide):
    c, n = wide.shape[0], wide.shape[1] // wide.shape[0]
    cblk = lax.broadcasted_iota(jnp.int32, wide.shape, 1) // c
    return jnp.concatenate([jnp.where(cblk == i, wide, 0.0) for i in range(n)], axis=0)


def _blockdiag_to_wide(full, c):
    n = full.shape[0] // c
    cblk = lax.broadcasted_iota(jnp.int32, (c, full.shape[1]), 1) // c
    wide = full[0:c]
    for i in range(1, n):
        wide = jnp.where(cblk == i, full[i * c:(i + 1) * c], wide)
    return wide


def _unit_lower_inverses(mats):
    c = mats[0].shape[0]
    row = lax.broadcasted_iota(jnp.int32, mats[0].shape, 0)
    col = lax.broadcasted_iota(jnp.int32, mats[0].shape, 1) % c
    s = INV_BLOCK
    diag = (row // s) == (col // s)
    pws = [jnp.where(diag, a, 0.0) for a in mats]
    eye = jnp.where(row == col, 1.0, 0.0)
    xs = [eye - pw for pw in pws]
    k = 2
    while k < s:
        pws = [_mm(pw, _wide_to_blockdiag(pw)) for pw in pws]
        xs = [x + _mm(x, _wide_to_blockdiag(pw)) for x, pw in zip(xs, pws)]
        k *= 2
    while s < c:
        pair = ((row // (2 * s)) == (col // (2 * s))) & ((row // s) != (col // s))
        ts = [_mm(x, _wide_to_blockdiag(jnp.where(pair, a, 0.0))) for x, a in zip(xs, mats)]
        xs = [x - _mm(t, _wide_to_blockdiag(x)) for x, t in zip(xs, ts)]
        s *= 2
    return xs


def _cumsum_rows(x):
    n = x.shape[0]
    row = lax.broadcasted_iota(jnp.int32, x.shape, 0)
    s = 1
    while s < n:
        x = x + jnp.where(row >= s, pltpu.roll(x, s, 0), 0.0)
        s *= 2
    return x


WQ_ROWS = 2 * DN_CHUNK
QKD_ROWS = DN_CHUNK + DN_DK


def _delta_intra_kernel(q_ref, k_ref, v_ref, bg_ref, u_ref, wq_ref, qkd_ref, eg_ref):
    c = DN_CHUNK
    r = INTRA_ROWS
    nc = r // c
    row = lax.broadcasted_iota(jnp.int32, (c, r), 0)
    col = lax.broadcasted_iota(jnp.int32, (c, r), 1) % c
    incl = row >= col
    strict = row > col

    jobs = [(blk, h) for blk in range(q_ref.shape[0] // r) for h in range(DN_HEADS)]
    gcs, egcs, ekds, gcts, bgs = [], [], [], [], []
    for blk in range(q_ref.shape[0] // r):
        bg = bg_ref[blk * r:(blk + 1) * r, :]
        gc = jnp.concatenate([_cumsum_rows(bg[ci * c:(ci + 1) * c, :]) for ci in range(nc)], axis=0)
        g_end = [gc[(ci + 1) * c - 1:(ci + 1) * c, :] for ci in range(nc)]
        glast = jnp.concatenate([jnp.broadcast_to(g, (c, LANES)) for g in g_end], axis=0)
        for ci in range(nc):
            rows = [jnp.broadcast_to(jnp.exp(g_end[ci][:, DN_HEADS + h:DN_HEADS + h + 1]), (1, LANES))
                    for h in range(DN_HEADS)]
            eg_ref[blk * nc + ci] = jnp.concatenate(
                rows + [jnp.zeros((SUBLANES - DN_HEADS, LANES), F32)], axis=0)
        bgs.append(bg)
        gcs.append(gc)
        gcts.append(gc.T)
        egcs.append(jnp.exp(gc))
        ekds.append(jnp.exp(glast - gc))

    prep = []
    for blk, h in jobs:
        rs = slice(blk * r, (blk + 1) * r)
        sl = slice(h * DN_DK, (h + 1) * DN_DK)
        q = q_ref[rs, sl]
        k = k_ref[rs, sl]
        kb = k * bgs[blk][:, h:h + 1]
        kq = lax.dot_general(jnp.concatenate([kb, q], axis=0).astype(BF16), k.astype(BF16),
                             (((1,), (1,)), ((), ())), preferred_element_type=F32)
        prep.append((q, k, kb, kq))
    a_mats, qks = [], []
    for (blk, h), (q, k, kb, kq) in zip(jobs, prep):
        gl = slice(DN_HEADS + h, DN_HEADS + h + 1)
        gcol = jnp.concatenate([jnp.broadcast_to(gcs[blk][ci * c:(ci + 1) * c, gl], (c, c))
                                for ci in range(nc)], axis=1)
        decay = jnp.exp(jnp.where(incl, gcol - gcts[blk][gl, :], -jnp.inf))
        a_mats.append(jnp.where(strict, _blockdiag_to_wide(kq[:r], c) * decay, 0.0))
        qks.append(_blockdiag_to_wide(kq[r:], c) * decay)
    tinvs = _unit_lower_inverses(a_mats)
    uws = []
    for (blk, h), (q, k, kb, kq), tinv in zip(jobs, prep, tinvs):
        rs = slice(blk * r, (blk + 1) * r)
        sl = slice(h * DN_DK, (h + 1) * DN_DK)
        gl = slice(DN_HEADS + h, DN_HEADS + h + 1)
        rhs = jnp.concatenate([v_ref[rs, sl] * bgs[blk][:, h:h + 1], kb * egcs[blk][:, gl]], axis=1)
        uws.append(_mm(_wide_to_blockdiag(tinv), rhs))
    for (blk, h), (q, k, kb, kq), uw, qk in zip(jobs, prep, uws, qks):
        rs = slice(blk * r, (blk + 1) * r)
        sl = slice(h * DN_DK, (h + 1) * DN_DK)
        gl = slice(DN_HEADS + h, DN_HEADS + h + 1)
        u_ref[rs, sl] = uw[:, :DN_DV]
        w_b = uw[:, DN_DV:].astype(BF16)
        qd_b = (q * egcs[blk][:, gl]).astype(BF16)
        for ci in range(nc):
            wq_ref[blk * nc + ci, h * WQ_ROWS:h * WQ_ROWS + c, :] = w_b[ci * c:(ci + 1) * c, :]
            wq_ref[blk * nc + ci, h * WQ_ROWS + c:(h + 1) * WQ_ROWS, :] = qd_b[ci * c:(ci + 1) * c, :]
        kdt = (k * ekds[blk][:, gl]).T
        for pj in range(nc // 2):
            ls = slice(pj * LANES, (pj + 1) * LANES)
            pi = blk * (nc // 2) + pj
            qkd_ref[pi, h * QKD_ROWS:h * QKD_ROWS + c, :] = qk[:, ls].astype(BF16)
            qkd_ref[pi, h * QKD_ROWS + c:(h + 1) * QKD_ROWS, :] = kdt[:, ls].astype(BF16)


def _delta_scan_kernel(u_ref, wq_ref, qkd_ref, eg_ref, o_ref, sout_ref, s_ref):
    n = pl.program_id(0)
    c = DN_CHUNK

    @pl.when(n == 0)
    def _():
        s_ref[...] = jnp.zeros_like(s_ref)

    zeros = jnp.zeros((c, DN_DV), BF16)
    chains = [(b, h) for b in range(u_ref.shape[0]) for h in range(DN_HEADS)]
    for j in range(2):
        rs = slice(j * c, (j + 1) * c)
        ws = [jnp.dot(wq_ref[b, j, h * WQ_ROWS:(h + 1) * WQ_ROWS, :], s_ref[b, h].astype(BF16),
                      preferred_element_type=F32) for b, h in chains]
        rr = []
        for (b, h), w_h in zip(chains, ws):
            v_new = (u_ref[b, rs, h * DN_DV:(h + 1) * DN_DV] - w_h[:c]).astype(BF16)
            rhs = jnp.concatenate([v_new, zeros] if j == 0 else [zeros, v_new], axis=0)
            rr.append(jnp.dot(qkd_ref[b, h * QKD_ROWS:(h + 1) * QKD_ROWS, :], rhs,
                              preferred_element_type=F32))
        for (b, h), w_h, r_h in zip(chains, ws, rr):
            sl = slice(h * DN_DV, (h + 1) * DN_DV)
            s_ref[b, h] = s_ref[b, h] * eg_ref[b, j, h:h + 1, :] + r_h[c:]
            o_ref[b, rs, sl] = (w_h[c:] + r_h[:c]).astype(o_ref.dtype)

    @pl.when(n == pl.num_programs(0) - 1)
    def _():
        sout_ref[...] = s_ref[...]


def _delta_prompt(qkv, bg, ra):
    B, T, _ = qkv.shape
    c = DN_CHUNK
    n = T // c
    cpb = ra // c
    qkv_spec = lambda j: pl.BlockSpec((None, ra, DN_QK), lambda b, r: (b, r, j))
    u, wq, qkd, eg = pl.pallas_call(
        _delta_intra_kernel,
        grid=(B, T // ra),
        in_specs=[qkv_spec(0), qkv_spec(1), qkv_spec(2),
                  pl.BlockSpec((None, ra, LANES), lambda b, r: (b, r, 0))],
        out_specs=[pl.BlockSpec((None, ra, DN_V), lambda b, r: (b, r, 0)),
                   pl.BlockSpec((None, cpb, DN_HEADS * WQ_ROWS, DN_DK), lambda b, r: (b, r, 0, 0)),
                   pl.BlockSpec((None, cpb // 2, DN_HEADS * QKD_ROWS, LANES),
                                lambda b, r: (b, r, 0, 0)),
                   pl.BlockSpec((None, cpb, SUBLANES, LANES), lambda b, r: (b, r, 0, 0))],
        out_shape=[jax.ShapeDtypeStruct((B, T, DN_V), F32),
                   jax.ShapeDtypeStruct((B, n, DN_HEADS * WQ_ROWS, DN_DK), BF16),
                   jax.ShapeDtypeStruct((B, n // 2, DN_HEADS * QKD_ROWS, LANES), BF16),
                   jax.ShapeDtypeStruct((B, n, SUBLANES, LANES), F32)],
        compiler_params=_cparams(("parallel", "parallel")),
        name="delta_intra",
    )(qkv, qkv, qkv, bg)
    return pl.pallas_call(
        _delta_scan_kernel,
        grid=(n // 2,),
        in_specs=[pl.BlockSpec((B, 2 * c, DN_V), lambda i: (0, i, 0)),
                  pl.BlockSpec((B, 2, DN_HEADS * WQ_ROWS, DN_DK), lambda i: (0, i, 0, 0)),
                  pl.BlockSpec((B, None, DN_HEADS * QKD_ROWS, LANES), lambda i: (0, i, 0, 0)),
                  pl.BlockSpec((B, 2, SUBLANES, LANES), lambda i: (0, i, 0, 0))],
        out_specs=[pl.BlockSpec((B, 2 * c, DN_V), lambda i: (0, i, 0)),
                   pl.BlockSpec((B, DN_HEADS, DN_DK, DN_DV), lambda i: (0, 0, 0, 0))],
        out_shape=[jax.ShapeDtypeStruct((B, T, DN_V), BF16),
                   jax.ShapeDtypeStruct((B, DN_HEADS, DN_DK, DN_DV), F32)],
        scratch_shapes=[pltpu.VMEM((B, DN_HEADS, DN_DK, DN_DV), F32)],
        compiler_params=_cparams(("arbitrary",)),
        name="delta_scan",
    )(u, wq, qkd, eg)


def _delta_sample_kernel(n_prev, qkv_ref, bg_ref, z_ref, nw_ref, s0_ref, *refs):
    prev_refs = refs[:n_prev]
    o_ref, s_ref = refs[n_prev:]
    for k, prev_ref in enumerate(prev_refs):
        s_ref[k] = prev_ref[...]
    s_out = s_ref.at[n_prev] if n_prev else s_ref
    bb = qkv_ref.shape[0]
    nw = nw_ref[...]
    zeros = jnp.zeros((SUBLANES - 2, DN_DK), F32)
    row = lax.broadcasted_iota(jnp.int32, (bb, DN_DV), 0)
    heads = []
    for h in range(DN_HEADS):
        q = qkv_ref[:, h * DN_DK:(h + 1) * DN_DK]
        k = qkv_ref[:, DN_QK + h * DN_DK:DN_QK + (h + 1) * DN_DK]
        v = qkv_ref[:, 2 * DN_QK + h * DN_DV:2 * DN_QK + (h + 1) * DN_DV]
        beta = bg_ref[:, h:h + 1]
        eg = jnp.exp(bg_ref[:, DN_HEADS + h:DN_HEADS + h + 1])
        w = k * beta * eg
        qd = q * eg
        ws = [jnp.dot(jnp.concatenate([w[i:i + 1], qd[i:i + 1], zeros], axis=0), s0_ref[i, h],
                      preferred_element_type=F32) for i in range(bb)]
        heads.append((q, k, v, beta, eg, ws))
    updates = []
    for q, k, v, beta, eg, ws in heads:
        w_s = jnp.concatenate([ws[i][0:1] for i in range(bb)], axis=0)
        q_s = jnp.concatenate([ws[i][1:2] for i in range(bb)], axis=0)
        v_new = v * beta - w_s
        kt = k.T
        kv = [jnp.dot(kt, jnp.where(row == i, v_new, 0.0), preferred_element_type=F32)
              for i in range(bb)]
        updates.append((v_new, q_s, kv))
    for h, ((q, k, v, beta, eg, ws), (v_new, q_s, kv)) in enumerate(zip(heads, updates)):
        sl = slice(h * DN_DK, (h + 1) * DN_DK)
        for i in range(bb):
            s_out[i, h] = s0_ref[i, h] * eg[i:i + 1] + kv[i]
        o = q_s + jnp.sum(q * k, axis=-1, keepdims=True) * v_new
        o_ref[:, sl] = _gated_norm(o, z_ref[:, sl], nw)


def _delta_sample(qkv, bg, z, nw, s0, l, bb, prev=()):
    R = qkv.shape[0]
    state = (bb, DN_HEADS, DN_DK, DN_DV)
    state_spec = pl.BlockSpec(state, lambda i: (i, 0, 0, 0))
    if prev:
        out_spec = pl.BlockSpec((len(prev) + 1,) + state, lambda i: (0, i, 0, 0, 0))
        out_shape = (len(prev) + 1,) + s0.shape[1:]
    else:
        out_spec, out_shape = state_spec, s0.shape[1:]
    return pl.pallas_call(
        functools.partial(_delta_sample_kernel, len(prev)),
        grid=(R // bb,),
        in_specs=[pl.BlockSpec((bb, DN_CONV_CH), lambda i: (i, 0)),
                  pl.BlockSpec((bb, LANES), lambda i: (i, 0)),
                  pl.BlockSpec((bb, DN_V), lambda i: (i, 0)),
                  _const_spec((1, DN_DV)),
                  pl.BlockSpec((None,) + state, lambda i: (l, i, 0, 0, 0))]
        + [state_spec] * len(prev),
        out_specs=[pl.BlockSpec((bb, DN_V), lambda i: (i, 0)), out_spec],
        out_shape=[jax.ShapeDtypeStruct((R, DN_V), F32),
                   jax.ShapeDtypeStruct(out_shape, F32)],
        compiler_params=_cparams(("parallel",)),
        name="delta_sample",
    )(qkv, bg, z, nw, s0, *prev)


def _swa_prompt_kernel(q_ref, kc_ref, vc_ref, kp_ref, vp_ref, sink_ref, o_ref):
    n = pl.program_id(1)
    w = WINDOW
    nq = q_ref.shape[0] // w
    half = SWA_DH
    key = lax.broadcasted_iota(jnp.int32, (2 * w, w), 0)
    qry = lax.broadcasted_iota(jnp.int32, (2 * w, w), 1)
    band = (key >= qry) & (key <= qry + w)
    bias_mid = jnp.where(band, 0.0, -jnp.inf)
    bias_first = jnp.where(band & (key >= jnp.where(n == 0, w, 0)), 0.0, -jnp.inf)
    kall = jnp.concatenate([kp_ref[...], kc_ref[...]], axis=0)
    vall = jnp.concatenate([vp_ref[...], vc_ref[...]], axis=0)
    low = lax.broadcasted_iota(jnp.int32, kall.shape, 1) < half
    kall_r = pltpu.roll(kall, half, 1)
    k_lo = [jnp.where(low, kall, 0.0).astype(BF16), jnp.where(low, kall_r, 0.0).astype(BF16)]
    k_hi = [jnp.where(low, 0.0, kall_r).astype(BF16), jnp.where(low, 0.0, kall).astype(BF16)]
    v_t = vall.T.astype(BF16)
    zeros_v = jnp.zeros((half, 2 * w), BF16)
    sum_rows = 2 * SUBLANES
    ones8 = jnp.ones((sum_rows, 2 * w), BF16)
    zeros8 = jnp.zeros((sum_rows, 2 * w), BF16)
    first_pair = lax.broadcasted_iota(jnp.int32, (1, 2 * w), 1) < w
    log2e = math.log2(math.e)
    scale = SWA_DH ** -0.5 * log2e

    units = [(i, j) for i in range(nq) for j in range(SWA_KV_HEADS)]
    scores = []
    for i, j in units:
        rs = slice(i * w, (i + 2) * w)
        kk = jnp.concatenate([k_lo[j][rs], k_hi[j][rs]], axis=0)
        c0 = j * SWA_GROUP * SWA_DH
        q = jnp.concatenate([q_ref[i * w:(i + 1) * w, c0:c0 + LANES],
                             q_ref[i * w:(i + 1) * w, c0 + LANES:c0 + 2 * LANES]], axis=0)
        scores.append(lax.dot_general(kk, (q * scale).astype(BF16), (((1,), (1,)), ((), ())),
                                      preferred_element_type=F32))
    probs, sink_rows, maxima = [], [], []
    for (i, j), s in zip(units, scores):
        bias = bias_first if i == 0 else bias_mid
        bias2 = jnp.concatenate([bias, bias], axis=1)
        p_hf, sk_hf, mx_hf = [], [], []
        for hf in range(2):
            heads = [j * SWA_GROUP + 2 * pr + hf for pr in range(2)]
            sink = log2e * jnp.where(first_pair, sink_ref[0:1, heads[0]:heads[0] + 1],
                                     sink_ref[0:1, heads[1]:heads[1] + 1])
            masked = s[hf * 2 * w:(hf + 1) * 2 * w, :] + bias2
            mx = jnp.maximum(jnp.max(masked, axis=0, keepdims=True), sink)
            p_hf.append(jnp.exp2(masked - mx).astype(BF16))
            sk_hf.append(sink)
            mx_hf.append(mx)
        probs.append(jnp.concatenate(p_hf, axis=0))
        sink_rows.append(sk_hf)
        maxima.append(mx_hf)
    outs = []
    for (i, j), p in zip(units, probs):
        vt = v_t[j * half:(j + 1) * half, i * w:(i + 2) * w]
        lhs = jnp.concatenate([jnp.concatenate([vt, zeros_v], axis=1),
                               jnp.concatenate([zeros_v, vt], axis=1),
                               jnp.concatenate([ones8, zeros8], axis=1),
                               jnp.concatenate([zeros8, ones8], axis=1)], axis=0)
        outs.append(jnp.dot(lhs, p, preferred_element_type=F32))
    for (i, j), o_t, sk_hf, mx_hf in zip(units, outs, sink_rows, maxima):
        c0 = j * SWA_GROUP * SWA_DH
        parts = []
        for hf in range(2):
            den = (o_t[2 * half + hf * sum_rows:2 * half + hf * sum_rows + 1, :]
                   + jnp.exp2(sk_hf[hf] - mx_hf[hf]))
            parts.append(o_t[hf * half:(hf + 1) * half, :] / den)
        o = jnp.concatenate(parts, axis=0).T
        for pr in range(2):
            o_ref[i * w:(i + 1) * w, c0 + pr * LANES:c0 + (pr + 1) * LANES] = (
                o[pr * w:(pr + 1) * w, :].astype(o_ref.dtype))


def _swa_prompt(swa, sinks, nq):
    B, T, _ = swa.shape
    w = WINDOW
    rq = nq * w
    kcol = SWA_Q // SWA_KV
    prev = lambda n: jnp.maximum(n * nq - 1, 0)
    return pl.pallas_call(
        _swa_prompt_kernel,
        grid=(B, T // rq),
        in_specs=[pl.BlockSpec((None, rq, SWA_Q), lambda b, n: (b, n, 0)),
                  pl.BlockSpec((None, rq, SWA_KV), lambda b, n: (b, n, kcol)),
                  pl.BlockSpec((None, rq, SWA_KV), lambda b, n: (b, n, kcol + 1)),
                  pl.BlockSpec((None, w, SWA_KV), lambda b, n: (b, prev(n), kcol)),
                  pl.BlockSpec((None, w, SWA_KV), lambda b, n: (b, prev(n), kcol + 1)),
                  _const_spec((1, LANES))],
        out_specs=pl.BlockSpec((None, rq, SWA_Q), lambda b, n: (b, n, 0)),
        out_shape=jax.ShapeDtypeStruct((B, T, SWA_Q), BF16),
        compiler_params=_cparams(("parallel", "parallel")),
        name="swa_prompt",
    )(swa, swa, swa, swa, swa, sinks)


def _swa_sample_kernel(swa_ref, kc_ref, vc_ref, sink_ref, o_ref, kn_ref, vn_ref):
    bb = swa_ref.shape[0]
    w = kc_ref.shape[1]
    scale = SWA_DH ** -0.5
    for i in range(bb):
        kn_ref[i, 0:w - 1, :] = kc_ref[i, 1:w, :]
        kn_ref[i, w - 1:w, :] = swa_ref[i:i + 1, SWA_Q:SWA_Q + SWA_KV]
        vn_ref[i, 0:w - 1, :] = vc_ref[i, 1:w, :]
        vn_ref[i, w - 1:w, :] = swa_ref[i:i + 1, SWA_Q + SWA_KV:]
    units = [(i, j) for i in range(bb) for j in range(SWA_KV_HEADS)]
    sinks = [jnp.concatenate(
        [sink_ref[0:1, j * SWA_GROUP + g:j * SWA_GROUP + g + 1] for g in range(SWA_GROUP)]
        + [jnp.zeros((SUBLANES - SWA_GROUP, 1), F32)], axis=0) for j in range(SWA_KV_HEADS)]
    qs, scs = [], []
    for i, j in units:
        q4 = jnp.concatenate(
            [swa_ref[i:i + 1, (j * SWA_GROUP + g) * SWA_DH:(j * SWA_GROUP + g + 1) * SWA_DH]
             for g in range(SWA_GROUP)] + [jnp.zeros((SUBLANES - SWA_GROUP, SWA_DH), F32)], axis=0)
        qs.append(q4)
        scs.append(lax.dot_general(q4, kc_ref[i, :, j * SWA_DH:(j + 1) * SWA_DH],
                                   (((1,), (1,)), ((), ())), preferred_element_type=F32) * scale)
    soft = []
    for (i, j), q4, sc in zip(units, qs, scs):
        knew = swa_ref[i:i + 1, SWA_Q + j * SWA_DH:SWA_Q + (j + 1) * SWA_DH]
        sn = jnp.sum(q4 * knew, axis=-1, keepdims=True) * scale
        mx = jnp.maximum(jnp.maximum(jnp.max(sc, axis=-1, keepdims=True), sn), sinks[j])
        pc = jnp.exp(sc - mx)
        pn = jnp.exp(sn - mx)
        den = jnp.sum(pc, axis=-1, keepdims=True) + pn + jnp.exp(sinks[j] - mx)
        soft.append((pc, pn, den))
    pvs = [jnp.dot(pc, vc_ref[i, :, j * SWA_DH:(j + 1) * SWA_DH], preferred_element_type=F32)
           for (i, j), (pc, pn, den) in zip(units, soft)]
    for (i, j), (pc, pn, den), pv in zip(units, soft, pvs):
        vnew = swa_ref[i:i + 1, SWA_Q + SWA_KV + j * SWA_DH:SWA_Q + SWA_KV + (j + 1) * SWA_DH]
        o = (pv + pn * vnew) / den
        for g in range(SWA_GROUP):
            hq = j * SWA_GROUP + g
            o_ref[i:i + 1, hq * SWA_DH:(hq + 1) * SWA_DH] = o[g:g + 1, :]


def _swa_sample(swa, kc, vc, sinks, bb):
    R = swa.shape[0]
    w = kc.shape[1]
    cache_spec = pl.BlockSpec((bb, w, SWA_KV), lambda i: (i, 0, 0))
    return pl.pallas_call(
        _swa_sample_kernel,
        grid=(R // bb,),
        in_specs=[pl.BlockSpec((bb, SWA_OUT), lambda i: (i, 0)), cache_spec, cache_spec,
                  _const_spec((1, LANES))],
        out_specs=[pl.BlockSpec((bb, SWA_Q), lambda i: (i, 0)), cache_spec, cache_spec],
        out_shape=[jax.ShapeDtypeStruct((R, SWA_Q), F32),
                   jax.ShapeDtypeStruct(kc.shape, F32),
                   jax.ShapeDtypeStruct(vc.shape, F32)],
        compiler_params=_cparams(("parallel",)),
        name="swa_sample",
    )(swa, kc, vc, sinks)


def _outproj_ffn_kernel(gate_dn, x_ref, dn_ref, sw_ref, *refs):
    g5_ref, sh_ref, sc_ref, ga_ref, g_ref, woa_ref, wob_ref, wgu_ref, wd_ref, o_ref = refs[-10:]
    if gate_dn:
        z_ref, nw_ref = refs[:2]
        nw = nw_ref[...]
        dn = jnp.concatenate(
            [_gated_norm(dn_ref[:, h * DN_DV:(h + 1) * DN_DV].astype(F32),
                         z_ref[:, h * DN_DV:(h + 1) * DN_DV], nw).astype(BF16)
             for h in range(DN_HEADS)], axis=1)
    else:
        dn = dn_ref[...].astype(BF16)
    mix = (jnp.dot(dn, woa_ref[...], preferred_element_type=F32)
           + jnp.dot(sw_ref[...].astype(BF16), wob_ref[...], preferred_element_type=F32))
    x = x_ref[...] + g5_ref[...] * mix
    o_ref[...] = _ffn_body(x, sh_ref[...], sc_ref[...], ga_ref[...], g_ref[...], wgu_ref, wd_ref)


def _outproj_ffn(x, dn, sw, m, g, woa, wob, wgu, wd, l, tm, z=None, nw=None):
    G, R, _ = x.shape
    g5 = _mod_specs(m, 1, tm)[2]
    gate_dn = z is not None
    gate_specs = [_row_spec(tm, DN_V), _const_spec((1, DN_DV))] if gate_dn else []
    gate_args = (z, nw) if gate_dn else ()
    return pl.pallas_call(
        functools.partial(_outproj_ffn_kernel, gate_dn),
        grid=(G, R // tm),
        in_specs=[_row_spec(tm, D_MODEL), _row_spec(tm, DN_V), _row_spec(tm, SWA_Q)]
        + gate_specs + [g5] + _mod_specs(m, 2, tm) + [
            _const_spec((1, D_MODEL)),
            _layer_spec((DN_V, D_MODEL), l),
            _layer_spec((SWA_Q, D_MODEL), l),
            _layer_spec((D_MODEL, 2 * D_FF), l),
            _layer_spec((D_FF, D_MODEL), l),
        ],
        out_specs=_row_spec(tm, D_MODEL),
        out_shape=jax.ShapeDtypeStruct(x.shape, F32),
        compiler_params=_cparams(("parallel", "parallel")),
        name="outproj_ffn",
    )(x, dn, sw, *gate_args, m, m, m, m, g, woa, wob, wgu, wd)


def _rope_tables(pos):
    half = SWA_DH // 2
    inv = jnp.power(ROPE_THETA, -jnp.arange(half, dtype=F32) * 2.0 / SWA_DH)
    ang = pos.astype(F32)[:, None] * inv[None, :]
    cos, sin = jnp.cos(ang), jnp.sin(ang)
    reps = LANES // SWA_DH
    return (jnp.concatenate([cos, cos] * reps, axis=1),
            jnp.concatenate([-sin, sin] * reps, axis=1))


def _group_matrix(width):
    i = jnp.arange(width) // SWA_DH
    return jnp.where(i[:, None] == i[None, :], 1.0 / SWA_DH, 0.0).astype(BF16)


def _pack_w_in(w_in):
    conv = w_in[..., :DN_CONV_CH]
    o = DN_CONV_CH
    z = w_in[..., o:o + DN_V]
    o += DN_V
    ba = w_in[..., o:o + 2 * DN_HEADS]
    o += 2 * DN_HEADS
    sq = w_in[..., o:o + SWA_Q]
    o += SWA_Q
    sk = w_in[..., o:o + SWA_KV]
    o += SWA_KV
    sv = w_in[..., o:o + SWA_KV]
    pad = jnp.zeros(w_in.shape[:-1] + (LANES - 2 * DN_HEADS,), w_in.dtype)
    return jnp.concatenate([conv, z, sq, sk, sv, ba, pad], axis=-1).astype(BF16)


def _lane_row(vec, offset=0):
    return jnp.zeros((LANES,), F32).at[offset:offset + vec.shape[0]].set(vec.astype(F32))


def kernel(x_prompt, x_sample, c_prompt, c_sample, state_dn_conv, state_dn_S, cache_swa_k, cache_swa_v, w_ada, b_ada, g_ffn1, w_ffn1_gu, w_ffn1_down, g_mix, w_in, dn_conv_w, dn_A_log, dn_dt_bias, dn_norm_w, swa_q_norm, swa_k_norm, swa_sinks, w_out, g_ffn2, w_ffn2_gu, w_ffn2_down):
    L = w_ada.shape[0]
    B, T, _ = x_prompt.shape
    Bs = x_sample.shape[0]
    tm = min(ROW_TILE, T)
    bb = min(SAMPLE_SEQS, Bs)
    assert T % tm == 0 and T % min(INTRA_STEP_ROWS, T) == 0 and T % INTRA_ROWS == 0
    assert T % (min(SWA_QBLOCKS, T // WINDOW) * WINDOW) == 0 and Bs % bb == 0
    assert x_sample.shape[1] == 1 and cache_swa_k.shape[2] == WINDOW

    m_p, m_s = _adaln(c_prompt, c_sample, w_ada, b_ada)
    m_p = m_p.reshape(L, N_MOD, B, 1, D_MODEL)
    m_s = m_s.reshape(L, N_MOD, 1, Bs, D_MODEL)

    cos_p, sin_p = _rope_tables(jnp.arange(T))
    cos_s, sin_s = _rope_tables(PAST_LEN + jnp.arange(1))
    gq, gk = _group_matrix(SWA_Q), _group_matrix(SWA_KV)

    wgu1, wd1 = w_ffn1_gu.astype(BF16), w_ffn1_down.astype(BF16)
    wgu2, wd2 = w_ffn2_gu.astype(BF16), w_ffn2_down.astype(BF16)
    win = _pack_w_in(w_in)
    woa, wob = w_out[:, :DN_V].astype(BF16), w_out[:, DN_V:].astype(BF16)

    yp = x_prompt
    ys = x_sample.reshape(1, Bs, D_MODEL)
    outs = [[] for _ in range(8)]
    for l in range(L):
        g1, gm, g2 = (g_ffn1[l].reshape(1, D_MODEL), g_mix[l].reshape(1, D_MODEL),
                      g_ffn2[l].reshape(1, D_MODEL))
        cw = dn_conv_w[l]
        nw = dn_norm_w[l].reshape(1, DN_DV)
        prm = jnp.stack([
            _lane_row(-jnp.exp(dn_A_log[l].astype(F32)), DN_HEADS),
            _lane_row(dn_dt_bias[l], DN_HEADS),
            jnp.concatenate([swa_q_norm[l]] * (LANES // SWA_DH)).astype(F32),
            jnp.concatenate([swa_k_norm[l]] * (LANES // SWA_DH)).astype(F32),
        ] + [jnp.zeros((LANES,), F32)] * (SUBLANES - 4))
        sinks = _lane_row(swa_sinks[l]).reshape(1, LANES)

        yp = _ffn(yp, m_p[l], 0, g1, wgu1, wd1, l, tm)
        qkv, z, bg, swa, tail = _inproj_prompt(yp, m_p[l], gm, win, cw, prm, cos_p, sin_p, gq, gk,
                                               l, tm)
        dn_o, s_new = _delta_prompt(qkv, bg, min(INTRA_STEP_ROWS, T))
        sw_o = _swa_prompt(swa, sinks, min(SWA_QBLOCKS, T // WINDOW))
        yp = _outproj_ffn(yp, dn_o, sw_o, m_p[l], g2, woa, wob, wgu2, wd2, l, tm, z, nw)
        nkeep = min(WINDOW, T)
        outs[0].append(tail[:, HALO - (DN_CONV - 1):])
        outs[1].append(s_new)
        outs[2].append(swa[:, T - nkeep:, SWA_Q:SWA_Q + SWA_KV].reshape(B, nkeep, SWA_KV_HEADS, SWA_DH))
        outs[3].append(swa[:, T - nkeep:, SWA_Q + SWA_KV:].reshape(B, nkeep, SWA_KV_HEADS, SWA_DH))

        ys = _ffn(ys, m_s[l], 0, g1, wgu1, wd1, l, Bs)
        buf = jnp.transpose(state_dn_conv[l], (1, 0, 2))
        qkv, z, bg, swa, cnew = _inproj_sample(ys, m_s[l], gm, win, cw, prm, cos_s, sin_s, gq, gk,
                                               buf, l)
        dn_o, s_new = _delta_sample(qkv[0], bg[0], z[0], nw, state_dn_S, l, bb,
                                    tuple(outs[5]) if l == L - 1 else ())
        w0 = cache_swa_k.shape[2]
        sw_o, k_new, v_new = _swa_sample(swa[0], cache_swa_k[l].reshape(Bs, w0, SWA_KV),
                                         cache_swa_v[l].reshape(Bs, w0, SWA_KV), sinks, bb)
        ys = _outproj_ffn(ys, dn_o[None], sw_o[None], m_s[l], g2, woa, wob, wgu2, wd2, l, Bs)
        outs[4].append(jnp.transpose(cnew, (1, 0, 2)))
        outs[5].append(s_new)
        outs[6].append(k_new.reshape(Bs, w0, SWA_KV_HEADS, SWA_DH))
        outs[7].append(v_new.reshape(Bs, w0, SWA_KV_HEADS, SWA_DH))

    s_sample = outs[5][-1] if L > 1 else outs[5][-1][None]
    stacked = [s_sample if i == 5 else jnp.stack(o) for i, o in enumerate(outs)]
    return (yp, ys.reshape(Bs, 1, D_MODEL)) + tuple(stacked)
```

```python
import functools
import math

import jax
import jax.numpy as jnp
from jax import lax
from jax.experimental import pallas as pl
from jax.experimental.pallas import tpu as pltpu

F32 = jnp.float32
BF16 = jnp.bfloat16

D_MODEL = 1024
D_FF = 2816
N_MOD = 9
EPS = 1e-6
DN_HEADS = 4
DN_DK = 128
DN_DV = 128
DN_QK = DN_HEADS * DN_DK
DN_V = DN_HEADS * DN_DV
DN_CONV = 4
DN_CONV_CH = 2 * DN_QK + DN_V
DN_CHUNK = 64
INV_BLOCK = 16
INTRA_ROWS = 256
SWA_DH = 64
SWA_HEADS = 8
SWA_KV_HEADS = 2
SWA_GROUP = SWA_HEADS // SWA_KV_HEADS
SWA_Q = SWA_HEADS * SWA_DH
SWA_KV = SWA_KV_HEADS * SWA_DH
WINDOW = 128
ROPE_THETA = 10000.0
PAST_LEN = 16384

LANES = 128
SUBLANES = 8
HALO = SUBLANES

C_CONV = 0
C_Z = C_CONV + DN_CONV_CH
C_SQ = C_Z + DN_V
C_SK = C_SQ + SWA_Q
C_SV = C_SK + SWA_KV
C_BG = C_SV + SWA_KV
IN_PACKED = C_BG + LANES
SWA_OUT = SWA_Q + 2 * SWA_KV

VMEM_LIMIT = 56 * 1024 * 1024

ROW_TILE = 512
INTRA_STEP_ROWS = 1024
SWA_QBLOCKS = 8
SAMPLE_SEQS = SUBLANES


def _cparams(sem):
    return pltpu.CompilerParams(dimension_semantics=sem, vmem_limit_bytes=VMEM_LIMIT)


def _const_spec(shape):
    nd = len(shape)
    return pl.BlockSpec(shape, lambda *_: (0,) * nd, pipeline_mode=pl.Buffered(1))


def _layer_spec(shape, l):
    nd = len(shape)
    return pl.BlockSpec((None,) + tuple(shape), lambda *_: (l,) + (0,) * nd,
                        pipeline_mode=pl.Buffered(1))


def _sigmoid(x):
    return 1.0 / (1.0 + jnp.exp(-x))


def _silu(x):
    return x * _sigmoid(x)


def _adaln_kernel(cp_ref, cs_ref, w_ref, b_ref, mp_ref, ms_ref):
    w = w_ref[...].astype(BF16)
    b = b_ref[...]
    for c_ref, m_ref in ((cp_ref, mp_ref), (cs_ref, ms_ref)):
        c = c_ref[...]
        m_ref[...] = jnp.dot(_silu(c).astype(BF16), w, preferred_element_type=F32) + b


def _adaln(c_p, c_s, w_ada, b_ada):
    L = w_ada.shape[0]
    bp, bs = c_p.shape[0], c_s.shape[0]
    b4 = b_ada.reshape(L, N_MOD, 1, D_MODEL)
    return pl.pallas_call(
        _adaln_kernel,
        grid=(L, N_MOD),
        in_specs=[
            pl.BlockSpec((bp, D_MODEL), lambda l, j: (0, 0)),
            pl.BlockSpec((bs, D_MODEL), lambda l, j: (0, 0)),
            pl.BlockSpec((None, D_MODEL, D_MODEL), lambda l, j: (l, 0, j)),
            pl.BlockSpec((None, None, 1, D_MODEL), lambda l, j: (l, j, 0, 0)),
        ],
        out_specs=[
            pl.BlockSpec((None, None, bp, D_MODEL), lambda l, j: (l, j, 0, 0)),
            pl.BlockSpec((None, None, bs, D_MODEL), lambda l, j: (l, j, 0, 0)),
        ],
        out_shape=[
            jax.ShapeDtypeStruct((L, N_MOD, bp, D_MODEL), F32),
            jax.ShapeDtypeStruct((L, N_MOD, bs, D_MODEL), F32),
        ],
        compiler_params=_cparams(("parallel", "parallel")),
        name="adaln",
    )(c_p, c_s, w_ada, b4)


def _mod_norm(x, g, shift, scale):
    ms = jnp.mean(x * x, axis=-1, keepdims=True)
    return (x * lax.rsqrt(ms + EPS)) * (g * (1.0 + scale)) + shift


def _ffn_body(x, shift, scale, gate, g, wgu_ref, wd_ref):
    h = _mod_norm(x, g, shift, scale).astype(BF16)
    gu = jnp.dot(h, wgu_ref[...], preferred_element_type=F32)
    a = gu[:, :D_FF]
    b = gu[:, D_FF:]
    act = (_silu(a) * b).astype(BF16)
    y = jnp.dot(act, wd_ref[...], preferred_element_type=F32)
    return x + (0.5 * gate) * y


def _mod_specs(m, sub, tm):
    rm = m.shape[2]
    blk = 1 if rm == 1 else tm

    def spec(i):
        if rm == 1:
            return pl.BlockSpec((None, None, 1, D_MODEL), lambda g, r: (i, g, 0, 0))
        return pl.BlockSpec((None, None, blk, D_MODEL), lambda g, r: (i, g, r, 0))

    return [spec(3 * sub), spec(3 * sub + 1), spec(3 * sub + 2)]


def _row_spec(tm, width):
    return pl.BlockSpec((None, tm, width), lambda g, r: (g, r, 0))


def _ffn_kernel(x_ref, sh_ref, sc_ref, ga_ref, g_ref, wgu_ref, wd_ref, o_ref):
    o_ref[...] = _ffn_body(x_ref[...], sh_ref[...], sc_ref[...], ga_ref[...], g_ref[...],
                           wgu_ref, wd_ref)


def _ffn(x, m, sub, g, wgu, wd, l, tm):
    G, R, _ = x.shape
    return pl.pallas_call(
        _ffn_kernel,
        grid=(G, R // tm),
        in_specs=[_row_spec(tm, D_MODEL)] + _mod_specs(m, sub, tm) + [
            _const_spec((1, D_MODEL)),
            _layer_spec((D_MODEL, 2 * D_FF), l),
            _layer_spec((D_FF, D_MODEL), l),
        ],
        out_specs=_row_spec(tm, D_MODEL),
        out_shape=jax.ShapeDtypeStruct(x.shape, F32),
        compiler_params=_cparams(("parallel", "parallel")),
        name="ffn",
    )(x, m, m, m, g, wgu, wd)


def _group_mean_sq(x, gmat_ref):
    return jnp.dot((x * x).astype(BF16), gmat_ref[...], preferred_element_type=F32)


def _rope(x, cos, sin_signed):
    width = x.shape[-1]
    reps = width // LANES
    lane = lax.broadcasted_iota(jnp.int32, x.shape, 1)
    first_half = (lane & (SWA_DH - 1)) < (SWA_DH // 2)
    partner = jnp.where(first_half,
                        pltpu.roll(x, width - SWA_DH // 2, 1),
                        pltpu.roll(x, SWA_DH // 2, 1))
    if reps > 1:
        cos = jnp.concatenate([cos] * reps, axis=1)
        sin_signed = jnp.concatenate([sin_signed] * reps, axis=1)
    return x * cos + partner * sin_signed


def _swa_post(p, prm_ref, cos, sin, gq_ref, gk_ref, swa_ref):
    pq = p[:, C_SQ:C_SQ + SWA_Q]
    pk = p[:, C_SK:C_SK + SWA_KV]
    wq = jnp.concatenate([prm_ref[2:3, :]] * (SWA_Q // LANES), axis=1)
    wk = prm_ref[3:4, :]
    qn = pq * lax.rsqrt(_group_mean_sq(pq, gq_ref) + EPS) * wq
    kn = pk * lax.rsqrt(_group_mean_sq(pk, gk_ref) + EPS) * wk
    swa_ref[:, :SWA_Q] = _rope(qn, cos, sin)
    swa_ref[:, SWA_Q:SWA_Q + SWA_KV] = _rope(kn, cos, sin)
    swa_ref[:, SWA_Q + SWA_KV:] = p[:, C_SV:C_SV + SWA_KV]


def _dn_group(a, gi, qkv_ref):
    for hh in range(DN_HEADS):
        cols = slice(gi * DN_QK + hh * DN_DK, gi * DN_QK + (hh + 1) * DN_DK)
        xh = a[:, hh * DN_DK:(hh + 1) * DN_DK]
        if gi < 2:
            s = lax.rsqrt(jnp.sum(xh * xh, axis=-1, keepdims=True) + EPS)
            if gi == 0:
                s = s * (DN_DK ** -0.5)
            xh = xh * s
        qkv_ref[:, cols] = xh.astype(qkv_ref.dtype)


def _bg_post(pb, prm_ref, bg_ref):
    neg_a = prm_ref[0:1, :]
    dt_b = prm_ref[1:2, :]
    xa = pb + dt_b
    softplus = jnp.maximum(xa, 0.0) + jnp.log(1.0 + jnp.exp(-jnp.abs(xa)))
    lane = lax.broadcasted_iota(jnp.int32, pb.shape, 1)
    bg_ref[...] = jnp.where(lane < DN_HEADS, _sigmoid(pb), neg_a * softplus)


def _dn_post(p, qkv_conv, prm_ref, qkv_ref, z_ref, bg_ref):
    qkv = _silu(qkv_conv)
    for gi in range(3):
        _dn_group(qkv[:, gi * DN_QK:(gi + 1) * DN_QK], gi, qkv_ref)
    z_ref[...] = p[:, C_Z:C_Z + DN_V].astype(z_ref.dtype)
    _bg_post(p[:, C_BG:C_BG + LANES], prm_ref, bg_ref)


class _Cols:
    def __init__(self, segments):
        self.segments = segments

    def __getitem__(self, idx):
        cols = idx[1]
        for start, arr in self.segments:
            if start <= cols.start and cols.stop <= start + arr.shape[1]:
                return arr[:, cols.start - start:cols.stop - start]
        raise KeyError(cols)


def _inproj_prompt_kernel(x_ref, sh_ref, sc_ref, g_ref, win_ref, cw_ref, prm_ref, cos_ref, sin_ref,
                          gq_ref, gk_ref, qkv_ref, z_ref, bg_ref, swa_ref, tail_ref, ext_ref):
    tm = x_ref.shape[0]
    first = pl.program_id(1) == 0

    @pl.when(first)
    def _():
        ext_ref[0:HALO, :] = jnp.zeros((HALO, DN_CONV_CH), F32)

    @pl.when(jnp.logical_not(first))
    def _():
        ext_ref[0:HALO, :] = ext_ref[tm:tm + HALO, :]

    h = _mod_norm(x_ref[...], g_ref[...], sh_ref[...], sc_ref[...]).astype(BF16)
    proj = lambda a, b: jnp.dot(h, win_ref[:, a:b], preferred_element_type=F32)
    p_att = _Cols([(C_SQ, proj(C_SQ, C_SK)), (C_SK, proj(C_SK, IN_PACKED))])
    _swa_post(p_att, prm_ref, cos_ref[...], sin_ref[...], gq_ref, gk_ref, swa_ref)
    _bg_post(p_att[:, C_BG:C_BG + LANES], prm_ref, bg_ref)
    for gi in range(3):
        cs = slice(gi * DN_QK, (gi + 1) * DN_QK)
        pc = proj(cs.start, cs.stop)
        ext_ref[HALO:, cs] = pc
        y = pc * cw_ref[DN_CONV - 1:DN_CONV, cs]
        for j in range(DN_CONV - 1):
            off = HALO - (DN_CONV - 1) + j
            y = y + ext_ref[off:off + tm, cs] * cw_ref[j:j + 1, cs]
        _dn_group(_silu(y), gi, qkv_ref)
    tail_ref[...] = ext_ref[tm:tm + HALO, :]
    z_ref[...] = proj(C_Z, C_Z + DN_V).astype(z_ref.dtype)


def _inproj_sample_kernel(x_ref, sh_ref, sc_ref, g_ref, win_ref, cw_ref, prm_ref, cos_ref, sin_ref,
                          gq_ref, gk_ref, buf_ref, qkv_ref, z_ref, bg_ref, swa_ref, cnew_ref):
    h = _mod_norm(x_ref[...], g_ref[...], sh_ref[...], sc_ref[...]).astype(BF16)
    p = jnp.dot(h, win_ref[...], preferred_element_type=F32)
    pc = p[:, :DN_CONV_CH]
    y = pc * cw_ref[DN_CONV - 1:DN_CONV, :]
    for j in range(DN_CONV - 1):
        y = y + buf_ref[j] * cw_ref[j:j + 1, :]
    for j in range(DN_CONV - 2):
        cnew_ref[j] = buf_ref[j + 1]
    cnew_ref[DN_CONV - 2] = pc
    _swa_post(p, prm_ref, cos_ref[...], sin_ref[...], gq_ref, gk_ref, swa_ref)
    _dn_post(p, y, prm_ref, qkv_ref, z_ref, bg_ref)


def _inproj_common_specs(m, tm, l):
    sh, sc, _ = _mod_specs(m, 1, tm)
    return [_row_spec(tm, D_MODEL), sh, sc,
            _const_spec((1, D_MODEL)),
            _layer_spec((D_MODEL, IN_PACKED), l),
            _const_spec((DN_CONV, DN_CONV_CH)),
            _const_spec((SUBLANES, LANES))]


def _inproj_prompt(x, m, g, win, cw, prm, cos, sin, gq, gk, l, tm):
    B, T, _ = x.shape
    table = pl.BlockSpec((tm, LANES), lambda b, r: (r, 0))
    return pl.pallas_call(
        _inproj_prompt_kernel,
        grid=(B, T // tm),
        in_specs=_inproj_common_specs(m, tm, l) + [
            table, table,
            _const_spec((SWA_Q, SWA_Q)),
            _const_spec((SWA_KV, SWA_KV)),
        ],
        out_specs=[
            _row_spec(tm, DN_CONV_CH), _row_spec(tm, DN_V), _row_spec(tm, LANES),
            _row_spec(tm, SWA_OUT),
            pl.BlockSpec((None, HALO, DN_CONV_CH), lambda b, r: (b, 0, 0)),
        ],
        out_shape=[
            jax.ShapeDtypeStruct((B, T, DN_CONV_CH), BF16),
            jax.ShapeDtypeStruct((B, T, DN_V), BF16),
            jax.ShapeDtypeStruct((B, T, LANES), F32),
            jax.ShapeDtypeStruct((B, T, SWA_OUT), F32),
            jax.ShapeDtypeStruct((B, HALO, DN_CONV_CH), F32),
        ],
        scratch_shapes=[pltpu.VMEM((HALO + tm, DN_CONV_CH), F32)],
        compiler_params=_cparams(("parallel", "arbitrary")),
        name="inproj_prompt",
    )(x, m, m, g, win, cw, prm, cos, sin, gq, gk)


def _inproj_sample(x, m, g, win, cw, prm, cos, sin, gq, gk, buf, l):
    _, R, _ = x.shape
    nbuf = DN_CONV - 1
    return pl.pallas_call(
        _inproj_sample_kernel,
        grid=(1, 1),
        in_specs=_inproj_common_specs(m, R, l) + [
            _const_spec((1, LANES)), _const_spec((1, LANES)),
            _const_spec((SWA_Q, SWA_Q)), _const_spec((SWA_KV, SWA_KV)),
            _const_spec((nbuf, R, DN_CONV_CH)),
        ],
        out_specs=[
            _row_spec(R, DN_CONV_CH), _row_spec(R, DN_V), _row_spec(R, LANES), _row_spec(R, SWA_OUT),
            pl.BlockSpec((nbuf, R, DN_CONV_CH), lambda g_, r: (0, 0, 0)),
        ],
        out_shape=[
            jax.ShapeDtypeStruct((1, R, DN_CONV_CH), F32),
            jax.ShapeDtypeStruct((1, R, DN_V), F32),
            jax.ShapeDtypeStruct((1, R, LANES), F32),
            jax.ShapeDtypeStruct((1, R, SWA_OUT), F32),
            jax.ShapeDtypeStruct((nbuf, R, DN_CONV_CH), F32),
        ],
        compiler_params=_cparams(("arbitrary", "arbitrary")),
        name="inproj_sample",
    )(x, m, m, g, win, cw, prm, cos, sin, gq, gk, buf)


def _gated_norm(o, z, nw):
    on = o * lax.rsqrt(jnp.mean(o * o, axis=-1, keepdims=True) + EPS) * nw
    return on * _silu(z.astype(F32))


def _mm(a, b):
    return jnp.dot(a.astype(BF16), b.astype(BF16), preferred_element_type=F32)


def _wide_to_blockdiag(wide):
    c, n = wide.shape[0], wide.shape[1] // wide.shape[0]
    cblk = lax.broadcasted_iota(jnp.int32, wide.shape, 1) // c
    return jnp.concatenate([jnp.where(cblk == i, wide, 0.0) for i in range(n)], axis=0)


def _blockdiag_to_wide(full, c):
    n = full.shape[0] // c
    cblk = lax.broadcasted_iota(jnp.int32, (c, full.shape[1]), 1) // c
    wide = full[0:c]
    for i in range(1, n):
        wide = jnp.where(cblk == i, full[i * c:(i + 1) * c], wide)
    return wide


def _unit_lower_inverses(mats):
    c = mats[0].shape[0]
    row = lax.broadcasted_iota(jnp.int32, mats[0].shape, 0)
    col = lax.broadcasted_iota(jnp.int32, mats[0].shape, 1) % c
    s = INV_BLOCK
    diag = (row // s) == (col // s)
    pws = [jnp.where(diag, a, 0.0) for a in mats]
    eye = jnp.where(row == col, 1.0, 0.0)
    xs = [eye - pw for pw in pws]
    k = 2
    while k < s:
        pws = [_mm(pw, _wide_to_blockdiag(pw)) for pw in pws]
        xs = [x + _mm(x, _wide_to_blockdiag(pw)) for x, pw in zip(xs, pws)]
        k *= 2
    while s < c:
        pair = ((row // (2 * s)) == (col // (2 * s))) & ((row // s) != (col // s))
        ts = [_mm(x, _wide_to_blockdiag(jnp.where(pair, a, 0.0))) for x, a in zip(xs, mats)]
        xs = [x - _mm(t, _wide_to_blockdiag(x)) for x, t in zip(xs, ts)]
        s *= 2
    return xs


def _cumsum_rows(x):
    n = x.shape[0]
    row = lax.broadcasted_iota(jnp.int32, x.shape, 0)
    s = 1
    while s < n:
        x = x + jnp.where(row >= s, pltpu.roll(x, s, 0), 0.0)
        s *= 2
    return x


WQ_ROWS = 2 * DN_CHUNK
QKD_ROWS = DN_CHUNK + DN_DK


def _delta_intra_kernel(q_ref, k_ref, v_ref, bg_ref, u_ref, wq_ref, qkd_ref, eg_ref):
    c = DN_CHUNK
    r = INTRA_ROWS
    nc = r // c
    row = lax.broadcasted_iota(jnp.int32, (c, r), 0)
    col = lax.broadcasted_iota(jnp.int32, (c, r), 1) % c
    incl = row >= col
    strict = row > col

    jobs = [(blk, h) for blk in range(q_ref.shape[0] // r) for h in range(DN_HEADS)]
    gcs, egcs, ekds, gcts, bgs = [], [], [], [], []
    for blk in range(q_ref.shape[0] // r):
        bg = bg_ref[blk * r:(blk + 1) * r, :]
        gc = jnp.concatenate([_cumsum_rows(bg[ci * c:(ci + 1) * c, :]) for ci in range(nc)], axis=0)
        g_end = [gc[(ci + 1) * c - 1:(ci + 1) * c, :] for ci in range(nc)]
        glast = jnp.concatenate([jnp.broadcast_to(g, (c, LANES)) for g in g_end], axis=0)
        for ci in range(nc):
            rows = [jnp.broadcast_to(jnp.exp(g_end[ci][:, DN_HEADS + h:DN_HEADS + h + 1]), (1, LANES))
                    for h in range(DN_HEADS)]
            eg_ref[blk * nc + ci] = jnp.concatenate(
                rows + [jnp.zeros((SUBLANES - DN_HEADS, LANES), F32)], axis=0)
        bgs.append(bg)
        gcs.append(gc)
        gcts.append(gc.T)
        egcs.append(jnp.exp(gc))
        ekds.append(jnp.exp(glast - gc))

    prep = []
    for blk, h in jobs:
        rs = slice(blk * r, (blk + 1) * r)
        sl = slice(h * DN_DK, (h + 1) * DN_DK)
        q = q_ref[rs, sl]
        k = k_ref[rs, sl]
        kb = k * bgs[blk][:, h:h + 1]
        kq = lax.dot_general(jnp.concatenate([kb, q], axis=0).astype(BF16), k.astype(BF16),
                             (((1,), (1,)), ((), ())), preferred_element_type=F32)
        prep.append((q, k, kb, kq))
    a_mats, qks = [], []
    for (blk, h), (q, k, kb, kq) in zip(jobs, prep):
        gl = slice(DN_HEADS + h, DN_HEADS + h + 1)
        gcol = jnp.concatenate([jnp.broadcast_to(gcs[blk][ci * c:(ci + 1) * c, gl], (c, c))
                                for ci in range(nc)], axis=1)
        decay = jnp.exp(jnp.where(incl, gcol - gcts[blk][gl, :], -jnp.inf))
        a_mats.append(jnp.where(strict, _blockdiag_to_wide(kq[:r], c) * decay, 0.0))
        qks.append(_blockdiag_to_wide(kq[r:], c) * decay)
    tinvs = _unit_lower_inverses(a_mats)
    uws = []
    for (blk, h), (q, k, kb, kq), tinv in zip(jobs, prep, tinvs):
        rs = slice(blk * r, (blk + 1) * r)
        sl = slice(h * DN_DK, (h + 1) * DN_DK)
        gl = slice(DN_HEADS + h, DN_HEADS + h + 1)
        rhs = jnp.concatenate([v_ref[rs, sl] * bgs[blk][:, h:h + 1], kb * egcs[blk][:, gl]], axis=1)
        uws.append(_mm(_wide_to_blockdiag(tinv), rhs))
    for (blk, h), (q, k, kb, kq), uw, qk in zip(jobs, prep, uws, qks):
        rs = slice(blk * r, (blk + 1) * r)
        sl = slice(h * DN_DK, (h + 1) * DN_DK)
        gl = slice(DN_HEADS + h, DN_HEADS + h + 1)
        u_ref[rs, sl] = uw[:, :DN_DV]
        w_b = uw[:, DN_DV:].astype(BF16)
        qd_b = (q * egcs[blk][:, gl]).astype(BF16)
        for ci in range(nc):
            wq_ref[blk * nc + ci, h * WQ_ROWS:h * WQ_ROWS + c, :] = w_b[ci * c:(ci + 1) * c, :]
            wq_ref[blk * nc + ci, h * WQ_ROWS + c:(h + 1) * WQ_ROWS, :] = qd_b[ci * c:(ci + 1) * c, :]
        kdt = (k * ekds[blk][:, gl]).T
        for pj in range(nc // 2):
            ls = slice(pj * LANES, (pj + 1) * LANES)
            pi = blk * (nc // 2) + pj
            qkd_ref[pi, h * QKD_ROWS:h * QKD_ROWS + c, :] = qk[:, ls].astype(BF16)
            qkd_ref[pi, h * QKD_ROWS + c:(h + 1) * QKD_ROWS, :] = kdt[:, ls].astype(BF16)


def _delta_scan_kernel(u_ref, wq_ref, qkd_ref, eg_ref, o_ref, sout_ref, s_ref):
    n = pl.program_id(0)
    c = DN_CHUNK

    @pl.when(n == 0)
    def _():
        s_ref[...] = jnp.zeros_like(s_ref)

    zeros = jnp.zeros((c, DN_DV), BF16)
    chains = [(b, h) for b in range(u_ref.shape[0]) for h in range(DN_HEADS)]
    for j in range(2):
        rs = slice(j * c, (j + 1) * c)
        ws = [jnp.dot(wq_ref[b, j, h * WQ_ROWS:(h + 1) * WQ_ROWS, :], s_ref[b, h].astype(BF16),
                      preferred_element_type=F32) for b, h in chains]
        rr = []
        for (b, h), w_h in zip(chains, ws):
            v_new = (u_ref[b, rs, h * DN_DV:(h + 1) * DN_DV] - w_h[:c]).astype(BF16)
            rhs = jnp.concatenate([v_new, zeros] if j == 0 else [zeros, v_new], axis=0)
            rr.append(jnp.dot(qkd_ref[b, h * QKD_ROWS:(h + 1) * QKD_ROWS, :], rhs,
                              preferred_element_type=F32))
        for (b, h), w_h, r_h in zip(chains, ws, rr):
            sl = slice(h * DN_DV, (h + 1) * DN_DV)
            s_ref[b, h] = s_ref[b, h] * eg_ref[b, j, h:h + 1, :] + r_h[c:]
            o_ref[b, rs, sl] = (w_h[c:] + r_h[:c]).astype(o_ref.dtype)

    @pl.when(n == pl.num_programs(0) - 1)
    def _():
        sout_ref[...] = s_ref[...]


def _delta_prompt(qkv, bg, ra):
    B, T, _ = qkv.shape
    c = DN_CHUNK
    n = T // c
    cpb = ra // c
    qkv_spec = lambda j: pl.BlockSpec((None, ra, DN_QK), lambda b, r: (b, r, j))
    u, wq, qkd, eg = pl.pallas_call(
        _delta_intra_kernel,
        grid=(B, T // ra),
        in_specs=[qkv_spec(0), qkv_spec(1), qkv_spec(2),
                  pl.BlockSpec((None, ra, LANES), lambda b, r: (b, r, 0))],
        out_specs=[pl.BlockSpec((None, ra, DN_V), lambda b, r: (b, r, 0)),
                   pl.BlockSpec((None, cpb, DN_HEADS * WQ_ROWS, DN_DK), lambda b, r: (b, r, 0, 0)),
                   pl.BlockSpec((None, cpb // 2, DN_HEADS * QKD_ROWS, LANES),
                                lambda b, r: (b, r, 0, 0)),
                   pl.BlockSpec((None, cpb, SUBLANES, LANES), lambda b, r: (b, r, 0, 0))],
        out_shape=[jax.ShapeDtypeStruct((B, T, DN_V), F32),
                   jax.ShapeDtypeStruct((B, n, DN_HEADS * WQ_ROWS, DN_DK), BF16),
                   jax.ShapeDtypeStruct((B, n // 2, DN_HEADS * QKD_ROWS, LANES), BF16),
                   jax.ShapeDtypeStruct((B, n, SUBLANES, LANES), F32)],
        compiler_params=_cparams(("parallel", "parallel")),
        name="delta_intra",
    )(qkv, qkv, qkv, bg)
    return pl.pallas_call(
        _delta_scan_kernel,
        grid=(n // 2,),
        in_specs=[pl.BlockSpec((B, 2 * c, DN_V), lambda i: (0, i, 0)),
                  pl.BlockSpec((B, 2, DN_HEADS * WQ_ROWS, DN_DK), lambda i: (0, i, 0, 0)),
                  pl.BlockSpec((B, None, DN_HEADS * QKD_ROWS, LANES), lambda i: (0, i, 0, 0)),
                  pl.BlockSpec((B, 2, SUBLANES, LANES), lambda i: (0, i, 0, 0))],
        out_specs=[pl.BlockSpec((B, 2 * c, DN_V), lambda i: (0, i, 0)),
                   pl.BlockSpec((B, DN_HEADS, DN_DK, DN_DV), lambda i: (0, 0, 0, 0))],
        out_shape=[jax.ShapeDtypeStruct((B, T, DN_V), BF16),
                   jax.ShapeDtypeStruct((B, DN_HEADS, DN_DK, DN_DV), F32)],
        scratch_shapes=[pltpu.VMEM((B, DN_HEADS, DN_DK, DN_DV), F32)],
        compiler_params=_cparams(("arbitrary",)),
        name="delta_scan",
    )(u, wq, qkd, eg)


def _delta_sample_kernel(n_prev, qkv_ref, bg_ref, z_ref, nw_ref, s0_ref, *refs):
    prev_refs = refs[:n_prev]
    o_ref, s_ref = refs[n_prev:]
    for k, prev_ref in enumerate(prev_refs):
        s_ref[k] = prev_ref[...]
    s_out = s_ref.at[n_prev] if n_prev else s_ref
    bb = qkv_ref.shape[0]
    nw = nw_ref[...]
    zeros = jnp.zeros((SUBLANES - 2, DN_DK), F32)
    row = lax.broadcasted_iota(jnp.int32, (bb, DN_DV), 0)
    heads = []
    for h in range(DN_HEADS):
        q = qkv_ref[:, h * DN_DK:(h + 1) * DN_DK]
        k = qkv_ref[:, DN_QK + h * DN_DK:DN_QK + (h + 1) * DN_DK]
        v = qkv_ref[:, 2 * DN_QK + h * DN_DV:2 * DN_QK + (h + 1) * DN_DV]
        beta = bg_ref[:, h:h + 1]
        eg = jnp.exp(bg_ref[:, DN_HEADS + h:DN_HEADS + h + 1])
        w = k * beta * eg
        qd = q * eg
        ws = [jnp.dot(jnp.concatenate([w[i:i + 1], qd[i:i + 1], zeros], axis=0), s0_ref[i, h],
                      preferred_element_type=F32) for i in range(bb)]
        heads.append((q, k, v, beta, eg, ws))
    updates = []
    for q, k, v, beta, eg, ws in heads:
        w_s = jnp.concatenate([ws[i][0:1] for i in range(bb)], axis=0)
        q_s = jnp.concatenate([ws[i][1:2] for i in range(bb)], axis=0)
        v_new = v * beta - w_s
        kt = k.T
        kv = [jnp.dot(kt, jnp.where(row == i, v_new, 0.0), preferred_element_type=F32)
              for i in range(bb)]
        updates.append((v_new, q_s, kv))
    for h, ((q, k, v, beta, eg, ws), (v_new, q_s, kv)) in enumerate(zip(heads, updates)):
        sl = slice(h * DN_DK, (h + 1) * DN_DK)
        for i in range(bb):
            s_out[i, h] = s0_ref[i, h] * eg[i:i + 1] + kv[i]
        o = q_s + jnp.sum(q * k, axis=-1, keepdims=True) * v_new
        o_ref[:, sl] = _gated_norm(o, z_ref[:, sl], nw)


def _delta_sample(qkv, bg, z, nw, s0, l, bb, prev=()):
    R = qkv.shape[0]
    state = (bb, DN_HEADS, DN_DK, DN_DV)
    state_spec = pl.BlockSpec(state, lambda i: (i, 0, 0, 0))
    if prev:
        out_spec = pl.BlockSpec((len(prev) + 1,) + state, lambda i: (0, i, 0, 0, 0))
        out_shape = (len(prev) + 1,) + s0.shape[1:]
    else:
        out_spec, out_shape = state_spec, s0.shape[1:]
    return pl.pallas_call(
        functools.partial(_delta_sample_kernel, len(prev)),
        grid=(R // bb,),
        in_specs=[pl.BlockSpec((bb, DN_CONV_CH), lambda i: (i, 0)),
                  pl.BlockSpec((bb, LANES), lambda i: (i, 0)),
                  pl.BlockSpec((bb, DN_V), lambda i: (i, 0)),
                  _const_spec((1, DN_DV)),
                  pl.BlockSpec((None,) + state, lambda i: (l, i, 0, 0, 0))]
        + [state_spec] * len(prev),
        out_specs=[pl.BlockSpec((bb, DN_V), lambda i: (i, 0)), out_spec],
        out_shape=[jax.ShapeDtypeStruct((R, DN_V), F32),
                   jax.ShapeDtypeStruct(out_shape, F32)],
        compiler_params=_cparams(("parallel",)),
        name="delta_sample",
    )(qkv, bg, z, nw, s0, *prev)


def _swa_prompt_kernel(q_ref, kc_ref, vc_ref, kp_ref, vp_ref, sink_ref, o_ref):
    n = pl.program_id(1)
    w = WINDOW
    nq = q_ref.shape[0] // w
    half = SWA_DH
    key = lax.broadcasted_iota(jnp.int32, (2 * w, w), 0)
    qry = lax.broadcasted_iota(jnp.int32, (2 * w, w), 1)
    band = (key >= qry) & (key <= qry + w)
    bias_mid = jnp.where(band, 0.0, -jnp.inf)
    bias_first = jnp.where(band & (key >= jnp.where(n == 0, w, 0)), 0.0, -jnp.inf)
    kall = jnp.concatenate([kp_ref[...], kc_ref[...]], axis=0)
    vall = jnp.concatenate([vp_ref[...], vc_ref[...]], axis=0)
    low = lax.broadcasted_iota(jnp.int32, kall.shape, 1) < half
    kall_r = pltpu.roll(kall, half, 1)
    k_lo = [jnp.where(low, kall, 0.0).astype(BF16), jnp.where(low, kall_r, 0.0).astype(BF16)]
    k_hi = [jnp.where(low, 0.0, kall_r).astype(BF16), jnp.where(low, 0.0, kall).astype(BF16)]
    v_t = vall.T.astype(BF16)
    zeros_v = jnp.zeros((half, 2 * w), BF16)
    sum_rows = 2 * SUBLANES
    ones8 = jnp.ones((sum_rows, 2 * w), BF16)
    zeros8 = jnp.zeros((sum_rows, 2 * w), BF16)
    first_pair = lax.broadcasted_iota(jnp.int32, (1, 2 * w), 1) < w
    log2e = math.log2(math.e)
    scale = SWA_DH ** -0.5 * log2e

    units = [(i, j) for i in range(nq) for j in range(SWA_KV_HEADS)]
    scores = []
    for i, j in units:
        rs = slice(i * w, (i + 2) * w)
        kk = jnp.concatenate([k_lo[j][rs], k_hi[j][rs]], axis=0)
        c0 = j * SWA_GROUP * SWA_DH
        q = jnp.concatenate([q_ref[i * w:(i + 1) * w, c0:c0 + LANES],
                             q_ref[i * w:(i + 1) * w, c0 + LANES:c0 + 2 * LANES]], axis=0)
        scores.append(lax.dot_general(kk, (q * scale).astype(BF16), (((1,), (1,)), ((), ())),
                                      preferred_element_type=F32))
    probs, sink_rows, maxima = [], [], []
    for (i, j), s in zip(units, scores):
        bias = bias_first if i == 0 else bias_mid
        bias2 = jnp.concatenate([bias, bias], axis=1)
        p_hf, sk_hf, mx_hf = [], [], []
        for hf in range(2):
            heads = [j * SWA_GROUP + 2 * pr + hf for pr in range(2)]
            sink = log2e * jnp.where(first_pair, sink_ref[0:1, heads[0]:heads[0] + 1],
                                     sink_ref[0:1, heads[1]:heads[1] + 1])
            masked = s[hf * 2 * w:(hf + 1) * 2 * w, :] + bias2
            mx = jnp.maximum(jnp.max(masked, axis=0, keepdims=True), sink)
            p_hf.append(jnp.exp2(masked - mx).astype(BF16))
            sk_hf.append(sink)
            mx_hf.append(mx)
        probs.append(jnp.concatenate(p_hf, axis=0))
        sink_rows.append(sk_hf)
        maxima.append(mx_hf)
    outs = []
    for (i, j), p in zip(units, probs):
        vt = v_t[j * half:(j + 1) * half, i * w:(i + 2) * w]
        lhs = jnp.concatenate([jnp.concatenate([vt, zeros_v], axis=1),
                               jnp.concatenate([zeros_v, vt], axis=1),
                               jnp.concatenate([ones8, zeros8], axis=1),
                               jnp.concatenate([zeros8, ones8], axis=1)], axis=0)
        outs.append(jnp.dot(lhs, p, preferred_element_type=F32))
    for (i, j), o_t, sk_hf, mx_hf in zip(units, outs, sink_rows, maxima):
        c0 = j * SWA_GROUP * SWA_DH
        parts = []
        for hf in range(2):
            den = (o_t[2 * half + hf * sum_rows:2 * half + hf * sum_rows + 1, :]
                   + jnp.exp2(sk_hf[hf] - mx_hf[hf]))
            parts.append(o_t[hf * half:(hf + 1) * half, :] / den)
        o = jnp.concatenate(parts, axis=0).T
        for pr in range(2):
            o_ref[i * w:(i + 1) * w, c0 + pr * LANES:c0 + (pr + 1) * LANES] = (
                o[pr * w:(pr + 1) * w, :].astype(o_ref.dtype))


def _swa_prompt(swa, sinks, nq):
    B, T, _ = swa.shape
    w = WINDOW
    rq = nq * w
    kcol = SWA_Q // SWA_KV
    prev = lambda n: jnp.maximum(n * nq - 1, 0)
    return pl.pallas_call(
        _swa_prompt_kernel,
        grid=(B, T // rq),
        in_specs=[pl.BlockSpec((None, rq, SWA_Q), lambda b, n: (b, n, 0)),
                  pl.BlockSpec((None, rq, SWA_KV), lambda b, n: (b, n, kcol)),
                  pl.BlockSpec((None, rq, SWA_KV), lambda b, n: (b, n, kcol + 1)),
                  pl.BlockSpec((None, w, SWA_KV), lambda b, n: (b, prev(n), kcol)),
                  pl.BlockSpec((None, w, SWA_KV), lambda b, n: (b, prev(n), kcol + 1)),
                  _const_spec((1, LANES))],
        out_specs=pl.BlockSpec((None, rq, SWA_Q), lambda b, n: (b, n, 0)),
        out_shape=jax.ShapeDtypeStruct((B, T, SWA_Q), BF16),
        compiler_params=_cparams(("parallel", "parallel")),
        name="swa_prompt",
    )(swa, swa, swa, swa, swa, sinks)


def _swa_sample_kernel(n_prev, swa_ref, kc_ref, vc_ref, sink_ref, *refs):
    o_ref, kn_ref, vn_ref = refs[2 * n_prev:]
    for k in range(n_prev):
        kn_ref[k] = refs[k][...]
        vn_ref[k] = refs[n_prev + k][...]
    kn_out = kn_ref.at[n_prev] if n_prev else kn_ref
    vn_out = vn_ref.at[n_prev] if n_prev else vn_ref
    bb = swa_ref.shape[0]
    w = kc_ref.shape[1]
    scale = SWA_DH ** -0.5
    for i in range(bb):
        kn_out[i, 0:w - 1, :] = kc_ref[i, 1:w, :]
        kn_out[i, w - 1:w, :] = swa_ref[i:i + 1, SWA_Q:SWA_Q + SWA_KV]
        vn_out[i, 0:w - 1, :] = vc_ref[i, 1:w, :]
        vn_out[i, w - 1:w, :] = swa_ref[i:i + 1, SWA_Q + SWA_KV:]
    units = [(i, j) for i in range(bb) for j in range(SWA_KV_HEADS)]
    sinks = [jnp.concatenate(
        [sink_ref[0:1, j * SWA_GROUP + g:j * SWA_GROUP + g + 1] for g in range(SWA_GROUP)]
        + [jnp.zeros((SUBLANES - SWA_GROUP, 1), F32)], axis=0) for j in range(SWA_KV_HEADS)]
    qs, scs = [], []
    for i, j in units:
        q4 = jnp.concatenate(
            [swa_ref[i:i + 1, (j * SWA_GROUP + g) * SWA_DH:(j * SWA_GROUP + g + 1) * SWA_DH]
             for g in range(SWA_GROUP)] + [jnp.zeros((SUBLANES - SWA_GROUP, SWA_DH), F32)], axis=0)
        qs.append(q4)
        scs.append(lax.dot_general(q4, kc_ref[i, :, j * SWA_DH:(j + 1) * SWA_DH],
                                   (((1,), (1,)), ((), ())), preferred_element_type=F32) * scale)
    soft = []
    for (i, j), q4, sc in zip(units, qs, scs):
        knew = swa_ref[i:i + 1, SWA_Q + j * SWA_DH:SWA_Q + (j + 1) * SWA_DH]
        sn = jnp.sum(q4 * knew, axis=-1, keepdims=True) * scale
        mx = jnp.maximum(jnp.maximum(jnp.max(sc, axis=-1, keepdims=True), sn), sinks[j])
        pc = jnp.exp(sc - mx)
        pn = jnp.exp(sn - mx)
        den = jnp.sum(pc, axis=-1, keepdims=True) + pn + jnp.exp(sinks[j] - mx)
        soft.append((pc, pn, den))
    pvs = [jnp.dot(pc, vc_ref[i, :, j * SWA_DH:(j + 1) * SWA_DH], preferred_element_type=F32)
           for (i, j), (pc, pn, den) in zip(units, soft)]
    for (i, j), (pc, pn, den), pv in zip(units, soft, pvs):
        vnew = swa_ref[i:i + 1, SWA_Q + SWA_KV + j * SWA_DH:SWA_Q + SWA_KV + (j + 1) * SWA_DH]
        o = (pv + pn * vnew) / den
        for g in range(SWA_GROUP):
            hq = j * SWA_GROUP + g
            o_ref[i:i + 1, hq * SWA_DH:(hq + 1) * SWA_DH] = o[g:g + 1, :]


def _swa_sample(swa, kc, vc, sinks, bb, prev_k=(), prev_v=()):
    R = swa.shape[0]
    w = kc.shape[1]
    n_prev = len(prev_k)
    cache_spec = pl.BlockSpec((bb, w, SWA_KV), lambda i: (i, 0, 0))
    if n_prev:
        out_spec = pl.BlockSpec((n_prev + 1, bb, w, SWA_KV), lambda i: (0, i, 0, 0))
        out_shape = (n_prev + 1,) + kc.shape
    else:
        out_spec, out_shape = cache_spec, kc.shape
    return pl.pallas_call(
        functools.partial(_swa_sample_kernel, n_prev),
        grid=(R // bb,),
        in_specs=[pl.BlockSpec((bb, SWA_OUT), lambda i: (i, 0)), cache_spec, cache_spec,
                  _const_spec((1, LANES))] + [cache_spec] * (2 * n_prev),
        out_specs=[pl.BlockSpec((bb, SWA_Q), lambda i: (i, 0)), out_spec, out_spec],
        out_shape=[jax.ShapeDtypeStruct((R, SWA_Q), F32),
                   jax.ShapeDtypeStruct(out_shape, F32),
                   jax.ShapeDtypeStruct(out_shape, F32)],
        compiler_params=_cparams(("parallel",)),
        name="swa_sample",
    )(swa, kc, vc, sinks, *prev_k, *prev_v)


def _outproj_ffn_kernel(gate_dn, x_ref, dn_ref, sw_ref, *refs):
    g5_ref, sh_ref, sc_ref, ga_ref, g_ref, woa_ref, wob_ref, wgu_ref, wd_ref, o_ref = refs[-10:]
    if gate_dn:
        z_ref, nw_ref = refs[:2]
        nw = nw_ref[...]
        dn = jnp.concatenate(
            [_gated_norm(dn_ref[:, h * DN_DV:(h + 1) * DN_DV].astype(F32),
                         z_ref[:, h * DN_DV:(h + 1) * DN_DV], nw).astype(BF16)
             for h in range(DN_HEADS)], axis=1)
    else:
        dn = dn_ref[...].astype(BF16)
    mix = (jnp.dot(dn, woa_ref[...], preferred_element_type=F32)
           + jnp.dot(sw_ref[...].astype(BF16), wob_ref[...], preferred_element_type=F32))
    x = x_ref[...] + g5_ref[...] * mix
    o_ref[...] = _ffn_body(x, sh_ref[...], sc_ref[...], ga_ref[...], g_ref[...], wgu_ref, wd_ref)


def _outproj_ffn(x, dn, sw, m, g, woa, wob, wgu, wd, l, tm, z=None, nw=None):
    G, R, _ = x.shape
    g5 = _mod_specs(m, 1, tm)[2]
    gate_dn = z is not None
    gate_specs = [_row_spec(tm, DN_V), _const_spec((1, DN_DV))] if gate_dn else []
    gate_args = (z, nw) if gate_dn else ()
    return pl.pallas_call(
        functools.partial(_outproj_ffn_kernel, gate_dn),
        grid=(G, R // tm),
        in_specs=[_row_spec(tm, D_MODEL), _row_spec(tm, DN_V), _row_spec(tm, SWA_Q)]
        + gate_specs + [g5] + _mod_specs(m, 2, tm) + [
            _const_spec((1, D_MODEL)),
            _layer_spec((DN_V, D_MODEL), l),
            _layer_spec((SWA_Q, D_MODEL), l),
            _layer_spec((D_MODEL, 2 * D_FF), l),
            _layer_spec((D_FF, D_MODEL), l),
        ],
        out_specs=_row_spec(tm, D_MODEL),
        out_shape=jax.ShapeDtypeStruct(x.shape, F32),
        compiler_params=_cparams(("parallel", "parallel")),
        name="outproj_ffn",
    )(x, dn, sw, *gate_args, m, m, m, m, g, woa, wob, wgu, wd)


def _rope_tables(pos):
    half = SWA_DH // 2
    inv = jnp.power(ROPE_THETA, -jnp.arange(half, dtype=F32) * 2.0 / SWA_DH)
    ang = pos.astype(F32)[:, None] * inv[None, :]
    cos, sin = jnp.cos(ang), jnp.sin(ang)
    reps = LANES // SWA_DH
    return (jnp.concatenate([cos, cos] * reps, axis=1),
            jnp.concatenate([-sin, sin] * reps, axis=1))


def _group_matrix(width):
    i = jnp.arange(width) // SWA_DH
    return jnp.where(i[:, None] == i[None, :], 1.0 / SWA_DH, 0.0).astype(BF16)


def _pack_w_in(w_in):
    conv = w_in[..., :DN_CONV_CH]
    o = DN_CONV_CH
    z = w_in[..., o:o + DN_V]
    o += DN_V
    ba = w_in[..., o:o + 2 * DN_HEADS]
    o += 2 * DN_HEADS
    sq = w_in[..., o:o + SWA_Q]
    o += SWA_Q
    sk = w_in[..., o:o + SWA_KV]
    o += SWA_KV
    sv = w_in[..., o:o + SWA_KV]
    pad = jnp.zeros(w_in.shape[:-1] + (LANES - 2 * DN_HEADS,), w_in.dtype)
    return jnp.concatenate([conv, z, sq, sk, sv, ba, pad], axis=-1).astype(BF16)


def _lane_row(vec, offset=0):
    return jnp.zeros((LANES,), F32).at[offset:offset + vec.shape[0]].set(vec.astype(F32))


def kernel(x_prompt, x_sample, c_prompt, c_sample, state_dn_conv, state_dn_S, cache_swa_k, cache_swa_v, w_ada, b_ada, g_ffn1, w_ffn1_gu, w_ffn1_down, g_mix, w_in, dn_conv_w, dn_A_log, dn_dt_bias, dn_norm_w, swa_q_norm, swa_k_norm, swa_sinks, w_out, g_ffn2, w_ffn2_gu, w_ffn2_down):
    L = w_ada.shape[0]
    B, T, _ = x_prompt.shape
    Bs = x_sample.shape[0]
    tm = min(ROW_TILE, T)
    bb = min(SAMPLE_SEQS, Bs)
    assert T % tm == 0 and T % min(INTRA_STEP_ROWS, T) == 0 and T % INTRA_ROWS == 0
    assert T % (min(SWA_QBLOCKS, T // WINDOW) * WINDOW) == 0 and Bs % bb == 0
    assert x_sample.shape[1] == 1 and cache_swa_k.shape[2] == WINDOW

    m_p, m_s = _adaln(c_prompt, c_sample, w_ada, b_ada)
    m_p = m_p.reshape(L, N_MOD, B, 1, D_MODEL)
    m_s = m_s.reshape(L, N_MOD, 1, Bs, D_MODEL)

    cos_p, sin_p = _rope_tables(jnp.arange(T))
    cos_s, sin_s = _rope_tables(PAST_LEN + jnp.arange(1))
    gq, gk = _group_matrix(SWA_Q), _group_matrix(SWA_KV)

    wgu1, wd1 = w_ffn1_gu.astype(BF16), w_ffn1_down.astype(BF16)
    wgu2, wd2 = w_ffn2_gu.astype(BF16), w_ffn2_down.astype(BF16)
    win = _pack_w_in(w_in)
    woa, wob = w_out[:, :DN_V].astype(BF16), w_out[:, DN_V:].astype(BF16)

    yp = x_prompt
    ys = x_sample.reshape(1, Bs, D_MODEL)
    outs = [[] for _ in range(8)]
    for l in range(L):
        g1, gm, g2 = (g_ffn1[l].reshape(1, D_MODEL), g_mix[l].reshape(1, D_MODEL),
                      g_ffn2[l].reshape(1, D_MODEL))
        cw = dn_conv_w[l]
        nw = dn_norm_w[l].reshape(1, DN_DV)
        prm = jnp.stack([
            _lane_row(-jnp.exp(dn_A_log[l].astype(F32)), DN_HEADS),
            _lane_row(dn_dt_bias[l], DN_HEADS),
            jnp.concatenate([swa_q_norm[l]] * (LANES // SWA_DH)).astype(F32),
            jnp.concatenate([swa_k_norm[l]] * (LANES // SWA_DH)).astype(F32),
        ] + [jnp.zeros((LANES,), F32)] * (SUBLANES - 4))
        sinks = _lane_row(swa_sinks[l]).reshape(1, LANES)

        yp = _ffn(yp, m_p[l], 0, g1, wgu1, wd1, l, tm)
        qkv, z, bg, swa, tail = _inproj_prompt(yp, m_p[l], gm, win, cw, prm, cos_p, sin_p, gq, gk,
                                               l, tm)
        dn_o, s_new = _delta_prompt(qkv, bg, min(INTRA_STEP_ROWS, T))
        sw_o = _swa_prompt(swa, sinks, min(SWA_QBLOCKS, T // WINDOW))
        yp = _outproj_ffn(yp, dn_o, sw_o, m_p[l], g2, woa, wob, wgu2, wd2, l, tm, z, nw)
        nkeep = min(WINDOW, T)
        outs[0].append(tail[:, HALO - (DN_CONV - 1):])
        outs[1].append(s_new)
        outs[2].append(swa[:, T - nkeep:, SWA_Q:SWA_Q + SWA_KV].reshape(B, nkeep, SWA_KV_HEADS, SWA_DH))
        outs[3].append(swa[:, T - nkeep:, SWA_Q + SWA_KV:].reshape(B, nkeep, SWA_KV_HEADS, SWA_DH))

        ys = _ffn(ys, m_s[l], 0, g1, wgu1, wd1, l, Bs)
        buf = jnp.transpose(state_dn_conv[l], (1, 0, 2))
        qkv, z, bg, swa, cnew = _inproj_sample(ys, m_s[l], gm, win, cw, prm, cos_s, sin_s, gq, gk,
                                               buf, l)
        last = l == L - 1
        dn_o, s_new = _delta_sample(qkv[0], bg[0], z[0], nw, state_dn_S, l, bb,
                                    tuple(outs[5]) if last else ())
        w0 = cache_swa_k.shape[2]
        sw_o, k_new, v_new = _swa_sample(swa[0], cache_swa_k[l].reshape(Bs, w0, SWA_KV),
                                         cache_swa_v[l].reshape(Bs, w0, SWA_KV), sinks, bb,
                                         tuple(outs[6]) if last else (),
                                         tuple(outs[7]) if last else ())
        ys = _outproj_ffn(ys, dn_o[None], sw_o[None], m_s[l], g2, woa, wob, wgu2, wd2, l, Bs)
        outs[4].append(jnp.transpose(cnew, (1, 0, 2)))
        outs[5].append(s_new)
        outs[6].append(k_new)
        outs[7].append(v_new)

    in_kernel = [o[-1] if L > 1 else o[-1][None] for o in outs[5:]]
    caches = [c.reshape(L, Bs, w0, SWA_KV_HEADS, SWA_DH) for c in in_kernel[1:]]
    stacked = [jnp.stack(o) for o in outs[:5]] + [in_kernel[0]] + caches
    return (yp, ys.reshape(Bs, 1, D_MODEL)) + tuple(stacked)
```

```python
import functools
import math

import jax
import jax.numpy as jnp
from jax import lax
from jax.experimental import pallas as pl
from jax.experimental.pallas import tpu as pltpu

F32 = jnp.float32
BF16 = jnp.bfloat16

D_MODEL = 1024
D_FF = 2816
N_MOD = 9
EPS = 1e-6
DN_HEADS = 4
DN_DK = 128
DN_DV = 128
DN_QK = DN_HEADS * DN_DK
DN_V = DN_HEADS * DN_DV
DN_CONV = 4
DN_CONV_CH = 2 * DN_QK + DN_V
DN_CHUNK = 64
INV_BLOCK = 16
INTRA_ROWS = 256
SWA_DH = 64
SWA_HEADS = 8
SWA_KV_HEADS = 2
SWA_GROUP = SWA_HEADS // SWA_KV_HEADS
SWA_Q = SWA_HEADS * SWA_DH
SWA_KV = SWA_KV_HEADS * SWA_DH
WINDOW = 128
ROPE_THETA = 10000.0
PAST_LEN = 16384

LANES = 128
SUBLANES = 8
HALO = SUBLANES

C_CONV = 0
C_Z = C_CONV + DN_CONV_CH
C_SQ = C_Z + DN_V
C_SK = C_SQ + SWA_Q
C_SV = C_SK + SWA_KV
C_BG = C_SV + SWA_KV
IN_PACKED = C_BG + LANES
SWA_OUT = SWA_Q + 2 * SWA_KV

VMEM_LIMIT = 56 * 1024 * 1024

ROW_TILE = 512
INTRA_STEP_ROWS = 1024
SWA_QBLOCKS = 8
SAMPLE_SEQS = SUBLANES


def _cparams(sem):
    return pltpu.CompilerParams(dimension_semantics=sem, vmem_limit_bytes=VMEM_LIMIT)


def _const_spec(shape):
    nd = len(shape)
    return pl.BlockSpec(shape, lambda *_: (0,) * nd, pipeline_mode=pl.Buffered(1))


def _layer_spec(shape, l):
    nd = len(shape)
    return pl.BlockSpec((None,) + tuple(shape), lambda *_: (l,) + (0,) * nd,
                        pipeline_mode=pl.Buffered(1))


def _sigmoid(x):
    return 1.0 / (1.0 + jnp.exp(-x))


def _silu(x):
    return x * _sigmoid(x)


def _adaln_kernel(cp_ref, cs_ref, w_ref, b_ref, mp_ref, ms_ref):
    w = w_ref[...].astype(BF16)
    b = b_ref[...]
    for c_ref, m_ref in ((cp_ref, mp_ref), (cs_ref, ms_ref)):
        c = c_ref[...]
        m_ref[...] = jnp.dot(_silu(c).astype(BF16), w, preferred_element_type=F32) + b


def _adaln(c_p, c_s, w_ada, b_ada):
    L = w_ada.shape[0]
    bp, bs = c_p.shape[0], c_s.shape[0]
    b4 = b_ada.reshape(L, N_MOD, 1, D_MODEL)
    return pl.pallas_call(
        _adaln_kernel,
        grid=(L, N_MOD),
        in_specs=[
            pl.BlockSpec((bp, D_MODEL), lambda l, j: (0, 0)),
            pl.BlockSpec((bs, D_MODEL), lambda l, j: (0, 0)),
            pl.BlockSpec((None, D_MODEL, D_MODEL), lambda l, j: (l, 0, j)),
            pl.BlockSpec((None, None, 1, D_MODEL), lambda l, j: (l, j, 0, 0)),
        ],
        out_specs=[
            pl.BlockSpec((None, None, bp, D_MODEL), lambda l, j: (l, j, 0, 0)),
            pl.BlockSpec((None, None, bs, D_MODEL), lambda l, j: (l, j, 0, 0)),
        ],
        out_shape=[
            jax.ShapeDtypeStruct((L, N_MOD, bp, D_MODEL), F32),
            jax.ShapeDtypeStruct((L, N_MOD, bs, D_MODEL), F32),
        ],
        compiler_params=_cparams(("parallel", "parallel")),
        name="adaln",
    )(c_p, c_s, w_ada, b4)


def _mod_norm(x, g, shift, scale):
    ms = jnp.mean(x * x, axis=-1, keepdims=True)
    xn = (x * lax.rsqrt(ms + EPS)).astype(BF16)
    return xn * (g * (1.0 + scale)).astype(BF16) + shift.astype(BF16)


def _ffn_body(x, shift, scale, gate, g, wgu_ref, wd_ref):
    h = _mod_norm(x, g, shift, scale).astype(BF16)
    gu = jnp.dot(h, wgu_ref[...], preferred_element_type=F32)
    a = gu[:, :D_FF]
    b = gu[:, D_FF:]
    act = (_silu(a) * b).astype(BF16)
    y = jnp.dot(act, wd_ref[...], preferred_element_type=F32)
    return x + (0.5 * gate) * y


def _mod_specs(m, sub, tm):
    rm = m.shape[2]
    blk = 1 if rm == 1 else tm

    def spec(i):
        if rm == 1:
            return pl.BlockSpec((None, None, 1, D_MODEL), lambda g, r: (i, g, 0, 0))
        return pl.BlockSpec((None, None, blk, D_MODEL), lambda g, r: (i, g, r, 0))

    return [spec(3 * sub), spec(3 * sub + 1), spec(3 * sub + 2)]


def _row_spec(tm, width):
    return pl.BlockSpec((None, tm, width), lambda g, r: (g, r, 0))


def _ffn_kernel(x_ref, sh_ref, sc_ref, ga_ref, g_ref, wgu_ref, wd_ref, o_ref):
    o_ref[...] = _ffn_body(x_ref[...], sh_ref[...], sc_ref[...], ga_ref[...], g_ref[...],
                           wgu_ref, wd_ref)


def _ffn(x, m, sub, g, wgu, wd, l, tm):
    G, R, _ = x.shape
    return pl.pallas_call(
        _ffn_kernel,
        grid=(G, R // tm),
        in_specs=[_row_spec(tm, D_MODEL)] + _mod_specs(m, sub, tm) + [
            _const_spec((1, D_MODEL)),
            _layer_spec((D_MODEL, 2 * D_FF), l),
            _layer_spec((D_FF, D_MODEL), l),
        ],
        out_specs=_row_spec(tm, D_MODEL),
        out_shape=jax.ShapeDtypeStruct(x.shape, F32),
        compiler_params=_cparams(("parallel", "parallel")),
        name="ffn",
    )(x, m, m, m, g, wgu, wd)


def _group_mean_sq(x, gmat_ref):
    return jnp.dot((x * x).astype(BF16), gmat_ref[...], preferred_element_type=F32)


def _rope(x, cos, sin_signed):
    width = x.shape[-1]
    reps = width // LANES
    lane = lax.broadcasted_iota(jnp.int32, x.shape, 1)
    first_half = (lane & (SWA_DH - 1)) < (SWA_DH // 2)
    partner = jnp.where(first_half,
                        pltpu.roll(x, width - SWA_DH // 2, 1),
                        pltpu.roll(x, SWA_DH // 2, 1))
    if reps > 1:
        cos = jnp.concatenate([cos] * reps, axis=1)
        sin_signed = jnp.concatenate([sin_signed] * reps, axis=1)
    return x * cos + partner * sin_signed


def _swa_post(p, prm_ref, cos, sin, gq_ref, gk_ref, swa_ref):
    pq = p[:, C_SQ:C_SQ + SWA_Q]
    pk = p[:, C_SK:C_SK + SWA_KV]
    wq = jnp.concatenate([prm_ref[2:3, :]] * (SWA_Q // LANES), axis=1)
    wk = prm_ref[3:4, :]
    qn = pq * lax.rsqrt(_group_mean_sq(pq, gq_ref) + EPS) * wq
    kn = pk * lax.rsqrt(_group_mean_sq(pk, gk_ref) + EPS) * wk
    swa_ref[:, :SWA_Q] = _rope(qn, cos, sin)
    swa_ref[:, SWA_Q:SWA_Q + SWA_KV] = _rope(kn, cos, sin)
    swa_ref[:, SWA_Q + SWA_KV:] = p[:, C_SV:C_SV + SWA_KV]


def _dn_group(a, gi, qkv_ref):
    for hh in range(DN_HEADS):
        cols = slice(gi * DN_QK + hh * DN_DK, gi * DN_QK + (hh + 1) * DN_DK)
        xh = a[:, hh * DN_DK:(hh + 1) * DN_DK]
        if gi < 2:
            s = lax.rsqrt(jnp.sum(xh * xh, axis=-1, keepdims=True) + EPS)
            if gi == 0:
                s = s * (DN_DK ** -0.5)
            xh = xh * s
        qkv_ref[:, cols] = xh.astype(qkv_ref.dtype)


def _bg_post(pb, prm_ref, bg_ref):
    neg_a = prm_ref[0:1, :]
    dt_b = prm_ref[1:2, :]
    xa = pb + dt_b
    softplus = jnp.maximum(xa, 0.0) + jnp.log(1.0 + jnp.exp(-jnp.abs(xa)))
    lane = lax.broadcasted_iota(jnp.int32, pb.shape, 1)
    bg_ref[...] = jnp.where(lane < DN_HEADS, _sigmoid(pb), neg_a * softplus)


def _dn_post(p, qkv_conv, prm_ref, qkv_ref, z_ref, bg_ref):
    qkv = _silu(qkv_conv)
    for gi in range(3):
        _dn_group(qkv[:, gi * DN_QK:(gi + 1) * DN_QK], gi, qkv_ref)
    z_ref[...] = p[:, C_Z:C_Z + DN_V].astype(z_ref.dtype)
    _bg_post(p[:, C_BG:C_BG + LANES], prm_ref, bg_ref)


class _Cols:
    def __init__(self, segments):
        self.segments = segments

    def __getitem__(self, idx):
        cols = idx[1]
        for start, arr in self.segments:
            if start <= cols.start and cols.stop <= start + arr.shape[1]:
                return arr[:, cols.start - start:cols.stop - start]
        raise KeyError(cols)


def _inproj_prompt_kernel(x_ref, sh_ref, sc_ref, g_ref, win_ref, cw_ref, prm_ref, cos_ref, sin_ref,
                          gq_ref, gk_ref, qkv_ref, z_ref, bg_ref, swa_ref, tail_ref, ext_ref):
    tm = x_ref.shape[0]
    first = pl.program_id(1) == 0

    @pl.when(first)
    def _():
        ext_ref[0:HALO, :] = jnp.zeros((HALO, DN_CONV_CH), F32)

    @pl.when(jnp.logical_not(first))
    def _():
        ext_ref[0:HALO, :] = ext_ref[tm:tm + HALO, :]

    h = _mod_norm(x_ref[...], g_ref[...], sh_ref[...], sc_ref[...]).astype(BF16)
    proj = lambda a, b: jnp.dot(h, win_ref[:, a:b], preferred_element_type=F32)
    p_att = _Cols([(C_SQ, proj(C_SQ, C_SK)), (C_SK, proj(C_SK, IN_PACKED))])
    _swa_post(p_att, prm_ref, cos_ref[...], sin_ref[...], gq_ref, gk_ref, swa_ref)
    _bg_post(p_att[:, C_BG:C_BG + LANES], prm_ref, bg_ref)
    for gi in range(3):
        cs = slice(gi * DN_QK, (gi + 1) * DN_QK)
        pc = proj(cs.start, cs.stop)
        ext_ref[HALO:, cs] = pc
        y = pc * cw_ref[DN_CONV - 1:DN_CONV, cs]
        for j in range(DN_CONV - 1):
            off = HALO - (DN_CONV - 1) + j
            y = y + ext_ref[off:off + tm, cs] * cw_ref[j:j + 1, cs]
        _dn_group(_silu(y.astype(BF16)).astype(F32), gi, qkv_ref)
    tail_ref[...] = ext_ref[tm:tm + HALO, :]
    z_ref[...] = proj(C_Z, C_Z + DN_V).astype(z_ref.dtype)


def _inproj_sample_kernel(x_ref, sh_ref, sc_ref, g_ref, win_ref, cw_ref, prm_ref, cos_ref, sin_ref,
                          gq_ref, gk_ref, buf_ref, qkv_ref, z_ref, bg_ref, swa_ref, cnew_ref):
    h = _mod_norm(x_ref[...], g_ref[...], sh_ref[...], sc_ref[...]).astype(BF16)
    p = jnp.dot(h, win_ref[...], preferred_element_type=F32)
    pc = p[:, :DN_CONV_CH]
    y = pc * cw_ref[DN_CONV - 1:DN_CONV, :]
    for j in range(DN_CONV - 1):
        y = y + buf_ref[j] * cw_ref[j:j + 1, :]
    for j in range(DN_CONV - 2):
        cnew_ref[j] = buf_ref[j + 1]
    cnew_ref[DN_CONV - 2] = pc
    _swa_post(p, prm_ref, cos_ref[...], sin_ref[...], gq_ref, gk_ref, swa_ref)
    _dn_post(p, y, prm_ref, qkv_ref, z_ref, bg_ref)


def _inproj_common_specs(m, tm, l):
    sh, sc, _ = _mod_specs(m, 1, tm)
    return [_row_spec(tm, D_MODEL), sh, sc,
            _const_spec((1, D_MODEL)),
            _layer_spec((D_MODEL, IN_PACKED), l),
            _const_spec((DN_CONV, DN_CONV_CH)),
            _const_spec((SUBLANES, LANES))]


def _inproj_prompt(x, m, g, win, cw, prm, cos, sin, gq, gk, l, tm):
    B, T, _ = x.shape
    table = pl.BlockSpec((tm, LANES), lambda b, r: (r, 0))
    return pl.pallas_call(
        _inproj_prompt_kernel,
        grid=(B, T // tm),
        in_specs=_inproj_common_specs(m, tm, l) + [
            table, table,
            _const_spec((SWA_Q, SWA_Q)),
            _const_spec((SWA_KV, SWA_KV)),
        ],
        out_specs=[
            _row_spec(tm, DN_CONV_CH), _row_spec(tm, DN_V), _row_spec(tm, LANES),
            _row_spec(tm, SWA_OUT),
            pl.BlockSpec((None, HALO, DN_CONV_CH), lambda b, r: (b, 0, 0)),
        ],
        out_shape=[
            jax.ShapeDtypeStruct((B, T, DN_CONV_CH), BF16),
            jax.ShapeDtypeStruct((B, T, DN_V), BF16),
            jax.ShapeDtypeStruct((B, T, LANES), F32),
            jax.ShapeDtypeStruct((B, T, SWA_OUT), F32),
            jax.ShapeDtypeStruct((B, HALO, DN_CONV_CH), F32),
        ],
        scratch_shapes=[pltpu.VMEM((HALO + tm, DN_CONV_CH), F32)],
        compiler_params=_cparams(("parallel", "arbitrary")),
        name="inproj_prompt",
    )(x, m, m, g, win, cw, prm, cos, sin, gq, gk)


def _inproj_sample(x, m, g, win, cw, prm, cos, sin, gq, gk, buf, l):
    _, R, _ = x.shape
    nbuf = DN_CONV - 1
    return pl.pallas_call(
        _inproj_sample_kernel,
        grid=(1, 1),
        in_specs=_inproj_common_specs(m, R, l) + [
            _const_spec((1, LANES)), _const_spec((1, LANES)),
            _const_spec((SWA_Q, SWA_Q)), _const_spec((SWA_KV, SWA_KV)),
            _const_spec((nbuf, R, DN_CONV_CH)),
        ],
        out_specs=[
            _row_spec(R, DN_CONV_CH), _row_spec(R, DN_V), _row_spec(R, LANES), _row_spec(R, SWA_OUT),
            pl.BlockSpec((nbuf, R, DN_CONV_CH), lambda g_, r: (0, 0, 0)),
        ],
        out_shape=[
            jax.ShapeDtypeStruct((1, R, DN_CONV_CH), F32),
            jax.ShapeDtypeStruct((1, R, DN_V), F32),
            jax.ShapeDtypeStruct((1, R, LANES), F32),
            jax.ShapeDtypeStruct((1, R, SWA_OUT), F32),
            jax.ShapeDtypeStruct((nbuf, R, DN_CONV_CH), F32),
        ],
        compiler_params=_cparams(("arbitrary", "arbitrary")),
        name="inproj_sample",
    )(x, m, m, g, win, cw, prm, cos, sin, gq, gk, buf)


def _gated_norm(o, z, nw):
    on = o * lax.rsqrt(jnp.mean(o * o, axis=-1, keepdims=True) + EPS) * nw
    return on * _silu(z.astype(F32))


def _mm(a, b):
    return jnp.dot(a.astype(BF16), b.astype(BF16), preferred_element_type=F32)


def _wide_to_blockdiag(wide):
    c, n = wide.shape[0], wide.shape[1] // wide.shape[0]
    cblk = lax.broadcasted_iota(jnp.int32, wide.shape, 1) // c
    return jnp.concatenate([jnp.where(cblk == i, wide, 0.0) for i in range(n)], axis=0)


def _blockdiag_to_wide(full, c):
    n = full.shape[0] // c
    cblk = lax.broadcasted_iota(jnp.int32, (c, full.shape[1]), 1) // c
    wide = full[0:c]
    for i in range(1, n):
        wide = jnp.where(cblk == i, full[i * c:(i + 1) * c], wide)
    return wide


def _unit_lower_inverses(mats):
    c = mats[0].shape[0]
    row = lax.broadcasted_iota(jnp.int32, mats[0].shape, 0)
    col = lax.broadcasted_iota(jnp.int32, mats[0].shape, 1) % c
    s = INV_BLOCK
    diag = (row // s) == (col // s)
    pws = [jnp.where(diag, a, 0.0) for a in mats]
    eye = jnp.where(row == col, 1.0, 0.0)
    xs = [eye - pw for pw in pws]
    k = 2
    while k < s:
        pws = [_mm(pw, _wide_to_blockdiag(pw)) for pw in pws]
        xs = [x + _mm(x, _wide_to_blockdiag(pw)) for x, pw in zip(xs, pws)]
        k *= 2
    while s < c:
        pair = ((row // (2 * s)) == (col // (2 * s))) & ((row // s) != (col // s))
        ts = [_mm(x, _wide_to_blockdiag(jnp.where(pair, a, 0.0))) for x, a in zip(xs, mats)]
        xs = [x - _mm(t, _wide_to_blockdiag(x)) for x, t in zip(xs, ts)]
        s *= 2
    return xs


def _cumsum_rows(x):
    n = x.shape[0]
    row = lax.broadcasted_iota(jnp.int32, x.shape, 0)
    s = 1
    while s < n:
        x = x + jnp.where(row >= s, pltpu.roll(x, s, 0), 0.0)
        s *= 2
    return x


WQ_ROWS = 2 * DN_CHUNK
QKD_ROWS = DN_CHUNK + DN_DK


def _delta_intra_kernel(q_ref, k_ref, v_ref, bg_ref, u_ref, wq_ref, qkd_ref, eg_ref):
    c = DN_CHUNK
    r = INTRA_ROWS
    nc = r // c
    row = lax.broadcasted_iota(jnp.int32, (c, r), 0)
    col = lax.broadcasted_iota(jnp.int32, (c, r), 1) % c
    incl = row >= col
    strict = row > col

    jobs = [(blk, h) for blk in range(q_ref.shape[0] // r) for h in range(DN_HEADS)]
    gcs, egcs, ekds, gcts, bgs = [], [], [], [], []
    for blk in range(q_ref.shape[0] // r):
        bg = bg_ref[blk * r:(blk + 1) * r, :]
        gc = jnp.concatenate([_cumsum_rows(bg[ci * c:(ci + 1) * c, :]) for ci in range(nc)], axis=0)
        g_end = [gc[(ci + 1) * c - 1:(ci + 1) * c, :] for ci in range(nc)]
        glast = jnp.concatenate([jnp.broadcast_to(g, (c, LANES)) for g in g_end], axis=0)
        for ci in range(nc):
            rows = [jnp.broadcast_to(jnp.exp(g_end[ci][:, DN_HEADS + h:DN_HEADS + h + 1]), (1, LANES))
                    for h in range(DN_HEADS)]
            eg_ref[blk * nc + ci] = jnp.concatenate(
                rows + [jnp.zeros((SUBLANES - DN_HEADS, LANES), F32)], axis=0)
        bgs.append(bg)
        gcs.append(gc)
        gcts.append(gc.T)
        egcs.append(jnp.exp(gc))
        ekds.append(jnp.exp(glast - gc))

    prep = []
    for blk, h in jobs:
        rs = slice(blk * r, (blk + 1) * r)
        sl = slice(h * DN_DK, (h + 1) * DN_DK)
        q = q_ref[rs, sl]
        k = k_ref[rs, sl]
        kb = k * bgs[blk][:, h:h + 1]
        kq = lax.dot_general(jnp.concatenate([kb, q], axis=0).astype(BF16), k.astype(BF16),
                             (((1,), (1,)), ((), ())), preferred_element_type=F32)
        prep.append((q, k, kb, kq))
    a_mats, qks = [], []
    for (blk, h), (q, k, kb, kq) in zip(jobs, prep):
        gl = slice(DN_HEADS + h, DN_HEADS + h + 1)
        gcol = jnp.concatenate([jnp.broadcast_to(gcs[blk][ci * c:(ci + 1) * c, gl], (c, c))
                                for ci in range(nc)], axis=1)
        decay = jnp.exp(jnp.where(incl, gcol - gcts[blk][gl, :], -jnp.inf))
        a_mats.append(jnp.where(strict, _blockdiag_to_wide(kq[:r], c) * decay, 0.0))
        qks.append(_blockdiag_to_wide(kq[r:], c) * decay)
    tinvs = _unit_lower_inverses(a_mats)
    uws = []
    for (blk, h), (q, k, kb, kq), tinv in zip(jobs, prep, tinvs):
        rs = slice(blk * r, (blk + 1) * r)
        sl = slice(h * DN_DK, (h + 1) * DN_DK)
        gl = slice(DN_HEADS + h, DN_HEADS + h + 1)
        rhs = jnp.concatenate([v_ref[rs, sl] * bgs[blk][:, h:h + 1], kb * egcs[blk][:, gl]], axis=1)
        uws.append(_mm(_wide_to_blockdiag(tinv), rhs))
    for (blk, h), (q, k, kb, kq), uw, qk in zip(jobs, prep, uws, qks):
        rs = slice(blk * r, (blk + 1) * r)
        sl = slice(h * DN_DK, (h + 1) * DN_DK)
        gl = slice(DN_HEADS + h, DN_HEADS + h + 1)
        u_ref[rs, sl] = uw[:, :DN_DV]
        w_b = uw[:, DN_DV:].astype(BF16)
        qd_b = (q * egcs[blk][:, gl]).astype(BF16)
        for ci in range(nc):
            wq_ref[blk * nc + ci, h * WQ_ROWS:h * WQ_ROWS + c, :] = w_b[ci * c:(ci + 1) * c, :]
            wq_ref[blk * nc + ci, h * WQ_ROWS + c:(h + 1) * WQ_ROWS, :] = qd_b[ci * c:(ci + 1) * c, :]
        kdt = (k * ekds[blk][:, gl]).T
        for pj in range(nc // 2):
            ls = slice(pj * LANES, (pj + 1) * LANES)
            pi = blk * (nc // 2) + pj
            qkd_ref[pi, h * QKD_ROWS:h * QKD_ROWS + c, :] = qk[:, ls].astype(BF16)
            qkd_ref[pi, h * QKD_ROWS + c:(h + 1) * QKD_ROWS, :] = kdt[:, ls].astype(BF16)


def _delta_scan_kernel(u_ref, wq_ref, qkd_ref, eg_ref, o_ref, sout_ref, s_ref):
    n = pl.program_id(0)
    c = DN_CHUNK

    @pl.when(n == 0)
    def _():
        s_ref[...] = jnp.zeros_like(s_ref)

    zeros = jnp.zeros((c, DN_DV), BF16)
    chains = [(b, h) for b in range(u_ref.shape[0]) for h in range(DN_HEADS)]
    for j in range(2):
        rs = slice(j * c, (j + 1) * c)
        ws = [jnp.dot(wq_ref[b, j, h * WQ_ROWS:(h + 1) * WQ_ROWS, :], s_ref[b, h].astype(BF16),
                      preferred_element_type=F32) for b, h in chains]
        rr = []
        for (b, h), w_h in zip(chains, ws):
            v_new = (u_ref[b, rs, h * DN_DV:(h + 1) * DN_DV] - w_h[:c]).astype(BF16)
            rhs = jnp.concatenate([v_new, zeros] if j == 0 else [zeros, v_new], axis=0)
            rr.append(jnp.dot(qkd_ref[b, h * QKD_ROWS:(h + 1) * QKD_ROWS, :], rhs,
                              preferred_element_type=F32))
        for (b, h), w_h, r_h in zip(chains, ws, rr):
            sl = slice(h * DN_DV, (h + 1) * DN_DV)
            s_ref[b, h] = s_ref[b, h] * eg_ref[b, j, h:h + 1, :] + r_h[c:]
            o_ref[b, rs, sl] = (w_h[c:] + r_h[:c]).astype(o_ref.dtype)

    @pl.when(n == pl.num_programs(0) - 1)
    def _():
        sout_ref[...] = s_ref[...]


def _delta_prompt(qkv, bg, ra):
    B, T, _ = qkv.shape
    c = DN_CHUNK
    n = T // c
    cpb = ra // c
    qkv_spec = lambda j: pl.BlockSpec((None, ra, DN_QK), lambda b, r: (b, r, j))
    u, wq, qkd, eg = pl.pallas_call(
        _delta_intra_kernel,
        grid=(B, T // ra),
        in_specs=[qkv_spec(0), qkv_spec(1), qkv_spec(2),
                  pl.BlockSpec((None, ra, LANES), lambda b, r: (b, r, 0))],
        out_specs=[pl.BlockSpec((None, ra, DN_V), lambda b, r: (b, r, 0)),
                   pl.BlockSpec((None, cpb, DN_HEADS * WQ_ROWS, DN_DK), lambda b, r: (b, r, 0, 0)),
                   pl.BlockSpec((None, cpb // 2, DN_HEADS * QKD_ROWS, LANES),
                                lambda b, r: (b, r, 0, 0)),
                   pl.BlockSpec((None, cpb, SUBLANES, LANES), lambda b, r: (b, r, 0, 0))],
        out_shape=[jax.ShapeDtypeStruct((B, T, DN_V), F32),
                   jax.ShapeDtypeStruct((B, n, DN_HEADS * WQ_ROWS, DN_DK), BF16),
                   jax.ShapeDtypeStruct((B, n // 2, DN_HEADS * QKD_ROWS, LANES), BF16),
                   jax.ShapeDtypeStruct((B, n, SUBLANES, LANES), F32)],
        compiler_params=_cparams(("parallel", "parallel")),
        name="delta_intra",
    )(qkv, qkv, qkv, bg)
    return pl.pallas_call(
        _delta_scan_kernel,
        grid=(n // 2,),
        in_specs=[pl.BlockSpec((B, 2 * c, DN_V), lambda i: (0, i, 0)),
                  pl.BlockSpec((B, 2, DN_HEADS * WQ_ROWS, DN_DK), lambda i: (0, i, 0, 0)),
                  pl.BlockSpec((B, None, DN_HEADS * QKD_ROWS, LANES), lambda i: (0, i, 0, 0)),
                  pl.BlockSpec((B, 2, SUBLANES, LANES), lambda i: (0, i, 0, 0))],
        out_specs=[pl.BlockSpec((B, 2 * c, DN_V), lambda i: (0, i, 0)),
                   pl.BlockSpec((B, DN_HEADS, DN_DK, DN_DV), lambda i: (0, 0, 0, 0))],
        out_shape=[jax.ShapeDtypeStruct((B, T, DN_V), BF16),
                   jax.ShapeDtypeStruct((B, DN_HEADS, DN_DK, DN_DV), F32)],
        scratch_shapes=[pltpu.VMEM((B, DN_HEADS, DN_DK, DN_DV), F32)],
        compiler_params=_cparams(("arbitrary",)),
        name="delta_scan",
    )(u, wq, qkd, eg)


def _delta_sample_kernel(n_prev, qkv_ref, bg_ref, z_ref, nw_ref, s0_ref, *refs):
    prev_refs = refs[:n_prev]
    o_ref, s_ref = refs[n_prev:]
    for k, prev_ref in enumerate(prev_refs):
        s_ref[k] = prev_ref[...]
    s_out = s_ref.at[n_prev] if n_prev else s_ref
    bb = qkv_ref.shape[0]
    nw = nw_ref[...]
    zeros = jnp.zeros((SUBLANES - 2, DN_DK), F32)
    row = lax.broadcasted_iota(jnp.int32, (bb, DN_DV), 0)
    heads = []
    for h in range(DN_HEADS):
        q = qkv_ref[:, h * DN_DK:(h + 1) * DN_DK]
        k = qkv_ref[:, DN_QK + h * DN_DK:DN_QK + (h + 1) * DN_DK]
        v = qkv_ref[:, 2 * DN_QK + h * DN_DV:2 * DN_QK + (h + 1) * DN_DV]
        beta = bg_ref[:, h:h + 1]
        eg = jnp.exp(bg_ref[:, DN_HEADS + h:DN_HEADS + h + 1])
        w = k * beta * eg
        qd = q * eg
        ws = [jnp.dot(jnp.concatenate([w[i:i + 1], qd[i:i + 1], zeros], axis=0), s0_ref[i, h],
                      preferred_element_type=F32) for i in range(bb)]
        heads.append((q, k, v, beta, eg, ws))
    updates = []
    for q, k, v, beta, eg, ws in heads:
        w_s = jnp.concatenate([ws[i][0:1] for i in range(bb)], axis=0)
        q_s = jnp.concatenate([ws[i][1:2] for i in range(bb)], axis=0)
        v_new = v * beta - w_s
        kt = k.T
        kv = [jnp.dot(kt, jnp.where(row == i, v_new, 0.0), preferred_element_type=F32)
              for i in range(bb)]
        updates.append((v_new, q_s, kv))
    for h, ((q, k, v, beta, eg, ws), (v_new, q_s, kv)) in enumerate(zip(heads, updates)):
        sl = slice(h * DN_DK, (h + 1) * DN_DK)
        for i in range(bb):
            s_out[i, h] = s0_ref[i, h] * eg[i:i + 1] + kv[i]
        o = q_s + jnp.sum(q * k, axis=-1, keepdims=True) * v_new
        o_ref[:, sl] = _gated_norm(o, z_ref[:, sl], nw)


def _delta_sample(qkv, bg, z, nw, s0, l, bb, prev=()):
    R = qkv.shape[0]
    state = (bb, DN_HEADS, DN_DK, DN_DV)
    state_spec = pl.BlockSpec(state, lambda i: (i, 0, 0, 0))
    if prev:
        out_spec = pl.BlockSpec((len(prev) + 1,) + state, lambda i: (0, i, 0, 0, 0))
        out_shape = (len(prev) + 1,) + s0.shape[1:]
    else:
        out_spec, out_shape = state_spec, s0.shape[1:]
    return pl.pallas_call(
        functools.partial(_delta_sample_kernel, len(prev)),
        grid=(R // bb,),
        in_specs=[pl.BlockSpec((bb, DN_CONV_CH), lambda i: (i, 0)),
                  pl.BlockSpec((bb, LANES), lambda i: (i, 0)),
                  pl.BlockSpec((bb, DN_V), lambda i: (i, 0)),
                  _const_spec((1, DN_DV)),
                  pl.BlockSpec((None,) + state, lambda i: (l, i, 0, 0, 0))]
        + [state_spec] * len(prev),
        out_specs=[pl.BlockSpec((bb, DN_V), lambda i: (i, 0)), out_spec],
        out_shape=[jax.ShapeDtypeStruct((R, DN_V), F32),
                   jax.ShapeDtypeStruct(out_shape, F32)],
        compiler_params=_cparams(("parallel",)),
        name="delta_sample",
    )(qkv, bg, z, nw, s0, *prev)


def _swa_prompt_kernel(q_ref, kc_ref, vc_ref, kp_ref, vp_ref, sink_ref, o_ref):
    n = pl.program_id(1)
    w = WINDOW
    nq = q_ref.shape[0] // w
    half = SWA_DH
    key = lax.broadcasted_iota(jnp.int32, (2 * w, w), 0)
    qry = lax.broadcasted_iota(jnp.int32, (2 * w, w), 1)
    band = (key >= qry) & (key <= qry + w)
    bias_mid = jnp.where(band, 0.0, -jnp.inf)
    bias_first = jnp.where(band & (key >= jnp.where(n == 0, w, 0)), 0.0, -jnp.inf)
    kall = jnp.concatenate([kp_ref[...], kc_ref[...]], axis=0)
    vall = jnp.concatenate([vp_ref[...], vc_ref[...]], axis=0)
    low = lax.broadcasted_iota(jnp.int32, kall.shape, 1) < half
    kall_r = pltpu.roll(kall, half, 1)
    k_lo = [jnp.where(low, kall, 0.0).astype(BF16), jnp.where(low, kall_r, 0.0).astype(BF16)]
    k_hi = [jnp.where(low, 0.0, kall_r).astype(BF16), jnp.where(low, 0.0, kall).astype(BF16)]
    v_t = vall.T.astype(BF16)
    zeros_v = jnp.zeros((half, 2 * w), BF16)
    sum_rows = 2 * SUBLANES
    ones8 = jnp.ones((sum_rows, 2 * w), BF16)
    zeros8 = jnp.zeros((sum_rows, 2 * w), BF16)
    first_pair = lax.broadcasted_iota(jnp.int32, (1, 2 * w), 1) < w
    log2e = math.log2(math.e)
    scale = SWA_DH ** -0.5 * log2e

    units = [(i, j) for i in range(nq) for j in range(SWA_KV_HEADS)]
    scores = []
    for i, j in units:
        rs = slice(i * w, (i + 2) * w)
        kk = jnp.concatenate([k_lo[j][rs], k_hi[j][rs]], axis=0)
        c0 = j * SWA_GROUP * SWA_DH
        q = jnp.concatenate([q_ref[i * w:(i + 1) * w, c0:c0 + LANES],
                             q_ref[i * w:(i + 1) * w, c0 + LANES:c0 + 2 * LANES]], axis=0)
        scores.append(lax.dot_general(kk, (q * scale).astype(BF16), (((1,), (1,)), ((), ())),
                                      preferred_element_type=F32))
    probs, sink_rows, maxima = [], [], []
    for (i, j), s in zip(units, scores):
        bias = bias_first if i == 0 else bias_mid
        bias2 = jnp.concatenate([bias, bias], axis=1)
        p_hf, sk_hf, mx_hf = [], [], []
        for hf in range(2):
            heads = [j * SWA_GROUP + 2 * pr + hf for pr in range(2)]
            sink = log2e * jnp.where(first_pair, sink_ref[0:1, heads[0]:heads[0] + 1],
                                     sink_ref[0:1, heads[1]:heads[1] + 1])
            masked = s[hf * 2 * w:(hf + 1) * 2 * w, :] + bias2
            mx = jnp.maximum(jnp.max(masked, axis=0, keepdims=True), sink)
            p_hf.append(jnp.exp2(masked - mx).astype(BF16))
            sk_hf.append(sink)
            mx_hf.append(mx)
        probs.append(jnp.concatenate(p_hf, axis=0))
        sink_rows.append(sk_hf)
        maxima.append(mx_hf)
    outs = []
    for (i, j), p in zip(units, probs):
        vt = v_t[j * half:(j + 1) * half, i * w:(i + 2) * w]
        lhs = jnp.concatenate([jnp.concatenate([vt, zeros_v], axis=1),
                               jnp.concatenate([zeros_v, vt], axis=1),
                               jnp.concatenate([ones8, zeros8], axis=1),
                               jnp.concatenate([zeros8, ones8], axis=1)], axis=0)
        outs.append(jnp.dot(lhs, p, preferred_element_type=F32))
    for (i, j), o_t, sk_hf, mx_hf in zip(units, outs, sink_rows, maxima):
        c0 = j * SWA_GROUP * SWA_DH
        parts = []
        for hf in range(2):
            den = (o_t[2 * half + hf * sum_rows:2 * half + hf * sum_rows + 1, :]
                   + jnp.exp2(sk_hf[hf] - mx_hf[hf]))
            parts.append(o_t[hf * half:(hf + 1) * half, :] / den)
        o = jnp.concatenate(parts, axis=0).T
        for pr in range(2):
            o_ref[i * w:(i + 1) * w, c0 + pr * LANES:c0 + (pr + 1) * LANES] = (
                o[pr * w:(pr + 1) * w, :].astype(o_ref.dtype))


def _swa_prompt(swa, sinks, nq):
    B, T, _ = swa.shape
    w = WINDOW
    rq = nq * w
    kcol = SWA_Q // SWA_KV
    prev = lambda n: jnp.maximum(n * nq - 1, 0)
    return pl.pallas_call(
        _swa_prompt_kernel,
        grid=(B, T // rq),
        in_specs=[pl.BlockSpec((None, rq, SWA_Q), lambda b, n: (b, n, 0)),
                  pl.BlockSpec((None, rq, SWA_KV), lambda b, n: (b, n, kcol)),
                  pl.BlockSpec((None, rq, SWA_KV), lambda b, n: (b, n, kcol + 1)),
                  pl.BlockSpec((None, w, SWA_KV), lambda b, n: (b, prev(n), kcol)),
                  pl.BlockSpec((None, w, SWA_KV), lambda b, n: (b, prev(n), kcol + 1)),
                  _const_spec((1, LANES))],
        out_specs=pl.BlockSpec((None, rq, SWA_Q), lambda b, n: (b, n, 0)),
        out_shape=jax.ShapeDtypeStruct((B, T, SWA_Q), BF16),
        compiler_params=_cparams(("parallel", "parallel")),
        name="swa_prompt",
    )(swa, swa, swa, swa, swa, sinks)


def _swa_sample_kernel(swa_ref, kc_ref, vc_ref, sink_ref, o_ref, kn_ref, vn_ref):
    bb = swa_ref.shape[0]
    w = kc_ref.shape[1]
    scale = SWA_DH ** -0.5
    for i in range(bb):
        kn_ref[i, 0:w - 1, :] = kc_ref[i, 1:w, :]
        kn_ref[i, w - 1:w, :] = swa_ref[i:i + 1, SWA_Q:SWA_Q + SWA_KV]
        vn_ref[i, 0:w - 1, :] = vc_ref[i, 1:w, :]
        vn_ref[i, w - 1:w, :] = swa_ref[i:i + 1, SWA_Q + SWA_KV:]
    units = [(i, j) for i in range(bb) for j in range(SWA_KV_HEADS)]
    sinks = [jnp.concatenate(
        [sink_ref[0:1, j * SWA_GROUP + g:j * SWA_GROUP + g + 1] for g in range(SWA_GROUP)]
        + [jnp.zeros((SUBLANES - SWA_GROUP, 1), F32)], axis=0) for j in range(SWA_KV_HEADS)]
    qs, scs = [], []
    for i, j in units:
        q4 = jnp.concatenate(
            [swa_ref[i:i + 1, (j * SWA_GROUP + g) * SWA_DH:(j * SWA_GROUP + g + 1) * SWA_DH]
             for g in range(SWA_GROUP)] + [jnp.zeros((SUBLANES - SWA_GROUP, SWA_DH), F32)], axis=0)
        qs.append(q4)
        scs.append(lax.dot_general(q4, kc_ref[i, :, j * SWA_DH:(j + 1) * SWA_DH],
                                   (((1,), (1,)), ((), ())), preferred_element_type=F32) * scale)
    soft = []
    for (i, j), q4, sc in zip(units, qs, scs):
        knew = swa_ref[i:i + 1, SWA_Q + j * SWA_DH:SWA_Q + (j + 1) * SWA_DH]
        sn = jnp.sum(q4 * knew, axis=-1, keepdims=True) * scale
        mx = jnp.maximum(jnp.maximum(jnp.max(sc, axis=-1, keepdims=True), sn), sinks[j])
        pc = jnp.exp(sc - mx)
        pn = jnp.exp(sn - mx)
        den = jnp.sum(pc, axis=-1, keepdims=True) + pn + jnp.exp(sinks[j] - mx)
        soft.append((pc, pn, den))
    pvs = [jnp.dot(pc, vc_ref[i, :, j * SWA_DH:(j + 1) * SWA_DH], preferred_element_type=F32)
           for (i, j), (pc, pn, den) in zip(units, soft)]
    for (i, j), (pc, pn, den), pv in zip(units, soft, pvs):
        vnew = swa_ref[i:i + 1, SWA_Q + SWA_KV + j * SWA_DH:SWA_Q + SWA_KV + (j + 1) * SWA_DH]
        o = (pv + pn * vnew) / den
        for g in range(SWA_GROUP):
            hq = j * SWA_GROUP + g
            o_ref[i:i + 1, hq * SWA_DH:(hq + 1) * SWA_DH] = o[g:g + 1, :]


def _swa_sample(swa, kc, vc, sinks, bb):
    R = swa.shape[0]
    w = kc.shape[1]
    cache_spec = pl.BlockSpec((bb, w, SWA_KV), lambda i: (i, 0, 0))
    return pl.pallas_call(
        _swa_sample_kernel,
        grid=(R // bb,),
        in_specs=[pl.BlockSpec((bb, SWA_OUT), lambda i: (i, 0)), cache_spec, cache_spec,
                  _const_spec((1, LANES))],
        out_specs=[pl.BlockSpec((bb, SWA_Q), lambda i: (i, 0)), cache_spec, cache_spec],
        out_shape=[jax.ShapeDtypeStruct((R, SWA_Q), F32),
                   jax.ShapeDtypeStruct(kc.shape, F32),
                   jax.ShapeDtypeStruct(vc.shape, F32)],
        compiler_params=_cparams(("parallel",)),
        name="swa_sample",
    )(swa, kc, vc, sinks)


def _outproj_ffn_kernel(gate_dn, x_ref, dn_ref, sw_ref, *refs):
    g5_ref, sh_ref, sc_ref, ga_ref, g_ref, woa_ref, wob_ref, wgu_ref, wd_ref, o_ref = refs[-10:]
    if gate_dn:
        z_ref, nw_ref = refs[:2]
        nw = nw_ref[...]
        dn = jnp.concatenate(
            [_gated_norm(dn_ref[:, h * DN_DV:(h + 1) * DN_DV].astype(F32),
                         z_ref[:, h * DN_DV:(h + 1) * DN_DV], nw).astype(BF16)
             for h in range(DN_HEADS)], axis=1)
    else:
        dn = dn_ref[...].astype(BF16)
    mix = (jnp.dot(dn, woa_ref[...], preferred_element_type=F32)
           + jnp.dot(sw_ref[...].astype(BF16), wob_ref[...], preferred_element_type=F32))
    x = x_ref[...] + g5_ref[...] * mix
    o_ref[...] = _ffn_body(x, sh_ref[...], sc_ref[...], ga_ref[...], g_ref[...], wgu_ref, wd_ref)


def _outproj_ffn(x, dn, sw, m, g, woa, wob, wgu, wd, l, tm, z=None, nw=None):
    G, R, _ = x.shape
    g5 = _mod_specs(m, 1, tm)[2]
    gate_dn = z is not None
    gate_specs = [_row_spec(tm, DN_V), _const_spec((1, DN_DV))] if gate_dn else []
    gate_args = (z, nw) if gate_dn else ()
    return pl.pallas_call(
        functools.partial(_outproj_ffn_kernel, gate_dn),
        grid=(G, R // tm),
        in_specs=[_row_spec(tm, D_MODEL), _row_spec(tm, DN_V), _row_spec(tm, SWA_Q)]
        + gate_specs + [g5] + _mod_specs(m, 2, tm) + [
            _const_spec((1, D_MODEL)),
            _layer_spec((DN_V, D_MODEL), l),
            _layer_spec((SWA_Q, D_MODEL), l),
            _layer_spec((D_MODEL, 2 * D_FF), l),
            _layer_spec((D_FF, D_MODEL), l),
        ],
        out_specs=_row_spec(tm, D_MODEL),
        out_shape=jax.ShapeDtypeStruct(x.shape, F32),
        compiler_params=_cparams(("parallel", "parallel")),
        name="outproj_ffn",
    )(x, dn, sw, *gate_args, m, m, m, m, g, woa, wob, wgu, wd)


def _rope_tables(pos):
    half = SWA_DH // 2
    inv = jnp.power(ROPE_THETA, -jnp.arange(half, dtype=F32) * 2.0 / SWA_DH)
    ang = pos.astype(F32)[:, None] * inv[None, :]
    cos, sin = jnp.cos(ang), jnp.sin(ang)
    reps = LANES // SWA_DH
    return (jnp.concatenate([cos, cos] * reps, axis=1),
            jnp.concatenate([-sin, sin] * reps, axis=1))


def _group_matrix(width):
    i = jnp.arange(width) // SWA_DH
    return jnp.where(i[:, None] == i[None, :], 1.0 / SWA_DH, 0.0).astype(BF16)


def _pack_w_in(w_in):
    conv = w_in[..., :DN_CONV_CH]
    o = DN_CONV_CH
    z = w_in[..., o:o + DN_V]
    o += DN_V
    ba = w_in[..., o:o + 2 * DN_HEADS]
    o += 2 * DN_HEADS
    sq = w_in[..., o:o + SWA_Q]
    o += SWA_Q
    sk = w_in[..., o:o + SWA_KV]
    o += SWA_KV
    sv = w_in[..., o:o + SWA_KV]
    pad = jnp.zeros(w_in.shape[:-1] + (LANES - 2 * DN_HEADS,), w_in.dtype)
    return jnp.concatenate([conv, z, sq, sk, sv, ba, pad], axis=-1).astype(BF16)


def _lane_row(vec, offset=0):
    return jnp.zeros((LANES,), F32).at[offset:offset + vec.shape[0]].set(vec.astype(F32))


def kernel(x_prompt, x_sample, c_prompt, c_sample, state_dn_conv, state_dn_S, cache_swa_k, cache_swa_v, w_ada, b_ada, g_ffn1, w_ffn1_gu, w_ffn1_down, g_mix, w_in, dn_conv_w, dn_A_log, dn_dt_bias, dn_norm_w, swa_q_norm, swa_k_norm, swa_sinks, w_out, g_ffn2, w_ffn2_gu, w_ffn2_down):
    L = w_ada.shape[0]
    B, T, _ = x_prompt.shape
    Bs = x_sample.shape[0]
    tm = min(ROW_TILE, T)
    bb = min(SAMPLE_SEQS, Bs)
    assert T % tm == 0 and T % min(INTRA_STEP_ROWS, T) == 0 and T % INTRA_ROWS == 0
    assert T % (min(SWA_QBLOCKS, T // WINDOW) * WINDOW) == 0 and Bs % bb == 0
    assert x_sample.shape[1] == 1 and cache_swa_k.shape[2] == WINDOW

    m_p, m_s = _adaln(c_prompt, c_sample, w_ada, b_ada)
    m_p = m_p.reshape(L, N_MOD, B, 1, D_MODEL)
    m_s = m_s.reshape(L, N_MOD, 1, Bs, D_MODEL)

    cos_p, sin_p = _rope_tables(jnp.arange(T))
    cos_s, sin_s = _rope_tables(PAST_LEN + jnp.arange(1))
    gq, gk = _group_matrix(SWA_Q), _group_matrix(SWA_KV)

    wgu1, wd1 = w_ffn1_gu.astype(BF16), w_ffn1_down.astype(BF16)
    wgu2, wd2 = w_ffn2_gu.astype(BF16), w_ffn2_down.astype(BF16)
    win = _pack_w_in(w_in)
    woa, wob = w_out[:, :DN_V].astype(BF16), w_out[:, DN_V:].astype(BF16)

    yp = x_prompt
    ys = x_sample.reshape(1, Bs, D_MODEL)
    outs = [[] for _ in range(8)]
    for l in range(L):
        g1, gm, g2 = (g_ffn1[l].reshape(1, D_MODEL), g_mix[l].reshape(1, D_MODEL),
                      g_ffn2[l].reshape(1, D_MODEL))
        cw = dn_conv_w[l]
        nw = dn_norm_w[l].reshape(1, DN_DV)
        prm = jnp.stack([
            _lane_row(-jnp.exp(dn_A_log[l].astype(F32)), DN_HEADS),
            _lane_row(dn_dt_bias[l], DN_HEADS),
            jnp.concatenate([swa_q_norm[l]] * (LANES // SWA_DH)).astype(F32),
            jnp.concatenate([swa_k_norm[l]] * (LANES // SWA_DH)).astype(F32),
        ] + [jnp.zeros((LANES,), F32)] * (SUBLANES - 4))
        sinks = _lane_row(swa_sinks[l]).reshape(1, LANES)

        yp = _ffn(yp, m_p[l], 0, g1, wgu1, wd1, l, tm)
        qkv, z, bg, swa, tail = _inproj_prompt(yp, m_p[l], gm, win, cw, prm, cos_p, sin_p, gq, gk,
                                               l, tm)
        dn_o, s_new = _delta_prompt(qkv, bg, min(INTRA_STEP_ROWS, T))
        sw_o = _swa_prompt(swa, sinks, min(SWA_QBLOCKS, T // WINDOW))
        yp = _outproj_ffn(yp, dn_o, sw_o, m_p[l], g2, woa, wob, wgu2, wd2, l, tm, z, nw)
        nkeep = min(WINDOW, T)
        outs[0].append(tail[:, HALO - (DN_CONV - 1):])
        outs[1].append(s_new)
        outs[2].append(swa[:, T - nkeep:, SWA_Q:SWA_Q + SWA_KV].reshape(B, nkeep, SWA_KV_HEADS, SWA_DH))
        outs[3].append(swa[:, T - nkeep:, SWA_Q + SWA_KV:].reshape(B, nkeep, SWA_KV_HEADS, SWA_DH))

        ys = _ffn(ys, m_s[l], 0, g1, wgu1, wd1, l, Bs)
        buf = jnp.transpose(state_dn_conv[l], (1, 0, 2))
        qkv, z, bg, swa, cnew = _inproj_sample(ys, m_s[l], gm, win, cw, prm, cos_s, sin_s, gq, gk,
                                               buf, l)
        dn_o, s_new = _delta_sample(qkv[0], bg[0], z[0], nw, state_dn_S, l, bb,
                                    tuple(outs[5]) if l == L - 1 else ())
        w0 = cache_swa_k.shape[2]
        sw_o, k_new, v_new = _swa_sample(swa[0], cache_swa_k[l].reshape(Bs, w0, SWA_KV),
                                         cache_swa_v[l].reshape(Bs, w0, SWA_KV), sinks, bb)
        ys = _outproj_ffn(ys, dn_o[None], sw_o[None], m_s[l], g2, woa, wob, wgu2, wd2, l, Bs)
        outs[4].append(jnp.transpose(cnew, (1, 0, 2)))
        outs[5].append(s_new)
        outs[6].append(k_new.reshape(Bs, w0, SWA_KV_HEADS, SWA_DH))
        outs[7].append(v_new.reshape(Bs, w0, SWA_KV_HEADS, SWA_DH))

    s_sample = outs[5][-1] if L > 1 else outs[5][-1][None]
    stacked = [s_sample if i == 5 else jnp.stack(o) for i, o in enumerate(outs)]
    return (yp, ys.reshape(Bs, 1, D_MODEL)) + tuple(stacked)
```

```python
import functools
import math

import jax
import jax.numpy as jnp
from jax import lax
from jax.experimental import pallas as pl
from jax.experimental.pallas import tpu as pltpu

F32 = jnp.float32
BF16 = jnp.bfloat16

D_MODEL = 1024
D_FF = 2816
N_MOD = 9
EPS = 1e-6
DN_HEADS = 4
DN_DK = 128
DN_DV = 128
DN_QK = DN_HEADS * DN_DK
DN_V = DN_HEADS * DN_DV
DN_CONV = 4
DN_CONV_CH = 2 * DN_QK + DN_V
DN_CHUNK = 64
INV_BLOCK = 16
INTRA_ROWS = 256
SWA_DH = 64
SWA_HEADS = 8
SWA_KV_HEADS = 2
SWA_GROUP = SWA_HEADS // SWA_KV_HEADS
SWA_Q = SWA_HEADS * SWA_DH
SWA_KV = SWA_KV_HEADS * SWA_DH
WINDOW = 128
ROPE_THETA = 10000.0
PAST_LEN = 16384

LANES = 128
SUBLANES = 8
HALO = SUBLANES

C_CONV = 0
C_Z = C_CONV + DN_CONV_CH
C_SQ = C_Z + DN_V
C_SK = C_SQ + SWA_Q
C_SV = C_SK + SWA_KV
C_BG = C_SV + SWA_KV
IN_PACKED = C_BG + LANES
SWA_OUT = SWA_Q + 2 * SWA_KV

VMEM_LIMIT = 56 * 1024 * 1024

ROW_TILE = 512
INTRA_STEP_ROWS = 1024
SWA_QBLOCKS = 8
SAMPLE_SEQS = SUBLANES


def _cparams(sem):
    return pltpu.CompilerParams(dimension_semantics=sem, vmem_limit_bytes=VMEM_LIMIT)


def _const_spec(shape):
    nd = len(shape)
    return pl.BlockSpec(shape, lambda *_: (0,) * nd, pipeline_mode=pl.Buffered(1))


def _layer_spec(shape, l):
    nd = len(shape)
    return pl.BlockSpec((None,) + tuple(shape), lambda *_: (l,) + (0,) * nd,
                        pipeline_mode=pl.Buffered(1))


def _sigmoid(x):
    return 1.0 / (1.0 + jnp.exp(-x))


def _silu(x):
    return x * _sigmoid(x)


def _adaln_kernel(cp_ref, cs_ref, w_ref, b_ref, mp_ref, ms_ref):
    w = w_ref[...].astype(BF16)
    b = b_ref[...]
    for c_ref, m_ref in ((cp_ref, mp_ref), (cs_ref, ms_ref)):
        c = c_ref[...]
        m_ref[...] = jnp.dot(_silu(c).astype(BF16), w, preferred_element_type=F32) + b


def _adaln(c_p, c_s, w_ada, b_ada):
    L = w_ada.shape[0]
    bp, bs = c_p.shape[0], c_s.shape[0]
    b4 = b_ada.reshape(L, N_MOD, 1, D_MODEL)
    return pl.pallas_call(
        _adaln_kernel,
        grid=(L, N_MOD),
        in_specs=[
            pl.BlockSpec((bp, D_MODEL), lambda l, j: (0, 0)),
            pl.BlockSpec((bs, D_MODEL), lambda l, j: (0, 0)),
            pl.BlockSpec((None, D_MODEL, D_MODEL), lambda l, j: (l, 0, j)),
            pl.BlockSpec((None, None, 1, D_MODEL), lambda l, j: (l, j, 0, 0)),
        ],
        out_specs=[
            pl.BlockSpec((None, None, bp, D_MODEL), lambda l, j: (l, j, 0, 0)),
            pl.BlockSpec((None, None, bs, D_MODEL), lambda l, j: (l, j, 0, 0)),
        ],
        out_shape=[
            jax.ShapeDtypeStruct((L, N_MOD, bp, D_MODEL), F32),
            jax.ShapeDtypeStruct((L, N_MOD, bs, D_MODEL), F32),
        ],
        compiler_params=_cparams(("parallel", "parallel")),
        name="adaln",
    )(c_p, c_s, w_ada, b4)


def _mod_norm(x, g, shift, scale):
    ms = jnp.mean(x * x, axis=-1, keepdims=True)
    return (x * lax.rsqrt(ms + EPS)) * (g * (1.0 + scale)) + shift


def _ffn_body(x, shift, scale, gate, g, wgu_ref, wd_ref):
    h = _mod_norm(x, g, shift, scale).astype(BF16)
    gu = jnp.dot(h, wgu_ref[...], preferred_element_type=F32)
    a = gu[:, :D_FF]
    b = gu[:, D_FF:]
    act = (_silu(a) * b).astype(BF16)
    y = jnp.dot(act, wd_ref[...], preferred_element_type=F32)
    return x + (0.5 * gate) * y


def _mod_specs(m, sub, tm):
    rm = m.shape[2]
    blk = 1 if rm == 1 else tm

    def spec(i):
        if rm == 1:
            return pl.BlockSpec((None, None, 1, D_MODEL), lambda g, r: (i, g, 0, 0))
        return pl.BlockSpec((None, None, blk, D_MODEL), lambda g, r: (i, g, r, 0))

    return [spec(3 * sub), spec(3 * sub + 1), spec(3 * sub + 2)]


def _row_spec(tm, width):
    return pl.BlockSpec((None, tm, width), lambda g, r: (g, r, 0))


def _ffn_kernel(x_ref, sh_ref, sc_ref, ga_ref, g_ref, wgu_ref, wd_ref, o_ref):
    o_ref[...] = _ffn_body(x_ref[...], sh_ref[...], sc_ref[...], ga_ref[...], g_ref[...],
                           wgu_ref, wd_ref)


def _ffn(x, m, sub, g, wgu, wd, l, tm):
    G, R, _ = x.shape
    return pl.pallas_call(
        _ffn_kernel,
        grid=(G, R // tm),
        in_specs=[_row_spec(tm, D_MODEL)] + _mod_specs(m, sub, tm) + [
            _const_spec((1, D_MODEL)),
            _layer_spec((D_MODEL, 2 * D_FF), l),
            _layer_spec((D_FF, D_MODEL), l),
        ],
        out_specs=_row_spec(tm, D_MODEL),
        out_shape=jax.ShapeDtypeStruct(x.shape, F32),
        compiler_params=_cparams(("parallel", "parallel")),
        name="ffn",
    )(x, m, m, m, g, wgu, wd)


def _group_mean_sq(x, gmat_ref):
    return jnp.dot((x * x).astype(BF16), gmat_ref[...], preferred_element_type=F32)


def _rope(x, cos, sin_signed):
    width = x.shape[-1]
    reps = width // LANES
    lane = lax.broadcasted_iota(jnp.int32, x.shape, 1)
    first_half = (lane & (SWA_DH - 1)) < (SWA_DH // 2)
    partner = jnp.where(first_half,
                        pltpu.roll(x, width - SWA_DH // 2, 1),
                        pltpu.roll(x, SWA_DH // 2, 1))
    if reps > 1:
        cos = jnp.concatenate([cos] * reps, axis=1)
        sin_signed = jnp.concatenate([sin_signed] * reps, axis=1)
    return x * cos + partner * sin_signed


def _swa_post(p, prm_ref, cos, sin, gq_ref, gk_ref, swa_ref):
    pq = p[:, C_SQ:C_SQ + SWA_Q]
    pk = p[:, C_SK:C_SK + SWA_KV]
    wq = jnp.concatenate([prm_ref[2:3, :]] * (SWA_Q // LANES), axis=1)
    wk = prm_ref[3:4, :]
    qn = pq * lax.rsqrt(_group_mean_sq(pq, gq_ref) + EPS) * wq
    kn = pk * lax.rsqrt(_group_mean_sq(pk, gk_ref) + EPS) * wk
    swa_ref[:, :SWA_Q] = _rope(qn, cos, sin)
    swa_ref[:, SWA_Q:SWA_Q + SWA_KV] = _rope(kn, cos, sin)
    swa_ref[:, SWA_Q + SWA_KV:] = p[:, C_SV:C_SV + SWA_KV]


def _dn_group(a, gi, qkv_ref):
    for hh in range(DN_HEADS):
        cols = slice(gi * DN_QK + hh * DN_DK, gi * DN_QK + (hh + 1) * DN_DK)
        xh = a[:, hh * DN_DK:(hh + 1) * DN_DK]
        if gi < 2:
            s = lax.rsqrt(jnp.sum(xh * xh, axis=-1, keepdims=True) + EPS)
            if gi == 0:
                s = s * (DN_DK ** -0.5)
            xh = xh * s
        qkv_ref[:, cols] = xh.astype(qkv_ref.dtype)


def _bg_post(pb, prm_ref, bg_ref):
    neg_a = prm_ref[0:1, :]
    dt_b = prm_ref[1:2, :]
    xa = pb + dt_b
    softplus = jnp.maximum(xa, 0.0) + jnp.log(1.0 + jnp.exp(-jnp.abs(xa)))
    lane = lax.broadcasted_iota(jnp.int32, pb.shape, 1)
    bg_ref[...] = jnp.where(lane < DN_HEADS, _sigmoid(pb), neg_a * softplus)


def _dn_post(p, qkv_conv, prm_ref, qkv_ref, z_ref, bg_ref):
    qkv = _silu(qkv_conv)
    for gi in range(3):
        _dn_group(qkv[:, gi * DN_QK:(gi + 1) * DN_QK], gi, qkv_ref)
    z_ref[...] = p[:, C_Z:C_Z + DN_V].astype(z_ref.dtype)
    _bg_post(p[:, C_BG:C_BG + LANES], prm_ref, bg_ref)


class _Cols:
    def __init__(self, segments):
        self.segments = segments

    def __getitem__(self, idx):
        cols = idx[1]
        for start, arr in self.segments:
            if start <= cols.start and cols.stop <= start + arr.shape[1]:
                return arr[:, cols.start - start:cols.stop - start]
        raise KeyError(cols)


def _inproj_prompt_kernel(x_ref, sh_ref, sc_ref, g_ref, win_ref, cw_ref, prm_ref, cos_ref, sin_ref,
                          gq_ref, gk_ref, qkv_ref, z_ref, bg_ref, swa_ref, tail_ref, ext_ref):
    tm = x_ref.shape[0]
    first = pl.program_id(1) == 0

    @pl.when(first)
    def _():
        ext_ref[0:HALO, :] = jnp.zeros((HALO, DN_CONV_CH), F32)

    @pl.when(jnp.logical_not(first))
    def _():
        ext_ref[0:HALO, :] = ext_ref[tm:tm + HALO, :]

    h = _mod_norm(x_ref[...], g_ref[...], sh_ref[...], sc_ref[...]).astype(BF16)
    proj = lambda a, b: jnp.dot(h, win_ref[:, a:b], preferred_element_type=F32)
    p_att = _Cols([(C_SQ, proj(C_SQ, C_SK)), (C_SK, proj(C_SK, IN_PACKED))])
    _swa_post(p_att, prm_ref, cos_ref[...], sin_ref[...], gq_ref, gk_ref, swa_ref)
    _bg_post(p_att[:, C_BG:C_BG + LANES], prm_ref, bg_ref)
    for gi in range(3):
        cs = slice(gi * DN_QK, (gi + 1) * DN_QK)
        pc = proj(cs.start, cs.stop)
        ext_ref[HALO:, cs] = pc
        y = pc * cw_ref[DN_CONV - 1:DN_CONV, cs]
        for j in range(DN_CONV - 1):
            off = HALO - (DN_CONV - 1) + j
            y = y + ext_ref[off:off + tm, cs] * cw_ref[j:j + 1, cs]
        _dn_group(_silu(y), gi, qkv_ref)
    tail_ref[...] = ext_ref[tm:tm + HALO, :]
    z_ref[...] = proj(C_Z, C_Z + DN_V).astype(z_ref.dtype)


def _inproj_sample_kernel(x_ref, sh_ref, sc_ref, g_ref, win_ref, cw_ref, prm_ref, cos_ref, sin_ref,
                          gq_ref, gk_ref, buf_ref, qkv_ref, z_ref, bg_ref, swa_ref, cnew_ref):
    h = _mod_norm(x_ref[...], g_ref[...], sh_ref[...], sc_ref[...]).astype(BF16)
    p = jnp.dot(h, win_ref[...], preferred_element_type=F32)
    pc = p[:, :DN_CONV_CH]
    y = pc * cw_ref[DN_CONV - 1:DN_CONV, :]
    for j in range(DN_CONV - 1):
        y = y + buf_ref[j] * cw_ref[j:j + 1, :]
    for j in range(DN_CONV - 2):
        cnew_ref[j] = buf_ref[j + 1]
    cnew_ref[DN_CONV - 2] = pc
    _swa_post(p, prm_ref, cos_ref[...], sin_ref[...], gq_ref, gk_ref, swa_ref)
    _dn_post(p, y, prm_ref, qkv_ref, z_ref, bg_ref)


def _inproj_common_specs(m, tm, l):
    sh, sc, _ = _mod_specs(m, 1, tm)
    return [_row_spec(tm, D_MODEL), sh, sc,
            _const_spec((1, D_MODEL)),
            _layer_spec((D_MODEL, IN_PACKED), l),
            _const_spec((DN_CONV, DN_CONV_CH)),
            _const_spec((SUBLANES, LANES))]


def _inproj_prompt(x, m, g, win, cw, prm, cos, sin, gq, gk, l, tm):
    B, T, _ = x.shape
    table = pl.BlockSpec((tm, LANES), lambda b, r: (r, 0))
    return pl.pallas_call(
        _inproj_prompt_kernel,
        grid=(B, T // tm),
        in_specs=_inproj_common_specs(m, tm, l) + [
            table, table,
            _const_spec((SWA_Q, SWA_Q)),
            _const_spec((SWA_KV, SWA_KV)),
        ],
        out_specs=[
            _row_spec(tm, DN_CONV_CH), _row_spec(tm, DN_V), _row_spec(tm, LANES),
            _row_spec(tm, SWA_OUT),
            pl.BlockSpec((None, HALO, DN_CONV_CH), lambda b, r: (b, 0, 0)),
        ],
        out_shape=[
            jax.ShapeDtypeStruct((B, T, DN_CONV_CH), BF16),
            jax.ShapeDtypeStruct((B, T, DN_V), BF16),
            jax.ShapeDtypeStruct((B, T, LANES), F32),
            jax.ShapeDtypeStruct((B, T, SWA_OUT), F32),
            jax.ShapeDtypeStruct((B, HALO, DN_CONV_CH), F32),
        ],
        scratch_shapes=[pltpu.VMEM((HALO + tm, DN_CONV_CH), F32)],
        compiler_params=_cparams(("parallel", "arbitrary")),
        name="inproj_prompt",
    )(x, m, m, g, win, cw, prm, cos, sin, gq, gk)


def _inproj_sample(x, m, g, win, cw, prm, cos, sin, gq, gk, buf, l):
    _, R, _ = x.shape
    nbuf = DN_CONV - 1
    return pl.pallas_call(
        _inproj_sample_kernel,
        grid=(1, 1),
        in_specs=_inproj_common_specs(m, R, l) + [
            _const_spec((1, LANES)), _const_spec((1, LANES)),
            _const_spec((SWA_Q, SWA_Q)), _const_spec((SWA_KV, SWA_KV)),
            _const_spec((nbuf, R, DN_CONV_CH)),
        ],
        out_specs=[
            _row_spec(R, DN_CONV_CH), _row_spec(R, DN_V), _row_spec(R, LANES), _row_spec(R, SWA_OUT),
            pl.BlockSpec((nbuf, R, DN_CONV_CH), lambda g_, r: (0, 0, 0)),
        ],
        out_shape=[
            jax.ShapeDtypeStruct((1, R, DN_CONV_CH), F32),
            jax.ShapeDtypeStruct((1, R, DN_V), F32),
            jax.ShapeDtypeStruct((1, R, LANES), F32),
            jax.ShapeDtypeStruct((1, R, SWA_OUT), F32),
            jax.ShapeDtypeStruct((nbuf, R, DN_CONV_CH), F32),
        ],
        compiler_params=_cparams(("arbitrary", "arbitrary")),
        name="inproj_sample",
    )(x, m, m, g, win, cw, prm, cos, sin, gq, gk, buf)


def _gated_norm(o, z, nw):
    on = o * lax.rsqrt(jnp.mean(o * o, axis=-1, keepdims=True) + EPS) * nw
    return on * _silu(z.astype(F32))


def _mm(a, b):
    return jnp.dot(a.astype(BF16), b.astype(BF16), preferred_element_type=F32)


def _wide_to_blockdiag(wide):
    c, n = wide.shape[0], wide.shape[1] // wide.shape[0]
    cblk = lax.broadcasted_iota(jnp.int32, wide.shape, 1) // c
    return jnp.concatenate([jnp.where(cblk == i, wide, 0.0) for i in range(n)], axis=0)


def _blockdiag_to_wide(full, c):
    n = full.shape[0] // c
    cblk = lax.broadcasted_iota(jnp.int32, (c, full.shape[1]), 1) // c
    wide = full[0:c]
    for i in range(1, n):
        wide = jnp.where(cblk == i, full[i * c:(i + 1) * c], wide)
    return wide


def _unit_lower_inverses(mats):
    c = mats[0].shape[0]
    row = lax.broadcasted_iota(jnp.int32, mats[0].shape, 0)
    col = lax.broadcasted_iota(jnp.int32, mats[0].shape, 1) % c
    s = INV_BLOCK
    diag = (row // s) == (col // s)
    pws = [jnp.where(diag, a, 0.0) for a in mats]
    eye = jnp.where(row == col, 1.0, 0.0)
    xs = [eye - pw for pw in pws]
    k = 2
    while k < s:
        pws = [_mm(pw, _wide_to_blockdiag(pw)) for pw in pws]
        xs = [x + _mm(x, _wide_to_blockdiag(pw)) for x, pw in zip(xs, pws)]
        k *= 2
    while s < c:
        pair = ((row // (2 * s)) == (col // (2 * s))) & ((row // s) != (col // s))
        ts = [_mm(x, _wide_to_blockdiag(jnp.where(pair, a, 0.0))) for x, a in zip(xs, mats)]
        xs = [x - _mm(t, _wide_to_blockdiag(x)) for x, t in zip(xs, ts)]
        s *= 2
    return xs


def _cumsum_rows(x):
    n = x.shape[0]
    row = lax.broadcasted_iota(jnp.int32, x.shape, 0)
    s = 1
    while s < n:
        x = x + jnp.where(row >= s, pltpu.roll(x, s, 0), 0.0)
        s *= 2
    return x


WQ_ROWS = 2 * DN_CHUNK
QKD_ROWS = DN_CHUNK + DN_DK


def _delta_intra_kernel(q_ref, k_ref, v_ref, bg_ref, u_ref, wq_ref, qkd_ref, eg_ref):
    c = DN_CHUNK
    r = INTRA_ROWS
    nc = r // c
    row = lax.broadcasted_iota(jnp.int32, (c, r), 0)
    col = lax.broadcasted_iota(jnp.int32, (c, r), 1) % c
    incl = row >= col
    strict = row > col

    jobs = [(blk, h) for blk in range(q_ref.shape[0] // r) for h in range(DN_HEADS)]
    gcs, egcs, ekds, gcts, bgs = [], [], [], [], []
    for blk in range(q_ref.shape[0] // r):
        bg = bg_ref[blk * r:(blk + 1) * r, :]
        gc = jnp.concatenate([_cumsum_rows(bg[ci * c:(ci + 1) * c, :]) for ci in range(nc)], axis=0)
        g_end = [gc[(ci + 1) * c - 1:(ci + 1) * c, :] for ci in range(nc)]
        glast = jnp.concatenate([jnp.broadcast_to(g, (c, LANES)) for g in g_end], axis=0)
        for ci in range(nc):
            rows = [jnp.broadcast_to(jnp.exp(g_end[ci][:, DN_HEADS + h:DN_HEADS + h + 1]), (1, LANES))
                    for h in range(DN_HEADS)]
            eg_ref[blk * nc + ci] = jnp.concatenate(
                rows + [jnp.zeros((SUBLANES - DN_HEADS, LANES), F32)], axis=0)
        bgs.append(bg)
        gcs.append(gc)
        gcts.append(gc.T)
        egcs.append(jnp.exp(gc))
        ekds.append(jnp.exp(glast - gc).T)

    prep = []
    for blk, h in jobs:
        rs = slice(blk * r, (blk + 1) * r)
        sl = slice(h * DN_DK, (h + 1) * DN_DK)
        q = q_ref[rs, sl]
        k = k_ref[rs, sl]
        kb = k * bgs[blk][:, h:h + 1]
        kq = lax.dot_general(jnp.concatenate([kb, q], axis=0).astype(BF16), k.astype(BF16),
                             (((1,), (1,)), ((), ())), preferred_element_type=F32)
        prep.append((q, k, kb, kq))
    a_mats, qks = [], []
    for (blk, h), (q, k, kb, kq) in zip(jobs, prep):
        gl = slice(DN_HEADS + h, DN_HEADS + h + 1)
        gcol = jnp.concatenate([jnp.broadcast_to(gcs[blk][ci * c:(ci + 1) * c, gl], (c, c))
                                for ci in range(nc)], axis=1)
        decay = jnp.exp(jnp.where(incl, gcol - gcts[blk][gl, :], -jnp.inf))
        a_mats.append(jnp.where(strict, _blockdiag_to_wide(kq[:r], c) * decay, 0.0))
        qks.append(_blockdiag_to_wide(kq[r:], c) * decay)
    tinvs = _unit_lower_inverses(a_mats)
    uws = []
    for (blk, h), (q, k, kb, kq), tinv in zip(jobs, prep, tinvs):
        rs = slice(blk * r, (blk + 1) * r)
        sl = slice(h * DN_DK, (h + 1) * DN_DK)
        gl = slice(DN_HEADS + h, DN_HEADS + h + 1)
        rhs = jnp.concatenate([v_ref[rs, sl] * bgs[blk][:, h:h + 1], kb * egcs[blk][:, gl]], axis=1)
        uws.append(_mm(_wide_to_blockdiag(tinv), rhs))
    for (blk, h), (q, k, kb, kq), uw, qk in zip(jobs, prep, uws, qks):
        rs = slice(blk * r, (blk + 1) * r)
        sl = slice(h * DN_DK, (h + 1) * DN_DK)
        gl = slice(DN_HEADS + h, DN_HEADS + h + 1)
        u_ref[rs, sl] = uw[:, :DN_DV]
        w_b = uw[:, DN_DV:].astype(BF16)
        qd_b = (q * egcs[blk][:, gl]).astype(BF16)
        for ci in range(nc):
            wq_ref[blk * nc + ci, h * WQ_ROWS:h * WQ_ROWS + c, :] = w_b[ci * c:(ci + 1) * c, :]
            wq_ref[blk * nc + ci, h * WQ_ROWS + c:(h + 1) * WQ_ROWS, :] = qd_b[ci * c:(ci + 1) * c, :]
        kdt = k.T * ekds[blk][gl, :]
        for pj in range(nc // 2):
            ls = slice(pj * LANES, (pj + 1) * LANES)
            pi = blk * (nc // 2) + pj
            qkd_ref[pi, h * QKD_ROWS:h * QKD_ROWS + c, :] = qk[:, ls].astype(BF16)
            qkd_ref[pi, h * QKD_ROWS + c:(h + 1) * QKD_ROWS, :] = kdt[:, ls].astype(BF16)


def _delta_scan_kernel(u_ref, wq_ref, qkd_ref, eg_ref, o_ref, sout_ref, s_ref):
    n = pl.program_id(0)
    c = DN_CHUNK

    @pl.when(n == 0)
    def _():
        s_ref[...] = jnp.zeros_like(s_ref)

    zeros = jnp.zeros((c, DN_DV), BF16)
    chains = [(b, h) for b in range(u_ref.shape[0]) for h in range(DN_HEADS)]
    for j in range(2):
        rs = slice(j * c, (j + 1) * c)
        ws = [jnp.dot(wq_ref[b, j, h * WQ_ROWS:(h + 1) * WQ_ROWS, :], s_ref[b, h].astype(BF16),
                      preferred_element_type=F32) for b, h in chains]
        rr = []
        for (b, h), w_h in zip(chains, ws):
            v_new = (u_ref[b, rs, h * DN_DV:(h + 1) * DN_DV] - w_h[:c]).astype(BF16)
            rhs = jnp.concatenate([v_new, zeros] if j == 0 else [zeros, v_new], axis=0)
            rr.append(jnp.dot(qkd_ref[b, h * QKD_ROWS:(h + 1) * QKD_ROWS, :], rhs,
                              preferred_element_type=F32))
        for (b, h), w_h, r_h in zip(chains, ws, rr):
            sl = slice(h * DN_DV, (h + 1) * DN_DV)
            s_ref[b, h] = s_ref[b, h] * eg_ref[b, j, h:h + 1, :] + r_h[c:]
            o_ref[b, rs, sl] = (w_h[c:] + r_h[:c]).astype(o_ref.dtype)

    @pl.when(n == pl.num_programs(0) - 1)
    def _():
        sout_ref[...] = s_ref[...]


def _delta_prompt(qkv, bg, ra):
    B, T, _ = qkv.shape
    c = DN_CHUNK
    n = T // c
    cpb = ra // c
    qkv_spec = lambda j: pl.BlockSpec((None, ra, DN_QK), lambda b, r: (b, r, j))
    u, wq, qkd, eg = pl.pallas_call(
        _delta_intra_kernel,
        grid=(B, T // ra),
        in_specs=[qkv_spec(0), qkv_spec(1), qkv_spec(2),
                  pl.BlockSpec((None, ra, LANES), lambda b, r: (b, r, 0))],
        out_specs=[pl.BlockSpec((None, ra, DN_V), lambda b, r: (b, r, 0)),
                   pl.BlockSpec((None, cpb, DN_HEADS * WQ_ROWS, DN_DK), lambda b, r: (b, r, 0, 0)),
                   pl.BlockSpec((None, cpb // 2, DN_HEADS * QKD_ROWS, LANES),
                                lambda b, r: (b, r, 0, 0)),
                   pl.BlockSpec((None, cpb, SUBLANES, LANES), lambda b, r: (b, r, 0, 0))],
        out_shape=[jax.ShapeDtypeStruct((B, T, DN_V), F32),
                   jax.ShapeDtypeStruct((B, n, DN_HEADS * WQ_ROWS, DN_DK), BF16),
                   jax.ShapeDtypeStruct((B, n // 2, DN_HEADS * QKD_ROWS, LANES), BF16),
                   jax.ShapeDtypeStruct((B, n, SUBLANES, LANES), F32)],
        compiler_params=_cparams(("parallel", "parallel")),
        name="delta_intra",
    )(qkv, qkv, qkv, bg)
    return pl.pallas_call(
        _delta_scan_kernel,
        grid=(n // 2,),
        in_specs=[pl.BlockSpec((B, 2 * c, DN_V), lambda i: (0, i, 0)),
                  pl.BlockSpec((B, 2, DN_HEADS * WQ_ROWS, DN_DK), lambda i: (0, i, 0, 0)),
                  pl.BlockSpec((B, None, DN_HEADS * QKD_ROWS, LANES), lambda i: (0, i, 0, 0)),
                  pl.BlockSpec((B, 2, SUBLANES, LANES), lambda i: (0, i, 0, 0))],
        out_specs=[pl.BlockSpec((B, 2 * c, DN_V), lambda i: (0, i, 0)),
                   pl.BlockSpec((B, DN_HEADS, DN_DK, DN_DV), lambda i: (0, 0, 0, 0))],
        out_shape=[jax.ShapeDtypeStruct((B, T, DN_V), BF16),
                   jax.ShapeDtypeStruct((B, DN_HEADS, DN_DK, DN_DV), F32)],
        scratch_shapes=[pltpu.VMEM((B, DN_HEADS, DN_DK, DN_DV), F32)],
        compiler_params=_cparams(("arbitrary",)),
        name="delta_scan",
    )(u, wq, qkd, eg)


def _delta_sample_kernel(n_prev, qkv_ref, bg_ref, z_ref, nw_ref, s0_ref, *refs):
    prev_refs = refs[:n_prev]
    o_ref, s_ref = refs[n_prev:]
    for k, prev_ref in enumerate(prev_refs):
        s_ref[k] = prev_ref[...]
    s_out = s_ref.at[n_prev] if n_prev else s_ref
    bb = qkv_ref.shape[0]
    nw = nw_ref[...]
    zeros = jnp.zeros((SUBLANES - 2, DN_DK), F32)
    row = lax.broadcasted_iota(jnp.int32, (bb, DN_DV), 0)
    heads = []
    for h in range(DN_HEADS):
        q = qkv_ref[:, h * DN_DK:(h + 1) * DN_DK]
        k = qkv_ref[:, DN_QK + h * DN_DK:DN_QK + (h + 1) * DN_DK]
        v = qkv_ref[:, 2 * DN_QK + h * DN_DV:2 * DN_QK + (h + 1) * DN_DV]
        beta = bg_ref[:, h:h + 1]
        eg = jnp.exp(bg_ref[:, DN_HEADS + h:DN_HEADS + h + 1])
        w = k * beta * eg
        qd = q * eg
        ws = [jnp.dot(jnp.concatenate([w[i:i + 1], qd[i:i + 1], zeros], axis=0), s0_ref[i, h],
                      preferred_element_type=F32) for i in range(bb)]
        heads.append((q, k, v, beta, eg, ws))
    updates = []
    for q, k, v, beta, eg, ws in heads:
        w_s = jnp.concatenate([ws[i][0:1] for i in range(bb)], axis=0)
        q_s = jnp.concatenate([ws[i][1:2] for i in range(bb)], axis=0)
        v_new = v * beta - w_s
        kt = k.T
        kv = [jnp.dot(kt, jnp.where(row == i, v_new, 0.0), preferred_element_type=F32)
              for i in range(bb)]
        updates.append((v_new, q_s, kv))
    for h, ((q, k, v, beta, eg, ws), (v_new, q_s, kv)) in enumerate(zip(heads, updates)):
        sl = slice(h * DN_DK, (h + 1) * DN_DK)
        for i in range(bb):
            s_out[i, h] = s0_ref[i, h] * eg[i:i + 1] + kv[i]
        o = q_s + jnp.sum(q * k, axis=-1, keepdims=True) * v_new
        o_ref[:, sl] = _gated_norm(o, z_ref[:, sl], nw)


def _delta_sample(qkv, bg, z, nw, s0, l, bb, prev=()):
    R = qkv.shape[0]
    state = (bb, DN_HEADS, DN_DK, DN_DV)
    state_spec = pl.BlockSpec(state, lambda i: (i, 0, 0, 0))
    if prev:
        out_spec = pl.BlockSpec((len(prev) + 1,) + state, lambda i: (0, i, 0, 0, 0))
        out_shape = (len(prev) + 1,) + s0.shape[1:]
    else:
        out_spec, out_shape = state_spec, s0.shape[1:]
    return pl.pallas_call(
        functools.partial(_delta_sample_kernel, len(prev)),
        grid=(R // bb,),
        in_specs=[pl.BlockSpec((bb, DN_CONV_CH), lambda i: (i, 0)),
                  pl.BlockSpec((bb, LANES), lambda i: (i, 0)),
                  pl.BlockSpec((bb, DN_V), lambda i: (i, 0)),
                  _const_spec((1, DN_DV)),
                  pl.BlockSpec((None,) + state, lambda i: (l, i, 0, 0, 0))]
        + [state_spec] * len(prev),
        out_specs=[pl.BlockSpec((bb, DN_V), lambda i: (i, 0)), out_spec],
        out_shape=[jax.ShapeDtypeStruct((R, DN_V), F32),
                   jax.ShapeDtypeStruct(out_shape, F32)],
        compiler_params=_cparams(("parallel",)),
        name="delta_sample",
    )(qkv, bg, z, nw, s0, *prev)


def _swa_prompt_kernel(q_ref, kc_ref, vc_ref, kp_ref, vp_ref, sink_ref, o_ref):
    n = pl.program_id(1)
    w = WINDOW
    nq = q_ref.shape[0] // w
    half = SWA_DH
    key = lax.broadcasted_iota(jnp.int32, (2 * w, w), 0)
    qry = lax.broadcasted_iota(jnp.int32, (2 * w, w), 1)
    band = (key >= qry) & (key <= qry + w)
    bias_mid = jnp.where(band, 0.0, -jnp.inf)
    bias_first = jnp.where(band & (key >= jnp.where(n == 0, w, 0)), 0.0, -jnp.inf)
    kall = jnp.concatenate([kp_ref[...], kc_ref[...]], axis=0)
    vall = jnp.concatenate([vp_ref[...], vc_ref[...]], axis=0)
    low = lax.broadcasted_iota(jnp.int32, kall.shape, 1) < half
    kall_r = pltpu.roll(kall, half, 1)
    k_lo = [jnp.where(low, kall, 0.0).astype(BF16), jnp.where(low, kall_r, 0.0).astype(BF16)]
    k_hi = [jnp.where(low, 0.0, kall_r).astype(BF16), jnp.where(low, 0.0, kall).astype(BF16)]
    v_t = vall.T.astype(BF16)
    zeros_v = jnp.zeros((half, 2 * w), BF16)
    sum_rows = 2 * SUBLANES
    ones8 = jnp.ones((sum_rows, 2 * w), BF16)
    zeros8 = jnp.zeros((sum_rows, 2 * w), BF16)
    first_pair = lax.broadcasted_iota(jnp.int32, (1, 2 * w), 1) < w
    log2e = math.log2(math.e)
    scale = SWA_DH ** -0.5 * log2e

    units = [(i, j) for i in range(nq) for j in range(SWA_KV_HEADS)]
    scores = []
    for i, j in units:
        rs = slice(i * w, (i + 2) * w)
        kk = jnp.concatenate([k_lo[j][rs], k_hi[j][rs]], axis=0)
        c0 = j * SWA_GROUP * SWA_DH
        q = jnp.concatenate([q_ref[i * w:(i + 1) * w, c0:c0 + LANES],
                             q_ref[i * w:(i + 1) * w, c0 + LANES:c0 + 2 * LANES]], axis=0)
        scores.append(lax.dot_general(kk, (q * scale).astype(BF16), (((1,), (1,)), ((), ())),
                                      preferred_element_type=F32))
    probs, sink_rows, maxima = [], [], []
    for (i, j), s in zip(units, scores):
        bias = bias_first if i == 0 else bias_mid
        bias2 = jnp.concatenate([bias, bias], axis=1)
        p_hf, sk_hf, mx_hf = [], [], []
        for hf in range(2):
            heads = [j * SWA_GROUP + 2 * pr + hf for pr in range(2)]
            sink = log2e * jnp.where(first_pair, sink_ref[0:1, heads[0]:heads[0] + 1],
                                     sink_ref[0:1, heads[1]:heads[1] + 1])
            masked = s[hf * 2 * w:(hf + 1) * 2 * w, :] + bias2
            mx = jnp.maximum(jnp.max(masked, axis=0, keepdims=True), sink)
            p_hf.append(jnp.exp2(masked - mx).astype(BF16))
            sk_hf.append(sink)
            mx_hf.append(mx)
        probs.append(jnp.concatenate(p_hf, axis=0))
        sink_rows.append(sk_hf)
        maxima.append(mx_hf)
    outs = []
    for (i, j), p in zip(units, probs):
        vt = v_t[j * half:(j + 1) * half, i * w:(i + 2) * w]
        lhs = jnp.concatenate([jnp.concatenate([vt, zeros_v], axis=1),
                               jnp.concatenate([zeros_v, vt], axis=1),
                               jnp.concatenate([ones8, zeros8], axis=1),
                               jnp.concatenate([zeros8, ones8], axis=1)], axis=0)
        outs.append(jnp.dot(lhs, p, preferred_element_type=F32))
    for (i, j), o_t, sk_hf, mx_hf in zip(units, outs, sink_rows, maxima):
        c0 = j * SWA_GROUP * SWA_DH
        parts = []
        for hf in range(2):
            den = (o_t[2 * half + hf * sum_rows:2 * half + hf * sum_rows + 1, :]
                   + jnp.exp2(sk_hf[hf] - mx_hf[hf]))
            parts.append(o_t[hf * half:(hf + 1) * half, :] / den)
        o = jnp.concatenate(parts, axis=0).T
        for pr in range(2):
            o_ref[i * w:(i + 1) * w, c0 + pr * LANES:c0 + (pr + 1) * LANES] = (
                o[pr * w:(pr + 1) * w, :].astype(o_ref.dtype))


def _swa_prompt(swa, sinks, nq):
    B, T, _ = swa.shape
    w = WINDOW
    rq = nq * w
    kcol = SWA_Q // SWA_KV
    prev = lambda n: jnp.maximum(n * nq - 1, 0)
    return pl.pallas_call(
        _swa_prompt_kernel,
        grid=(B, T // rq),
        in_specs=[pl.BlockSpec((None, rq, SWA_Q), lambda b, n: (b, n, 0)),
                  pl.BlockSpec((None, rq, SWA_KV), lambda b, n: (b, n, kcol)),
                  pl.BlockSpec((None, rq, SWA_KV), lambda b, n: (b, n, kcol + 1)),
                  pl.BlockSpec((None, w, SWA_KV), lambda b, n: (b, prev(n), kcol)),
                  pl.BlockSpec((None, w, SWA_KV), lambda b, n: (b, prev(n), kcol + 1)),
                  _const_spec((1, LANES))],
        out_specs=pl.BlockSpec((None, rq, SWA_Q), lambda b, n: (b, n, 0)),
        out_shape=jax.ShapeDtypeStruct((B, T, SWA_Q), BF16),
        compiler_params=_cparams(("parallel", "parallel")),
        name="swa_prompt",
    )(swa, swa, swa, swa, swa, sinks)


def _swa_sample_kernel(swa_ref, kc_ref, vc_ref, sink_ref, o_ref, kn_ref, vn_ref):
    bb = swa_ref.shape[0]
    w = kc_ref.shape[1]
    scale = SWA_DH ** -0.5
    for i in range(bb):
        kn_ref[i, 0:w - 1, :] = kc_ref[i, 1:w, :]
        kn_ref[i, w - 1:w, :] = swa_ref[i:i + 1, SWA_Q:SWA_Q + SWA_KV]
        vn_ref[i, 0:w - 1, :] = vc_ref[i, 1:w, :]
        vn_ref[i, w - 1:w, :] = swa_ref[i:i + 1, SWA_Q + SWA_KV:]
    units = [(i, j) for i in range(bb) for j in range(SWA_KV_HEADS)]
    sinks = [jnp.concatenate(
        [sink_ref[0:1, j * SWA_GROUP + g:j * SWA_GROUP + g + 1] for g in range(SWA_GROUP)]
        + [jnp.zeros((SUBLANES - SWA_GROUP, 1), F32)], axis=0) for j in range(SWA_KV_HEADS)]
    qs, scs = [], []
    for i, j in units:
        q4 = jnp.concatenate(
            [swa_ref[i:i + 1, (j * SWA_GROUP + g) * SWA_DH:(j * SWA_GROUP + g + 1) * SWA_DH]
             for g in range(SWA_GROUP)] + [jnp.zeros((SUBLANES - SWA_GROUP, SWA_DH), F32)], axis=0)
        qs.append(q4)
        scs.append(lax.dot_general(q4, kc_ref[i, :, j * SWA_DH:(j + 1) * SWA_DH],
                                   (((1,), (1,)), ((), ())), preferred_element_type=F32) * scale)
    soft = []
    for (i, j), q4, sc in zip(units, qs, scs):
        knew = swa_ref[i:i + 1, SWA_Q + j * SWA_DH:SWA_Q + (j + 1) * SWA_DH]
        sn = jnp.sum(q4 * knew, axis=-1, keepdims=True) * scale
        mx = jnp.maximum(jnp.maximum(jnp.max(sc, axis=-1, keepdims=True), sn), sinks[j])
        pc = jnp.exp(sc - mx)
        pn = jnp.exp(sn - mx)
        den = jnp.sum(pc, axis=-1, keepdims=True) + pn + jnp.exp(sinks[j] - mx)
        soft.append((pc, pn, den))
    pvs = [jnp.dot(pc, vc_ref[i, :, j * SWA_DH:(j + 1) * SWA_DH], preferred_element_type=F32)
           for (i, j), (pc, pn, den) in zip(units, soft)]
    for (i, j), (pc, pn, den), pv in zip(units, soft, pvs):
        vnew = swa_ref[i:i + 1, SWA_Q + SWA_KV + j * SWA_DH:SWA_Q + SWA_KV + (j + 1) * SWA_DH]
        o = (pv + pn * vnew) / den
        for g in range(SWA_GROUP):
            hq = j * SWA_GROUP + g
            o_ref[i:i + 1, hq * SWA_DH:(hq + 1) * SWA_DH] = o[g:g + 1, :]


def _swa_sample(swa, kc, vc, sinks, bb):
    R = swa.shape[0]
    w = kc.shape[1]
    cache_spec = pl.BlockSpec((bb, w, SWA_KV), lambda i: (i, 0, 0))
    return pl.pallas_call(
        _swa_sample_kernel,
        grid=(R // bb,),
        in_specs=[pl.BlockSpec((bb, SWA_OUT), lambda i: (i, 0)), cache_spec, cache_spec,
                  _const_spec((1, LANES))],
        out_specs=[pl.BlockSpec((bb, SWA_Q), lambda i: (i, 0)), cache_spec, cache_spec],
        out_shape=[jax.ShapeDtypeStruct((R, SWA_Q), F32),
                   jax.ShapeDtypeStruct(kc.shape, F32),
                   jax.ShapeDtypeStruct(vc.shape, F32)],
        compiler_params=_cparams(("parallel",)),
        name="swa_sample",
    )(swa, kc, vc, sinks)


def _outproj_ffn_kernel(gate_dn, x_ref, dn_ref, sw_ref, *refs):
    g5_ref, sh_ref, sc_ref, ga_ref, g_ref, woa_ref, wob_ref, wgu_ref, wd_ref, o_ref = refs[-10:]
    if gate_dn:
        z_ref, nw_ref = refs[:2]
        nw = nw_ref[...]
        dn = jnp.concatenate(
            [_gated_norm(dn_ref[:, h * DN_DV:(h + 1) * DN_DV].astype(F32),
                         z_ref[:, h * DN_DV:(h + 1) * DN_DV], nw).astype(BF16)
             for h in range(DN_HEADS)], axis=1)
    else:
        dn = dn_ref[...].astype(BF16)
    mix = (jnp.dot(dn, woa_ref[...], preferred_element_type=F32)
           + jnp.dot(sw_ref[...].astype(BF16), wob_ref[...], preferred_element_type=F32))
    x = x_ref[...] + g5_ref[...] * mix
    o_ref[...] = _ffn_body(x, sh_ref[...], sc_ref[...], ga_ref[...], g_ref[...], wgu_ref, wd_ref)


def _outproj_ffn(x, dn, sw, m, g, woa, wob, wgu, wd, l, tm, z=None, nw=None):
    G, R, _ = x.shape
    g5 = _mod_specs(m, 1, tm)[2]
    gate_dn = z is not None
    gate_specs = [_row_spec(tm, DN_V), _const_spec((1, DN_DV))] if gate_dn else []
    gate_args = (z, nw) if gate_dn else ()
    return pl.pallas_call(
        functools.partial(_outproj_ffn_kernel, gate_dn),
        grid=(G, R // tm),
        in_specs=[_row_spec(tm, D_MODEL), _row_spec(tm, DN_V), _row_spec(tm, SWA_Q)]
        + gate_specs + [g5] + _mod_specs(m, 2, tm) + [
            _const_spec((1, D_MODEL)),
            _layer_spec((DN_V, D_MODEL), l),
            _layer_spec((SWA_Q, D_MODEL), l),
            _layer_spec((D_MODEL, 2 * D_FF), l),
            _layer_spec((D_FF, D_MODEL), l),
        ],
        out_specs=_row_spec(tm, D_MODEL),
        out_shape=jax.ShapeDtypeStruct(x.shape, F32),
        compiler_params=_cparams(("parallel", "parallel")),
        name="outproj_ffn",
    )(x, dn, sw, *gate_args, m, m, m, m, g, woa, wob, wgu, wd)


def _rope_tables(pos):
    half = SWA_DH // 2
    inv = jnp.power(ROPE_THETA, -jnp.arange(half, dtype=F32) * 2.0 / SWA_DH)
    ang = pos.astype(F32)[:, None] * inv[None, :]
    cos, sin = jnp.cos(ang), jnp.sin(ang)
    reps = LANES // SWA_DH
    return (jnp.concatenate([cos, cos] * reps, axis=1),
            jnp.concatenate([-sin, sin] * reps, axis=1))


def _group_matrix(width):
    i = jnp.arange(width) // SWA_DH
    return jnp.where(i[:, None] == i[None, :], 1.0 / SWA_DH, 0.0).astype(BF16)


def _pack_w_in(w_in):
    conv = w_in[..., :DN_CONV_CH]
    o = DN_CONV_CH
    z = w_in[..., o:o + DN_V]
    o += DN_V
    ba = w_in[..., o:o + 2 * DN_HEADS]
    o += 2 * DN_HEADS
    sq = w_in[..., o:o + SWA_Q]
    o += SWA_Q
    sk = w_in[..., o:o + SWA_KV]
    o += SWA_KV
    sv = w_in[..., o:o + SWA_KV]
    pad = jnp.zeros(w_in.shape[:-1] + (LANES - 2 * DN_HEADS,), w_in.dtype)
    return jnp.concatenate([conv, z, sq, sk, sv, ba, pad], axis=-1).astype(BF16)


def _lane_row(vec, offset=0):
    return jnp.zeros((LANES,), F32).at[offset:offset + vec.shape[0]].set(vec.astype(F32))


def kernel(x_prompt, x_sample, c_prompt, c_sample, state_dn_conv, state_dn_S, cache_swa_k, cache_swa_v, w_ada, b_ada, g_ffn1, w_ffn1_gu, w_ffn1_down, g_mix, w_in, dn_conv_w, dn_A_log, dn_dt_bias, dn_norm_w, swa_q_norm, swa_k_norm, swa_sinks, w_out, g_ffn2, w_ffn2_gu, w_ffn2_down):
    L = w_ada.shape[0]
    B, T, _ = x_prompt.shape
    Bs = x_sample.shape[0]
    tm = min(ROW_TILE, T)
    bb = min(SAMPLE_SEQS, Bs)
    assert T % tm == 0 and T % min(INTRA_STEP_ROWS, T) == 0 and T % INTRA_ROWS == 0
    assert T % (min(SWA_QBLOCKS, T // WINDOW) * WINDOW) == 0 and Bs % bb == 0
    assert x_sample.shape[1] == 1 and cache_swa_k.shape[2] == WINDOW

    m_p, m_s = _adaln(c_prompt, c_sample, w_ada, b_ada)
    m_p = m_p.reshape(L, N_MOD, B, 1, D_MODEL)
    m_s = m_s.reshape(L, N_MOD, 1, Bs, D_MODEL)

    cos_p, sin_p = _rope_tables(jnp.arange(T))
    cos_s, sin_s = _rope_tables(PAST_LEN + jnp.arange(1))
    gq, gk = _group_matrix(SWA_Q), _group_matrix(SWA_KV)

    wgu1, wd1 = w_ffn1_gu.astype(BF16), w_ffn1_down.astype(BF16)
    wgu2, wd2 = w_ffn2_gu.astype(BF16), w_ffn2_down.astype(BF16)
    win = _pack_w_in(w_in)
    woa, wob = w_out[:, :DN_V].astype(BF16), w_out[:, DN_V:].astype(BF16)

    yp = x_prompt
    ys = x_sample.reshape(1, Bs, D_MODEL)
    outs = [[] for _ in range(8)]
    for l in range(L):
        g1, gm, g2 = (g_ffn1[l].reshape(1, D_MODEL), g_mix[l].reshape(1, D_MODEL),
                      g_ffn2[l].reshape(1, D_MODEL))
        cw = dn_conv_w[l]
        nw = dn_norm_w[l].reshape(1, DN_DV)
        prm = jnp.stack([
            _lane_row(-jnp.exp(dn_A_log[l].astype(F32)), DN_HEADS),
            _lane_row(dn_dt_bias[l], DN_HEADS),
            jnp.concatenate([swa_q_norm[l]] * (LANES // SWA_DH)).astype(F32),
            jnp.concatenate([swa_k_norm[l]] * (LANES // SWA_DH)).astype(F32),
        ] + [jnp.zeros((LANES,), F32)] * (SUBLANES - 4))
        sinks = _lane_row(swa_sinks[l]).reshape(1, LANES)

        yp = _ffn(yp, m_p[l], 0, g1, wgu1, wd1, l, tm)
        qkv, z, bg, swa, tail = _inproj_prompt(yp, m_p[l], gm, win, cw, prm, cos_p, sin_p, gq, gk,
                                               l, tm)
        dn_o, s_new = _delta_prompt(qkv, bg, min(INTRA_STEP_ROWS, T))
        sw_o = _swa_prompt(swa, sinks, min(SWA_QBLOCKS, T // WINDOW))
        yp = _outproj_ffn(yp, dn_o, sw_o, m_p[l], g2, woa, wob, wgu2, wd2, l, tm, z, nw)
        nkeep = min(WINDOW, T)
        outs[0].append(tail[:, HALO - (DN_CONV - 1):])
        outs[1].append(s_new)
        outs[2].append(swa[:, T - nkeep:, SWA_Q:SWA_Q + SWA_KV].reshape(B, nkeep, SWA_KV_HEADS, SWA_DH))
        outs[3].append(swa[:, T - nkeep:, SWA_Q + SWA_KV:].reshape(B, nkeep, SWA_KV_HEADS, SWA_DH))

        ys = _ffn(ys, m_s[l], 0, g1, wgu1, wd1, l, Bs)
        buf = jnp.transpose(state_dn_conv[l], (1, 0, 2))
        qkv, z, bg, swa, cnew = _inproj_sample(ys, m_s[l], gm, win, cw, prm, cos_s, sin_s, gq, gk,
                                               buf, l)
        dn_o, s_new = _delta_sample(qkv[0], bg[0], z[0], nw, state_dn_S, l, bb,
                                    tuple(outs[5]) if l == L - 1 else ())
        w0 = cache_swa_k.shape[2]
        sw_o, k_new, v_new = _swa_sample(swa[0], cache_swa_k[l].reshape(Bs, w0, SWA_KV),
                                         cache_swa_v[l].reshape(Bs, w0, SWA_KV), sinks, bb)
        ys = _outproj_ffn(ys, dn_o[None], sw_o[None], m_s[l], g2, woa, wob, wgu2, wd2, l, Bs)
        outs[4].append(jnp.transpose(cnew, (1, 0, 2)))
        outs[5].append(s_new)
        outs[6].append(k_new.reshape(Bs, w0, SWA_KV_HEADS, SWA_DH))
        outs[7].append(v_new.reshape(Bs, w0, SWA_KV_HEADS, SWA_DH))

    s_sample = outs[5][-1] if L > 1 else outs[5][-1][None]
    stacked = [s_sample if i == 5 else jnp.stack(o) for i, o in enumerate(outs)]
    return (yp, ys.reshape(Bs, 1, D_MODEL)) + tuple(stacked)
```
